```python
import jax, jax.numpy as jnp
from jax import lax
import numpy as np

D_MODEL = 1024
BATCH = 2
SEQ = 8192
DEPTH = 1
DEC_BATCH = 32
DEC_SEQ = 32
PAST_LEN = 2048

CHUNK = 64
GLA_HEADS = 4
GLA_DK = 128
GLA_DV = 128
GLA_RANK = 16
GLA_TAU = 16.0
SWA_HEADS = 8
SWA_KV_HEADS = 2
SWA_GROUP = SWA_HEADS // SWA_KV_HEADS
SWA_HD = 64
WINDOW = 128
SWA_BAND = WINDOW // CHUNK
ROPE_DIM = SWA_HD // 4
ROPE_THETA = 500000.0
MEM_TOKENS = 256
MEM_HEADS = 4
MEM_HD = 128
D_FF = 2816
CONV_W = 3
N_BRANCH = 3
ALPHA = (2 * DEPTH) ** 0.25
BETA = (8 * DEPTH) ** -0.25
LN_EPS = 1e-5

IN_SIZES = (GLA_HEADS * GLA_DK, GLA_HEADS * GLA_DK, GLA_HEADS * GLA_DV, GLA_HEADS * GLA_DV, GLA_RANK,
            SWA_HEADS * SWA_HD, SWA_KV_HEADS * SWA_HD, SWA_KV_HEADS * SWA_HD,
            MEM_HEADS * MEM_HD, N_BRANCH * D_MODEL)
IN_COLS = sum(IN_SIZES)
SPLIT_AT = tuple(int(c) for c in np.cumsum(IN_SIZES)[:-1])

kernel_name = "hybrid_gla_swa_mem_convffn_step"


def layer_norm(x, g, b):
    xf = x.astype(jnp.float32)
    mu = jnp.mean(xf, axis=-1, keepdims=True)
    var = jnp.mean(jnp.square(xf - mu), axis=-1, keepdims=True)
    return ((xf - mu) * lax.rsqrt(var + LN_EPS) * g + b).astype(x.dtype)


def rms_norm(x, g):
    xf = x.astype(jnp.float32)
    return (xf * lax.rsqrt(jnp.mean(xf * xf, axis=-1, keepdims=True) + LN_EPS) * g).astype(x.dtype)


def partial_rope(x, pos):
    half = ROPE_DIM // 2
    inv = ROPE_THETA ** (-jnp.arange(half, dtype=jnp.float32) / half)
    ang = pos.astype(jnp.float32)[:, None] * inv[None, :]
    cos, sin = jnp.cos(ang)[:, None, :], jnp.sin(ang)[:, None, :]
    xr = x[..., :ROPE_DIM].astype(jnp.float32)
    x1, x2 = xr[..., :half], xr[..., half:]
    rot = jnp.concatenate([x1 * cos - x2 * sin, x2 * cos + x1 * sin], axis=-1).astype(x.dtype)
    return jnp.concatenate([rot, x[..., ROPE_DIM:]], axis=-1)


def gla_chunked(q, k, v, log_a, s0, block):
    B, T, H, DK = q.shape
    DV = v.shape[-1]
    n = T // block
    f32 = jnp.float32
    qc = q.reshape(B, n, block, H, DK).astype(f32)
    kc = k.reshape(B, n, block, H, DK).astype(f32)
    vc = v.reshape(B, n, block, H, DV).astype(f32)
    b = jnp.cumsum(log_a.reshape(B, n, block, H, DK).astype(f32), axis=2)
    b_last = b[:, :, -1:]
    q_dec = qc * jnp.exp(b)
    k_inv = kc * jnp.exp(-b)
    k_tail = kc * jnp.exp(b_last - b)
    causal = jnp.tril(jnp.ones((block, block), dtype=bool))
    att = jnp.where(causal, jnp.einsum("bnthk,bnshk->bnhts", q_dec, k_inv), 0.0)
    o_intra = jnp.einsum("bnhts,bnshv->bnthv", att, vc)
    ds = jnp.einsum("bnshk,bnshv->bnhkv", k_tail, vc)
    decay = jnp.exp(b_last[:, :, 0])

    def step(S, inp):
        d, dS = inp
        return S * d[..., None] + dS, S

    s_final, s_prev = lax.scan(step, s0.astype(f32), (jnp.moveaxis(decay, 1, 0), jnp.moveaxis(ds, 1, 0)))
    s_prev = jnp.moveaxis(s_prev, 0, 1)
    o_inter = jnp.einsum("bnthk,bnhkv->bnthv", q_dec, s_prev)
    o = (o_intra + o_inter).reshape(B, T, H, DV)
    return o.astype(q.dtype), s_final.astype(s0.dtype)


def sink_softmax(s, sinks):
    sk = sinks.astype(jnp.float32).reshape(SWA_KV_HEADS, SWA_GROUP, 1, 1)
    m = jnp.maximum(jnp.max(s, axis=-1, keepdims=True), sk)
    p = jnp.exp(s - m)
    return p / (jnp.sum(p, axis=-1, keepdims=True) + jnp.exp(sk - m))


def swa_banded(q, k, v, sinks):
    B, T = q.shape[:2]
    n = T // CHUNK
    nb = SWA_BAND + 1
    f32 = jnp.float32
    qc = q.reshape(B, n, CHUNK, SWA_KV_HEADS, SWA_GROUP, SWA_HD).astype(f32)
    pad = ((0, 0), (SWA_BAND * CHUNK, 0), (0, 0), (0, 0))
    kp = jnp.pad(k, pad).reshape(B, n + SWA_BAND, CHUNK, SWA_KV_HEADS, SWA_HD)
    vp = jnp.pad(v, pad).reshape(B, n + SWA_BAND, CHUNK, SWA_KV_HEADS, SWA_HD)
    kb = jnp.concatenate([kp[:, j:j + n] for j in range(nb)], axis=2).astype(f32)
    vb = jnp.concatenate([vp[:, j:j + n] for j in range(nb)], axis=2).astype(f32)
    key_chunk = jnp.arange(n)[:, None] + jnp.arange(nb)[None, :] - SWA_BAND
    valid = jnp.repeat(key_chunk >= 0, CHUNK, axis=1)
    s = jnp.einsum("bnqhgd,bnkhd->bnhgqk", qc, kb) * SWA_HD ** -0.5
    s = jnp.where(valid[None, :, None, None, None, :], s, -jnp.inf)
    p = sink_softmax(s, sinks)
    o = jnp.einsum("bnhgqk,bnkhd->bnqhgd", p, vb)
    return o.reshape(B, T, SWA_HEADS * SWA_HD).astype(q.dtype)


def swa_with_history(q, k_all, v_all, sinks):
    B, T = q.shape[:2]
    f32 = jnp.float32
    qg = q.reshape(B, T, SWA_KV_HEADS, SWA_GROUP, SWA_HD).astype(f32)
    s = jnp.einsum("bqhgd,bkhd->bhgqk", qg, k_all.astype(f32)) * SWA_HD ** -0.5
    p = sink_softmax(s, sinks)
    o = jnp.einsum("bhgqk,bkhd->bqhgd", p, v_all.astype(f32))
    return o.reshape(B, T, SWA_HEADS * SWA_HD).astype(q.dtype)


def mem_attention(q, mk, mv):
    B, T = q.shape[:2]
    s = jnp.einsum("bthd,bmhd->bhtm", q.astype(jnp.float32), mk.astype(jnp.float32)) * MEM_HD ** -0.5
    p = jax.nn.softmax(s, axis=-1)
    o = jnp.einsum("bhtm,bmhd->bthd", p, mv.astype(jnp.float32))
    return o.reshape(B, T, MEM_HEADS * MEM_HD).astype(q.dtype)


def conv_ffn(x, hist, w_up, conv_w, conv_b, w_down):
    T = x.shape[1]
    u = x @ w_up
    ue = jnp.concatenate([hist.astype(u.dtype), u], axis=1)
    c = conv_b + sum(ue[:, j:j + T] * conv_w[j] for j in range(CONV_W))
    gate, val = jnp.split(c, 2, axis=-1)
    return (jax.nn.gelu(gate) * val) @ w_down, ue[:, -(CONV_W - 1):]


def _layer(x, pos, gla_s0, swa_k_hist, swa_v_hist, mem_k, mem_v, conv_hist, w):
    B, T, _ = x.shape
    z = x @ w["w_in"]
    gq, gk, gv, gr, ga, sq, sk, sv, mq, gl = jnp.split(z, SPLIT_AT, axis=-1)
    log_a = jax.nn.log_sigmoid((ga @ w["w_gla_a2"] + w["b_gla_a"]).astype(jnp.float32)) / GLA_TAU
    o_a, gla_s = gla_chunked(gq.reshape(B, T, GLA_HEADS, GLA_DK) * GLA_DK ** -0.5,
                             gk.reshape(B, T, GLA_HEADS, GLA_DK),
                             gv.reshape(B, T, GLA_HEADS, GLA_DV),
                             log_a.reshape(B, T, GLA_HEADS, GLA_DK), gla_s0, min(CHUNK, T))
    o_a = rms_norm(o_a, w["gla_norm_g"]) * jax.nn.silu(gr.reshape(B, T, GLA_HEADS, GLA_DV))
    y_a = o_a.reshape(B, T, GLA_HEADS * GLA_DV) @ w["w_br_gla"]
    q = partial_rope(sq.reshape(B, T, SWA_HEADS, SWA_HD), pos)
    k = partial_rope(sk.reshape(B, T, SWA_KV_HEADS, SWA_HD), pos)
    v = sv.reshape(B, T, SWA_KV_HEADS, SWA_HD)
    if swa_k_hist is None:
        o_b = swa_banded(q, k, v, w["swa_sinks"])
        k_all, v_all = k, v
    else:
        k_all = jnp.concatenate([swa_k_hist.astype(k.dtype), k], axis=1)
        v_all = jnp.concatenate([swa_v_hist.astype(v.dtype), v], axis=1)
        o_b = swa_with_history(q, k_all, v_all, w["swa_sinks"])
    y_b = o_b @ w["w_br_swa"]
    o_c = mem_attention(mq.reshape(B, T, MEM_HEADS, MEM_HD), mem_k, mem_v)
    y_c = o_c @ w["w_br_mem"]
    g_a, g_b, g_c = jnp.split(jax.nn.sigmoid(gl + w["b_gate"]), N_BRANCH, axis=-1)
    mix = (g_a * y_a + g_b * y_b + g_c * y_c) @ w["w_o"]
    x = layer_norm(ALPHA * x + mix, w["ln1_g"], w["ln1_b"])
    f, conv_new = conv_ffn(x, conv_hist, w["w_up"], w["conv_w"], w["conv_b"], w["w_down"])
    x = layer_norm(ALPHA * x + f, w["ln2_g"], w["ln2_b"])
    return x, gla_s, k_all[:, -WINDOW:], v_all[:, -WINDOW:], conv_new


def setup_inputs(seed: int = 0) -> dict:
    key = jax.random.key(seed)
    ks = iter(jax.random.split(key, 32))

    def nrm(shape, scale):
        return jax.random.normal(next(ks), shape, jnp.float32) * scale

    L, D, F = DEPTH, D_MODEL, D_FF
    return {
        "x_prompt": nrm((BATCH, SEQ, D), 1.0),
        "x_sample": nrm((DEC_BATCH, DEC_SEQ, D), 1.0),
        "cache_swa_k": nrm((L, DEC_BATCH, WINDOW, SWA_KV_HEADS, SWA_HD), 1.0),
        "cache_swa_v": nrm((L, DEC_BATCH, WINDOW, SWA_KV_HEADS, SWA_HD), 1.0),
        "state_gla": nrm((L, DEC_BATCH, GLA_HEADS, GLA_DK, GLA_DV), 1.0),
        "cache_mem_k": nrm((L, DEC_BATCH, MEM_TOKENS, MEM_HEADS, MEM_HD), 1.0),
        "cache_mem_v": nrm((L, DEC_BATCH, MEM_TOKENS, MEM_HEADS, MEM_HD), 1.0),
        "cache_ffn_conv": nrm((L, DEC_BATCH, CONV_W - 1, 2 * F), 1.0),
        "mem_prompt": nrm((BATCH, MEM_TOKENS, D), 1.0),
        "ln1_g": 1.0 + nrm((L, D), 0.02),
        "ln1_b": nrm((L, D), 0.02),
        "ln2_g": 1.0 + nrm((L, D), 0.02),
        "ln2_b": nrm((L, D), 0.02),
        "w_in": nrm((L, D, IN_COLS), D ** -0.5),
        "b_gate": nrm((L, N_BRANCH * D), 0.02),
        "w_gla_a2": nrm((L, GLA_RANK, GLA_HEADS * GLA_DK), GLA_RANK ** -0.5),
        "b_gla_a": nrm((L, GLA_HEADS * GLA_DK), 0.1),
        "gla_norm_g": 1.0 + nrm((L, GLA_DV), 0.02),
        "swa_sinks": nrm((L, SWA_HEADS), 0.5),
        "w_mem_kv": nrm((L, D, 2 * MEM_HEADS * MEM_HD), D ** -0.5),
        "w_br_gla": nrm((L, GLA_HEADS * GLA_DV, D), (GLA_HEADS * GLA_DV) ** -0.5),
        "w_br_swa": nrm((L, SWA_HEADS * SWA_HD, D), (SWA_HEADS * SWA_HD) ** -0.5),
        "w_br_mem": nrm((L, MEM_HEADS * MEM_HD, D), (MEM_HEADS * MEM_HD) ** -0.5),
        "w_o": nrm((L, D, D), D ** -0.5 * BETA),
        "w_up": nrm((L, D, 2 * F), D ** -0.5),
        "conv_w": nrm((L, CONV_W, 2 * F), CONV_W ** -0.5),
        "conv_b": nrm((L, 2 * F), 0.02),
        "w_down": nrm((L, F, D), F ** -0.5 * BETA),
    }


def reference(x_prompt, x_sample, cache_swa_k, cache_swa_v, state_gla, cache_mem_k, cache_mem_v,
              cache_ffn_conv, mem_prompt, ln1_g, ln1_b, ln2_g, ln2_b, w_in, b_gate, w_gla_a2, b_gla_a,
              gla_norm_g, swa_sinks, w_mem_kv, w_br_gla, w_br_swa, w_br_mem, w_o, w_up, conv_w, conv_b,
              w_down):
    Bp, Tp, _ = x_prompt.shape
    Bs, Ts, _ = x_sample.shape
    pos_p = jnp.arange(Tp, dtype=jnp.int32)
    pos_s = PAST_LEN + jnp.arange(Ts, dtype=jnp.int32)
    hp, hs = x_prompt, x_sample
    p_swk, p_swv, p_gla, p_mk, p_mv, p_conv = [], [], [], [], [], []
    s_swk, s_swv, s_gla, s_conv = [], [], [], []
    for l in range(DEPTH):
        w = dict(ln1_g=ln1_g[l], ln1_b=ln1_b[l], ln2_g=ln2_g[l], ln2_b=ln2_b[l], w_in=w_in[l],
                 b_gate=b_gate[l], w_gla_a2=w_gla_a2[l], b_gla_a=b_gla_a[l], gla_norm_g=gla_norm_g[l],
                 swa_sinks=swa_sinks[l], w_br_gla=w_br_gla[l], w_br_swa=w_br_swa[l], w_br_mem=w_br_mem[l],
                 w_o=w_o[l], w_up=w_up[l], conv_w=conv_w[l], conv_b=conv_b[l], w_down=w_down[l])
        mk_p, mv_p = [t.reshape(Bp, MEM_TOKENS, MEM_HEADS, MEM_HD)
                      for t in jnp.split(mem_prompt @ w_mem_kv[l], 2, axis=-1)]
        gla0 = jnp.zeros((Bp, GLA_HEADS, GLA_DK, GLA_DV), x_prompt.dtype)
        conv0 = jnp.zeros((Bp, CONV_W - 1, 2 * D_FF), x_prompt.dtype)
        hp, g_p, k_p, v_p, c_p = _layer(hp, pos_p, gla0, None, None, mk_p, mv_p, conv0, w)
        hs, g_s, k_s, v_s, c_s = _layer(hs, pos_s, state_gla[l], cache_swa_k[l], cache_swa_v[l],
                                        cache_mem_k[l], cache_mem_v[l], cache_ffn_conv[l], w)
        p_swk.append(k_p); p_swv.append(v_p); p_gla.append(g_p)
        p_mk.append(mk_p); p_mv.append(mv_p); p_conv.append(c_p)
        s_swk.append(k_s); s_swv.append(v_s); s_gla.append(g_s); s_conv.append(c_s)
    return (hp, hs,
            jnp.stack(p_swk), jnp.stack(p_swv), jnp.stack(p_gla), jnp.stack(p_mk), jnp.stack(p_mv),
            jnp.stack(p_conv),
            jnp.stack(s_swk), jnp.stack(s_swv), jnp.stack(s_gla), jnp.stack(s_conv))
```

```python
import functools

import jax
import jax.numpy as jnp
import numpy as np
from jax import lax
from jax.experimental import pallas as pl
from jax.experimental.pallas import tpu as pltpu

D_MODEL = 1024
CHUNK = 64
GLA_HEADS = 4
GLA_DK = 128
GLA_DV = 128
GLA_RANK = 16
GLA_TAU = 16.0
SWA_HEADS = 8
SWA_KV_HEADS = 2
SWA_HD = 64
WINDOW = 128
ROPE_DIM = 16
ROPE_THETA = 500000.0
MEM_TOKENS = 256
MEM_HEADS = 4
MEM_HD = 128
D_FF = 2816
CONV_W = 3
N_BRANCH = 3
PAST_LEN = 2048
LN_EPS = 1e-5

LANES = 128
V7X_VMEM_LIMIT = 56 * 1024 * 1024

GLA_W = GLA_HEADS * GLA_DK
SWA_QW = SWA_HEADS * SWA_HD
SWA_KW = SWA_KV_HEADS * SWA_HD
MEM_W = MEM_HEADS * MEM_HD
GATE_W = N_BRANCH * D_MODEL

OFF_GQ = 0
OFF_GK = OFF_GQ + GLA_W
OFF_GV = OFF_GK + GLA_W
OFF_GR = OFF_GV + GLA_W
OFF_SQ = OFF_GR + GLA_W
OFF_SK = OFF_SQ + SWA_QW
OFF_SV = OFF_SK + SWA_KW
OFF_MQ = OFF_SV + SWA_KW
OFF_GL = OFF_MQ + MEM_W
OFF_GA = OFF_GL + GATE_W
IN_COLS_PACKED = OFF_GA + LANES

_BF = jnp.bfloat16
_F32 = jnp.float32


def _bf(x):
    return x.astype(_BF)


def _dot(a, b):
    return jnp.dot(a, b, preferred_element_type=_F32)


def _dot_nt(a, b):
    return lax.dot_general(a, b, (((1,), (1,)), ((), ())), preferred_element_type=_F32)


def _layer_norm(h, g, b):
    mu = jnp.mean(h, axis=-1, keepdims=True)
    d = h - mu
    var = jnp.mean(d * d, axis=-1, keepdims=True)
    return d * lax.rsqrt(var + LN_EPS) * g + b


def _log_sigmoid(x):
    return -(jnp.maximum(-x, 0.0) + jnp.log(1.0 + jnp.exp(-jnp.abs(x))))


def _rope(a, c, s_dn, s_up):
    outs = []
    for j in range(a.shape[1] // LANES):
        slab = a[:, LANES * j:LANES * (j + 1)]
        fwd = pltpu.roll(slab, LANES - ROPE_DIM // 2, axis=1)
        bwd = pltpu.roll(slab, ROPE_DIM // 2, axis=1)
        outs.append(slab * c + fwd * s_dn + bwd * s_up)
    return outs[0] if len(outs) == 1 else jnp.concatenate(outs, axis=1)


def _gla_tile(q, k, v, log_a, chunk, nchunk, state_in, state_out, chained):
    R = chunk * nchunk
    row = lax.broadcasted_iota(jnp.int32, (R, R), 0)
    col = lax.broadcasted_iota(jnp.int32, (R, R), 1)
    tril = (row // chunk == col // chunk) & (col <= row)
    ltri = jnp.where(tril, 1.0, 0.0).astype(_BF)
    hi = _bf(log_a)
    lo = _bf(log_a - hi.astype(_F32))
    b = _dot(ltri, hi) + _dot(ltri, lo)
    b_last = [b[chunk * (c + 1) - 1:chunk * (c + 1), :] for c in range(nchunk)]
    bl_rows = jnp.concatenate([jnp.broadcast_to(bl, (chunk, GLA_W)) for bl in b_last], axis=0) \
        if nchunk > 1 else jnp.broadcast_to(b_last[0], (chunk, GLA_W))
    q_dec = _bf(q * (GLA_DK ** -0.5) * jnp.exp(b))
    k_inv = _bf(k * jnp.exp(-b))
    k_tail = _bf(k * jnp.exp(bl_rows - b))
    vb = _bf(v)
    outs = []
    for h in range(GLA_HEADS):
        hs = slice(GLA_DK * h, GLA_DK * (h + 1))
        att = jnp.where(tril, _dot_nt(q_dec[:, hs], k_inv[:, hs]), 0.0)
        o_intra = _dot(_bf(att), vb[:, hs])
        o_inter = []
        S = None
        for c in range(nchunk):
            rs = slice(chunk * c, chunk * (c + 1))
            if c == 0 or not chained:
                S = state_in(c, h)
            o_inter.append(_dot(q_dec[rs, hs], _bf(S)))
            decay = jnp.exp(b_last[c][:, hs])
            dcol = jnp.transpose(jnp.broadcast_to(decay, (GLA_DV, GLA_DK)))
            ds = _dot(jnp.transpose(k_tail[rs, hs].astype(_F32)).astype(_BF), vb[rs, hs])
            S = S * dcol + ds
            if c == nchunk - 1 or not chained:
                state_out(c, h, S)
        o_inter = jnp.concatenate(o_inter, axis=0) if nchunk > 1 else o_inter[0]
        outs.append(o_intra + o_inter)
    return jnp.concatenate(outs, axis=1)


def _gla_gate(o, gr, g):
    outs = []
    for h in range(GLA_HEADS):
        hs = slice(GLA_DV * h, GLA_DV * (h + 1))
        oh = o[:, hs]
        ms = jnp.mean(oh * oh, axis=-1, keepdims=True)
        grh = gr[:, hs]
        outs.append(oh * lax.rsqrt(ms + LN_EPS) * g * (grh * jax.nn.sigmoid(grh)))
    return jnp.concatenate(outs, axis=1)


def _lane_half_masks(shape):
    lane = lax.broadcasted_iota(jnp.int32, shape, 1)
    return lane < SWA_HD, lane >= SWA_HD


def _swa_split_q(q):
    lo_m, hi_m = _lane_half_masks((q.shape[0], LANES))
    out = []
    for j in range(SWA_QW // LANES):
        slab = q[:, LANES * j:LANES * (j + 1)]
        out.append((_bf(jnp.where(lo_m, slab, 0.0)), _bf(jnp.where(hi_m, slab, 0.0))))
    return out


def _swa_dup_k(k):
    lo_m, _ = _lane_half_masks(k.shape)
    kr = pltpu.roll(k, SWA_HD, axis=1)
    return [_bf(jnp.where(lo_m, k, kr)), _bf(jnp.where(lo_m, kr, k))]


def _swa_place_v(v):
    lo_m, hi_m = _lane_half_masks(v.shape)
    vr = pltpu.roll(v, SWA_HD, axis=1)
    return [(_bf(jnp.where(lo_m, v, 0.0)), _bf(jnp.where(hi_m, vr, 0.0))),
            (_bf(jnp.where(lo_m, vr, 0.0)), _bf(jnp.where(hi_m, v, 0.0)))]


def _swa_block(qrows, kk, vlo, vhi, sink_col, valid, tq):
    s = _dot_nt(qrows, kk)
    if valid is not None:
        s = jnp.where(valid, s, -jnp.inf)
    m = jnp.maximum(jnp.max(s, axis=-1, keepdims=True), sink_col)
    p = jnp.exp(s - m)
    den = jnp.sum(p, axis=-1, keepdims=True) + jnp.exp(sink_col - m)
    p = _bf(p / den)
    o0 = _dot(p[0:tq], vlo) + _dot(p[tq:2 * tq], vhi)
    o1 = _dot(p[2 * tq:3 * tq], vlo) + _dot(p[3 * tq:4 * tq], vhi)
    return o0, o1


def _sink_col(sinks_ref, g, tq):
    r = lax.broadcasted_iota(jnp.int32, (4 * tq, 1), 0)
    base = 4 * g
    return jnp.where(r < tq, sinks_ref[base],
                     jnp.where(r < 2 * tq, sinks_ref[base + 1],
                               jnp.where(r < 3 * tq, sinks_ref[base + 2], sinks_ref[base + 3])))


def _mem_attention(q, mk, mv):
    qb = _bf(q)
    outs = []
    for h in range(MEM_HEADS):
        hs = slice(MEM_HD * h, MEM_HD * (h + 1))
        s = _dot_nt(qb[:, hs], mk[:, hs])
        m = jnp.max(s, axis=-1, keepdims=True)
        p = jnp.exp(s - m)
        p = p / jnp.sum(p, axis=-1, keepdims=True)
        outs.append(_dot(_bf(p), mv[:, hs]))
    return jnp.concatenate(outs, axis=1)


def _merge_ln1(x, xb, y_a, y_b, y_c, w_in_ref, bgate_ref, wo_ref, g_ref, b_ref, alpha):
    gates = []
    for i in range(N_BRANCH):
        gl = _dot(xb, w_in_ref[:, OFF_GL + D_MODEL * i:OFF_GL + D_MODEL * (i + 1)])
        gates.append(jax.nn.sigmoid(gl + bgate_ref[:, D_MODEL * i:D_MODEL * (i + 1)]))
    mix = _dot(_bf(gates[0] * y_a + gates[1] * y_b + gates[2] * y_c), wo_ref[...])
    return _layer_norm(alpha * x + mix, g_ref[...], b_ref[...])


def _forget_log(xb, w_in_ref, w2_ref, ba_ref):
    ga = _dot(xb, w_in_ref[:, OFF_GA:OFF_GA + LANES])
    return _log_sigmoid(_dot(_bf(ga), w2_ref[...]) + ba_ref[...]) * (1.0 / GLA_TAU)


def _prompt_mixer_kernel(sinks_ref, x_ref, rc_ref, rdn_ref, rup_ref, mk_ref, mv_ref, w_in_ref, w2_ref,
                         ba_ref, gng_ref, wbg_ref, wbs_ref, wbm_ref, bgate_ref, wo_ref, ln_g_ref, ln_b_ref,
                         out_ref, gla_out_ref, swak_out_ref, swav_out_ref,
                         s_scr, kk_scr, vv_scr, *, rows, alpha):
    t = pl.program_id(1)
    nt = pl.num_programs(1)
    nchunk = rows // CHUNK

    @pl.when(t == 0)
    def _():
        s_scr[...] = jnp.zeros_like(s_scr)
        kk_scr[:, 0:WINDOW, :] = jnp.zeros((SWA_KV_HEADS, WINDOW, LANES), _BF)
        vv_scr[:, 0:WINDOW, :] = jnp.zeros((2 * SWA_KV_HEADS, WINDOW, LANES), _BF)

    x = x_ref[0]
    xb = _bf(x)

    def proj(off, width):
        return _dot(xb, w_in_ref[:, off:off + width])

    log_a = _forget_log(xb, w_in_ref, w2_ref, ba_ref)

    def state_in(c, h):
        return s_scr[h]

    def state_out(c, h, S):
        s_scr[h] = S

    o_a = _gla_tile(proj(OFF_GQ, GLA_W), proj(OFF_GK, GLA_W), proj(OFF_GV, GLA_W), log_a,
                    CHUNK, nchunk, state_in, state_out, chained=True)
    o_a = _gla_gate(o_a, proj(OFF_GR, GLA_W), gng_ref[...])
    y_a = _dot(_bf(o_a), wbg_ref[...])

    @pl.when(t == nt - 1)
    def _():
        gla_out_ref[0] = s_scr[...]

    rc, rdn, rup = rc_ref[...], rdn_ref[...], rup_ref[...]
    q = _rope(proj(OFF_SQ, SWA_QW), rc, rdn, rup) * (SWA_HD ** -0.5)
    k = _rope(proj(OFF_SK, SWA_KW), rc, rdn, rup)
    v = proj(OFF_SV, SWA_KW)

    @pl.when(t == nt - 1)
    def _():
        swak_out_ref[0] = k[rows - WINDOW:rows, :]
        swav_out_ref[0] = v[rows - WINDOW:rows, :]

    kk = _swa_dup_k(k)
    vv = _swa_place_v(v)
    for g in range(SWA_KV_HEADS):
        kk_scr[g, WINDOW:WINDOW + rows, :] = kk[g]
        vv_scr[2 * g, WINDOW:WINDOW + rows, :] = vv[g][0]
        vv_scr[2 * g + 1, WINDOW:WINDOW + rows, :] = vv[g][1]
    qs = _swa_split_q(q)
    nk = WINDOW + CHUNK
    kcol = lax.broadcasted_iota(jnp.int32, (4 * CHUNK, nk), 1)
    slabs = [[None] * nchunk for _ in range(SWA_QW // LANES)]
    for g in range(SWA_KV_HEADS):
        sink_col = _sink_col(sinks_ref, g, CHUNK)
        for c in range(nchunk):
            rs = slice(CHUNK * c, CHUNK * (c + 1))
            ks = slice(CHUNK * c, CHUNK * c + nk)
            qrows = jnp.concatenate([qs[2 * g][0][rs], qs[2 * g][1][rs],
                                     qs[2 * g + 1][0][rs], qs[2 * g + 1][1][rs]], axis=0)
            valid = (kcol + (t * rows + CHUNK * c - WINDOW)) >= 0
            o0, o1 = _swa_block(qrows, kk_scr[g, ks, :], vv_scr[2 * g, ks, :], vv_scr[2 * g + 1, ks, :],
                                sink_col, valid, CHUNK)
            slabs[2 * g][c] = o0
            slabs[2 * g + 1][c] = o1
    o_b = jnp.concatenate([jnp.concatenate(s, axis=0) if nchunk > 1 else s[0] for s in slabs], axis=1)
    y_b = _dot(_bf(o_b), wbs_ref[...])
    for g in range(SWA_KV_HEADS):
        kk_scr[g, 0:WINDOW, :] = kk[g][rows - WINDOW:rows]
        vv_scr[2 * g, 0:WINDOW, :] = vv[g][0][rows - WINDOW:rows]
        vv_scr[2 * g + 1, 0:WINDOW, :] = vv[g][1][rows - WINDOW:rows]

    o_c = _mem_attention(proj(OFF_MQ, MEM_W) * (MEM_HD ** -0.5), _bf(mk_ref[0]), _bf(mv_ref[0]))
    y_c = _dot(_bf(o_c), wbm_ref[...])

    out_ref[0] = _merge_ln1(x, xb, y_a, y_b, y_c, w_in_ref, bgate_ref, wo_ref, ln_g_ref, ln_b_ref, alpha)


def _sample_mixer_kernel(sinks_ref, x_ref, rc_ref, rdn_ref, rup_ref, mk_ref, mv_ref, gla_in_ref,
                         swak_in_ref, swav_in_ref, w_in_ref, w2_ref,
                         ba_ref, gng_ref, wbg_ref, wbs_ref, wbm_ref, bgate_ref, wo_ref, ln_g_ref, ln_b_ref,
                         out_ref, gla_out_ref, swak_out_ref, swav_out_ref, *, nb, tq, alpha):
    rows = nb * tq
    x = x_ref[...].reshape(rows, D_MODEL)
    xb = _bf(x)

    def proj(off, width):
        return _dot(xb, w_in_ref[:, off:off + width])

    log_a = _forget_log(xb, w_in_ref, w2_ref, ba_ref)

    def state_in(c, h):
        return gla_in_ref[c, h]

    def state_out(c, h, S):
        gla_out_ref[c, h] = S

    o_a = _gla_tile(proj(OFF_GQ, GLA_W), proj(OFF_GK, GLA_W), proj(OFF_GV, GLA_W), log_a,
                    tq, nb, state_in, state_out, chained=False)
    o_a = _gla_gate(o_a, proj(OFF_GR, GLA_W), gng_ref[...])
    y_a = _dot(_bf(o_a), wbg_ref[...])

    rc = jnp.concatenate([rc_ref[...]] * nb, axis=0)
    rdn = jnp.concatenate([rdn_ref[...]] * nb, axis=0)
    rup = jnp.concatenate([rup_ref[...]] * nb, axis=0)
    q = _rope(proj(OFF_SQ, SWA_QW), rc, rdn, rup) * (SWA_HD ** -0.5)
    k = _rope(proj(OFF_SK, SWA_KW), rc, rdn, rup)
    v = proj(OFF_SV, SWA_KW)
    qs = _swa_split_q(q)
    slabs = [[None] * nb for _ in range(SWA_QW // LANES)]
    for bi in range(nb):
        rs = slice(tq * bi, tq * (bi + 1))
        k_all = jnp.concatenate([swak_in_ref[bi], k[rs]], axis=0)
        v_all = jnp.concatenate([swav_in_ref[bi], v[rs]], axis=0)
        swak_out_ref[bi] = k_all[tq:tq + WINDOW]
        swav_out_ref[bi] = v_all[tq:tq + WINDOW]
        kk = _swa_dup_k(k_all)
        vv = _swa_place_v(v_all)
        for g in range(SWA_KV_HEADS):
            qrows = jnp.concatenate([qs[2 * g][0][rs], qs[2 * g][1][rs],
                                     qs[2 * g + 1][0][rs], qs[2 * g + 1][1][rs]], axis=0)
            o0, o1 = _swa_block(qrows, kk[g], vv[g][0], vv[g][1], _sink_col(sinks_ref, g, tq), None, tq)
            slabs[2 * g][bi] = o0
            slabs[2 * g + 1][bi] = o1
    o_b = jnp.concatenate([jnp.concatenate(s, axis=0) for s in slabs], axis=1)
    y_b = _dot(_bf(o_b), wbs_ref[...])

    qm = proj(OFF_MQ, MEM_W) * (MEM_HD ** -0.5)
    o_c = jnp.concatenate([_mem_attention(qm[tq * bi:tq * (bi + 1)], _bf(mk_ref[bi]), _bf(mv_ref[bi]))
                           for bi in range(nb)], axis=0)
    y_c = _dot(_bf(o_c), wbm_ref[...])

    out = _merge_ln1(x, xb, y_a, y_b, y_c, w_in_ref, bgate_ref, wo_ref, ln_g_ref, ln_b_ref, alpha)
    out_ref[...] = out.reshape(nb, tq, D_MODEL)


def _conv_geglu(u, h0, h1, cw_ref, cb_ref):
    T = u.shape[0]
    pos = lax.broadcasted_iota(jnp.int32, (T, 1), 0)
    u1 = jnp.where(pos == 0, h1, pltpu.roll(u, 1, axis=0))
    u2 = jnp.where(pos == 0, h0, jnp.where(pos == 1, h1, pltpu.roll(u, 2, axis=0)))
    c = cb_ref[...] + u2 * cw_ref[0:1, :] + u1 * cw_ref[1:2, :] + u * cw_ref[2:3, :]
    return jax.nn.gelu(c[:, :D_FF], approximate=True) * c[:, D_FF:]


def _prompt_ffn_kernel(x_ref, wup_ref, cw_ref, cb_ref, wdn_ref, ln_g_ref, ln_b_ref,
                       out_ref, conv_out_ref, hist_scr, *, rows, alpha):
    t = pl.program_id(1)
    nt = pl.num_programs(1)

    @pl.when(t == 0)
    def _():
        hist_scr[...] = jnp.zeros_like(hist_scr)

    x = x_ref[0]
    u = _dot(_bf(x), wup_ref[...])
    h = _conv_geglu(u, hist_scr[0:1, :], hist_scr[1:2, :], cw_ref, cb_ref)
    hist_scr[...] = u[rows - (CONV_W - 1):rows, :]

    @pl.when(t == nt - 1)
    def _():
        conv_out_ref[0] = u[rows - (CONV_W - 1):rows, :]

    f = _dot(_bf(h), wdn_ref[...])
    out_ref[0] = _layer_norm(alpha * x + f, ln_g_ref[...], ln_b_ref[...])


def _sample_ffn_kernel(x_ref, hist_ref, wup_ref, cw_ref, cb_ref, wdn_ref, ln_g_ref, ln_b_ref,
                       out_ref, conv_out_ref, *, nb, tq, alpha):
    rows = nb * tq
    x = x_ref[...].reshape(rows, D_MODEL)
    u = _dot(_bf(x), wup_ref[...])
    hs = []
    for bi in range(nb):
        ub = u[tq * bi:tq * (bi + 1)]
        hist = hist_ref[bi]
        hs.append(_conv_geglu(ub, hist[0:1, :], hist[1:2, :], cw_ref, cb_ref))
        conv_out_ref[bi] = jnp.concatenate([hist, ub], axis=0)[tq:tq + CONV_W - 1]
    f = _dot(_bf(jnp.concatenate(hs, axis=0)), wdn_ref[...])
    out = _layer_norm(alpha * x + f, ln_g_ref[...], ln_b_ref[...])
    out_ref[...] = out.reshape(nb, tq, D_MODEL)


def _mem_kv_kernel(m_ref, w_ref, k_ref, v_ref):
    kv = _dot(_bf(m_ref[0]), w_ref[...])
    k_ref[0] = kv[:, :MEM_W]
    v_ref[0] = kv[:, MEM_W:]


def _const_spec(shape):
    nd = len(shape)
    return pl.BlockSpec(shape, lambda *_: (0,) * nd, pipeline_mode=pl.Buffered(1))


def _rope_tables(pos):
    half = ROPE_DIM // 2
    inv = ROPE_THETA ** (-jnp.arange(half, dtype=_F32) / half)
    ang = pos.astype(_F32)[:, None] * inv[None, :]
    cos, sin = jnp.cos(ang), jnp.sin(ang)
    T = pos.shape[0]
    ones = jnp.ones((T, SWA_HD - ROPE_DIM), _F32)
    zeros = jnp.zeros((T, SWA_HD - ROPE_DIM), _F32)
    zh = jnp.zeros((T, half), _F32)
    c = jnp.concatenate([cos, cos, ones], axis=1)
    dn = jnp.concatenate([-sin, zh, zeros], axis=1)
    up = jnp.concatenate([zh, sin, zeros], axis=1)
    rep = LANES // SWA_HD
    return tuple(jnp.concatenate([a] * rep, axis=1) for a in (c, dn, up))


def _pack_w_in(w_in):
    n_gla = 4 * GLA_W
    ga = w_in[:, n_gla:n_gla + GLA_RANK]
    rest = w_in[:, n_gla + GLA_RANK:]
    pad = jnp.zeros((D_MODEL, LANES - GLA_RANK), w_in.dtype)
    return _bf(jnp.concatenate([w_in[:, :n_gla], rest, ga, pad], axis=1))


def _mixer_weight_specs():
    return [
        _const_spec((D_MODEL, IN_COLS_PACKED)),
        _const_spec((LANES, GLA_W)),
        _const_spec((1, GLA_W)),
        _const_spec((1, GLA_DV)),
        _const_spec((GLA_W, D_MODEL)),
        _const_spec((SWA_QW, D_MODEL)),
        _const_spec((MEM_W, D_MODEL)),
        _const_spec((1, GATE_W)),
        _const_spec((D_MODEL, D_MODEL)),
        _const_spec((1, D_MODEL)),
        _const_spec((1, D_MODEL)),
    ]


def _ffn_weight_specs():
    return [
        _const_spec((D_MODEL, 2 * D_FF)),
        _const_spec((CONV_W, 2 * D_FF)),
        _const_spec((1, 2 * D_FF)),
        _const_spec((D_FF, D_MODEL)),
        _const_spec((1, D_MODEL)),
        _const_spec((1, D_MODEL)),
    ]


_SMEM_SPEC = pl.BlockSpec(memory_space=pltpu.SMEM)


def _params(sem):
    return pltpu.CompilerParams(dimension_semantics=sem, vmem_limit_bytes=V7X_VMEM_LIMIT)


def _prompt_layer(x, mem, sinks, mixer_w, ffn_w, w_mem_kv, alpha, rows):
    B, T, _ = x.shape
    nt = T // rows
    assert T % rows == 0 and rows % CHUNK == 0 and rows >= WINDOW
    f32 = jnp.float32
    mk, mv = pl.pallas_call(
        _mem_kv_kernel,
        grid=(B,),
        in_specs=[pl.BlockSpec((1, MEM_TOKENS, D_MODEL), lambda b: (b, 0, 0)),
                  _const_spec((D_MODEL, 2 * MEM_W))],
        out_specs=[pl.BlockSpec((1, MEM_TOKENS, MEM_W), lambda b: (b, 0, 0))] * 2,
        out_shape=[jax.ShapeDtypeStruct((B, MEM_TOKENS, MEM_W), f32)] * 2,
        compiler_params=_params(("arbitrary",)),
        name="mem_kv",
    )(mem, w_mem_kv)

    rope = _rope_tables(jnp.arange(T, dtype=jnp.int32))
    rope_spec = pl.BlockSpec((rows, LANES), lambda b, t: (t, 0))
    per_batch = lambda *blk: pl.BlockSpec((1,) + blk, lambda b, t: (b,) + (0,) * len(blk))
    x1, gla, swak, swav = pl.pallas_call(
        functools.partial(_prompt_mixer_kernel, rows=rows, alpha=alpha),
        grid=(B, nt),
        in_specs=[_SMEM_SPEC,
                  pl.BlockSpec((1, rows, D_MODEL), lambda b, t: (b, t, 0)),
                  rope_spec, rope_spec, rope_spec,
                  per_batch(MEM_TOKENS, MEM_W), per_batch(MEM_TOKENS, MEM_W)] + _mixer_weight_specs(),
        out_specs=[pl.BlockSpec((1, rows, D_MODEL), lambda b, t: (b, t, 0)),
                   per_batch(GLA_HEADS, GLA_DK, GLA_DV),
                   per_batch(WINDOW, SWA_KW), per_batch(WINDOW, SWA_KW)],
        out_shape=[jax.ShapeDtypeStruct((B, T, D_MODEL), f32),
                   jax.ShapeDtypeStruct((B, GLA_HEADS, GLA_DK, GLA_DV), f32),
                   jax.ShapeDtypeStruct((B, WINDOW, SWA_KW), f32),
                   jax.ShapeDtypeStruct((B, WINDOW, SWA_KW), f32)],
        scratch_shapes=[pltpu.VMEM((GLA_HEADS, GLA_DK, GLA_DV), f32),
                        pltpu.VMEM((SWA_KV_HEADS, WINDOW + rows, LANES), _BF),
                        pltpu.VMEM((2 * SWA_KV_HEADS, WINDOW + rows, LANES), _BF)],
        compiler_params=_params(("arbitrary", "arbitrary")),
        name="prompt_mixer",
    )(sinks, x, *rope, mk, mv, *mixer_w)

    y, conv = pl.pallas_call(
        functools.partial(_prompt_ffn_kernel, rows=rows, alpha=alpha),
        grid=(B, nt),
        in_specs=[pl.BlockSpec((1, rows, D_MODEL), lambda b, t: (b, t, 0))] + _ffn_weight_specs(),
        out_specs=[pl.BlockSpec((1, rows, D_MODEL), lambda b, t: (b, t, 0)),
                   per_batch(CONV_W - 1, 2 * D_FF)],
        out_shape=[jax.ShapeDtypeStruct((B, T, D_MODEL), f32),
                   jax.ShapeDtypeStruct((B, CONV_W - 1, 2 * D_FF), f32)],
        scratch_shapes=[pltpu.VMEM((CONV_W - 1, 2 * D_FF), f32)],
        compiler_params=_params(("arbitrary", "arbitrary")),
        name="prompt_ffn",
    )(x1, *ffn_w)
    return y, gla, swak, swav, mk, mv, conv


def _sample_layer(x, gla0, swak0, swav0, memk, memv, conv0, sinks, mixer_w, ffn_w, alpha, nb):
    B, tq, _ = x.shape
    assert B % nb == 0 and tq % 16 == 0
    f32 = jnp.float32
    rope = _rope_tables(PAST_LEN + jnp.arange(tq, dtype=jnp.int32))
    rope_spec = pl.BlockSpec((tq, LANES), lambda i: (0, 0))
    blk = lambda *s: pl.BlockSpec((nb,) + s, lambda i: (i,) + (0,) * len(s))
    x1, gla, swak, swav = pl.pallas_call(
        functools.partial(_sample_mixer_kernel, nb=nb, tq=tq, alpha=alpha),
        grid=(B // nb,),
        in_specs=[_SMEM_SPEC, blk(tq, D_MODEL), rope_spec, rope_spec, rope_spec,
                  blk(MEM_TOKENS, MEM_W), blk(MEM_TOKENS, MEM_W),
                  blk(GLA_HEADS, GLA_DK, GLA_DV), blk(WINDOW, SWA_KW), blk(WINDOW, SWA_KW)]
                 + _mixer_weight_specs(),
        out_specs=[blk(tq, D_MODEL), blk(GLA_HEADS, GLA_DK, GLA_DV), blk(WINDOW, SWA_KW), blk(WINDOW, SWA_KW)],
        out_shape=[jax.ShapeDtypeStruct((B, tq, D_MODEL), f32),
                   jax.ShapeDtypeStruct((B, GLA_HEADS, GLA_DK, GLA_DV), f32),
                   jax.ShapeDtypeStruct((B, WINDOW, SWA_KW), f32),
                   jax.ShapeDtypeStruct((B, WINDOW, SWA_KW), f32)],
        compiler_params=_params(("arbitrary",)),
        name="sample_mixer",
    )(sinks, x, *rope, memk, memv, gla0, swak0, swav0, *mixer_w)

    y, conv = pl.pallas_call(
        functools.partial(_sample_ffn_kernel, nb=nb, tq=tq, alpha=alpha),
        grid=(B // nb,),
        in_specs=[blk(tq, D_MODEL), blk(CONV_W - 1, 2 * D_FF)] + _ffn_weight_specs(),
        out_specs=[blk(tq, D_MODEL), blk(CONV_W - 1, 2 * D_FF)],
        out_shape=[jax.ShapeDtypeStruct((B, tq, D_MODEL), f32),
                   jax.ShapeDtypeStruct((B, CONV_W - 1, 2 * D_FF), f32)],
        compiler_params=_params(("arbitrary",)),
        name="sample_ffn",
    )(x1, conv0, *ffn_w)
    return y, gla, swak, swav, conv


def kernel(x_prompt, x_sample, cache_swa_k, cache_swa_v, state_gla, cache_mem_k, cache_mem_v, cache_ffn_conv, mem_prompt, ln1_g, ln1_b, ln2_g, ln2_b, w_in, b_gate, w_gla_a2, b_gla_a, gla_norm_g, swa_sinks, w_mem_kv, w_br_gla, w_br_swa, w_br_mem, w_o, w_up, conv_w, conv_b, w_down, *, prompt_rows=256, sample_batches=8):
    depth = w_in.shape[0]
    alpha = float((2 * depth) ** 0.25)
    Bp = x_prompt.shape[0]
    Bs = x_sample.shape[0]
    hp, hs = x_prompt, x_sample
    outs = [[] for _ in range(10)]
    row = lambda a: a.reshape(1, -1)
    for l in range(depth):
        w2 = jnp.concatenate([w_gla_a2[l], jnp.zeros((LANES - GLA_RANK, GLA_W), w_gla_a2.dtype)], axis=0)
        mixer_w = (_pack_w_in(w_in[l]), _bf(w2), row(b_gla_a[l]), row(gla_norm_g[l]),
                   _bf(w_br_gla[l]), _bf(w_br_swa[l]), _bf(w_br_mem[l]), row(b_gate[l]), _bf(w_o[l]),
                   row(ln1_g[l]), row(ln1_b[l]))
        ffn_w = (_bf(w_up[l]), conv_w[l], row(conv_b[l]), _bf(w_down[l]), row(ln2_g[l]), row(ln2_b[l]))
        sinks = swa_sinks[l]
        hp, g_p, k_p, v_p, mk_p, mv_p, c_p = _prompt_layer(
            hp, mem_prompt, sinks, mixer_w, ffn_w, _bf(w_mem_kv[l]), alpha, prompt_rows)
        hs, g_s, k_s, v_s, c_s = _sample_layer(
            hs, state_gla[l],
            cache_swa_k[l].reshape(Bs, WINDOW, SWA_KW), cache_swa_v[l].reshape(Bs, WINDOW, SWA_KW),
            cache_mem_k[l].reshape(Bs, MEM_TOKENS, MEM_W), cache_mem_v[l].reshape(Bs, MEM_TOKENS, MEM_W),
            cache_ffn_conv[l], sinks, mixer_w, ffn_w, alpha, sample_batches)
        kv5 = lambda a, b: a.reshape(b, WINDOW, SWA_KV_HEADS, SWA_HD)
        m5 = lambda a: a.reshape(Bp, MEM_TOKENS, MEM_HEADS, MEM_HD)
        for lst, val in zip(outs, (kv5(k_p, Bp), kv5(v_p, Bp), g_p, m5(mk_p), m5(mv_p), c_p,
                                   kv5(k_s, Bs), kv5(v_s, Bs), g_s, c_s)):
            lst.append(val)
    return (hp, hs) + tuple(jnp.stack(o) for o in outs)
```

```python
import functools

import jax
import jax.numpy as jnp
import numpy as np
from jax import lax
from jax.experimental import pallas as pl
from jax.experimental.pallas import tpu as pltpu

D_MODEL = 1024
CHUNK = 64
GLA_HEADS = 4
GLA_DK = 128
GLA_DV = 128
GLA_RANK = 16
GLA_TAU = 16.0
SWA_HEADS = 8
SWA_KV_HEADS = 2
SWA_HD = 64
WINDOW = 128
ROPE_DIM = 16
ROPE_THETA = 500000.0
MEM_TOKENS = 256
MEM_HEADS = 4
MEM_HD = 128
D_FF = 2816
CONV_W = 3
N_BRANCH = 3
PAST_LEN = 2048
LN_EPS = 1e-5

LANES = 128
SUBLANES = 8
V7X_VMEM_LIMIT = 56 * 1024 * 1024
PROMPT_ROWS = 256
SAMPLE_ROWS = 256

GLA_W = GLA_HEADS * GLA_DK
SWA_QW = SWA_HEADS * SWA_HD
SWA_KW = SWA_KV_HEADS * SWA_HD
MEM_W = MEM_HEADS * MEM_HD
GATE_W = N_BRANCH * D_MODEL
SWA_PAIR = 2 * CHUNK

OFF_GQ = 0
OFF_GK = OFF_GQ + GLA_W
OFF_GV = OFF_GK + GLA_W
OFF_GR = OFF_GV + GLA_W
IN_A_COLS = OFF_GR + GLA_W
OFF_SQ = 0
OFF_SK = OFF_SQ + SWA_QW
OFF_SV = OFF_SK + SWA_KW
OFF_MQ = OFF_SV + SWA_KW
OFF_GL = OFF_MQ + MEM_W
IN_B_COLS = OFF_GL + GATE_W

_BF = jnp.bfloat16
_F32 = jnp.float32


def _bf(x):
    return x.astype(_BF)


def _dot(a, b):
    return jnp.dot(a, b, preferred_element_type=_F32)


def _dot_nt(a, b):
    return lax.dot_general(a, b, (((1,), (1,)), ((), ())), preferred_element_type=_F32)


def _layer_norm(h, g, b):
    mu = jnp.mean(h, axis=-1, keepdims=True)
    d = h - mu
    var = jnp.mean(d * d, axis=-1, keepdims=True)
    return d * lax.rsqrt(var + LN_EPS) * g + b


def _log_sigmoid(x):
    return -(jnp.maximum(-x, 0.0) + jnp.log(1.0 + jnp.exp(-jnp.abs(x))))


def _rope(a, c, s_dn, s_up):
    outs = []
    for j in range(a.shape[1] // LANES):
        slab = a[:, LANES * j:LANES * (j + 1)]
        fwd = pltpu.roll(slab, LANES - ROPE_DIM // 2, axis=1)
        bwd = pltpu.roll(slab, ROPE_DIM // 2, axis=1)
        outs.append(slab * c + fwd * s_dn + bwd * s_up)
    return outs[0] if len(outs) == 1 else jnp.concatenate(outs, axis=1)


def _gla_tile(q, k, v, log_a, chunk, nchunk, state_in, state_out, chained):
    R = chunk * nchunk
    row = lax.broadcasted_iota(jnp.int32, (R, R), 0)
    col = lax.broadcasted_iota(jnp.int32, (R, R), 1)
    tril = (row // chunk == col // chunk) & (col <= row)
    ltri = jnp.where(tril, 1.0, 0.0).astype(_BF)
    hi = _bf(log_a)
    lo = _bf(log_a - hi.astype(_F32))
    b = _dot(ltri, hi) + _dot(ltri, lo)
    b_last = [b[chunk * (c + 1) - 1:chunk * (c + 1), :] for c in range(nchunk)]
    bl_rows = jnp.concatenate([jnp.broadcast_to(bl, (chunk, GLA_W)) for bl in b_last], axis=0) \
        if nchunk > 1 else jnp.broadcast_to(b_last[0], (chunk, GLA_W))
    q_dec = _bf(q * (GLA_DK ** -0.5) * jnp.exp(b))
    k_inv = _bf(k * jnp.exp(-b))
    k_tail = _bf(k * jnp.exp(bl_rows - b))
    vb = _bf(v)
    outs = []
    for h in range(GLA_HEADS):
        hs = slice(GLA_DK * h, GLA_DK * (h + 1))
        att = jnp.where(tril, _dot_nt(q_dec[:, hs], k_inv[:, hs]), 0.0)
        o_intra = _dot(_bf(att), vb[:, hs])
        o_inter = []
        S = None
        for c in range(nchunk):
            rs = slice(chunk * c, chunk * (c + 1))
            if c == 0 or not chained:
                S = state_in(c, h)
            o_inter.append(_dot(q_dec[rs, hs], _bf(S)))
            decay = jnp.exp(b_last[c][:, hs])
            dcol = jnp.transpose(jnp.broadcast_to(decay, (GLA_DV, GLA_DK)))
            ds = _dot(jnp.transpose(k_tail[rs, hs].astype(_F32)).astype(_BF), vb[rs, hs])
            S = S * dcol + ds
            if c == nchunk - 1 or not chained:
                state_out(c, h, S)
        o_inter = jnp.concatenate(o_inter, axis=0) if nchunk > 1 else o_inter[0]
        outs.append(o_intra + o_inter)
    return jnp.concatenate(outs, axis=1)


def _gla_gate(o, gr, g):
    outs = []
    for h in range(GLA_HEADS):
        hs = slice(GLA_DV * h, GLA_DV * (h + 1))
        oh = o[:, hs]
        ms = jnp.mean(oh * oh, axis=-1, keepdims=True)
        grh = gr[:, hs]
        outs.append(oh * lax.rsqrt(ms + LN_EPS) * g * (grh * jax.nn.sigmoid(grh)))
    return jnp.concatenate(outs, axis=1)


def _lane_half_masks(shape):
    lane = lax.broadcasted_iota(jnp.int32, shape, 1)
    return lane < SWA_HD, lane >= SWA_HD


def _swa_split_q(q):
    lo_m, hi_m = _lane_half_masks((q.shape[0], LANES))
    out = []
    for j in range(SWA_QW // LANES):
        slab = q[:, LANES * j:LANES * (j + 1)]
        out.append((_bf(jnp.where(lo_m, slab, 0.0)), _bf(jnp.where(hi_m, slab, 0.0))))
    return out


def _swa_dup_k(k):
    lo_m, _ = _lane_half_masks(k.shape)
    kr = pltpu.roll(k, SWA_HD, axis=1)
    return [_bf(jnp.where(lo_m, k, kr)), _bf(jnp.where(lo_m, kr, k))]


def _swa_place_v(v):
    lo_m, hi_m = _lane_half_masks(v.shape)
    vr = pltpu.roll(v, SWA_HD, axis=1)
    return [(_bf(jnp.where(lo_m, v, 0.0)), _bf(jnp.where(hi_m, vr, 0.0))),
            (_bf(jnp.where(lo_m, vr, 0.0)), _bf(jnp.where(hi_m, v, 0.0)))]


def _swa_block(qrows, kk, vlo, vhi, sink_col, lower, upper, tq):
    s = _dot_nt(qrows, kk)
    if lower is not None:
        kcol = lax.broadcasted_iota(jnp.int32, s.shape, 1)
        s = jnp.where(kcol >= lower, jnp.where(kcol <= upper, s, -jnp.inf), -jnp.inf)
    m = jnp.maximum(jnp.max(s, axis=-1, keepdims=True), sink_col)
    p = jnp.exp(s - m)
    den = jnp.sum(p, axis=-1, keepdims=True) + jnp.exp(sink_col - m)
    p = _bf(p / den)
    o0 = _dot(p[0:tq], vlo) + _dot(p[tq:2 * tq], vhi)
    o1 = _dot(p[2 * tq:3 * tq], vlo) + _dot(p[3 * tq:4 * tq], vhi)
    return o0, o1


def _sink_col(sinks_ref, g, tq):
    r = lax.broadcasted_iota(jnp.int32, (4 * tq, 1), 0)
    base = 4 * g
    return jnp.where(r < tq, sinks_ref[base],
                     jnp.where(r < 2 * tq, sinks_ref[base + 1],
                               jnp.where(r < 3 * tq, sinks_ref[base + 2], sinks_ref[base + 3])))


def _mem_head(ref, h):
    return _bf(ref[pl.ds(h, MEM_TOKENS, stride=MEM_HEADS), :])


def _mem_attention(q, mk_ref, mv_ref):
    qb = _bf(q)
    outs = []
    for h in range(MEM_HEADS):
        s = _dot_nt(qb[:, MEM_HD * h:MEM_HD * (h + 1)], _mem_head(mk_ref, h))
        m = jnp.max(s, axis=-1, keepdims=True)
        p = jnp.exp(s - m)
        p = p / jnp.sum(p, axis=-1, keepdims=True)
        outs.append(_dot(_bf(p), _mem_head(mv_ref, h)))
    return jnp.concatenate(outs, axis=1)


def _merge_ln1(x, xb, y_a, y_b, y_c, w_inb_ref, bgate_ref, wo_ref, g_ref, b_ref, alpha):
    gates = []
    for i in range(N_BRANCH):
        gl = _dot(xb, w_inb_ref[:, OFF_GL + D_MODEL * i:OFF_GL + D_MODEL * (i + 1)])
        gates.append(jax.nn.sigmoid(gl + bgate_ref[:, D_MODEL * i:D_MODEL * (i + 1)]))
    mix = _dot(_bf(gates[0] * y_a + gates[1] * y_b + gates[2] * y_c), wo_ref[...])
    return _layer_norm(alpha * x + mix, g_ref[...], b_ref[...])


def _forget_log(xb, w_ga_ref, w2_ref, ba_ref):
    ga = _dot(xb, w_ga_ref[...])
    return _log_sigmoid(_dot(_bf(ga), w2_ref[...]) + ba_ref[...]) * (1.0 / GLA_TAU)


def _prompt_mixer_kernel(sinks_ref, x_ref, rc_ref, rdn_ref, rup_ref, mk_ref, mv_ref, w_ina_ref, w_ga_ref,
                         w_inb_ref, w2_ref, ba_ref, gng_ref, wbg_ref, wbs_ref, wbm_ref, bgate_ref, wo_ref,
                         ln_g_ref, ln_b_ref,
                         out_ref, gla_out_ref, swak_out_ref, swav_out_ref,
                         s_scr, kk_scr, vv_scr, *, rows, alpha):
    t = pl.program_id(1)
    nchunk = rows // CHUNK

    @pl.when(t == 0)
    def _():
        s_scr[...] = jnp.zeros_like(s_scr)
        kk_scr[:, 0:WINDOW, :] = jnp.zeros((SWA_KV_HEADS, WINDOW, LANES), _BF)
        vv_scr[:, 0:WINDOW, :] = jnp.zeros((2 * SWA_KV_HEADS, WINDOW, LANES), _BF)

    x = x_ref[0]
    xb = _bf(x)

    log_a = _forget_log(xb, w_ga_ref, w2_ref, ba_ref)

    def proj_a(off):
        return _dot(xb, w_ina_ref[:, off:off + GLA_W])

    def state_in(c, h):
        return s_scr[h]

    def state_out(c, h, S):
        s_scr[h] = S
        gla_out_ref[0, h] = S

    o_a = _gla_tile(proj_a(OFF_GQ), proj_a(OFF_GK), proj_a(OFF_GV), log_a,
                    CHUNK, nchunk, state_in, state_out, chained=True)
    o_a = _gla_gate(o_a, proj_a(OFF_GR), gng_ref[...])
    y_a = _dot(_bf(o_a), wbg_ref[...])

    rc, rdn, rup = rc_ref[...], rdn_ref[...], rup_ref[...]
    q = _rope(_dot(xb, w_inb_ref[:, OFF_SQ:OFF_SQ + SWA_QW]), rc, rdn, rup) * (SWA_HD ** -0.5)
    k = _rope(_dot(xb, w_inb_ref[:, OFF_SK:OFF_SK + SWA_KW]), rc, rdn, rup)
    v = _dot(xb, w_inb_ref[:, OFF_SV:OFF_SV + SWA_KW])
    swak_out_ref[0] = k[rows - WINDOW:rows, :]
    swav_out_ref[0] = v[rows - WINDOW:rows, :]
    kk = _swa_dup_k(k)
    vv = _swa_place_v(v)
    for g in range(SWA_KV_HEADS):
        kk_scr[g, WINDOW:WINDOW + rows, :] = kk[g]
        vv_scr[2 * g, WINDOW:WINDOW + rows, :] = vv[g][0]
        vv_scr[2 * g + 1, WINDOW:WINDOW + rows, :] = vv[g][1]
    qs = _swa_split_q(q)
    nk = WINDOW + SWA_PAIR
    npair = rows // SWA_PAIR
    qchunk = (lax.broadcasted_iota(jnp.int32, (4 * SWA_PAIR, 1), 0) % SWA_PAIR) // CHUNK
    upper = CHUNK * qchunk + (WINDOW + CHUNK - 1)
    slabs = [[None] * npair for _ in range(SWA_QW // LANES)]
    for g in range(SWA_KV_HEADS):
        sink_col = _sink_col(sinks_ref, g, SWA_PAIR)
        for p in range(npair):
            rs = slice(SWA_PAIR * p, SWA_PAIR * (p + 1))
            ks = slice(SWA_PAIR * p, SWA_PAIR * p + nk)
            qrows = jnp.concatenate([qs[2 * g][0][rs], qs[2 * g][1][rs],
                                     qs[2 * g + 1][0][rs], qs[2 * g + 1][1][rs]], axis=0)
            lower = jnp.maximum(CHUNK * qchunk, WINDOW - (t * rows + SWA_PAIR * p))
            o0, o1 = _swa_block(qrows, kk_scr[g, ks, :], vv_scr[2 * g, ks, :], vv_scr[2 * g + 1, ks, :],
                                sink_col, lower, upper, SWA_PAIR)
            slabs[2 * g][p] = o0
            slabs[2 * g + 1][p] = o1
    o_b = jnp.concatenate([jnp.concatenate(s, axis=0) if npair > 1 else s[0] for s in slabs], axis=1)
    y_b = _dot(_bf(o_b), wbs_ref[...])
    for g in range(SWA_KV_HEADS):
        kk_scr[g, 0:WINDOW, :] = kk[g][rows - WINDOW:rows]
        vv_scr[2 * g, 0:WINDOW, :] = vv[g][0][rows - WINDOW:rows]
        vv_scr[2 * g + 1, 0:WINDOW, :] = vv[g][1][rows - WINDOW:rows]

    qm = _dot(xb, w_inb_ref[:, OFF_MQ:OFF_MQ + MEM_W]) * (MEM_HD ** -0.5)
    o_c = _mem_attention(qm, mk_ref.at[0], mv_ref.at[0])
    y_c = _dot(_bf(o_c), wbm_ref[...])

    out_ref[0] = _merge_ln1(x, xb, y_a, y_b, y_c, w_inb_ref, bgate_ref, wo_ref, ln_g_ref, ln_b_ref, alpha)


def _sample_mixer_kernel(sinks_ref, x_ref, rc_ref, rdn_ref, rup_ref, mk_ref, mv_ref, gla_in_ref,
                         swak_in_ref, swav_in_ref, w_ina_ref, w_ga_ref, w_inb_ref, w2_ref,
                         ba_ref, gng_ref, wbg_ref, wbs_ref, wbm_ref, bgate_ref, wo_ref, ln_g_ref, ln_b_ref,
                         out_ref, gla_out_ref, swak_out_ref, swav_out_ref, *, nb, tq, alpha):
    rows = nb * tq
    x = x_ref[...].reshape(rows, D_MODEL)
    xb = _bf(x)

    log_a = _forget_log(xb, w_ga_ref, w2_ref, ba_ref)

    def proj_a(off):
        return _dot(xb, w_ina_ref[:, off:off + GLA_W])

    def state_in(c, h):
        return gla_in_ref[c, h]

    def state_out(c, h, S):
        gla_out_ref[c, h] = S

    o_a = _gla_tile(proj_a(OFF_GQ), proj_a(OFF_GK), proj_a(OFF_GV), log_a,
                    tq, nb, state_in, state_out, chained=False)
    o_a = _gla_gate(o_a, proj_a(OFF_GR), gng_ref[...])
    y_a = _dot(_bf(o_a), wbg_ref[...])

    rc = jnp.concatenate([rc_ref[...]] * nb, axis=0)
    rdn = jnp.concatenate([rdn_ref[...]] * nb, axis=0)
    rup = jnp.concatenate([rup_ref[...]] * nb, axis=0)
    q = _rope(_dot(xb, w_inb_ref[:, OFF_SQ:OFF_SQ + SWA_QW]), rc, rdn, rup) * (SWA_HD ** -0.5)
    k = _rope(_dot(xb, w_inb_ref[:, OFF_SK:OFF_SK + SWA_KW]), rc, rdn, rup)
    v = _dot(xb, w_inb_ref[:, OFF_SV:OFF_SV + SWA_KW])
    qs = _swa_split_q(q)
    slabs = [[None] * nb for _ in range(SWA_QW // LANES)]
    for bi in range(nb):
        rs = slice(tq * bi, tq * (bi + 1))
        k_all = jnp.concatenate([swak_in_ref[bi], k[rs]], axis=0)
        v_all = jnp.concatenate([swav_in_ref[bi], v[rs]], axis=0)
        swak_out_ref[bi] = k_all[tq:tq + WINDOW]
        swav_out_ref[bi] = v_all[tq:tq + WINDOW]
        kk = _swa_dup_k(k_all)
        vv = _swa_place_v(v_all)
        for g in range(SWA_KV_HEADS):
            qrows = jnp.concatenate([qs[2 * g][0][rs], qs[2 * g][1][rs],
                                     qs[2 * g + 1][0][rs], qs[2 * g + 1][1][rs]], axis=0)
            o0, o1 = _swa_block(qrows, kk[g], vv[g][0], vv[g][1], _sink_col(sinks_ref, g, tq),
                                None, None, tq)
            slabs[2 * g][bi] = o0
            slabs[2 * g + 1][bi] = o1
    o_b = jnp.concatenate([jnp.concatenate(s, axis=0) if nb > 1 else s[0] for s in slabs], axis=1)
    y_b = _dot(_bf(o_b), wbs_ref[...])

    qm = _dot(xb, w_inb_ref[:, OFF_MQ:OFF_MQ + MEM_W]) * (MEM_HD ** -0.5)
    o_c = [_mem_attention(qm[tq * bi:tq * (bi + 1)], mk_ref.at[bi], mv_ref.at[bi]) for bi in range(nb)]
    o_c = jnp.concatenate(o_c, axis=0) if nb > 1 else o_c[0]
    y_c = _dot(_bf(o_c), wbm_ref[...])

    out = _merge_ln1(x, xb, y_a, y_b, y_c, w_inb_ref, bgate_ref, wo_ref, ln_g_ref, ln_b_ref, alpha)
    out_ref[...] = out.reshape(nb, tq, D_MODEL)


def _conv_geglu(u_ref, T, cw_ref, cb_ref):
    u0 = u_ref[SUBLANES:SUBLANES + T, :]
    u1 = u_ref[SUBLANES - 1:SUBLANES - 1 + T, :]
    u2 = u_ref[SUBLANES - 2:SUBLANES - 2 + T, :]
    c = cb_ref[...] + u2 * cw_ref[0:1, :] + u1 * cw_ref[1:2, :] + u0 * cw_ref[2:3, :]
    return jax.nn.gelu(c[:, :D_FF], approximate=True) * c[:, D_FF:]


def _prompt_ffn_kernel(x_ref, wup_ref, cw_ref, cb_ref, wdn_ref, ln_g_ref, ln_b_ref,
                       out_ref, conv_out_ref, u_scr, *, rows, alpha):
    @pl.when(pl.program_id(1) == 0)
    def _():
        u_scr[0:SUBLANES, :] = jnp.zeros((SUBLANES, 2 * D_FF), _F32)

    x = x_ref[0]
    u_scr[SUBLANES:SUBLANES + rows, :] = _dot(_bf(x), wup_ref[...])
    h = _conv_geglu(u_scr, rows, cw_ref, cb_ref)
    tail = u_scr[SUBLANES + rows - (CONV_W - 1):SUBLANES + rows, :]
    conv_out_ref[0] = tail
    u_scr[SUBLANES - (CONV_W - 1):SUBLANES, :] = tail
    f = _dot(_bf(h), wdn_ref[...])
    out_ref[0] = _layer_norm(alpha * x + f, ln_g_ref[...], ln_b_ref[...])


def _sample_ffn_kernel(x_ref, hist_ref, wup_ref, cw_ref, cb_ref, wdn_ref, ln_g_ref, ln_b_ref,
                       out_ref, conv_out_ref, u_scr, *, nb, tq, alpha):
    rows = nb * tq
    x = x_ref[...].reshape(rows, D_MODEL)
    u = _dot(_bf(x), wup_ref[...])
    hs = []
    for bi in range(nb):
        u_scr[bi, SUBLANES - (CONV_W - 1):SUBLANES, :] = hist_ref[bi]
        u_scr[bi, SUBLANES:SUBLANES + tq, :] = u[tq * bi:tq * (bi + 1)]
        hs.append(_conv_geglu(u_scr.at[bi], tq, cw_ref, cb_ref))
        conv_out_ref[bi] = u_scr[bi, SUBLANES + tq - (CONV_W - 1):SUBLANES + tq, :]
    f = _dot(_bf(jnp.concatenate(hs, axis=0) if nb > 1 else hs[0]), wdn_ref[...])
    out = _layer_norm(alpha * x + f, ln_g_ref[...], ln_b_ref[...])
    out_ref[...] = out.reshape(nb, tq, D_MODEL)


def _mem_kv_kernel(m_ref, w_ref, k_ref, v_ref):
    kv = _dot(_bf(m_ref[0]), w_ref[...])
    for h in range(MEM_HEADS):
        k_ref[0, pl.ds(h, MEM_TOKENS, stride=MEM_HEADS), :] = kv[:, MEM_HD * h:MEM_HD * (h + 1)]
        v_ref[0, pl.ds(h, MEM_TOKENS, stride=MEM_HEADS), :] = kv[:, MEM_W + MEM_HD * h:MEM_W + MEM_HD * (h + 1)]


def _const_spec(shape):
    nd = len(shape)
    return pl.BlockSpec(shape, lambda *_: (0,) * nd, pipeline_mode=pl.Buffered(1))


def _rope_tables(pos):
    half = ROPE_DIM // 2
    inv = np.float32(ROPE_THETA) ** (-np.arange(half, dtype=np.float32) / np.float32(half))
    ang = pos.astype(np.float32)[:, None] * inv[None, :]
    cos, sin = np.cos(ang), np.sin(ang)
    T = pos.shape[0]
    ones = np.ones((T, SWA_HD - ROPE_DIM), np.float32)
    zeros = np.zeros((T, SWA_HD - ROPE_DIM), np.float32)
    zh = np.zeros((T, half), np.float32)
    c = np.concatenate([cos, cos, ones], axis=1)
    dn = np.concatenate([-sin, zh, zeros], axis=1)
    up = np.concatenate([zh, sin, zeros], axis=1)
    rep = LANES // SWA_HD
    return tuple(jnp.asarray(np.concatenate([a] * rep, axis=1), dtype=_F32) for a in (c, dn, up))


def _split_w_in(w_in):
    ga = w_in[:, IN_A_COLS:IN_A_COLS + GLA_RANK]
    ga = jnp.pad(ga, ((0, 0), (0, LANES - GLA_RANK)))
    return _bf(w_in[:, :IN_A_COLS]), _bf(ga), _bf(w_in[:, IN_A_COLS + GLA_RANK:])


def _mixer_weight_specs():
    return [
        _const_spec((D_MODEL, IN_A_COLS)),
        _const_spec((D_MODEL, LANES)),
        _const_spec((D_MODEL, IN_B_COLS)),
        _const_spec((LANES, GLA_W)),
        _const_spec((1, GLA_W)),
        _const_spec((1, GLA_DV)),
        _const_spec((GLA_W, D_MODEL)),
        _const_spec((SWA_QW, D_MODEL)),
        _const_spec((MEM_W, D_MODEL)),
        _const_spec((1, GATE_W)),
        _const_spec((D_MODEL, D_MODEL)),
        _const_spec((1, D_MODEL)),
        _const_spec((1, D_MODEL)),
    ]


def _ffn_weight_specs():
    return [
        _const_spec((D_MODEL, 2 * D_FF)),
        _const_spec((CONV_W, 2 * D_FF)),
        _const_spec((1, 2 * D_FF)),
        _const_spec((D_FF, D_MODEL)),
        _const_spec((1, D_MODEL)),
        _const_spec((1, D_MODEL)),
    ]


_SMEM_SPEC = pl.BlockSpec(memory_space=pltpu.SMEM)
_MEM_ROWS = MEM_TOKENS * MEM_HEADS


def _params(sem):
    return pltpu.CompilerParams(dimension_semantics=sem, vmem_limit_bytes=V7X_VMEM_LIMIT)


def _prompt_layer(x, mem, sinks, mixer_w, ffn_w, w_mem_kv, alpha):
    B, T, _ = x.shape
    rows = min(PROMPT_ROWS, T)
    nt = T // rows
    assert T % rows == 0 and rows % SWA_PAIR == 0 and rows >= WINDOW
    f32 = jnp.float32
    mk, mv = pl.pallas_call(
        _mem_kv_kernel,
        grid=(B,),
        in_specs=[pl.BlockSpec((1, MEM_TOKENS, D_MODEL), lambda b: (b, 0, 0)),
                  _const_spec((D_MODEL, 2 * MEM_W))],
        out_specs=[pl.BlockSpec((1, _MEM_ROWS, MEM_HD), lambda b: (b, 0, 0))] * 2,
        out_shape=[jax.ShapeDtypeStruct((B, _MEM_ROWS, MEM_HD), f32)] * 2,
        compiler_params=_params(("arbitrary",)),
        name="mem_kv",
    )(mem, w_mem_kv)

    rope = _rope_tables(np.arange(T))
    rope_spec = pl.BlockSpec((rows, LANES), lambda b, t: (t, 0))
    per_batch = lambda *blk: pl.BlockSpec((1,) + blk, lambda b, t: (b,) + (0,) * len(blk))
    x1, gla, swak, swav = pl.pallas_call(
        functools.partial(_prompt_mixer_kernel, rows=rows, alpha=alpha),
        grid=(B, nt),
        in_specs=[_SMEM_SPEC,
                  pl.BlockSpec((1, rows, D_MODEL), lambda b, t: (b, t, 0)),
                  rope_spec, rope_spec, rope_spec,
                  per_batch(_MEM_ROWS, MEM_HD), per_batch(_MEM_ROWS, MEM_HD)] + _mixer_weight_specs(),
        out_specs=[pl.BlockSpec((1, rows, D_MODEL), lambda b, t: (b, t, 0)),
                   per_batch(GLA_HEADS, GLA_DK, GLA_DV),
                   per_batch(WINDOW, SWA_KW), per_batch(WINDOW, SWA_KW)],
        out_shape=[jax.ShapeDtypeStruct((B, T, D_MODEL), f32),
                   jax.ShapeDtypeStruct((B, GLA_HEADS, GLA_DK, GLA_DV), f32),
                   jax.ShapeDtypeStruct((B, WINDOW, SWA_KW), f32),
                   jax.ShapeDtypeStruct((B, WINDOW, SWA_KW), f32)],
        scratch_shapes=[pltpu.VMEM((GLA_HEADS, GLA_DK, GLA_DV), f32),
                        pltpu.VMEM((SWA_KV_HEADS, WINDOW + rows, LANES), _BF),
                        pltpu.VMEM((2 * SWA_KV_HEADS, WINDOW + rows, LANES), _BF)],
        compiler_params=_params(("arbitrary", "arbitrary")),
        name="prompt_mixer",
    )(sinks, x, *rope, mk, mv, *mixer_w)

    y, conv = pl.pallas_call(
        functools.partial(_prompt_ffn_kernel, rows=rows, alpha=alpha),
        grid=(B, nt),
        in_specs=[pl.BlockSpec((1, rows, D_MODEL), lambda b, t: (b, t, 0))] + _ffn_weight_specs(),
        out_specs=[pl.BlockSpec((1, rows, D_MODEL), lambda b, t: (b, t, 0)),
                   per_batch(CONV_W - 1, 2 * D_FF)],
        out_shape=[jax.ShapeDtypeStruct((B, T, D_MODEL), f32),
                   jax.ShapeDtypeStruct((B, CONV_W - 1, 2 * D_FF), f32)],
        scratch_shapes=[pltpu.VMEM((SUBLANES + rows, 2 * D_FF), f32)],
        compiler_params=_params(("arbitrary", "arbitrary")),
        name="prompt_ffn",
    )(x1, *ffn_w)
    return y, gla, swak, swav, mk, mv, conv


def _sample_layer(x, gla0, swak0, swav0, memk, memv, conv0, sinks, mixer_w, ffn_w, alpha):
    B, tq, _ = x.shape
    nb = max(1, min(B, SAMPLE_ROWS // tq))
    assert B % nb == 0 and tq % 16 == 0 and tq >= CONV_W - 1
    f32 = jnp.float32
    rope = _rope_tables(PAST_LEN + np.arange(tq))
    rope_spec = pl.BlockSpec((tq, LANES), lambda i: (0, 0))
    blk = lambda *s: pl.BlockSpec((nb,) + s, lambda i: (i,) + (0,) * len(s))
    x1, gla, swak, swav = pl.pallas_call(
        functools.partial(_sample_mixer_kernel, nb=nb, tq=tq, alpha=alpha),
        grid=(B // nb,),
        in_specs=[_SMEM_SPEC, blk(tq, D_MODEL), rope_spec, rope_spec, rope_spec,
                  blk(_MEM_ROWS, MEM_HD), blk(_MEM_ROWS, MEM_HD),
                  blk(GLA_HEADS, GLA_DK, GLA_DV), blk(WINDOW, SWA_KW), blk(WINDOW, SWA_KW)]
                 + _mixer_weight_specs(),
        out_specs=[blk(tq, D_MODEL), blk(GLA_HEADS, GLA_DK, GLA_DV), blk(WINDOW, SWA_KW), blk(WINDOW, SWA_KW)],
        out_shape=[jax.ShapeDtypeStruct((B, tq, D_MODEL), f32),
                   jax.ShapeDtypeStruct((B, GLA_HEADS, GLA_DK, GLA_DV), f32),
                   jax.ShapeDtypeStruct((B, WINDOW, SWA_KW), f32),
                   jax.ShapeDtypeStruct((B, WINDOW, SWA_KW), f32)],
        compiler_params=_params(("arbitrary",)),
        name="sample_mixer",
    )(sinks, x, *rope, memk, memv, gla0, swak0, swav0, *mixer_w)

    y, conv = pl.pallas_call(
        functools.partial(_sample_ffn_kernel, nb=nb, tq=tq, alpha=alpha),
        grid=(B // nb,),
        in_specs=[blk(tq, D_MODEL), blk(CONV_W - 1, 2 * D_FF)] + _ffn_weight_specs(),
        out_specs=[blk(tq, D_MODEL), blk(CONV_W - 1, 2 * D_FF)],
        out_shape=[jax.ShapeDtypeStruct((B, tq, D_MODEL), f32),
                   jax.ShapeDtypeStruct((B, CONV_W - 1, 2 * D_FF), f32)],
        scratch_shapes=[pltpu.VMEM((nb, SUBLANES + tq, 2 * D_FF), f32)],
        compiler_params=_params(("arbitrary",)),
        name="sample_ffn",
    )(x1, conv0, *ffn_w)
    return y, gla, swak, swav, conv


def kernel(x_prompt, x_sample, cache_swa_k, cache_swa_v, state_gla, cache_mem_k, cache_mem_v, cache_ffn_conv, mem_prompt, ln1_g, ln1_b, ln2_g, ln2_b, w_in, b_gate, w_gla_a2, b_gla_a, gla_norm_g, swa_sinks, w_mem_kv, w_br_gla, w_br_swa, w_br_mem, w_o, w_up, conv_w, conv_b, w_down):
    depth = w_in.shape[0]
    alpha = float((2 * depth) ** 0.25)
    Bp = x_prompt.shape[0]
    Bs = x_sample.shape[0]
    hp, hs = x_prompt, x_sample
    outs = [[] for _ in range(10)]
    row = lambda a: a.reshape(1, -1)
    for l in range(depth):
        w2 = jnp.pad(w_gla_a2[l], ((0, LANES - GLA_RANK), (0, 0)))
        mixer_w = _split_w_in(w_in[l]) + (
            _bf(w2), row(b_gla_a[l]), row(gla_norm_g[l]),
            _bf(w_br_gla[l]), _bf(w_br_swa[l]), _bf(w_br_mem[l]), row(b_gate[l]), _bf(w_o[l]),
            row(ln1_g[l]), row(ln1_b[l]))
        ffn_w = (_bf(w_up[l]), conv_w[l], row(conv_b[l]), _bf(w_down[l]), row(ln2_g[l]), row(ln2_b[l]))
        sinks = swa_sinks[l]
        hp, g_p, k_p, v_p, mk_p, mv_p, c_p = _prompt_layer(
            hp, mem_prompt, sinks, mixer_w, ffn_w, _bf(w_mem_kv[l]), alpha)
        hs, g_s, k_s, v_s, c_s = _sample_layer(
            hs, state_gla[l],
            cache_swa_k[l].reshape(Bs, WINDOW, SWA_KW), cache_swa_v[l].reshape(Bs, WINDOW, SWA_KW),
            cache_mem_k[l].reshape(Bs, _MEM_ROWS, MEM_HD), cache_mem_v[l].reshape(Bs, _MEM_ROWS, MEM_HD),
            cache_ffn_conv[l], sinks, mixer_w, ffn_w, alpha)
        kv5 = lambda a, b: a.reshape(b, WINDOW, SWA_KV_HEADS, SWA_HD)
        m5 = lambda a: a.reshape(Bp, MEM_TOKENS, MEM_HEADS, MEM_HD)
        for lst, val in zip(outs, (kv5(k_p, Bp), kv5(v_p, Bp), g_p, m5(mk_p), m5(mv_p), c_p,
                                   kv5(k_s, Bs), kv5(v_s, Bs), g_s, c_s)):
            lst.append(val)
    return (hp, hs) + tuple(jnp.stack(o) for o in outs)
```

```python
import functools

import jax
import jax.numpy as jnp
import numpy as np
from jax import lax
from jax.experimental import pallas as pl
from jax.experimental.pallas import tpu as pltpu

D_MODEL = 1024
CHUNK = 64
GLA_HEADS = 4
GLA_DK = 128
GLA_DV = 128
GLA_RANK = 16
GLA_TAU = 16.0
SWA_HEADS = 8
SWA_KV_HEADS = 2
SWA_HD = 64
WINDOW = 128
ROPE_DIM = 16
ROPE_THETA = 500000.0
MEM_TOKENS = 256
MEM_HEADS = 4
MEM_HD = 128
D_FF = 2816
CONV_W = 3
N_BRANCH = 3
PAST_LEN = 2048
LN_EPS = 1e-5

LANES = 128
SUBLANES = 8
V7X_VMEM_LIMIT = 56 * 1024 * 1024
PROMPT_ROWS = 256
SAMPLE_ROWS = 256

GLA_W = GLA_HEADS * GLA_DK
SWA_QW = SWA_HEADS * SWA_HD
SWA_KW = SWA_KV_HEADS * SWA_HD
MEM_W = MEM_HEADS * MEM_HD
GATE_W = N_BRANCH * D_MODEL
SWA_PAIR = 2 * CHUNK

OFF_GQ = 0
OFF_GK = OFF_GQ + GLA_W
OFF_GV = OFF_GK + GLA_W
OFF_GR = OFF_GV + GLA_W
IN_A_COLS = OFF_GR + GLA_W
OFF_SQ = 0
OFF_SK = OFF_SQ + SWA_QW
OFF_SV = OFF_SK + SWA_KW
OFF_MQ = OFF_SV + SWA_KW
OFF_GL = OFF_MQ + MEM_W
IN_B_COLS = OFF_GL + GATE_W

_BF = jnp.bfloat16
_F32 = jnp.float32


def _bf(x):
    return x.astype(_BF)


def _dot(a, b):
    return jnp.dot(a, b, preferred_element_type=_F32)


def _dot_nt(a, b):
    return lax.dot_general(a, b, (((1,), (1,)), ((), ())), preferred_element_type=_F32)


def _layer_norm(h, g, b):
    mu = jnp.mean(h, axis=-1, keepdims=True)
    d = h - mu
    var = jnp.mean(d * d, axis=-1, keepdims=True)
    return d * lax.rsqrt(var + LN_EPS) * g + b


def _log_sigmoid(x):
    return -(jnp.maximum(-x, 0.0) + jnp.log(1.0 + jnp.exp(-jnp.abs(x))))


def _rope(a, c, s_dn, s_up):
    outs = []
    for j in range(a.shape[1] // LANES):
        slab = a[:, LANES * j:LANES * (j + 1)]
        fwd = pltpu.roll(slab, LANES - ROPE_DIM // 2, axis=1)
        bwd = pltpu.roll(slab, ROPE_DIM // 2, axis=1)
        outs.append(slab * c + fwd * s_dn + bwd * s_up)
    return outs[0] if len(outs) == 1 else jnp.concatenate(outs, axis=1)


def _gla_tile(q, k, v, log_a, chunk, nchunk, state_in, state_out, chained):
    R = chunk * nchunk
    row = lax.broadcasted_iota(jnp.int32, (R, R), 0)
    col = lax.broadcasted_iota(jnp.int32, (R, R), 1)
    tril = (row // chunk == col // chunk) & (col <= row)
    ltri = jnp.where(tril, 1.0, 0.0).astype(_BF)
    hi = _bf(log_a)
    lo = _bf(log_a - hi.astype(_F32))
    b = _dot(ltri, hi) + _dot(ltri, lo)
    b_last = [b[chunk * (c + 1) - 1:chunk * (c + 1), :] for c in range(nchunk)]
    bl_rows = jnp.concatenate([jnp.broadcast_to(bl, (chunk, GLA_W)) for bl in b_last], axis=0) \
        if nchunk > 1 else jnp.broadcast_to(b_last[0], (chunk, GLA_W))
    q_dec = _bf(q * (GLA_DK ** -0.5) * jnp.exp(b))
    k_inv = _bf(k * jnp.exp(-b))
    k_tail = _bf(k * jnp.exp(bl_rows - b))
    vb = _bf(v)
    outs = []
    for h in range(GLA_HEADS):
        hs = slice(GLA_DK * h, GLA_DK * (h + 1))
        att = jnp.where(tril, _dot_nt(q_dec[:, hs], k_inv[:, hs]), 0.0)
        o_intra = _dot(_bf(att), vb[:, hs])
        o_inter = []
        S = None
        for c in range(nchunk):
            rs = slice(chunk * c, chunk * (c + 1))
            if c == 0 or not chained:
                S = state_in(c, h)
            o_inter.append(_dot(q_dec[rs, hs], _bf(S)))
            decay = jnp.exp(b_last[c][:, hs])
            dcol = jnp.transpose(jnp.broadcast_to(decay, (GLA_DV, GLA_DK)))
            ds = _dot(jnp.transpose(k_tail[rs, hs].astype(_F32)).astype(_BF), vb[rs, hs])
            S = S * dcol + ds
            if c == nchunk - 1 or not chained:
                state_out(c, h, S)
        o_inter = jnp.concatenate(o_inter, axis=0) if nchunk > 1 else o_inter[0]
        outs.append(o_intra + o_inter)
    return jnp.concatenate(outs, axis=1)


def _gla_gate(o, gr, g):
    outs = []
    for h in range(GLA_HEADS):
        hs = slice(GLA_DV * h, GLA_DV * (h + 1))
        oh = o[:, hs]
        ms = jnp.mean(oh * oh, axis=-1, keepdims=True)
        grh = gr[:, hs]
        outs.append(oh * lax.rsqrt(ms + LN_EPS) * g * (grh * jax.nn.sigmoid(grh)))
    return jnp.concatenate(outs, axis=1)


def _lane_half_masks(shape):
    lane = lax.broadcasted_iota(jnp.int32, shape, 1)
    return lane < SWA_HD, lane >= SWA_HD


def _swa_split_q(q):
    lo_m, hi_m = _lane_half_masks((q.shape[0], LANES))
    out = []
    for j in range(SWA_QW // LANES):
        slab = q[:, LANES * j:LANES * (j + 1)]
        out.append((_bf(jnp.where(lo_m, slab, 0.0)), _bf(jnp.where(hi_m, slab, 0.0))))
    return out


def _swa_dup_k(k):
    lo_m, _ = _lane_half_masks(k.shape)
    kr = pltpu.roll(k, SWA_HD, axis=1)
    return [_bf(jnp.where(lo_m, k, kr)), _bf(jnp.where(lo_m, kr, k))]


def _swa_place_v(v):
    lo_m, hi_m = _lane_half_masks(v.shape)
    vr = pltpu.roll(v, SWA_HD, axis=1)
    return [(_bf(jnp.where(lo_m, v, 0.0)), _bf(jnp.where(hi_m, vr, 0.0))),
            (_bf(jnp.where(lo_m, vr, 0.0)), _bf(jnp.where(hi_m, v, 0.0)))]


def _swa_blocks(qrows, kks, vlos, vhis, sink_col, lowers, upper, tq):
    n = len(kks)
    stack = lambda xs: jnp.concatenate(xs, axis=0) if n > 1 else xs[0]
    s = stack([_dot_nt(qrows[i], kks[i]) for i in range(n)])
    if lowers is not None:
        kcol = lax.broadcasted_iota(jnp.int32, s.shape, 1)
        s = jnp.where(kcol >= stack(lowers), jnp.where(kcol <= stack([upper] * n), s, -jnp.inf), -jnp.inf)
    sink = stack([sink_col] * n)
    m = jnp.maximum(jnp.max(s, axis=-1, keepdims=True), sink)
    p = jnp.exp(s - m)
    den = jnp.sum(p, axis=-1, keepdims=True) + jnp.exp(sink - m)
    p = _bf(p / den)
    outs = []
    for i in range(n):
        r = 4 * tq * i
        outs.append((_dot(p[r:r + tq], vlos[i]) + _dot(p[r + tq:r + 2 * tq], vhis[i]),
                     _dot(p[r + 2 * tq:r + 3 * tq], vlos[i]) + _dot(p[r + 3 * tq:r + 4 * tq], vhis[i])))
    return outs


def _sink_col(sinks_ref, g, tq):
    r = lax.broadcasted_iota(jnp.int32, (4 * tq, 1), 0)
    base = 4 * g
    return jnp.where(r < tq, sinks_ref[base],
                     jnp.where(r < 2 * tq, sinks_ref[base + 1],
                               jnp.where(r < 3 * tq, sinks_ref[base + 2], sinks_ref[base + 3])))


def _mem_head(ref, h):
    return _bf(ref[pl.ds(h, MEM_TOKENS, stride=MEM_HEADS), :])


def _mem_attention(q, mk_refs, mv_refs):
    qb = _bf(q)
    nseg = len(mk_refs)
    tq = q.shape[0] // nseg
    outs = []
    for h in range(MEM_HEADS):
        hs = slice(MEM_HD * h, MEM_HD * (h + 1))
        s = [_dot_nt(qb[tq * i:tq * (i + 1), hs], _mem_head(mk_refs[i], h)) for i in range(nseg)]
        s = jnp.concatenate(s, axis=0) if nseg > 1 else s[0]
        m = jnp.max(s, axis=-1, keepdims=True)
        p = jnp.exp(s - m)
        p = _bf(p / jnp.sum(p, axis=-1, keepdims=True))
        o = [_dot(p[tq * i:tq * (i + 1)], _mem_head(mv_refs[i], h)) for i in range(nseg)]
        outs.append(jnp.concatenate(o, axis=0) if nseg > 1 else o[0])
    return jnp.concatenate(outs, axis=1)


def _gate(xb, i, w_inb_ref, bgate_ref):
    gl = _dot(xb, w_inb_ref[:, OFF_GL + D_MODEL * i:OFF_GL + D_MODEL * (i + 1)])
    return jax.nn.sigmoid(gl + bgate_ref[:, D_MODEL * i:D_MODEL * (i + 1)])


def _merge_ln1(x, gated_sum, wo_ref, g_ref, b_ref, alpha):
    mix = _dot(_bf(gated_sum), wo_ref[...])
    return _layer_norm(alpha * x + mix, g_ref[...], b_ref[...])


def _forget_log(xb, w_ga_ref, w2_ref, ba_ref):
    ga = _dot(xb, w_ga_ref[...])
    return _log_sigmoid(_dot(_bf(ga), w2_ref[...]) + ba_ref[...]) * (1.0 / GLA_TAU)


def _prompt_mixer_kernel(sinks_ref, x_ref, rc_ref, rdn_ref, rup_ref, mk_ref, mv_ref, w_ina_ref, w_ga_ref,
                         w_inb_ref, w2_ref, ba_ref, gng_ref, wbg_ref, wbs_ref, wbm_ref, bgate_ref, wo_ref,
                         ln_g_ref, ln_b_ref,
                         out_ref, gla_out_ref, swak_out_ref, swav_out_ref,
                         s_scr, kk_scr, vv_scr, *, rows, alpha):
    t = pl.program_id(1)
    nchunk = rows // CHUNK

    @pl.when(t == 0)
    def _():
        s_scr[...] = jnp.zeros_like(s_scr)
        kk_scr[:, 0:WINDOW, :] = jnp.zeros((SWA_KV_HEADS, WINDOW, LANES), _BF)
        vv_scr[:, 0:WINDOW, :] = jnp.zeros((2 * SWA_KV_HEADS, WINDOW, LANES), _BF)

    x = x_ref[0]
    xb = _bf(x)

    log_a = _forget_log(xb, w_ga_ref, w2_ref, ba_ref)

    def proj_a(off):
        return _dot(xb, w_ina_ref[:, off:off + GLA_W])

    def state_in(c, h):
        return s_scr[h]

    def state_out(c, h, S):
        s_scr[h] = S
        gla_out_ref[0, h] = S

    gq, gk, gv = proj_a(OFF_GQ), proj_a(OFF_GK), proj_a(OFF_GV)
    gate_a = _gate(xb, 0, w_inb_ref, bgate_ref)
    o_a = _gla_tile(gq, gk, gv, log_a, CHUNK, nchunk, state_in, state_out, chained=True)
    o_a = _gla_gate(o_a, proj_a(OFF_GR), gng_ref[...])
    gated = gate_a * _dot(_bf(o_a), wbg_ref[...])

    rc, rdn, rup = rc_ref[...], rdn_ref[...], rup_ref[...]
    q = _rope(_dot(xb, w_inb_ref[:, OFF_SQ:OFF_SQ + SWA_QW]), rc, rdn, rup) * (SWA_HD ** -0.5)
    k = _rope(_dot(xb, w_inb_ref[:, OFF_SK:OFF_SK + SWA_KW]), rc, rdn, rup)
    v = _dot(xb, w_inb_ref[:, OFF_SV:OFF_SV + SWA_KW])
    swak_out_ref[0] = k[rows - WINDOW:rows, :]
    swav_out_ref[0] = v[rows - WINDOW:rows, :]
    gate_b = _gate(xb, 1, w_inb_ref, bgate_ref)
    kk = _swa_dup_k(k)
    vv = _swa_place_v(v)
    for g in range(SWA_KV_HEADS):
        kk_scr[g, WINDOW:WINDOW + rows, :] = kk[g]
        vv_scr[2 * g, WINDOW:WINDOW + rows, :] = vv[g][0]
        vv_scr[2 * g + 1, WINDOW:WINDOW + rows, :] = vv[g][1]
    qs = _swa_split_q(q)
    nk = WINDOW + SWA_PAIR
    npair = rows // SWA_PAIR
    qchunk = (lax.broadcasted_iota(jnp.int32, (4 * SWA_PAIR, 1), 0) % SWA_PAIR) // CHUNK
    upper = CHUNK * qchunk + (WINDOW + CHUNK - 1)
    slabs = [[None] * npair for _ in range(SWA_QW // LANES)]
    lowers = [jnp.maximum(CHUNK * qchunk, WINDOW - (t * rows + SWA_PAIR * p)) for p in range(npair)]
    for g in range(SWA_KV_HEADS):
        sink_col = _sink_col(sinks_ref, g, SWA_PAIR)
        for p in range(npair):
            rs = slice(SWA_PAIR * p, SWA_PAIR * (p + 1))
            ks = slice(SWA_PAIR * p, SWA_PAIR * p + nk)
            qrows = jnp.concatenate([qs[2 * g][0][rs], qs[2 * g][1][rs],
                                     qs[2 * g + 1][0][rs], qs[2 * g + 1][1][rs]], axis=0)
            (slabs[2 * g][p], slabs[2 * g + 1][p]), = _swa_blocks(
                [qrows], [kk_scr[g, ks, :]], [vv_scr[2 * g, ks, :]], [vv_scr[2 * g + 1, ks, :]],
                sink_col, [lowers[p]], upper, SWA_PAIR)
    o_b = jnp.concatenate([jnp.concatenate(s, axis=0) if npair > 1 else s[0] for s in slabs], axis=1)
    gated = gated + gate_b * _dot(_bf(o_b), wbs_ref[...])
    for g in range(SWA_KV_HEADS):
        kk_scr[g, 0:WINDOW, :] = kk[g][rows - WINDOW:rows]
        vv_scr[2 * g, 0:WINDOW, :] = vv[g][0][rows - WINDOW:rows]
        vv_scr[2 * g + 1, 0:WINDOW, :] = vv[g][1][rows - WINDOW:rows]

    qm = _dot(xb, w_inb_ref[:, OFF_MQ:OFF_MQ + MEM_W]) * (MEM_HD ** -0.5)
    gate_c = _gate(xb, 2, w_inb_ref, bgate_ref)
    o_c = _mem_attention(qm, [mk_ref.at[0]], [mv_ref.at[0]])
    gated = gated + gate_c * _dot(_bf(o_c), wbm_ref[...])

    out_ref[0] = _merge_ln1(x, gated, wo_ref, ln_g_ref, ln_b_ref, alpha)


def _sample_mixer_kernel(sinks_ref, x_ref, rc_ref, rdn_ref, rup_ref, mk_ref, mv_ref, gla_in_ref,
                         swak_in_ref, swav_in_ref, w_ina_ref, w_ga_ref, w_inb_ref, w2_ref,
                         ba_ref, gng_ref, wbg_ref, wbs_ref, wbm_ref, bgate_ref, wo_ref, ln_g_ref, ln_b_ref,
                         out_ref, gla_out_ref, swak_out_ref, swav_out_ref, *, nb, tq, alpha):
    rows = nb * tq
    x = x_ref[...].reshape(rows, D_MODEL)
    xb = _bf(x)

    log_a = _forget_log(xb, w_ga_ref, w2_ref, ba_ref)

    def proj_a(off):
        return _dot(xb, w_ina_ref[:, off:off + GLA_W])

    def state_in(c, h):
        return gla_in_ref[c, h]

    def state_out(c, h, S):
        gla_out_ref[c, h] = S

    gq, gk, gv = proj_a(OFF_GQ), proj_a(OFF_GK), proj_a(OFF_GV)
    gate_a = _gate(xb, 0, w_inb_ref, bgate_ref)
    o_a = _gla_tile(gq, gk, gv, log_a, tq, nb, state_in, state_out, chained=False)
    o_a = _gla_gate(o_a, proj_a(OFF_GR), gng_ref[...])
    gated = gate_a * _dot(_bf(o_a), wbg_ref[...])

    rc = jnp.concatenate([rc_ref[...]] * nb, axis=0)
    rdn = jnp.concatenate([rdn_ref[...]] * nb, axis=0)
    rup = jnp.concatenate([rup_ref[...]] * nb, axis=0)
    q = _rope(_dot(xb, w_inb_ref[:, OFF_SQ:OFF_SQ + SWA_QW]), rc, rdn, rup) * (SWA_HD ** -0.5)
    k = _rope(_dot(xb, w_inb_ref[:, OFF_SK:OFF_SK + SWA_KW]), rc, rdn, rup)
    v = _dot(xb, w_inb_ref[:, OFF_SV:OFF_SV + SWA_KW])
    gate_b = _gate(xb, 1, w_inb_ref, bgate_ref)
    qs = _swa_split_q(q)
    slabs = [[None] * nb for _ in range(SWA_QW // LANES)]
    kks, vvs = [], []
    for bi in range(nb):
        rs = slice(tq * bi, tq * (bi + 1))
        k_all = jnp.concatenate([swak_in_ref[bi], k[rs]], axis=0)
        v_all = jnp.concatenate([swav_in_ref[bi], v[rs]], axis=0)
        swak_out_ref[bi] = k_all[tq:tq + WINDOW]
        swav_out_ref[bi] = v_all[tq:tq + WINDOW]
        kks.append(_swa_dup_k(k_all))
        vvs.append(_swa_place_v(v_all))
    for g in range(SWA_KV_HEADS):
        qrows = []
        for bi in range(nb):
            rs = slice(tq * bi, tq * (bi + 1))
            qrows.append(jnp.concatenate([qs[2 * g][0][rs], qs[2 * g][1][rs],
                                          qs[2 * g + 1][0][rs], qs[2 * g + 1][1][rs]], axis=0))
        outs = _swa_blocks(qrows, [kk[g] for kk in kks], [vv[g][0] for vv in vvs], [vv[g][1] for vv in vvs],
                           _sink_col(sinks_ref, g, tq), None, None, tq)
        for bi in range(nb):
            slabs[2 * g][bi], slabs[2 * g + 1][bi] = outs[bi]
    o_b = jnp.concatenate([jnp.concatenate(s, axis=0) if nb > 1 else s[0] for s in slabs], axis=1)
    gated = gated + gate_b * _dot(_bf(o_b), wbs_ref[...])

    qm = _dot(xb, w_inb_ref[:, OFF_MQ:OFF_MQ + MEM_W]) * (MEM_HD ** -0.5)
    gate_c = _gate(xb, 2, w_inb_ref, bgate_ref)
    o_c = _mem_attention(qm, [mk_ref.at[bi] for bi in range(nb)], [mv_ref.at[bi] for bi in range(nb)])
    gated = gated + gate_c * _dot(_bf(o_c), wbm_ref[...])

    out = _merge_ln1(x, gated, wo_ref, ln_g_ref, ln_b_ref, alpha)
    out_ref[...] = out.reshape(nb, tq, D_MODEL)


def _conv_geglu(u_ref, T, cw_ref, cb_ref):
    u0 = u_ref[SUBLANES:SUBLANES + T, :]
    u1 = u_ref[SUBLANES - 1:SUBLANES - 1 + T, :]
    u2 = u_ref[SUBLANES - 2:SUBLANES - 2 + T, :]
    c = cb_ref[...] + u2 * cw_ref[0:1, :] + u1 * cw_ref[1:2, :] + u0 * cw_ref[2:3, :]
    return jax.nn.gelu(c[:, :D_FF], approximate=True) * c[:, D_FF:]


def _prompt_ffn_kernel(x_ref, wup_ref, cw_ref, cb_ref, wdn_ref, ln_g_ref, ln_b_ref,
                       out_ref, conv_out_ref, u_scr, *, rows, alpha):
    @pl.when(pl.program_id(1) == 0)
    def _():
        u_scr[0:SUBLANES, :] = jnp.zeros((SUBLANES, 2 * D_FF), _F32)

    x = x_ref[0]
    u_scr[SUBLANES:SUBLANES + rows, :] = _dot(_bf(x), wup_ref[...])
    h = _conv_geglu(u_scr, rows, cw_ref, cb_ref)
    tail = u_scr[SUBLANES + rows - (CONV_W - 1):SUBLANES + rows, :]
    conv_out_ref[0] = tail
    u_scr[SUBLANES - (CONV_W - 1):SUBLANES, :] = tail
    f = _dot(_bf(h), wdn_ref[...])
    out_ref[0] = _layer_norm(alpha * x + f, ln_g_ref[...], ln_b_ref[...])


def _sample_ffn_kernel(x_ref, hist_ref, wup_ref, cw_ref, cb_ref, wdn_ref, ln_g_ref, ln_b_ref,
                       out_ref, conv_out_ref, u_scr, *, nb, tq, alpha):
    rows = nb * tq
    x = x_ref[...].reshape(rows, D_MODEL)
    u = _dot(_bf(x), wup_ref[...])
    hs = []
    for bi in range(nb):
        u_scr[bi, SUBLANES - (CONV_W - 1):SUBLANES, :] = hist_ref[bi]
        u_scr[bi, SUBLANES:SUBLANES + tq, :] = u[tq * bi:tq * (bi + 1)]
        hs.append(_conv_geglu(u_scr.at[bi], tq, cw_ref, cb_ref))
        conv_out_ref[bi] = u_scr[bi, SUBLANES + tq - (CONV_W - 1):SUBLANES + tq, :]
    f = _dot(_bf(jnp.concatenate(hs, axis=0) if nb > 1 else hs[0]), wdn_ref[...])
    out = _layer_norm(alpha * x + f, ln_g_ref[...], ln_b_ref[...])
    out_ref[...] = out.reshape(nb, tq, D_MODEL)


def _mem_kv_kernel(m_ref, w_ref, k_ref, v_ref):
    kv = _dot(_bf(m_ref[0]), w_ref[...])
    for h in range(MEM_HEADS):
        k_ref[0, pl.ds(h, MEM_TOKENS, stride=MEM_HEADS), :] = kv[:, MEM_HD * h:MEM_HD * (h + 1)]
        v_ref[0, pl.ds(h, MEM_TOKENS, stride=MEM_HEADS), :] = kv[:, MEM_W + MEM_HD * h:MEM_W + MEM_HD * (h + 1)]


def _const_spec(shape):
    nd = len(shape)
    return pl.BlockSpec(shape, lambda *_: (0,) * nd, pipeline_mode=pl.Buffered(1))


def _rope_tables(pos):
    half = ROPE_DIM // 2
    inv = np.float32(ROPE_THETA) ** (-np.arange(half, dtype=np.float32) / np.float32(half))
    ang = pos.astype(np.float32)[:, None] * inv[None, :]
    cos, sin = np.cos(ang), np.sin(ang)
    T = pos.shape[0]
    ones = np.ones((T, SWA_HD - ROPE_DIM), np.float32)
    zeros = np.zeros((T, SWA_HD - ROPE_DIM), np.float32)
    zh = np.zeros((T, half), np.float32)
    c = np.concatenate([cos, cos, ones], axis=1)
    dn = np.concatenate([-sin, zh, zeros], axis=1)
    up = np.concatenate([zh, sin, zeros], axis=1)
    rep = LANES // SWA_HD
    return tuple(jnp.asarray(np.concatenate([a] * rep, axis=1), dtype=_F32) for a in (c, dn, up))


def _split_w_in(w_in):
    ga = w_in[:, IN_A_COLS:IN_A_COLS + GLA_RANK]
    ga = jnp.pad(ga, ((0, 0), (0, LANES - GLA_RANK)))
    return _bf(w_in[:, :IN_A_COLS]), _bf(ga), _bf(w_in[:, IN_A_COLS + GLA_RANK:])


def _mixer_weight_specs():
    return [
        _const_spec((D_MODEL, IN_A_COLS)),
        _const_spec((D_MODEL, LANES)),
        _const_spec((D_MODEL, IN_B_COLS)),
        _const_spec((LANES, GLA_W)),
        _const_spec((1, GLA_W)),
        _const_spec((1, GLA_DV)),
        _const_spec((GLA_W, D_MODEL)),
        _const_spec((SWA_QW, D_MODEL)),
        _const_spec((MEM_W, D_MODEL)),
        _const_spec((1, GATE_W)),
        _const_spec((D_MODEL, D_MODEL)),
        _const_spec((1, D_MODEL)),
        _const_spec((1, D_MODEL)),
    ]


def _ffn_weight_specs():
    return [
        _const_spec((D_MODEL, 2 * D_FF)),
        _const_spec((CONV_W, 2 * D_FF)),
        _const_spec((1, 2 * D_FF)),
        _const_spec((D_FF, D_MODEL)),
        _const_spec((1, D_MODEL)),
        _const_spec((1, D_MODEL)),
    ]


_SMEM_SPEC = pl.BlockSpec(memory_space=pltpu.SMEM)
_MEM_ROWS = MEM_TOKENS * MEM_HEADS


def _params(sem, flags=None):
    return pltpu.CompilerParams(dimension_semantics=sem, vmem_limit_bytes=V7X_VMEM_LIMIT, flags=flags)


def _prompt_layer(x, mem, sinks, mixer_w, ffn_w, w_mem_kv, alpha):
    B, T, _ = x.shape
    rows = min(PROMPT_ROWS, T)
    nt = T // rows
    assert T % rows == 0 and rows % SWA_PAIR == 0 and rows >= WINDOW
    f32 = jnp.float32
    mk, mv = pl.pallas_call(
        _mem_kv_kernel,
        grid=(B,),
        in_specs=[pl.BlockSpec((1, MEM_TOKENS, D_MODEL), lambda b: (b, 0, 0)),
                  _const_spec((D_MODEL, 2 * MEM_W))],
        out_specs=[pl.BlockSpec((1, _MEM_ROWS, MEM_HD), lambda b: (b, 0, 0))] * 2,
        out_shape=[jax.ShapeDtypeStruct((B, _MEM_ROWS, MEM_HD), f32)] * 2,
        compiler_params=_params(("arbitrary",)),
        name="mem_kv",
    )(mem, w_mem_kv)

    rope = _rope_tables(np.arange(T))
    rope_spec = pl.BlockSpec((rows, LANES), lambda b, t: (t, 0))
    per_batch = lambda *blk: pl.BlockSpec((1,) + blk, lambda b, t: (b,) + (0,) * len(blk))
    x1, gla, swak, swav = pl.pallas_call(
        functools.partial(_prompt_mixer_kernel, rows=rows, alpha=alpha),
        grid=(B, nt),
        in_specs=[_SMEM_SPEC,
                  pl.BlockSpec((1, rows, D_MODEL), lambda b, t: (b, t, 0)),
                  rope_spec, rope_spec, rope_spec,
                  per_batch(_MEM_ROWS, MEM_HD), per_batch(_MEM_ROWS, MEM_HD)] + _mixer_weight_specs(),
        out_specs=[pl.BlockSpec((1, rows, D_MODEL), lambda b, t: (b, t, 0)),
                   per_batch(GLA_HEADS, GLA_DK, GLA_DV),
                   per_batch(WINDOW, SWA_KW), per_batch(WINDOW, SWA_KW)],
        out_shape=[jax.ShapeDtypeStruct((B, T, D_MODEL), f32),
                   jax.ShapeDtypeStruct((B, GLA_HEADS, GLA_DK, GLA_DV), f32),
                   jax.ShapeDtypeStruct((B, WINDOW, SWA_KW), f32),
                   jax.ShapeDtypeStruct((B, WINDOW, SWA_KW), f32)],
        scratch_shapes=[pltpu.VMEM((GLA_HEADS, GLA_DK, GLA_DV), f32),
                        pltpu.VMEM((SWA_KV_HEADS, WINDOW + rows, LANES), _BF),
                        pltpu.VMEM((2 * SWA_KV_HEADS, WINDOW + rows, LANES), _BF)],
        compiler_params=_params(("arbitrary", "arbitrary")),
        name="prompt_mixer",
    )(sinks, x, *rope, mk, mv, *mixer_w)

    y, conv = pl.pallas_call(
        functools.partial(_prompt_ffn_kernel, rows=rows, alpha=alpha),
        grid=(B, nt),
        in_specs=[pl.BlockSpec((1, rows, D_MODEL), lambda b, t: (b, t, 0))] + _ffn_weight_specs(),
        out_specs=[pl.BlockSpec((1, rows, D_MODEL), lambda b, t: (b, t, 0)),
                   per_batch(CONV_W - 1, 2 * D_FF)],
        out_shape=[jax.ShapeDtypeStruct((B, T, D_MODEL), f32),
                   jax.ShapeDtypeStruct((B, CONV_W - 1, 2 * D_FF), f32)],
        scratch_shapes=[pltpu.VMEM((SUBLANES + rows, 2 * D_FF), f32)],
        compiler_params=_params(("arbitrary", "arbitrary")),
        name="prompt_ffn",
    )(x1, *ffn_w)
    return y, gla, swak, swav, mk, mv, conv


def _sample_layer(x, gla0, swak0, swav0, memk, memv, conv0, sinks, mixer_w, ffn_w, alpha):
    B, tq, _ = x.shape
    nb = max(1, min(B, SAMPLE_ROWS // tq))
    assert B % nb == 0 and tq % 16 == 0 and tq >= CONV_W - 1
    f32 = jnp.float32
    rope = _rope_tables(PAST_LEN + np.arange(tq))
    rope_spec = pl.BlockSpec((tq, LANES), lambda i: (0, 0))
    blk = lambda *s: pl.BlockSpec((nb,) + s, lambda i: (i,) + (0,) * len(s))
    blk1 = lambda *s: pl.BlockSpec((nb,) + s, lambda i: (i,) + (0,) * len(s), pipeline_mode=pl.Buffered(1))
    x1, gla, swak, swav = pl.pallas_call(
        functools.partial(_sample_mixer_kernel, nb=nb, tq=tq, alpha=alpha),
        grid=(B // nb,),
        in_specs=[_SMEM_SPEC, blk(tq, D_MODEL), rope_spec, rope_spec, rope_spec,
                  blk(_MEM_ROWS, MEM_HD), blk(_MEM_ROWS, MEM_HD),
                  blk1(GLA_HEADS, GLA_DK, GLA_DV), blk1(WINDOW, SWA_KW), blk1(WINDOW, SWA_KW)]
                 + _mixer_weight_specs(),
        out_specs=[blk(tq, D_MODEL), blk(GLA_HEADS, GLA_DK, GLA_DV), blk(WINDOW, SWA_KW), blk(WINDOW, SWA_KW)],
        out_shape=[jax.ShapeDtypeStruct((B, tq, D_MODEL), f32),
                   jax.ShapeDtypeStruct((B, GLA_HEADS, GLA_DK, GLA_DV), f32),
                   jax.ShapeDtypeStruct((B, WINDOW, SWA_KW), f32),
                   jax.ShapeDtypeStruct((B, WINDOW, SWA_KW), f32)],
        compiler_params=_params(("arbitrary",)),
        name="sample_mixer",
    )(sinks, x, *rope, memk, memv, gla0, swak0, swav0, *mixer_w)

    y, conv = pl.pallas_call(
        functools.partial(_sample_ffn_kernel, nb=nb, tq=tq, alpha=alpha),
        grid=(B // nb,),
        in_specs=[blk(tq, D_MODEL), blk(CONV_W - 1, 2 * D_FF)] + _ffn_weight_specs(),
        out_specs=[blk(tq, D_MODEL), blk(CONV_W - 1, 2 * D_FF)],
        out_shape=[jax.ShapeDtypeStruct((B, tq, D_MODEL), f32),
                   jax.ShapeDtypeStruct((B, CONV_W - 1, 2 * D_FF), f32)],
        scratch_shapes=[pltpu.VMEM((nb, SUBLANES + tq, 2 * D_FF), f32)],
        compiler_params=_params(("arbitrary",)),
        name="sample_ffn",
    )(x1, conv0, *ffn_w)
    return y, gla, swak, swav, conv


def kernel(x_prompt, x_sample, cache_swa_k, cache_swa_v, state_gla, cache_mem_k, cache_mem_v, cache_ffn_conv, mem_prompt, ln1_g, ln1_b, ln2_g, ln2_b, w_in, b_gate, w_gla_a2, b_gla_a, gla_norm_g, swa_sinks, w_mem_kv, w_br_gla, w_br_swa, w_br_mem, w_o, w_up, conv_w, conv_b, w_down):
    depth = w_in.shape[0]
    alpha = float((2 * depth) ** 0.25)
    Bp = x_prompt.shape[0]
    Bs = x_sample.shape[0]
    hp, hs = x_prompt, x_sample
    outs = [[] for _ in range(10)]
    row = lambda a: a.reshape(1, -1)
    for l in range(depth):
        w2 = jnp.pad(w_gla_a2[l], ((0, LANES - GLA_RANK), (0, 0)))
        mixer_w = _split_w_in(w_in[l]) + (
            _bf(w2), row(b_gla_a[l]), row(gla_norm_g[l]),
            _bf(w_br_gla[l]), _bf(w_br_swa[l]), _bf(w_br_mem[l]), row(b_gate[l]), _bf(w_o[l]),
            row(ln1_g[l]), row(ln1_b[l]))
        ffn_w = (_bf(w_up[l]), conv_w[l], row(conv_b[l]), _bf(w_down[l]), row(ln2_g[l]), row(ln2_b[l]))
        sinks = swa_sinks[l]
        hp, g_p, k_p, v_p, mk_p, mv_p, c_p = _prompt_layer(
            hp, mem_prompt, sinks, mixer_w, ffn_w, _bf(w_mem_kv[l]), alpha)
        hs, g_s, k_s, v_s, c_s = _sample_layer(
            hs, state_gla[l],
            cache_swa_k[l].reshape(Bs, WINDOW, SWA_KW), cache_swa_v[l].reshape(Bs, WINDOW, SWA_KW),
            cache_mem_k[l].reshape(Bs, _MEM_ROWS, MEM_HD), cache_mem_v[l].reshape(Bs, _MEM_ROWS, MEM_HD),
            cache_ffn_conv[l], sinks, mixer_w, ffn_w, alpha)
        kv5 = lambda a, b: a.reshape(b, WINDOW, SWA_KV_HEADS, SWA_HD)
        m5 = lambda a: a.reshape(Bp, MEM_TOKENS, MEM_HEADS, MEM_HD)
        for lst, val in zip(outs, (kv5(k_p, Bp), kv5(v_p, Bp), g_p, m5(mk_p), m5(mv_p), c_p,
                                   kv5(k_s, Bs), kv5(v_s, Bs), g_s, c_s)):
            lst.append(val)
    return (hp, hs) + tuple(jnp.stack(o) for o in outs)
```

```python
import functools

import jax
import jax.numpy as jnp
import numpy as np
from jax import lax
from jax.experimental import pallas as pl
from jax.experimental.pallas import tpu as pltpu

D_MODEL = 1024
CHUNK = 64
GLA_HEADS = 4
GLA_DK = 128
GLA_DV = 128
GLA_RANK = 16
GLA_TAU = 16.0
SWA_HEADS = 8
SWA_KV_HEADS = 2
SWA_HD = 64
WINDOW = 128
ROPE_DIM = 16
ROPE_THETA = 500000.0
MEM_TOKENS = 256
MEM_HEADS = 4
MEM_HD = 128
D_FF = 2816
CONV_W = 3
N_BRANCH = 3
PAST_LEN = 2048
LN_EPS = 1e-5

LANES = 128
SUBLANES = 8
V7X_VMEM_LIMIT = 56 * 1024 * 1024
PROMPT_ROWS = 256
PROMPT_FFN_ROWS = 512
SAMPLE_ROWS = 256
FILL_COLS = 256

GLA_W = GLA_HEADS * GLA_DK
SWA_QW = SWA_HEADS * SWA_HD
SWA_KW = SWA_KV_HEADS * SWA_HD
MEM_W = MEM_HEADS * MEM_HD
GATE_W = N_BRANCH * D_MODEL
SWA_PAIR = 2 * CHUNK

OFF_GQ = 0
OFF_GK = OFF_GQ + GLA_W
OFF_GV = OFF_GK + GLA_W
OFF_GR = OFF_GV + GLA_W
IN_A_COLS = OFF_GR + GLA_W
OFF_SQ = 0
OFF_SK = OFF_SQ + SWA_QW
OFF_SV = OFF_SK + SWA_KW
OFF_MQ = OFF_SV + SWA_KW
OFF_GL = OFF_MQ + MEM_W
IN_B_COLS = OFF_GL + GATE_W

_BF = jnp.bfloat16
_F32 = jnp.float32


def _bf(x):
    return x.astype(_BF)


def _dot(a, b):
    return jnp.dot(a, b, preferred_element_type=_F32)


def _dot_nt(a, b):
    return lax.dot_general(a, b, (((1,), (1,)), ((), ())), preferred_element_type=_F32)


def _layer_norm(h, g, b):
    mu = jnp.mean(h, axis=-1, keepdims=True)
    d = h - mu
    var = jnp.mean(d * d, axis=-1, keepdims=True)
    return d * lax.rsqrt(var + LN_EPS) * g + b


def _log_sigmoid(x):
    return -(jnp.maximum(-x, 0.0) + jnp.log(1.0 + jnp.exp(-jnp.abs(x))))


def _rope(a, c, s_dn, s_up):
    outs = []
    for j in range(a.shape[1] // LANES):
        slab = a[:, LANES * j:LANES * (j + 1)]
        fwd = pltpu.roll(slab, LANES - ROPE_DIM // 2, axis=1)
        bwd = pltpu.roll(slab, ROPE_DIM // 2, axis=1)
        outs.append(slab * c + fwd * s_dn + bwd * s_up)
    return outs[0] if len(outs) == 1 else jnp.concatenate(outs, axis=1)


class _DenseQueue:
    def __init__(self):
        self.jobs = []

    def add(self, fn):
        self.jobs.append(fn)

    def run(self, n):
        for _ in range(min(n, len(self.jobs))):
            self.jobs.pop(0)()

    def drain(self):
        self.run(len(self.jobs))


def _gla_tile(q, k, v, log_a, chunk, nchunk, state_in, state_out, chained, fill):
    R = chunk * nchunk
    row = lax.broadcasted_iota(jnp.int32, (R, R), 0)
    col = lax.broadcasted_iota(jnp.int32, (R, R), 1)
    tril = (row // chunk == col // chunk) & (col <= row)
    ltri = jnp.where(tril, 1.0, 0.0).astype(_BF)
    hi = _bf(log_a)
    lo = _bf(log_a - hi.astype(_F32))
    b = _dot(ltri, hi) + _dot(ltri, lo)
    fill("pre")
    b_last = [b[chunk * (c + 1) - 1:chunk * (c + 1), :] for c in range(nchunk)]
    bl_rows = jnp.concatenate([jnp.broadcast_to(bl, (chunk, GLA_W)) for bl in b_last], axis=0) \
        if nchunk > 1 else jnp.broadcast_to(b_last[0], (chunk, GLA_W))
    q_dec = _bf(q * (GLA_DK ** -0.5) * jnp.exp(b))
    k_inv = _bf(k * jnp.exp(-b))
    k_tail = _bf(k * jnp.exp(bl_rows - b))
    vb = _bf(v)
    outs = []
    for h in range(GLA_HEADS):
        hs = slice(GLA_DK * h, GLA_DK * (h + 1))
        att = jnp.where(tril, _dot_nt(q_dec[:, hs], k_inv[:, hs]), 0.0)
        o_intra = _dot(_bf(att), vb[:, hs])
        o_inter = []
        S = None
        for c in range(nchunk):
            rs = slice(chunk * c, chunk * (c + 1))
            if c == 0 or not chained:
                S = state_in(c, h)
            o_inter.append(_dot(q_dec[rs, hs], _bf(S)))
            decay = jnp.exp(b_last[c][:, hs])
            dcol = jnp.transpose(jnp.broadcast_to(decay, (GLA_DV, GLA_DK)))
            ds = _dot(jnp.transpose(k_tail[rs, hs].astype(_F32)).astype(_BF), vb[rs, hs])
            S = S * dcol + ds
            if c == nchunk - 1 or not chained:
                state_out(c, h, S)
        o_inter = jnp.concatenate(o_inter, axis=0) if nchunk > 1 else o_inter[0]
        outs.append(o_intra + o_inter)
        fill("head")
    return jnp.concatenate(outs, axis=1)


def _gla_gate(o, gr, g):
    outs = []
    for h in range(GLA_HEADS):
        hs = slice(GLA_DV * h, GLA_DV * (h + 1))
        oh = o[:, hs]
        ms = jnp.mean(oh * oh, axis=-1, keepdims=True)
        grh = gr[:, hs]
        outs.append(oh * lax.rsqrt(ms + LN_EPS) * g * (grh * jax.nn.sigmoid(grh)))
    return jnp.concatenate(outs, axis=1)


def _lane_half_masks(shape):
    lane = lax.broadcasted_iota(jnp.int32, shape, 1)
    return lane < SWA_HD, lane >= SWA_HD


def _swa_split_q(q):
    lo_m, hi_m = _lane_half_masks((q.shape[0], LANES))
    out = []
    for j in range(SWA_QW // LANES):
        slab = q[:, LANES * j:LANES * (j + 1)]
        out.append((_bf(jnp.where(lo_m, slab, 0.0)), _bf(jnp.where(hi_m, slab, 0.0))))
    return out


def _swa_dup_k(k):
    lo_m, _ = _lane_half_masks(k.shape)
    kr = pltpu.roll(k, SWA_HD, axis=1)
    return [_bf(jnp.where(lo_m, k, kr)), _bf(jnp.where(lo_m, kr, k))]


def _swa_place_v(v):
    lo_m, hi_m = _lane_half_masks(v.shape)
    vr = pltpu.roll(v, SWA_HD, axis=1)
    return [(_bf(jnp.where(lo_m, v, 0.0)), _bf(jnp.where(hi_m, vr, 0.0))),
            (_bf(jnp.where(lo_m, vr, 0.0)), _bf(jnp.where(hi_m, v, 0.0)))]


def _swa_blocks(qrows, kks, vlos, vhis, sink_col, biases, tq, fill):
    n = len(kks)
    stack = lambda xs: jnp.concatenate(xs, axis=0) if n > 1 else xs[0]
    s = stack([_dot_nt(qrows[i], kks[i]) for i in range(n)])
    fill("block")
    if biases is not None:
        s = s + stack(biases)
    sink = stack([sink_col] * n)
    m = jnp.maximum(jnp.max(s, axis=-1, keepdims=True), sink)
    p = jnp.exp(s - m)
    den = jnp.sum(p, axis=-1, keepdims=True) + jnp.exp(sink - m)
    p = _bf(p / den)
    outs = []
    for i in range(n):
        r = 4 * tq * i
        outs.append((_dot(p[r:r + tq], vlos[i]) + _dot(p[r + tq:r + 2 * tq], vhis[i]),
                     _dot(p[r + 2 * tq:r + 3 * tq], vlos[i]) + _dot(p[r + 3 * tq:r + 4 * tq], vhis[i])))
    return outs


def _sink_col(sinks_ref, g, tq):
    r = lax.broadcasted_iota(jnp.int32, (4 * tq, 1), 0)
    base = 4 * g
    return jnp.where(r < tq, sinks_ref[base],
                     jnp.where(r < 2 * tq, sinks_ref[base + 1],
                               jnp.where(r < 3 * tq, sinks_ref[base + 2], sinks_ref[base + 3])))


def _mem_head(ref, h):
    return _bf(ref[pl.ds(h, MEM_TOKENS, stride=MEM_HEADS), :])


def _mem_attention(q, mk_refs, mv_refs, fill):
    qb = _bf(q)
    nseg = len(mk_refs)
    tq = q.shape[0] // nseg
    outs = []
    for h in range(MEM_HEADS):
        hs = slice(MEM_HD * h, MEM_HD * (h + 1))
        s = [_dot_nt(qb[tq * i:tq * (i + 1), hs], _mem_head(mk_refs[i], h)) for i in range(nseg)]
        s = jnp.concatenate(s, axis=0) if nseg > 1 else s[0]
        fill("head")
        m = jnp.max(s, axis=-1, keepdims=True)
        p = jnp.exp(s - m)
        p = _bf(p / jnp.sum(p, axis=-1, keepdims=True))
        o = [_dot(p[tq * i:tq * (i + 1)], _mem_head(mv_refs[i], h)) for i in range(nseg)]
        outs.append(jnp.concatenate(o, axis=0) if nseg > 1 else o[0])
    return jnp.concatenate(outs, axis=1)


def _queue_dense(xb, w_ina_ref, w_inb_ref, bgate_ref):
    dq = _DenseQueue()
    parts, want = {}, {}

    def add(name, ref, off, width, post):
        want[name] = -(-width // FILL_COLS)
        parts[name] = []
        for j in range(0, width, FILL_COLS):
            w = min(FILL_COLS, width - j)
            dq.add(lambda name=name, ref=ref, c0=off + j, w=w, j=j:
                   parts[name].append(post(_dot(xb, ref[:, c0:c0 + w]), j, w)))

    ident = lambda y, j, w: y

    def add_gate(i):
        add("gate%d" % i, w_inb_ref, OFF_GL + D_MODEL * i, D_MODEL,
            lambda y, j, w: jax.nn.sigmoid(y + bgate_ref[:, D_MODEL * i + j:D_MODEL * i + j + w]))

    add_gate(0)
    add("gr", w_ina_ref, OFF_GR, GLA_W, ident)
    add_gate(1)
    add("sq", w_inb_ref, OFF_SQ, SWA_QW, ident)
    add("sk", w_inb_ref, OFF_SK, SWA_KW, ident)
    add("sv", w_inb_ref, OFF_SV, SWA_KW, ident)
    add_gate(2)
    add("mq", w_inb_ref, OFF_MQ, MEM_W, ident)

    def get(name):
        while len(parts[name]) < want[name]:
            dq.run(1)
        ps = parts[name]
        return ps[0] if len(ps) == 1 else jnp.concatenate(ps, axis=1)

    return dq, get


PROMPT_FILL = {"pre": 4, "head": 0, "block": 0}
SAMPLE_FILL = {"pre": 8, "head": 1, "block": 1}


def _merge_ln1(x, gated_sum, wo_ref, g_ref, b_ref, alpha):
    mix = _dot(_bf(gated_sum), wo_ref[...])
    return _layer_norm(alpha * x + mix, g_ref[...], b_ref[...])


def _forget_log(xb, w_ga_ref, w2_ref, ba_ref):
    ga = _dot(xb, w_ga_ref[...])
    return _log_sigmoid(_dot(_bf(ga), w2_ref[...]) + ba_ref[...]) * (1.0 / GLA_TAU)


def _prompt_mixer_kernel(sinks_ref, x_ref, rc_ref, rdn_ref, rup_ref, swab_ref, mk_ref, mv_ref, w_ina_ref, w_ga_ref,
                         w_inb_ref, w2_ref, ba_ref, gng_ref, wbg_ref, wbs_ref, wbm_ref, bgate_ref, wo_ref,
                         ln_g_ref, ln_b_ref,
                         out_ref, gla_out_ref, swak_out_ref, swav_out_ref,
                         s_scr, kk_scr, vv_scr, *, rows, alpha):
    t = pl.program_id(1)
    nchunk = rows // CHUNK

    @pl.when(t == 0)
    def _():
        s_scr[...] = jnp.zeros_like(s_scr)
        kk_scr[:, 0:WINDOW, :] = jnp.zeros((SWA_KV_HEADS, WINDOW, LANES), _BF)
        vv_scr[:, 0:WINDOW, :] = jnp.zeros((2 * SWA_KV_HEADS, WINDOW, LANES), _BF)

    x = x_ref[0]
    xb = _bf(x)

    log_a = _forget_log(xb, w_ga_ref, w2_ref, ba_ref)

    def proj_a(off):
        return _dot(xb, w_ina_ref[:, off:off + GLA_W])

    def state_in(c, h):
        return s_scr[h]

    def state_out(c, h, S):
        s_scr[h] = S
        gla_out_ref[0, h] = S

    gq, gk, gv = proj_a(OFF_GQ), proj_a(OFF_GK), proj_a(OFF_GV)
    dq, dense = _queue_dense(xb, w_ina_ref, w_inb_ref, bgate_ref)
    fill = lambda where: dq.run(PROMPT_FILL[where])
    o_a = _gla_tile(gq, gk, gv, log_a, CHUNK, nchunk, state_in, state_out, True, fill)
    o_a = _gla_gate(o_a, dense("gr"), gng_ref[...])
    y_a = _dot(_bf(o_a), wbg_ref[...])

    rc, rdn, rup = rc_ref[...], rdn_ref[...], rup_ref[...]
    q = _rope(dense("sq"), rc, rdn, rup) * (SWA_HD ** -0.5)
    k = _rope(dense("sk"), rc, rdn, rup)
    v = dense("sv")
    swak_out_ref[0] = k[rows - WINDOW:rows, :]
    swav_out_ref[0] = v[rows - WINDOW:rows, :]
    kk = _swa_dup_k(k)
    vv = _swa_place_v(v)
    for g in range(SWA_KV_HEADS):
        kk_scr[g, WINDOW:WINDOW + rows, :] = kk[g]
        vv_scr[2 * g, WINDOW:WINDOW + rows, :] = vv[g][0]
        vv_scr[2 * g + 1, WINDOW:WINDOW + rows, :] = vv[g][1]
    qs = _swa_split_q(q)
    nk = WINDOW + SWA_PAIR
    npair = rows // SWA_PAIR
    slabs = [[None] * npair for _ in range(SWA_QW // LANES)]
    biases = [swab_ref[jnp.where(t == 0, 1, 0)] if p == 0 else swab_ref[0] for p in range(npair)]
    for g in range(SWA_KV_HEADS):
        sink_col = _sink_col(sinks_ref, g, SWA_PAIR)
        for p in range(npair):
            rs = slice(SWA_PAIR * p, SWA_PAIR * (p + 1))
            ks = slice(SWA_PAIR * p, SWA_PAIR * p + nk)
            qrows = jnp.concatenate([qs[2 * g][0][rs], qs[2 * g][1][rs],
                                     qs[2 * g + 1][0][rs], qs[2 * g + 1][1][rs]], axis=0)
            (slabs[2 * g][p], slabs[2 * g + 1][p]), = _swa_blocks(
                [qrows], [kk_scr[g, ks, :]], [vv_scr[2 * g, ks, :]], [vv_scr[2 * g + 1, ks, :]],
                sink_col, [biases[p]], SWA_PAIR, fill)
    o_b = jnp.concatenate([jnp.concatenate(s, axis=0) if npair > 1 else s[0] for s in slabs], axis=1)
    y_b = _dot(_bf(o_b), wbs_ref[...])
    for g in range(SWA_KV_HEADS):
        kk_scr[g, 0:WINDOW, :] = kk[g][rows - WINDOW:rows]
        vv_scr[2 * g, 0:WINDOW, :] = vv[g][0][rows - WINDOW:rows]
        vv_scr[2 * g + 1, 0:WINDOW, :] = vv[g][1][rows - WINDOW:rows]

    o_c = _mem_attention(dense("mq") * (MEM_HD ** -0.5), [mk_ref.at[0]], [mv_ref.at[0]], fill)
    y_c = _dot(_bf(o_c), wbm_ref[...])

    gated = dense("gate0") * y_a + dense("gate1") * y_b + dense("gate2") * y_c
    out_ref[0] = _merge_ln1(x, gated, wo_ref, ln_g_ref, ln_b_ref, alpha)


def _sample_mixer_kernel(sinks_ref, x_ref, rc_ref, rdn_ref, rup_ref, mk_ref, mv_ref, gla_in_ref,
                         swak_in_ref, swav_in_ref, w_ina_ref, w_ga_ref, w_inb_ref, w2_ref,
                         ba_ref, gng_ref, wbg_ref, wbs_ref, wbm_ref, bgate_ref, wo_ref, ln_g_ref, ln_b_ref,
                         out_ref, gla_out_ref, swak_out_ref, swav_out_ref, *, nb, tq, alpha):
    rows = nb * tq
    x = x_ref[...].reshape(rows, D_MODEL)
    xb = _bf(x)

    log_a = _forget_log(xb, w_ga_ref, w2_ref, ba_ref)

    def proj_a(off):
        return _dot(xb, w_ina_ref[:, off:off + GLA_W])

    def state_in(c, h):
        return gla_in_ref[c, h]

    def state_out(c, h, S):
        gla_out_ref[c, h] = S

    gq, gk, gv = proj_a(OFF_GQ), proj_a(OFF_GK), proj_a(OFF_GV)
    dq, dense = _queue_dense(xb, w_ina_ref, w_inb_ref, bgate_ref)
    fill = lambda where: dq.run(SAMPLE_FILL[where])
    o_a = _gla_tile(gq, gk, gv, log_a, tq, nb, state_in, state_out, False, fill)
    o_a = _gla_gate(o_a, dense("gr"), gng_ref[...])
    y_a = _dot(_bf(o_a), wbg_ref[...])

    rc = jnp.concatenate([rc_ref[...]] * nb, axis=0)
    rdn = jnp.concatenate([rdn_ref[...]] * nb, axis=0)
    rup = jnp.concatenate([rup_ref[...]] * nb, axis=0)
    q = _rope(dense("sq"), rc, rdn, rup) * (SWA_HD ** -0.5)
    k = _rope(dense("sk"), rc, rdn, rup)
    v = dense("sv")
    qs = _swa_split_q(q)
    slabs = [[None] * nb for _ in range(SWA_QW // LANES)]
    kks, vvs = [], []
    for bi in range(nb):
        rs = slice(tq * bi, tq * (bi + 1))
        k_all = jnp.concatenate([swak_in_ref[bi], k[rs]], axis=0)
        v_all = jnp.concatenate([swav_in_ref[bi], v[rs]], axis=0)
        swak_out_ref[bi] = k_all[tq:tq + WINDOW]
        swav_out_ref[bi] = v_all[tq:tq + WINDOW]
        kks.append(_swa_dup_k(k_all))
        vvs.append(_swa_place_v(v_all))
    for g in range(SWA_KV_HEADS):
        qrows = []
        for bi in range(nb):
            rs = slice(tq * bi, tq * (bi + 1))
            qrows.append(jnp.concatenate([qs[2 * g][0][rs], qs[2 * g][1][rs],
                                          qs[2 * g + 1][0][rs], qs[2 * g + 1][1][rs]], axis=0))
        outs = _swa_blocks(qrows, [kk[g] for kk in kks], [vv[g][0] for vv in vvs], [vv[g][1] for vv in vvs],
                           _sink_col(sinks_ref, g, tq), None, tq, fill)
        for bi in range(nb):
            slabs[2 * g][bi], slabs[2 * g + 1][bi] = outs[bi]
    o_b = jnp.concatenate([jnp.concatenate(s, axis=0) if nb > 1 else s[0] for s in slabs], axis=1)
    y_b = _dot(_bf(o_b), wbs_ref[...])

    o_c = _mem_attention(dense("mq") * (MEM_HD ** -0.5), [mk_ref.at[bi] for bi in range(nb)],
                         [mv_ref.at[bi] for bi in range(nb)], fill)
    y_c = _dot(_bf(o_c), wbm_ref[...])

    gated = dense("gate0") * y_a + dense("gate1") * y_b + dense("gate2") * y_c
    out = _merge_ln1(x, gated, wo_ref, ln_g_ref, ln_b_ref, alpha)
    out_ref[...] = out.reshape(nb, tq, D_MODEL)


_GELU_K1 = -2.0 * float(np.log2(np.e)) * float(np.sqrt(2.0 / np.pi))
_GELU_K2 = _GELU_K1 * 0.044715


def _conv_geglu(u2, u1, u0, cw_ref, cb_ref):
    c = cb_ref[...] + u2 * cw_ref[0:1, :] + u1 * cw_ref[1:2, :] + u0 * cw_ref[2:3, :]
    g, v = c[:, :D_FF], c[:, D_FF:]
    return g * v / (1.0 + jnp.exp2(g * (g * g * _GELU_K2 + _GELU_K1)))


def _prompt_ffn_kernel(x_ref, wup_ref, cw_ref, cb_ref, wdn_ref, ln_g_ref, ln_b_ref,
                       out_ref, conv_out_ref, u_scr, *, rows, alpha):
    nj = rows // SUBLANES
    base = 2 * SUBLANES
    last = slice(base + rows - SUBLANES, base + rows)
    last2 = slice(base + rows - 2 * SUBLANES, base + rows - SUBLANES)

    @pl.when(pl.program_id(1) == 0)
    def _():
        u_scr[base + rows - 2 * SUBLANES:base + rows, :] = jnp.zeros((2 * SUBLANES, 2 * D_FF), _F32)

    prev_last, prev_last2 = u_scr[last, :], u_scr[last2, :]
    x = jnp.swapaxes(x_ref[0].reshape(SUBLANES, nj, D_MODEL), 0, 1).reshape(rows, D_MODEL)
    u_scr[base:base + rows, :] = _dot(_bf(x), wup_ref[...])
    cur_last, cur_last2 = u_scr[last, :], u_scr[last2, :]
    sub = lax.broadcasted_iota(jnp.int32, (SUBLANES, 2 * D_FF), 0)

    def wrap(prev, cur):
        return pltpu.roll(jnp.where(sub == SUBLANES - 1, prev, cur), 1, axis=0)

    u_scr[SUBLANES:base, :] = wrap(prev_last, cur_last)
    u_scr[0:SUBLANES, :] = wrap(prev_last2, cur_last2)
    h = _conv_geglu(u_scr[0:rows, :], u_scr[SUBLANES:SUBLANES + rows, :], u_scr[base:base + rows, :],
                    cw_ref, cb_ref)
    conv_out_ref[0] = jnp.concatenate([cur_last2[SUBLANES - 1:, :], cur_last[SUBLANES - 1:, :]], axis=0)
    f = _dot(_bf(h), wdn_ref[...])
    y = _layer_norm(alpha * x + f, ln_g_ref[...], ln_b_ref[...])
    out_ref[0] = jnp.swapaxes(y.reshape(nj, SUBLANES, D_MODEL), 0, 1).reshape(rows, D_MODEL)


def _sample_ffn_kernel(x_ref, hist_ref, wup_ref, cw_ref, cb_ref, wdn_ref, ln_g_ref, ln_b_ref,
                       out_ref, conv_out_ref, u_scr, *, nb, tq, alpha):
    rows = nb * tq
    x = x_ref[...].reshape(rows, D_MODEL)
    u = _dot(_bf(x), wup_ref[...])
    hs = []
    for bi in range(nb):
        u_scr[bi, SUBLANES - (CONV_W - 1):SUBLANES, :] = hist_ref[bi]
        u_scr[bi, SUBLANES:SUBLANES + tq, :] = u[tq * bi:tq * (bi + 1)]
        hs.append(_conv_geglu(u_scr[bi, SUBLANES - 2:SUBLANES - 2 + tq, :],
                              u_scr[bi, SUBLANES - 1:SUBLANES - 1 + tq, :],
                              u_scr[bi, SUBLANES:SUBLANES + tq, :], cw_ref, cb_ref))
        conv_out_ref[bi] = u_scr[bi, SUBLANES + tq - (CONV_W - 1):SUBLANES + tq, :]
    f = _dot(_bf(jnp.concatenate(hs, axis=0) if nb > 1 else hs[0]), wdn_ref[...])
    out = _layer_norm(alpha * x + f, ln_g_ref[...], ln_b_ref[...])
    out_ref[...] = out.reshape(nb, tq, D_MODEL)


def _mem_kv_kernel(m_ref, w_ref, k_ref, v_ref):
    kv = _dot(_bf(m_ref[0]), w_ref[...])
    for h in range(MEM_HEADS):
        k_ref[0, pl.ds(h, MEM_TOKENS, stride=MEM_HEADS), :] = kv[:, MEM_HD * h:MEM_HD * (h + 1)]
        v_ref[0, pl.ds(h, MEM_TOKENS, stride=MEM_HEADS), :] = kv[:, MEM_W + MEM_HD * h:MEM_W + MEM_HD * (h + 1)]


def _const_spec(shape):
    nd = len(shape)
    return pl.BlockSpec(shape, lambda *_: (0,) * nd, pipeline_mode=pl.Buffered(1))


def _rope_tables(pos):
    half = ROPE_DIM // 2
    inv = np.float32(ROPE_THETA) ** (-np.arange(half, dtype=np.float32) / np.float32(half))
    ang = pos.astype(np.float32)[:, None] * inv[None, :]
    cos, sin = np.cos(ang), np.sin(ang)
    T = pos.shape[0]
    ones = np.ones((T, SWA_HD - ROPE_DIM), np.float32)
    zeros = np.zeros((T, SWA_HD - ROPE_DIM), np.float32)
    zh = np.zeros((T, half), np.float32)
    c = np.concatenate([cos, cos, ones], axis=1)
    dn = np.concatenate([-sin, zh, zeros], axis=1)
    up = np.concatenate([zh, sin, zeros], axis=1)
    rep = LANES // SWA_HD
    return tuple(jnp.asarray(np.concatenate([a] * rep, axis=1), dtype=_F32) for a in (c, dn, up))


def _swa_bias_tables():
    r = np.arange(4 * SWA_PAIR)[:, None]
    j = np.arange(WINDOW + SWA_PAIR)[None, :]
    lower = CHUNK * ((r % SWA_PAIR) // CHUNK)
    band = (j >= lower) & (j <= lower + WINDOW + CHUNK - 1)
    first = band & (j >= WINDOW)
    tab = np.where(np.stack([band, first]), 0.0, -np.inf).astype(np.float32)
    return jnp.asarray(tab)


def _split_w_in(w_in):
    ga = w_in[:, IN_A_COLS:IN_A_COLS + GLA_RANK]
    ga = jnp.pad(ga, ((0, 0), (0, LANES - GLA_RANK)))
    return _bf(w_in[:, :IN_A_COLS]), _bf(ga), _bf(w_in[:, IN_A_COLS + GLA_RANK:])


def _mixer_weight_specs():
    return [
        _const_spec((D_MODEL, IN_A_COLS)),
        _const_spec((D_MODEL, LANES)),
        _const_spec((D_MODEL, IN_B_COLS)),
        _const_spec((LANES, GLA_W)),
        _const_spec((1, GLA_W)),
        _const_spec((1, GLA_DV)),
        _const_spec((GLA_W, D_MODEL)),
        _const_spec((SWA_QW, D_MODEL)),
        _const_spec((MEM_W, D_MODEL)),
        _const_spec((1, GATE_W)),
        _const_spec((D_MODEL, D_MODEL)),
        _const_spec((1, D_MODEL)),
        _const_spec((1, D_MODEL)),
    ]


def _ffn_weight_specs():
    return [
        _const_spec((D_MODEL, 2 * D_FF)),
        _const_spec((CONV_W, 2 * D_FF)),
        _const_spec((1, 2 * D_FF)),
        _const_spec((D_FF, D_MODEL)),
        _const_spec((1, D_MODEL)),
        _const_spec((1, D_MODEL)),
    ]


_SMEM_SPEC = pl.BlockSpec(memory_space=pltpu.SMEM)
_MEM_ROWS = MEM_TOKENS * MEM_HEADS


def _params(sem, flags=None):
    return pltpu.CompilerParams(dimension_semantics=sem, vmem_limit_bytes=V7X_VMEM_LIMIT, flags=flags)


def _prompt_layer(x, mem, sinks, mixer_w, ffn_w, w_mem_kv, alpha):
    B, T, _ = x.shape
    rows = min(PROMPT_ROWS, T)
    nt = T // rows
    assert T % rows == 0 and rows % SWA_PAIR == 0 and rows >= WINDOW
    f32 = jnp.float32
    mk, mv = pl.pallas_call(
        _mem_kv_kernel,
        grid=(B,),
        in_specs=[pl.BlockSpec((1, MEM_TOKENS, D_MODEL), lambda b: (b, 0, 0)),
                  _const_spec((D_MODEL, 2 * MEM_W))],
        out_specs=[pl.BlockSpec((1, _MEM_ROWS, MEM_HD), lambda b: (b, 0, 0))] * 2,
        out_shape=[jax.ShapeDtypeStruct((B, _MEM_ROWS, MEM_HD), f32)] * 2,
        compiler_params=_params(("arbitrary",)),
        name="mem_kv",
    )(mem, w_mem_kv)

    rope = _rope_tables(np.arange(T))
    rope_spec = pl.BlockSpec((rows, LANES), lambda b, t: (t, 0))
    per_batch = lambda *blk: pl.BlockSpec((1,) + blk, lambda b, t: (b,) + (0,) * len(blk))
    x1, gla, swak, swav = pl.pallas_call(
        functools.partial(_prompt_mixer_kernel, rows=rows, alpha=alpha),
        grid=(B, nt),
        in_specs=[_SMEM_SPEC,
                  pl.BlockSpec((1, rows, D_MODEL), lambda b, t: (b, t, 0)),
                  rope_spec, rope_spec, rope_spec,
                  _const_spec((2, 4 * SWA_PAIR, WINDOW + SWA_PAIR)),
                  per_batch(_MEM_ROWS, MEM_HD), per_batch(_MEM_ROWS, MEM_HD)] + _mixer_weight_specs(),
        out_specs=[pl.BlockSpec((1, rows, D_MODEL), lambda b, t: (b, t, 0)),
                   per_batch(GLA_HEADS, GLA_DK, GLA_DV),
                   per_batch(WINDOW, SWA_KW), per_batch(WINDOW, SWA_KW)],
        out_shape=[jax.ShapeDtypeStruct((B, T, D_MODEL), f32),
                   jax.ShapeDtypeStruct((B, GLA_HEADS, GLA_DK, GLA_DV), f32),
                   jax.ShapeDtypeStruct((B, WINDOW, SWA_KW), f32),
                   jax.ShapeDtypeStruct((B, WINDOW, SWA_KW), f32)],
        scratch_shapes=[pltpu.VMEM((GLA_HEADS, GLA_DK, GLA_DV), f32),
                        pltpu.VMEM((SWA_KV_HEADS, WINDOW + rows, LANES), _BF),
                        pltpu.VMEM((2 * SWA_KV_HEADS, WINDOW + rows, LANES), _BF)],
        compiler_params=_params(("arbitrary", "arbitrary")),
        name="prompt_mixer",
    )(sinks, x, *rope, _swa_bias_tables(), mk, mv, *mixer_w)

    frows = min(PROMPT_FFN_ROWS, T)
    assert T % frows == 0 and frows >= CONV_W - 1
    y, conv = pl.pallas_call(
        functools.partial(_prompt_ffn_kernel, rows=frows, alpha=alpha),
        grid=(B, T // frows),
        in_specs=[pl.BlockSpec((1, frows, D_MODEL), lambda b, t: (b, t, 0))] + _ffn_weight_specs(),
        out_specs=[pl.BlockSpec((1, frows, D_MODEL), lambda b, t: (b, t, 0)),
                   per_batch(CONV_W - 1, 2 * D_FF)],
        out_shape=[jax.ShapeDtypeStruct((B, T, D_MODEL), f32),
                   jax.ShapeDtypeStruct((B, CONV_W - 1, 2 * D_FF), f32)],
        scratch_shapes=[pltpu.VMEM((2 * SUBLANES + frows, 2 * D_FF), f32)],
        compiler_params=_params(("arbitrary", "arbitrary")),
        name="prompt_ffn",
    )(x1, *ffn_w)
    return y, gla, swak, swav, mk, mv, conv


def _sample_layer(x, gla0, swak0, swav0, memk, memv, conv0, sinks, mixer_w, ffn_w, alpha):
    B, tq, _ = x.shape
    nb = max(1, min(B, SAMPLE_ROWS // tq))
    assert B % nb == 0 and tq % 16 == 0 and tq >= CONV_W - 1
    f32 = jnp.float32
    rope = _rope_tables(PAST_LEN + np.arange(tq))
    rope_spec = pl.BlockSpec((tq, LANES), lambda i: (0, 0))
    blk = lambda *s: pl.BlockSpec((nb,) + s, lambda i: (i,) + (0,) * len(s))
    blk1 = lambda *s: pl.BlockSpec((nb,) + s, lambda i: (i,) + (0,) * len(s), pipeline_mode=pl.Buffered(1))
    x1, gla, swak, swav = pl.pallas_call(
        functools.partial(_sample_mixer_kernel, nb=nb, tq=tq, alpha=alpha),
        grid=(B // nb,),
        in_specs=[_SMEM_SPEC, blk(tq, D_MODEL), rope_spec, rope_spec, rope_spec,
                  blk(_MEM_ROWS, MEM_HD), blk(_MEM_ROWS, MEM_HD),
                  blk1(GLA_HEADS, GLA_DK, GLA_DV), blk1(WINDOW, SWA_KW), blk1(WINDOW, SWA_KW)]
                 + _mixer_weight_specs(),
        out_specs=[blk(tq, D_MODEL), blk(GLA_HEADS, GLA_DK, GLA_DV), blk(WINDOW, SWA_KW), blk(WINDOW, SWA_KW)],
        out_shape=[jax.ShapeDtypeStruct((B, tq, D_MODEL), f32),
                   jax.ShapeDtypeStruct((B, GLA_HEADS, GLA_DK, GLA_DV), f32),
                   jax.ShapeDtypeStruct((B, WINDOW, SWA_KW), f32),
                   jax.ShapeDtypeStruct((B, WINDOW, SWA_KW), f32)],
        compiler_params=_params(("arbitrary",)),
        name="sample_mixer",
    )(sinks, x, *rope, memk, memv, gla0, swak0, swav0, *mixer_w)

    y, conv = pl.pallas_call(
        functools.partial(_sample_ffn_kernel, nb=nb, tq=tq, alpha=alpha),
        grid=(B // nb,),
        in_specs=[blk(tq, D_MODEL), blk(CONV_W - 1, 2 * D_FF)] + _ffn_weight_specs(),
        out_specs=[blk(tq, D_MODEL), blk(CONV_W - 1, 2 * D_FF)],
        out_shape=[jax.ShapeDtypeStruct((B, tq, D_MODEL), f32),
                   jax.ShapeDtypeStruct((B, CONV_W - 1, 2 * D_FF), f32)],
        scratch_shapes=[pltpu.VMEM((nb, SUBLANES + tq, 2 * D_FF), f32)],
        compiler_params=_params(("arbitrary",)),
        name="sample_ffn",
    )(x1, conv0, *ffn_w)
    return y, gla, swak, swav, conv


def kernel(x_prompt, x_sample, cache_swa_k, cache_swa_v, state_gla, cache_mem_k, cache_mem_v, cache_ffn_conv, mem_prompt, ln1_g, ln1_b, ln2_g, ln2_b, w_in, b_gate, w_gla_a2, b_gla_a, gla_norm_g, swa_sinks, w_mem_kv, w_br_gla, w_br_swa, w_br_mem, w_o, w_up, conv_w, conv_b, w_down):
    depth = w_in.shape[0]
    alpha = float((2 * depth) ** 0.25)
    Bp = x_prompt.shape[0]
    Bs = x_sample.shape[0]
    hp, hs = x_prompt, x_sample
    outs = [[] for _ in range(10)]
    row = lambda a: a.reshape(1, -1)
    for l in range(depth):
        w2 = jnp.pad(w_gla_a2[l], ((0, LANES - GLA_RANK), (0, 0)))
        mixer_w = _split_w_in(w_in[l]) + (
            _bf(w2), row(b_gla_a[l]), row(gla_norm_g[l]),
            _bf(w_br_gla[l]), _bf(w_br_swa[l]), _bf(w_br_mem[l]), row(b_gate[l]), _bf(w_o[l]),
            row(ln1_g[l]), row(ln1_b[l]))
        ffn_w = (_bf(w_up[l]), conv_w[l], row(conv_b[l]), _bf(w_down[l]), row(ln2_g[l]), row(ln2_b[l]))
        sinks = swa_sinks[l]
        hp, g_p, k_p, v_p, mk_p, mv_p, c_p = _prompt_layer(
            hp, mem_prompt, sinks, mixer_w, ffn_w, _bf(w_mem_kv[l]), alpha)
        hs, g_s, k_s, v_s, c_s = _sample_layer(
            hs, state_gla[l],
            cache_swa_k[l].reshape(Bs, WINDOW, SWA_KW), cache_swa_v[l].reshape(Bs, WINDOW, SWA_KW),
            cache_mem_k[l].reshape(Bs, _MEM_ROWS, MEM_HD), cache_mem_v[l].reshape(Bs, _MEM_ROWS, MEM_HD),
            cache_ffn_conv[l], sinks, mixer_w, ffn_w, alpha)
        kv5 = lambda a, b: a.reshape(b, WINDOW, SWA_KV_HEADS, SWA_HD)
        m5 = lambda a: a.reshape(Bp, MEM_TOKENS, MEM_HEADS, MEM_HD)
        for lst, val in zip(outs, (kv5(k_p, Bp), kv5(v_p, Bp), g_p, m5(mk_p), m5(mv_p), c_p,
                                   kv5(k_s, Bs), kv5(v_s, Bs), g_s, c_s)):
            lst.append(val)
    return (hp, hs) + tuple(jnp.stack(o) for o in outs)
```

```python
import functools

import jax
import jax.numpy as jnp
import numpy as np
from jax import lax
from jax.experimental import pallas as pl
from jax.experimental.pallas import tpu as pltpu

D_MODEL = 1024
CHUNK = 64
GLA_HEADS = 4
GLA_DK = 128
GLA_DV = 128
GLA_RANK = 16
GLA_TAU = 16.0
SWA_HEADS = 8
SWA_KV_HEADS = 2
SWA_HD = 64
WINDOW = 128
ROPE_DIM = 16
ROPE_THETA = 500000.0
MEM_TOKENS = 256
MEM_HEADS = 4
MEM_HD = 128
D_FF = 2816
CONV_W = 3
N_BRANCH = 3
PAST_LEN = 2048
LN_EPS = 1e-5

LANES = 128
SUBLANES = 8
V7X_VMEM_LIMIT = 56 * 1024 * 1024
PROMPT_ROWS = 256
PROMPT_FFN_ROWS = 512
SAMPLE_ROWS = 256
FILL_COLS = 1024

GLA_W = GLA_HEADS * GLA_DK
SWA_QW = SWA_HEADS * SWA_HD
SWA_KW = SWA_KV_HEADS * SWA_HD
MEM_W = MEM_HEADS * MEM_HD
GATE_W = N_BRANCH * D_MODEL
SWA_PAIR = 2 * CHUNK

OFF_GQ = 0
OFF_GK = OFF_GQ + GLA_W
OFF_GV = OFF_GK + GLA_W
OFF_GR = OFF_GV + GLA_W
IN_A_COLS = OFF_GR + GLA_W
OFF_SQ = 0
OFF_SK = OFF_SQ + SWA_QW
OFF_SV = OFF_SK + SWA_KW
OFF_MQ = OFF_SV + SWA_KW
OFF_GL = OFF_MQ + MEM_W
IN_B_COLS = OFF_GL + GATE_W

_BF = jnp.bfloat16
_F32 = jnp.float32


def _bf(x):
    return x.astype(_BF)


def _dot(a, b):
    return jnp.dot(a, b, preferred_element_type=_F32)


def _dot_nt(a, b):
    return lax.dot_general(a, b, (((1,), (1,)), ((), ())), preferred_element_type=_F32)


def _layer_norm(h, g, b):
    mu = jnp.mean(h, axis=-1, keepdims=True)
    d = h - mu
    var = jnp.mean(d * d, axis=-1, keepdims=True)
    return d * lax.rsqrt(var + LN_EPS) * g + b


def _log_sigmoid(x):
    return -(jnp.maximum(-x, 0.0) + jnp.log(1.0 + jnp.exp(-jnp.abs(x))))


def _rope(a, c, s_dn, s_up):
    outs = []
    for j in range(a.shape[1] // LANES):
        slab = a[:, LANES * j:LANES * (j + 1)]
        fwd = pltpu.roll(slab, LANES - ROPE_DIM // 2, axis=1)
        bwd = pltpu.roll(slab, ROPE_DIM // 2, axis=1)
        outs.append(slab * c + fwd * s_dn + bwd * s_up)
    return outs[0] if len(outs) == 1 else jnp.concatenate(outs, axis=1)


class _DenseQueue:
    def __init__(self):
        self.jobs = []

    def add(self, fn):
        self.jobs.append(fn)

    def run(self, n):
        for _ in range(min(n, len(self.jobs))):
            self.jobs.pop(0)()

    def drain(self):
        self.run(len(self.jobs))


def _gla_tile(q, k, v, log_a, chunk, nchunk, state_in, state_out, chained, fill):
    R = chunk * nchunk
    row = lax.broadcasted_iota(jnp.int32, (R, R), 0)
    col = lax.broadcasted_iota(jnp.int32, (R, R), 1)
    tril = (row // chunk == col // chunk) & (col <= row)
    ltri = jnp.where(tril, 1.0, 0.0).astype(_BF)
    hi = _bf(log_a)
    lo = _bf(log_a - hi.astype(_F32))
    b = _dot(ltri, hi) + _dot(ltri, lo)
    fill("pre")
    b_last = [b[chunk * (c + 1) - 1:chunk * (c + 1), :] for c in range(nchunk)]
    bl_rows = jnp.concatenate([jnp.broadcast_to(bl, (chunk, GLA_W)) for bl in b_last], axis=0) \
        if nchunk > 1 else jnp.broadcast_to(b_last[0], (chunk, GLA_W))
    q_dec = _bf(q * (GLA_DK ** -0.5) * jnp.exp(b))
    k_inv = _bf(k * jnp.exp(-b))
    k_tail = _bf(k * jnp.exp(bl_rows - b))
    vb = _bf(v)
    outs = []
    for h in range(GLA_HEADS):
        hs = slice(GLA_DK * h, GLA_DK * (h + 1))
        att = jnp.where(tril, _dot_nt(q_dec[:, hs], k_inv[:, hs]), 0.0)
        o_intra = _dot(_bf(att), vb[:, hs])
        o_inter = []
        S = None
        for c in range(nchunk):
            rs = slice(chunk * c, chunk * (c + 1))
            if c == 0 or not chained:
                S = state_in(c, h)
            o_inter.append(_dot(q_dec[rs, hs], _bf(S)))
            decay = jnp.exp(b_last[c][:, hs])
            dcol = jnp.transpose(jnp.broadcast_to(decay, (GLA_DV, GLA_DK)))
            ds = _dot(jnp.transpose(k_tail[rs, hs].astype(_F32)).astype(_BF), vb[rs, hs])
            S = S * dcol + ds
            if c == nchunk - 1 or not chained:
                state_out(c, h, S)
        o_inter = jnp.concatenate(o_inter, axis=0) if nchunk > 1 else o_inter[0]
        outs.append(o_intra + o_inter)
        fill("head")
    return jnp.concatenate(outs, axis=1)


def _gla_gate(o, gr, g):
    outs = []
    for h in range(GLA_HEADS):
        hs = slice(GLA_DV * h, GLA_DV * (h + 1))
        oh = o[:, hs]
        ms = jnp.mean(oh * oh, axis=-1, keepdims=True)
        grh = gr[:, hs]
        outs.append(oh * lax.rsqrt(ms + LN_EPS) * g * (grh * jax.nn.sigmoid(grh)))
    return jnp.concatenate(outs, axis=1)


def _lane_half_masks(shape):
    lane = lax.broadcasted_iota(jnp.int32, shape, 1)
    return lane < SWA_HD, lane >= SWA_HD


def _swa_split_q(q):
    lo_m, hi_m = _lane_half_masks((q.shape[0], LANES))
    out = []
    for j in range(SWA_QW // LANES):
        slab = q[:, LANES * j:LANES * (j + 1)]
        out.append((_bf(jnp.where(lo_m, slab, 0.0)), _bf(jnp.where(hi_m, slab, 0.0))))
    return out


def _swa_dup_k(k):
    lo_m, _ = _lane_half_masks(k.shape)
    kr = pltpu.roll(k, SWA_HD, axis=1)
    return [_bf(jnp.where(lo_m, k, kr)), _bf(jnp.where(lo_m, kr, k))]


def _swa_place_v(v):
    lo_m, hi_m = _lane_half_masks(v.shape)
    vr = pltpu.roll(v, SWA_HD, axis=1)
    return [(_bf(jnp.where(lo_m, v, 0.0)), _bf(jnp.where(hi_m, vr, 0.0))),
            (_bf(jnp.where(lo_m, vr, 0.0)), _bf(jnp.where(hi_m, v, 0.0)))]


def _swa_blocks(qrows, kks, vlos, vhis, sink_col, biases, tq, fill):
    n = len(kks)
    stack = lambda xs: jnp.concatenate(xs, axis=0) if n > 1 else xs[0]
    s = stack([_dot_nt(qrows[i], kks[i]) for i in range(n)])
    fill("block")
    if biases is not None:
        s = s + stack(biases)
    sink = stack([sink_col] * n)
    m = jnp.maximum(jnp.max(s, axis=-1, keepdims=True), sink)
    p = jnp.exp(s - m)
    den = jnp.sum(p, axis=-1, keepdims=True) + jnp.exp(sink - m)
    p = _bf(p / den)
    outs = []
    for i in range(n):
        r = 4 * tq * i
        outs.append((_dot(p[r:r + tq], vlos[i]) + _dot(p[r + tq:r + 2 * tq], vhis[i]),
                     _dot(p[r + 2 * tq:r + 3 * tq], vlos[i]) + _dot(p[r + 3 * tq:r + 4 * tq], vhis[i])))
    return outs


def _sink_col(sinks_ref, g, tq):
    r = lax.broadcasted_iota(jnp.int32, (4 * tq, 1), 0)
    base = 4 * g
    return jnp.where(r < tq, sinks_ref[base],
                     jnp.where(r < 2 * tq, sinks_ref[base + 1],
                               jnp.where(r < 3 * tq, sinks_ref[base + 2], sinks_ref[base + 3])))


def _mem_head(ref, h):
    return _bf(ref[pl.ds(h, MEM_TOKENS, stride=MEM_HEADS), :])


def _mem_attention(q, mk_refs, mv_refs, fill):
    qb = _bf(q)
    nseg = len(mk_refs)
    tq = q.shape[0] // nseg
    outs = []
    for h in range(MEM_HEADS):
        hs = slice(MEM_HD * h, MEM_HD * (h + 1))
        s = [_dot_nt(qb[tq * i:tq * (i + 1), hs], _mem_head(mk_refs[i], h)) for i in range(nseg)]
        s = jnp.concatenate(s, axis=0) if nseg > 1 else s[0]
        fill("head")
        m = jnp.max(s, axis=-1, keepdims=True)
        p = jnp.exp(s - m)
        p = _bf(p / jnp.sum(p, axis=-1, keepdims=True))
        o = [_dot(p[tq * i:tq * (i + 1)], _mem_head(mv_refs[i], h)) for i in range(nseg)]
        outs.append(jnp.concatenate(o, axis=0) if nseg > 1 else o[0])
    return jnp.concatenate(outs, axis=1)


def _queue_dense(xb, w_ina_ref, w_inb_ref, bgate_ref):
    dq = _DenseQueue()
    parts, want = {}, {}

    def add(name, ref, off, width, post):
        want[name] = -(-width // FILL_COLS)
        parts[name] = []
        for j in range(0, width, FILL_COLS):
            w = min(FILL_COLS, width - j)
            dq.add(lambda name=name, ref=ref, c0=off + j, w=w, j=j:
                   parts[name].append(post(_dot(xb, ref[:, c0:c0 + w]), j, w)))

    ident = lambda y, j, w: y

    def add_gate(i):
        add("gate%d" % i, w_inb_ref, OFF_GL + D_MODEL * i, D_MODEL,
            lambda y, j, w: jax.nn.sigmoid(y + bgate_ref[:, D_MODEL * i + j:D_MODEL * i + j + w]))

    add_gate(0)
    add("gr", w_ina_ref, OFF_GR, GLA_W, ident)
    add_gate(1)
    add("sq", w_inb_ref, OFF_SQ, SWA_QW, ident)
    add("sk", w_inb_ref, OFF_SK, SWA_KW, ident)
    add("sv", w_inb_ref, OFF_SV, SWA_KW, ident)
    add_gate(2)
    add("mq", w_inb_ref, OFF_MQ, MEM_W, ident)

    def get(name):
        while len(parts[name]) < want[name]:
            dq.run(1)
        ps = parts[name]
        return ps[0] if len(ps) == 1 else jnp.concatenate(ps, axis=1)

    return dq, get


PROMPT_FILL = {"pre": 1, "head": 0, "block": 0}
SAMPLE_FILL = {"pre": 1, "head": 0, "block": 0}


def _merge_ln1(x, gated_sum, wo_ref, g_ref, b_ref, alpha):
    mix = _dot(_bf(gated_sum), wo_ref[...])
    return _layer_norm(alpha * x + mix, g_ref[...], b_ref[...])


def _forget_log(xb, w_ga_ref, w2_ref, ba_ref):
    ga = _dot(xb, w_ga_ref[...])
    return _log_sigmoid(_dot(_bf(ga), w2_ref[...]) + ba_ref[...]) * (1.0 / GLA_TAU)


def _prompt_mixer_kernel(sinks_ref, x_ref, rc_ref, rdn_ref, rup_ref, swab_ref, mk_ref, mv_ref, w_ina_ref, w_ga_ref,
                         w_inb_ref, w2_ref, ba_ref, gng_ref, wbg_ref, wbs_ref, wbm_ref, bgate_ref, wo_ref,
                         ln_g_ref, ln_b_ref,
                         out_ref, gla_out_ref, swak_out_ref, swav_out_ref,
                         s_scr, kk_scr, vv_scr, *, rows, alpha):
    t = pl.program_id(1)
    nchunk = rows // CHUNK

    @pl.when(t == 0)
    def _():
        s_scr[...] = jnp.zeros_like(s_scr)
        kk_scr[:, 0:WINDOW, :] = jnp.zeros((SWA_KV_HEADS, WINDOW, LANES), _BF)
        vv_scr[:, 0:WINDOW, :] = jnp.zeros((2 * SWA_KV_HEADS, WINDOW, LANES), _BF)

    x = x_ref[0]
    xb = _bf(x)

    log_a = _forget_log(xb, w_ga_ref, w2_ref, ba_ref)

    def proj_a(off):
        return _dot(xb, w_ina_ref[:, off:off + GLA_W])

    def state_in(c, h):
        return s_scr[h]

    def state_out(c, h, S):
        s_scr[h] = S
        gla_out_ref[0, h] = S

    gq, gk, gv = proj_a(OFF_GQ), proj_a(OFF_GK), proj_a(OFF_GV)
    dq, dense = _queue_dense(xb, w_ina_ref, w_inb_ref, bgate_ref)
    fill = lambda where: dq.run(PROMPT_FILL[where])
    o_a = _gla_tile(gq, gk, gv, log_a, CHUNK, nchunk, state_in, state_out, True, fill)
    o_a = _gla_gate(o_a, dense("gr"), gng_ref[...])
    y_a = _dot(_bf(o_a), wbg_ref[...])

    rc, rdn, rup = rc_ref[...], rdn_ref[...], rup_ref[...]
    q = _rope(dense("sq"), rc, rdn, rup) * (SWA_HD ** -0.5)
    k = _rope(dense("sk"), rc, rdn, rup)
    v = dense("sv")
    swak_out_ref[0] = k[rows - WINDOW:rows, :]
    swav_out_ref[0] = v[rows - WINDOW:rows, :]
    kk = _swa_dup_k(k)
    vv = _swa_place_v(v)
    for g in range(SWA_KV_HEADS):
        kk_scr[g, WINDOW:WINDOW + rows, :] = kk[g]
        vv_scr[2 * g, WINDOW:WINDOW + rows, :] = vv[g][0]
        vv_scr[2 * g + 1, WINDOW:WINDOW + rows, :] = vv[g][1]
    qs = _swa_split_q(q)
    nk = WINDOW + SWA_PAIR
    npair = rows // SWA_PAIR
    slabs = [[None] * npair for _ in range(SWA_QW // LANES)]
    biases = [swab_ref[jnp.where(t == 0, 1, 0)] if p == 0 else swab_ref[0] for p in range(npair)]
    for g in range(SWA_KV_HEADS):
        sink_col = _sink_col(sinks_ref, g, SWA_PAIR)
        for p in range(npair):
            rs = slice(SWA_PAIR * p, SWA_PAIR * (p + 1))
            ks = slice(SWA_PAIR * p, SWA_PAIR * p + nk)
            qrows = jnp.concatenate([qs[2 * g][0][rs], qs[2 * g][1][rs],
                                     qs[2 * g + 1][0][rs], qs[2 * g + 1][1][rs]], axis=0)
            (slabs[2 * g][p], slabs[2 * g + 1][p]), = _swa_blocks(
                [qrows], [kk_scr[g, ks, :]], [vv_scr[2 * g, ks, :]], [vv_scr[2 * g + 1, ks, :]],
                sink_col, [biases[p]], SWA_PAIR, fill)
    o_b = jnp.concatenate([jnp.concatenate(s, axis=0) if npair > 1 else s[0] for s in slabs], axis=1)
    y_b = _dot(_bf(o_b), wbs_ref[...])
    for g in range(SWA_KV_HEADS):
        kk_scr[g, 0:WINDOW, :] = kk[g][rows - WINDOW:rows]
        vv_scr[2 * g, 0:WINDOW, :] = vv[g][0][rows - WINDOW:rows]
        vv_scr[2 * g + 1, 0:WINDOW, :] = vv[g][1][rows - WINDOW:rows]

    o_c = _mem_attention(dense("mq") * (MEM_HD ** -0.5), [mk_ref.at[0]], [mv_ref.at[0]], fill)
    y_c = _dot(_bf(o_c), wbm_ref[...])

    gated = dense("gate0") * y_a + dense("gate1") * y_b + dense("gate2") * y_c
    out_ref[0] = _merge_ln1(x, gated, wo_ref, ln_g_ref, ln_b_ref, alpha)


def _sample_mixer_kernel(sinks_ref, x_ref, rc_ref, rdn_ref, rup_ref, mk_ref, mv_ref, gla_in_ref,
                         swak_in_ref, swav_in_ref, w_ina_ref, w_ga_ref, w_inb_ref, w2_ref,
                         ba_ref, gng_ref, wbg_ref, wbs_ref, wbm_ref, bgate_ref, wo_ref, ln_g_ref, ln_b_ref,
                         out_ref, gla_out_ref, swak_out_ref, swav_out_ref, *, nb, tq, alpha):
    rows = nb * tq
    x = x_ref[...].reshape(rows, D_MODEL)
    xb = _bf(x)

    log_a = _forget_log(xb, w_ga_ref, w2_ref, ba_ref)

    def proj_a(off):
        return _dot(xb, w_ina_ref[:, off:off + GLA_W])

    def state_in(c, h):
        return gla_in_ref[c, h]

    def state_out(c, h, S):
        gla_out_ref[c, h] = S

    gq, gk, gv = proj_a(OFF_GQ), proj_a(OFF_GK), proj_a(OFF_GV)
    dq, dense = _queue_dense(xb, w_ina_ref, w_inb_ref, bgate_ref)
    fill = lambda where: dq.run(SAMPLE_FILL[where])
    o_a = _gla_tile(gq, gk, gv, log_a, tq, nb, state_in, state_out, False, fill)
    o_a = _gla_gate(o_a, dense("gr"), gng_ref[...])
    y_a = _dot(_bf(o_a), wbg_ref[...])

    rc = jnp.concatenate([rc_ref[...]] * nb, axis=0)
    rdn = jnp.concatenate([rdn_ref[...]] * nb, axis=0)
    rup = jnp.concatenate([rup_ref[...]] * nb, axis=0)
    q = _rope(dense("sq"), rc, rdn, rup) * (SWA_HD ** -0.5)
    k = _rope(dense("sk"), rc, rdn, rup)
    v = dense("sv")
    qs = _swa_split_q(q)
    slabs = [[None] * nb for _ in range(SWA_QW // LANES)]
    kks, vvs = [], []
    for bi in range(nb):
        rs = slice(tq * bi, tq * (bi + 1))
        k_all = jnp.concatenate([swak_in_ref[bi], k[rs]], axis=0)
        v_all = jnp.concatenate([swav_in_ref[bi], v[rs]], axis=0)
        swak_out_ref[bi] = k_all[tq:tq + WINDOW]
        swav_out_ref[bi] = v_all[tq:tq + WINDOW]
        kks.append(_swa_dup_k(k_all))
        vvs.append(_swa_place_v(v_all))
    for g in range(SWA_KV_HEADS):
        qrows = []
        for bi in range(nb):
            rs = slice(tq * bi, tq * (bi + 1))
            qrows.append(jnp.concatenate([qs[2 * g][0][rs], qs[2 * g][1][rs],
                                          qs[2 * g + 1][0][rs], qs[2 * g + 1][1][rs]], axis=0))
        outs = _swa_blocks(qrows, [kk[g] for kk in kks], [vv[g][0] for vv in vvs], [vv[g][1] for vv in vvs],
                           _sink_col(sinks_ref, g, tq), None, tq, fill)
        for bi in range(nb):
            slabs[2 * g][bi], slabs[2 * g + 1][bi] = outs[bi]
    o_b = jnp.concatenate([jnp.concatenate(s, axis=0) if nb > 1 else s[0] for s in slabs], axis=1)
    y_b = _dot(_bf(o_b), wbs_ref[...])

    o_c = _mem_attention(dense("mq") * (MEM_HD ** -0.5), [mk_ref.at[bi] for bi in range(nb)],
                         [mv_ref.at[bi] for bi in range(nb)], fill)
    y_c = _dot(_bf(o_c), wbm_ref[...])

    gated = dense("gate0") * y_a + dense("gate1") * y_b + dense("gate2") * y_c
    out = _merge_ln1(x, gated, wo_ref, ln_g_ref, ln_b_ref, alpha)
    out_ref[...] = out.reshape(nb, tq, D_MODEL)


_GELU_K1 = -2.0 * float(np.log2(np.e)) * float(np.sqrt(2.0 / np.pi))
_GELU_K2 = _GELU_K1 * 0.044715


def _conv_geglu(u2, u1, u0, cw_ref, cb_ref):
    c = cb_ref[...] + u2 * cw_ref[0:1, :] + u1 * cw_ref[1:2, :] + u0 * cw_ref[2:3, :]
    g, v = c[:, :D_FF], c[:, D_FF:]
    return g * v / (1.0 + jnp.exp2(g * (g * g * _GELU_K2 + _GELU_K1)))


def _prompt_ffn_kernel(x_ref, wup_ref, cw_ref, cb_ref, wdn_ref, ln_g_ref, ln_b_ref,
                       out_ref, conv_out_ref, u_scr, *, rows, alpha):
    nj = rows // SUBLANES
    base = 2 * SUBLANES
    last = slice(base + rows - SUBLANES, base + rows)
    last2 = slice(base + rows - 2 * SUBLANES, base + rows - SUBLANES)

    @pl.when(pl.program_id(1) == 0)
    def _():
        u_scr[base + rows - 2 * SUBLANES:base + rows, :] = jnp.zeros((2 * SUBLANES, 2 * D_FF), _F32)

    prev_last, prev_last2 = u_scr[last, :], u_scr[last2, :]
    x = jnp.swapaxes(x_ref[0].reshape(SUBLANES, nj, D_MODEL), 0, 1).reshape(rows, D_MODEL)
    u_scr[base:base + rows, :] = _dot(_bf(x), wup_ref[...])
    cur_last, cur_last2 = u_scr[last, :], u_scr[last2, :]
    sub = lax.broadcasted_iota(jnp.int32, (SUBLANES, 2 * D_FF), 0)

    def wrap(prev, cur):
        return pltpu.roll(jnp.where(sub == SUBLANES - 1, prev, cur), 1, axis=0)

    u_scr[SUBLANES:base, :] = wrap(prev_last, cur_last)
    u_scr[0:SUBLANES, :] = wrap(prev_last2, cur_last2)
    h = _conv_geglu(u_scr[0:rows, :], u_scr[SUBLANES:SUBLANES + rows, :], u_scr[base:base + rows, :],
                    cw_ref, cb_ref)
    conv_out_ref[0] = jnp.concatenate([cur_last2[SUBLANES - 1:, :], cur_last[SUBLANES - 1:, :]], axis=0)
    f = _dot(_bf(h), wdn_ref[...])
    y = _layer_norm(alpha * x + f, ln_g_ref[...], ln_b_ref[...])
    out_ref[0] = jnp.swapaxes(y.reshape(nj, SUBLANES, D_MODEL), 0, 1).reshape(rows, D_MODEL)


def _sample_ffn_kernel(x_ref, hist_ref, wup_ref, cw_ref, cb_ref, wdn_ref, ln_g_ref, ln_b_ref,
                       out_ref, conv_out_ref, u_scr, *, nb, tq, alpha):
    rows = nb * tq
    base = (CONV_W - 1) * nb
    x = jnp.swapaxes(x_ref[...], 0, 1).reshape(rows, D_MODEL)
    u_scr[0:base, :] = jnp.swapaxes(hist_ref[...], 0, 1).reshape(base, 2 * D_FF)
    u_scr[base:base + rows, :] = _dot(_bf(x), wup_ref[...])
    h = _conv_geglu(u_scr[0:rows, :], u_scr[nb:nb + rows, :], u_scr[base:base + rows, :], cw_ref, cb_ref)
    conv_out_ref[...] = jnp.swapaxes(u_scr[rows:rows + base, :].reshape(CONV_W - 1, nb, 2 * D_FF), 0, 1)
    f = _dot(_bf(h), wdn_ref[...])
    y = _layer_norm(alpha * x + f, ln_g_ref[...], ln_b_ref[...])
    out_ref[...] = jnp.swapaxes(y.reshape(tq, nb, D_MODEL), 0, 1)


def _mem_kv_kernel(m_ref, w_ref, k_ref, v_ref):
    kv = _dot(_bf(m_ref[0]), w_ref[...])
    for h in range(MEM_HEADS):
        k_ref[0, pl.ds(h, MEM_TOKENS, stride=MEM_HEADS), :] = kv[:, MEM_HD * h:MEM_HD * (h + 1)]
        v_ref[0, pl.ds(h, MEM_TOKENS, stride=MEM_HEADS), :] = kv[:, MEM_W + MEM_HD * h:MEM_W + MEM_HD * (h + 1)]


def _const_spec(shape):
    nd = len(shape)
    return pl.BlockSpec(shape, lambda *_: (0,) * nd, pipeline_mode=pl.Buffered(1))


def _rope_tables(pos):
    half = ROPE_DIM // 2
    inv = np.float32(ROPE_THETA) ** (-np.arange(half, dtype=np.float32) / np.float32(half))
    ang = pos.astype(np.float32)[:, None] * inv[None, :]
    cos, sin = np.cos(ang), np.sin(ang)
    T = pos.shape[0]
    ones = np.ones((T, SWA_HD - ROPE_DIM), np.float32)
    zeros = np.zeros((T, SWA_HD - ROPE_DIM), np.float32)
    zh = np.zeros((T, half), np.float32)
    c = np.concatenate([cos, cos, ones], axis=1)
    dn = np.concatenate([-sin, zh, zeros], axis=1)
    up = np.concatenate([zh, sin, zeros], axis=1)
    rep = LANES // SWA_HD
    return tuple(jnp.asarray(np.concatenate([a] * rep, axis=1), dtype=_F32) for a in (c, dn, up))


def _swa_bias_tables():
    r = np.arange(4 * SWA_PAIR)[:, None]
    j = np.arange(WINDOW + SWA_PAIR)[None, :]
    lower = CHUNK * ((r % SWA_PAIR) // CHUNK)
    band = (j >= lower) & (j <= lower + WINDOW + CHUNK - 1)
    first = band & (j >= WINDOW)
    tab = np.where(np.stack([band, first]), 0.0, -np.inf).astype(np.float32)
    return jnp.asarray(tab)


def _split_w_in_kernel(w_ref, a_ref, ga_ref, b_ref):
    w = w_ref[...]
    a_ref[...] = _bf(w[:, :IN_A_COLS])
    ga = w[:, IN_A_COLS:IN_A_COLS + GLA_RANK]
    ga_ref[...] = _bf(jnp.concatenate([ga, jnp.zeros((w.shape[0], LANES - GLA_RANK), w.dtype)], axis=1))
    b_ref[...] = _bf(w[:, IN_A_COLS + GLA_RANK:])


def _split_w_in(w_in):
    rblk = D_MODEL // 8
    cols = w_in.shape[1]
    spec = lambda n: pl.BlockSpec((rblk, n), lambda i: (i, 0))
    return tuple(pl.pallas_call(
        _split_w_in_kernel,
        grid=(D_MODEL // rblk,),
        in_specs=[spec(cols)],
        out_specs=[spec(IN_A_COLS), spec(LANES), spec(IN_B_COLS)],
        out_shape=[jax.ShapeDtypeStruct((D_MODEL, n), _BF) for n in (IN_A_COLS, LANES, IN_B_COLS)],
        compiler_params=_params(("arbitrary",)),
        name="split_w_in",
    )(w_in))


def _mixer_weight_specs():
    return [
        _const_spec((D_MODEL, IN_A_COLS)),
        _const_spec((D_MODEL, LANES)),
        _const_spec((D_MODEL, IN_B_COLS)),
        _const_spec((LANES, GLA_W)),
        _const_spec((1, GLA_W)),
        _const_spec((1, GLA_DV)),
        _const_spec((GLA_W, D_MODEL)),
        _const_spec((SWA_QW, D_MODEL)),
        _const_spec((MEM_W, D_MODEL)),
        _const_spec((1, GATE_W)),
        _const_spec((D_MODEL, D_MODEL)),
        _const_spec((1, D_MODEL)),
        _const_spec((1, D_MODEL)),
    ]


def _ffn_weight_specs():
    return [
        _const_spec((D_MODEL, 2 * D_FF)),
        _const_spec((CONV_W, 2 * D_FF)),
        _const_spec((1, 2 * D_FF)),
        _const_spec((D_FF, D_MODEL)),
        _const_spec((1, D_MODEL)),
        _const_spec((1, D_MODEL)),
    ]


_SMEM_SPEC = pl.BlockSpec(memory_space=pltpu.SMEM)
_MEM_ROWS = MEM_TOKENS * MEM_HEADS


def _params(sem, flags=None):
    return pltpu.CompilerParams(dimension_semantics=sem, vmem_limit_bytes=V7X_VMEM_LIMIT, flags=flags)


def _prompt_layer(x, mem, sinks, mixer_w, ffn_w, w_mem_kv, alpha):
    B, T, _ = x.shape
    rows = min(PROMPT_ROWS, T)
    nt = T // rows
    assert T % rows == 0 and rows % SWA_PAIR == 0 and rows >= WINDOW
    f32 = jnp.float32
    mk, mv = pl.pallas_call(
        _mem_kv_kernel,
        grid=(B,),
        in_specs=[pl.BlockSpec((1, MEM_TOKENS, D_MODEL), lambda b: (b, 0, 0)),
                  _const_spec((D_MODEL, 2 * MEM_W))],
        out_specs=[pl.BlockSpec((1, _MEM_ROWS, MEM_HD), lambda b: (b, 0, 0))] * 2,
        out_shape=[jax.ShapeDtypeStruct((B, _MEM_ROWS, MEM_HD), f32)] * 2,
        compiler_params=_params(("arbitrary",)),
        name="mem_kv",
    )(mem, w_mem_kv)

    rope = _rope_tables(np.arange(T))
    rope_spec = pl.BlockSpec((rows, LANES), lambda b, t: (t, 0))
    per_batch = lambda *blk: pl.BlockSpec((1,) + blk, lambda b, t: (b,) + (0,) * len(blk))
    x1, gla, swak, swav = pl.pallas_call(
        functools.partial(_prompt_mixer_kernel, rows=rows, alpha=alpha),
        grid=(B, nt),
        in_specs=[_SMEM_SPEC,
                  pl.BlockSpec((1, rows, D_MODEL), lambda b, t: (b, t, 0)),
                  rope_spec, rope_spec, rope_spec,
                  _const_spec((2, 4 * SWA_PAIR, WINDOW + SWA_PAIR)),
                  per_batch(_MEM_ROWS, MEM_HD), per_batch(_MEM_ROWS, MEM_HD)] + _mixer_weight_specs(),
        out_specs=[pl.BlockSpec((1, rows, D_MODEL), lambda b, t: (b, t, 0)),
                   per_batch(GLA_HEADS, GLA_DK, GLA_DV),
                   per_batch(WINDOW, SWA_KW), per_batch(WINDOW, SWA_KW)],
        out_shape=[jax.ShapeDtypeStruct((B, T, D_MODEL), f32),
                   jax.ShapeDtypeStruct((B, GLA_HEADS, GLA_DK, GLA_DV), f32),
                   jax.ShapeDtypeStruct((B, WINDOW, SWA_KW), f32),
                   jax.ShapeDtypeStruct((B, WINDOW, SWA_KW), f32)],
        scratch_shapes=[pltpu.VMEM((GLA_HEADS, GLA_DK, GLA_DV), f32),
                        pltpu.VMEM((SWA_KV_HEADS, WINDOW + rows, LANES), _BF),
                        pltpu.VMEM((2 * SWA_KV_HEADS, WINDOW + rows, LANES), _BF)],
        compiler_params=_params(("arbitrary", "arbitrary")),
        name="prompt_mixer",
    )(sinks, x, *rope, _swa_bias_tables(), mk, mv, *mixer_w)

    frows = min(PROMPT_FFN_ROWS, T)
    assert T % frows == 0 and frows >= CONV_W - 1
    y, conv = pl.pallas_call(
        functools.partial(_prompt_ffn_kernel, rows=frows, alpha=alpha),
        grid=(B, T // frows),
        in_specs=[pl.BlockSpec((1, frows, D_MODEL), lambda b, t: (b, t, 0))] + _ffn_weight_specs(),
        out_specs=[pl.BlockSpec((1, frows, D_MODEL), lambda b, t: (b, t, 0)),
                   per_batch(CONV_W - 1, 2 * D_FF)],
        out_shape=[jax.ShapeDtypeStruct((B, T, D_MODEL), f32),
                   jax.ShapeDtypeStruct((B, CONV_W - 1, 2 * D_FF), f32)],
        scratch_shapes=[pltpu.VMEM((2 * SUBLANES + frows, 2 * D_FF), f32)],
        compiler_params=_params(("arbitrary", "arbitrary")),
        name="prompt_ffn",
    )(x1, *ffn_w)
    return y, gla, swak, swav, mk, mv, conv


def _sample_layer(x, gla0, swak0, swav0, memk, memv, conv0, sinks, mixer_w, ffn_w, alpha):
    B, tq, _ = x.shape
    nb = max(1, min(B, SAMPLE_ROWS // tq))
    assert B % nb == 0 and tq % 16 == 0 and tq >= CONV_W - 1
    f32 = jnp.float32
    rope = _rope_tables(PAST_LEN + np.arange(tq))
    rope_spec = pl.BlockSpec((tq, LANES), lambda i: (0, 0))
    blk = lambda *s: pl.BlockSpec((nb,) + s, lambda i: (i,) + (0,) * len(s))
    blk1 = lambda *s: pl.BlockSpec((nb,) + s, lambda i: (i,) + (0,) * len(s), pipeline_mode=pl.Buffered(1))
    x1, gla, swak, swav = pl.pallas_call(
        functools.partial(_sample_mixer_kernel, nb=nb, tq=tq, alpha=alpha),
        grid=(B // nb,),
        in_specs=[_SMEM_SPEC, blk(tq, D_MODEL), rope_spec, rope_spec, rope_spec,
                  blk(_MEM_ROWS, MEM_HD), blk(_MEM_ROWS, MEM_HD),
                  blk1(GLA_HEADS, GLA_DK, GLA_DV), blk1(WINDOW, SWA_KW), blk1(WINDOW, SWA_KW)]
                 + _mixer_weight_specs(),
        out_specs=[blk(tq, D_MODEL), blk(GLA_HEADS, GLA_DK, GLA_DV), blk(WINDOW, SWA_KW), blk(WINDOW, SWA_KW)],
        out_shape=[jax.ShapeDtypeStruct((B, tq, D_MODEL), f32),
                   jax.ShapeDtypeStruct((B, GLA_HEADS, GLA_DK, GLA_DV), f32),
                   jax.ShapeDtypeStruct((B, WINDOW, SWA_KW), f32),
                   jax.ShapeDtypeStruct((B, WINDOW, SWA_KW), f32)],
        compiler_params=_params(("arbitrary",)),
        name="sample_mixer",
    )(sinks, x, *rope, memk, memv, gla0, swak0, swav0, *mixer_w)

    assert B % SUBLANES == 0
    fblk = lambda *s: pl.BlockSpec((SUBLANES,) + s, lambda i: (i,) + (0,) * len(s))
    y, conv = pl.pallas_call(
        functools.partial(_sample_ffn_kernel, nb=SUBLANES, tq=tq, alpha=alpha),
        grid=(B // SUBLANES,),
        in_specs=[fblk(tq, D_MODEL), fblk(CONV_W - 1, 2 * D_FF)] + _ffn_weight_specs(),
        out_specs=[fblk(tq, D_MODEL), fblk(CONV_W - 1, 2 * D_FF)],
        out_shape=[jax.ShapeDtypeStruct((B, tq, D_MODEL), f32),
                   jax.ShapeDtypeStruct((B, CONV_W - 1, 2 * D_FF), f32)],
        scratch_shapes=[pltpu.VMEM(((CONV_W - 1 + tq) * SUBLANES, 2 * D_FF), f32)],
        compiler_params=_params(("arbitrary",)),
        name="sample_ffn",
    )(x1, conv0, *ffn_w)
    return y, gla, swak, swav, conv


def kernel(x_prompt, x_sample, cache_swa_k, cache_swa_v, state_gla, cache_mem_k, cache_mem_v, cache_ffn_conv, mem_prompt, ln1_g, ln1_b, ln2_g, ln2_b, w_in, b_gate, w_gla_a2, b_gla_a, gla_norm_g, swa_sinks, w_mem_kv, w_br_gla, w_br_swa, w_br_mem, w_o, w_up, conv_w, conv_b, w_down):
    depth = w_in.shape[0]
    alpha = float((2 * depth) ** 0.25)
    Bp = x_prompt.shape[0]
    Bs = x_sample.shape[0]
    hp, hs = x_prompt, x_sample
    outs = [[] for _ in range(10)]
    row = lambda a: a.reshape(1, -1)
    for l in range(depth):
        w2 = jnp.pad(w_gla_a2[l], ((0, LANES - GLA_RANK), (0, 0)))
        mixer_w = _split_w_in(w_in[l]) + (
            _bf(w2), row(b_gla_a[l]), row(gla_norm_g[l]),
            _bf(w_br_gla[l]), _bf(w_br_swa[l]), _bf(w_br_mem[l]), row(b_gate[l]), _bf(w_o[l]),
            row(ln1_g[l]), row(ln1_b[l]))
        ffn_w = (_bf(w_up[l]), conv_w[l], row(conv_b[l]), _bf(w_down[l]), row(ln2_g[l]), row(ln2_b[l]))
        sinks = swa_sinks[l]
        hp, g_p, k_p, v_p, mk_p, mv_p, c_p = _prompt_layer(
            hp, mem_prompt, sinks, mixer_w, ffn_w, _bf(w_mem_kv[l]), alpha)
        hs, g_s, k_s, v_s, c_s = _sample_layer(
            hs, state_gla[l],
            cache_swa_k[l].reshape(Bs, WINDOW, SWA_KW), cache_swa_v[l].reshape(Bs, WINDOW, SWA_KW),
            cache_mem_k[l].reshape(Bs, _MEM_ROWS, MEM_HD), cache_mem_v[l].reshape(Bs, _MEM_ROWS, MEM_HD),
            cache_ffn_conv[l], sinks, mixer_w, ffn_w, alpha)
        kv5 = lambda a, b: a.reshape(b, WINDOW, SWA_KV_HEADS, SWA_HD)
        m5 = lambda a: a.reshape(Bp, MEM_TOKENS, MEM_HEADS, MEM_HD)
        for lst, val in zip(outs, (kv5(k_p, Bp), kv5(v_p, Bp), g_p, m5(mk_p), m5(mv_p), c_p,
                                   kv5(k_s, Bs), kv5(v_s, Bs), g_s, c_s)):
            lst.append(val)
    return (hp, hs) + tuple(jnp.stack(o) for o in outs)
```

```python
import functools

import jax
import jax.numpy as jnp
import numpy as np
from jax import lax
from jax.experimental import pallas as pl
from jax.experimental.pallas import tpu as pltpu

D_MODEL = 1024
CHUNK = 64
GLA_HEADS = 4
GLA_DK = 128
GLA_DV = 128
GLA_RANK = 16
GLA_TAU = 16.0
SWA_HEADS = 8
SWA_KV_HEADS = 2
SWA_HD = 64
WINDOW = 128
ROPE_DIM = 16
ROPE_THETA = 500000.0
MEM_TOKENS = 256
MEM_HEADS = 4
MEM_HD = 128
D_FF = 2816
CONV_W = 3
N_BRANCH = 3
PAST_LEN = 2048
LN_EPS = 1e-5

LANES = 128
SUBLANES = 8
V7X_VMEM_LIMIT = 56 * 1024 * 1024
PROMPT_ROWS = 256
PROMPT_FFN_ROWS = 512
SAMPLE_ROWS = 256
FILL_COLS = 1024

GLA_W = GLA_HEADS * GLA_DK
SWA_QW = SWA_HEADS * SWA_HD
SWA_KW = SWA_KV_HEADS * SWA_HD
MEM_W = MEM_HEADS * MEM_HD
GATE_W = N_BRANCH * D_MODEL
SWA_PAIR = 2 * CHUNK

OFF_GQ = 0
OFF_GK = OFF_GQ + GLA_W
OFF_GV = OFF_GK + GLA_W
OFF_GR = OFF_GV + GLA_W
IN_A_COLS = OFF_GR + GLA_W
OFF_SQ = 0
OFF_SK = OFF_SQ + SWA_QW
OFF_SV = OFF_SK + SWA_KW
OFF_MQ = OFF_SV + SWA_KW
OFF_GL = OFF_MQ + MEM_W
IN_B_COLS = OFF_GL + GATE_W

_BF = jnp.bfloat16
_F32 = jnp.float32


def _bf(x):
    return x.astype(_BF)


def _dot(a, b):
    return jnp.dot(a, b, preferred_element_type=_F32)


def _dot_nt(a, b):
    return lax.dot_general(a, b, (((1,), (1,)), ((), ())), preferred_element_type=_F32)


def _layer_norm(h, g, b):
    mu = jnp.mean(h, axis=-1, keepdims=True)
    d = h - mu
    var = jnp.mean(d * d, axis=-1, keepdims=True)
    return d * lax.rsqrt(var + LN_EPS) * g + b


def _log_sigmoid(x):
    return -(jnp.maximum(-x, 0.0) + jnp.log(1.0 + jnp.exp(-jnp.abs(x))))


def _rope(a, c, s_dn, s_up):
    outs = []
    for j in range(a.shape[1] // LANES):
        slab = a[:, LANES * j:LANES * (j + 1)]
        fwd = pltpu.roll(slab, LANES - ROPE_DIM // 2, axis=1)
        bwd = pltpu.roll(slab, ROPE_DIM // 2, axis=1)
        outs.append(slab * c + fwd * s_dn + bwd * s_up)
    return outs[0] if len(outs) == 1 else jnp.concatenate(outs, axis=1)


class _DenseQueue:
    def __init__(self):
        self.jobs = []

    def add(self, fn):
        self.jobs.append(fn)

    def run(self, n):
        for _ in range(min(n, len(self.jobs))):
            self.jobs.pop(0)()

    def drain(self):
        self.run(len(self.jobs))


def _gla_tile(q, k, v, log_a, chunk, nchunk, state_in, state_out, chained, fill):
    R = chunk * nchunk
    row = lax.broadcasted_iota(jnp.int32, (R, R), 0)
    col = lax.broadcasted_iota(jnp.int32, (R, R), 1)
    tril = (row // chunk == col // chunk) & (col <= row)
    ltri = jnp.where(tril, 1.0, 0.0).astype(_BF)
    hi = _bf(log_a)
    lo = _bf(log_a - hi.astype(_F32))
    b = _dot(ltri, hi) + _dot(ltri, lo)
    fill("pre")
    b_last = [b[chunk * (c + 1) - 1:chunk * (c + 1), :] for c in range(nchunk)]
    bl_rows = jnp.concatenate([jnp.broadcast_to(bl, (chunk, GLA_W)) for bl in b_last], axis=0) \
        if nchunk > 1 else jnp.broadcast_to(b_last[0], (chunk, GLA_W))
    q_dec = _bf(q * (GLA_DK ** -0.5) * jnp.exp(b))
    k_inv = _bf(k * jnp.exp(-b))
    k_tail = _bf(k * jnp.exp(bl_rows - b))
    vb = _bf(v)
    outs = []
    for h in range(GLA_HEADS):
        hs = slice(GLA_DK * h, GLA_DK * (h + 1))
        att = jnp.where(tril, _dot_nt(q_dec[:, hs], k_inv[:, hs]), 0.0)
        o_intra = _dot(_bf(att), vb[:, hs])
        o_inter = []
        S = None
        for c in range(nchunk):
            rs = slice(chunk * c, chunk * (c + 1))
            if c == 0 or not chained:
                S = state_in(c, h)
            o_inter.append(_dot(q_dec[rs, hs], _bf(S)))
            decay = jnp.exp(b_last[c][:, hs])
            dcol = jnp.transpose(jnp.broadcast_to(decay, (GLA_DV, GLA_DK)))
            ds = _dot(jnp.transpose(k_tail[rs, hs].astype(_F32)).astype(_BF), vb[rs, hs])
            S = S * dcol + ds
            if c == nchunk - 1 or not chained:
                state_out(c, h, S)
        o_inter = jnp.concatenate(o_inter, axis=0) if nchunk > 1 else o_inter[0]
        outs.append(o_intra + o_inter)
        fill("head")
    return jnp.concatenate(outs, axis=1)


def _gla_gate(o, gr, g):
    outs = []
    for h in range(GLA_HEADS):
        hs = slice(GLA_DV * h, GLA_DV * (h + 1))
        oh = o[:, hs]
        ms = jnp.mean(oh * oh, axis=-1, keepdims=True)
        grh = gr[:, hs]
        outs.append(oh * lax.rsqrt(ms + LN_EPS) * g * (grh * jax.nn.sigmoid(grh)))
    return jnp.concatenate(outs, axis=1)


def _lane_half_masks(shape):
    lane = lax.broadcasted_iota(jnp.int32, shape, 1)
    return lane < SWA_HD, lane >= SWA_HD


def _swa_split_q(q):
    lo_m, hi_m = _lane_half_masks((q.shape[0], LANES))
    out = []
    for j in range(SWA_QW // LANES):
        slab = q[:, LANES * j:LANES * (j + 1)]
        out.append((_bf(jnp.where(lo_m, slab, 0.0)), _bf(jnp.where(hi_m, slab, 0.0))))
    return out


def _swa_dup_k(k):
    lo_m, _ = _lane_half_masks(k.shape)
    kr = pltpu.roll(k, SWA_HD, axis=1)
    return [_bf(jnp.where(lo_m, k, kr)), _bf(jnp.where(lo_m, kr, k))]


def _swa_place_v(v):
    lo_m, hi_m = _lane_half_masks(v.shape)
    vr = pltpu.roll(v, SWA_HD, axis=1)
    return [(_bf(jnp.where(lo_m, v, 0.0)), _bf(jnp.where(hi_m, vr, 0.0))),
            (_bf(jnp.where(lo_m, vr, 0.0)), _bf(jnp.where(hi_m, v, 0.0)))]


def _swa_blocks(qrows, kks, vlos, vhis, sink_col, biases, tq, fill):
    n = len(kks)
    stack = lambda xs: jnp.concatenate(xs, axis=0) if n > 1 else xs[0]
    s = stack([_dot_nt(qrows[i], kks[i]) for i in range(n)])
    fill("block")
    if biases is not None:
        s = s + stack(biases)
    sink = stack([sink_col] * n)
    m = jnp.maximum(jnp.max(s, axis=-1, keepdims=True), sink)
    p = jnp.exp(s - m)
    den = jnp.sum(p, axis=-1, keepdims=True) + jnp.exp(sink - m)
    p = _bf(p / den)
    outs = []
    for i in range(n):
        r = 4 * tq * i
        outs.append((_dot(p[r:r + tq], vlos[i]) + _dot(p[r + tq:r + 2 * tq], vhis[i]),
                     _dot(p[r + 2 * tq:r + 3 * tq], vlos[i]) + _dot(p[r + 3 * tq:r + 4 * tq], vhis[i])))
    return outs


def _sink_col(sinks_ref, g, tq):
    r = lax.broadcasted_iota(jnp.int32, (4 * tq, 1), 0)
    base = 4 * g
    return jnp.where(r < tq, sinks_ref[base],
                     jnp.where(r < 2 * tq, sinks_ref[base + 1],
                               jnp.where(r < 3 * tq, sinks_ref[base + 2], sinks_ref[base + 3])))


def _mem_head(ref, h):
    return _bf(ref[pl.ds(h, MEM_TOKENS, stride=MEM_HEADS), :])


def _mem_attention(q, mk_refs, mv_refs, fill):
    qb = _bf(q)
    nseg = len(mk_refs)
    tq = q.shape[0] // nseg
    outs = []
    for h in range(MEM_HEADS):
        hs = slice(MEM_HD * h, MEM_HD * (h + 1))
        s = [_dot_nt(qb[tq * i:tq * (i + 1), hs], _mem_head(mk_refs[i], h)) for i in range(nseg)]
        s = jnp.concatenate(s, axis=0) if nseg > 1 else s[0]
        fill("head")
        m = jnp.max(s, axis=-1, keepdims=True)
        p = jnp.exp(s - m)
        p = _bf(p / jnp.sum(p, axis=-1, keepdims=True))
        o = [_dot(p[tq * i:tq * (i + 1)], _mem_head(mv_refs[i], h)) for i in range(nseg)]
        outs.append(jnp.concatenate(o, axis=0) if nseg > 1 else o[0])
    return jnp.concatenate(outs, axis=1)


def _queue_dense(xb, w_ina_ref, w_inb_ref, bgate_ref):
    dq = _DenseQueue()
    parts, want = {}, {}

    def add(name, ref, off, width, post):
        want[name] = -(-width // FILL_COLS)
        parts[name] = []
        for j in range(0, width, FILL_COLS):
            w = min(FILL_COLS, width - j)
            dq.add(lambda name=name, ref=ref, c0=off + j, w=w, j=j:
                   parts[name].append(post(_dot(xb, ref[:, c0:c0 + w]), j, w)))

    ident = lambda y, j, w: y

    def add_gate(i):
        add("gate%d" % i, w_inb_ref, OFF_GL + D_MODEL * i, D_MODEL,
            lambda y, j, w: jax.nn.sigmoid(y + bgate_ref[:, D_MODEL * i + j:D_MODEL * i + j + w]))

    add_gate(0)
    add("gr", w_ina_ref, OFF_GR, GLA_W, ident)
    add_gate(1)
    add("sq", w_inb_ref, OFF_SQ, SWA_QW, ident)
    add("sk", w_inb_ref, OFF_SK, SWA_KW, ident)
    add("sv", w_inb_ref, OFF_SV, SWA_KW, ident)
    add_gate(2)
    add("mq", w_inb_ref, OFF_MQ, MEM_W, ident)

    def get(name):
        while len(parts[name]) < want[name]:
            dq.run(1)
        ps = parts[name]
        return ps[0] if len(ps) == 1 else jnp.concatenate(ps, axis=1)

    return dq, get


PROMPT_FILL = {"pre": 1, "head": 0, "block": 0}
SAMPLE_FILL = {"pre": 1, "head": 0, "block": 0}


def _merge_ln1(x, gated_sum, wo_ref, g_ref, b_ref, alpha):
    mix = _dot(_bf(gated_sum), wo_ref[...])
    return _layer_norm(alpha * x + mix, g_ref[...], b_ref[...])


def _forget_log(xb, w_ga_ref, w2_ref, ba_ref):
    ga = _dot(xb, w_ga_ref[...])
    return _log_sigmoid(_dot(_bf(ga), w2_ref[...]) + ba_ref[...]) * (1.0 / GLA_TAU)


def _prompt_mixer_kernel(sinks_ref, x_ref, rc_ref, rdn_ref, rup_ref, mk_ref, mv_ref, w_ina_ref, w_ga_ref,
                         w_inb_ref, w2_ref, ba_ref, gng_ref, wbg_ref, wbs_ref, wbm_ref, bgate_ref, wo_ref,
                         ln_g_ref, ln_b_ref,
                         out_ref, gla_out_ref, swak_out_ref, swav_out_ref,
                         s_scr, kk_scr, vv_scr, *, rows, alpha):
    t = pl.program_id(1)
    nchunk = rows // CHUNK

    @pl.when(t == 0)
    def _():
        s_scr[...] = jnp.zeros_like(s_scr)
        kk_scr[:, 0:WINDOW, :] = jnp.zeros((SWA_KV_HEADS, WINDOW, LANES), _BF)
        vv_scr[:, 0:WINDOW, :] = jnp.zeros((2 * SWA_KV_HEADS, WINDOW, LANES), _BF)

    x = x_ref[0]
    xb = _bf(x)

    log_a = _forget_log(xb, w_ga_ref, w2_ref, ba_ref)

    def proj_a(off):
        return _dot(xb, w_ina_ref[:, off:off + GLA_W])

    def state_in(c, h):
        return s_scr[h]

    def state_out(c, h, S):
        s_scr[h] = S
        gla_out_ref[0, h] = S

    gq, gk, gv = proj_a(OFF_GQ), proj_a(OFF_GK), proj_a(OFF_GV)
    dq, dense = _queue_dense(xb, w_ina_ref, w_inb_ref, bgate_ref)
    fill = lambda where: dq.run(PROMPT_FILL[where])
    o_a = _gla_tile(gq, gk, gv, log_a, CHUNK, nchunk, state_in, state_out, True, fill)
    o_a = _gla_gate(o_a, dense("gr"), gng_ref[...])
    y_a = _dot(_bf(o_a), wbg_ref[...])

    rc, rdn, rup = rc_ref[...], rdn_ref[...], rup_ref[...]
    q = _rope(dense("sq"), rc, rdn, rup) * (SWA_HD ** -0.5)
    k = _rope(dense("sk"), rc, rdn, rup)
    v = dense("sv")
    swak_out_ref[0] = k[rows - WINDOW:rows, :]
    swav_out_ref[0] = v[rows - WINDOW:rows, :]
    kk = _swa_dup_k(k)
    vv = _swa_place_v(v)
    for g in range(SWA_KV_HEADS):
        kk_scr[g, WINDOW:WINDOW + rows, :] = kk[g]
        vv_scr[2 * g, WINDOW:WINDOW + rows, :] = vv[g][0]
        vv_scr[2 * g + 1, WINDOW:WINDOW + rows, :] = vv[g][1]
    qs = _swa_split_q(q)
    nk = WINDOW + SWA_PAIR
    npair = rows // SWA_PAIR
    slabs = [[None] * npair for _ in range(SWA_QW // LANES)]
    krow = lax.broadcasted_iota(jnp.int32, (4 * SWA_PAIR, nk), 0)
    kcol = lax.broadcasted_iota(jnp.int32, (4 * SWA_PAIR, nk), 1)
    lower = CHUNK * ((krow % SWA_PAIR) // CHUNK)
    band = jnp.where(kcol >= lower, jnp.where(kcol <= lower + (WINDOW + CHUNK - 1), 0.0, -jnp.inf), -jnp.inf)
    biases = [jnp.where(kcol >= WINDOW - t * rows, band, -jnp.inf) if p == 0 else band for p in range(npair)]
    for g in range(SWA_KV_HEADS):
        sink_col = _sink_col(sinks_ref, g, SWA_PAIR)
        for p in range(npair):
            rs = slice(SWA_PAIR * p, SWA_PAIR * (p + 1))
            ks = slice(SWA_PAIR * p, SWA_PAIR * p + nk)
            qrows = jnp.concatenate([qs[2 * g][0][rs], qs[2 * g][1][rs],
                                     qs[2 * g + 1][0][rs], qs[2 * g + 1][1][rs]], axis=0)
            (slabs[2 * g][p], slabs[2 * g + 1][p]), = _swa_blocks(
                [qrows], [kk_scr[g, ks, :]], [vv_scr[2 * g, ks, :]], [vv_scr[2 * g + 1, ks, :]],
                sink_col, [biases[p]], SWA_PAIR, fill)
    o_b = jnp.concatenate([jnp.concatenate(s, axis=0) if npair > 1 else s[0] for s in slabs], axis=1)
    y_b = _dot(_bf(o_b), wbs_ref[...])
    for g in range(SWA_KV_HEADS):
        kk_scr[g, 0:WINDOW, :] = kk[g][rows - WINDOW:rows]
        vv_scr[2 * g, 0:WINDOW, :] = vv[g][0][rows - WINDOW:rows]
        vv_scr[2 * g + 1, 0:WINDOW, :] = vv[g][1][rows - WINDOW:rows]

    o_c = _mem_attention(dense("mq") * (MEM_HD ** -0.5), [mk_ref.at[0]], [mv_ref.at[0]], fill)
    y_c = _dot(_bf(o_c), wbm_ref[...])

    gated = dense("gate0") * y_a + dense("gate1") * y_b + dense("gate2") * y_c
    out_ref[0] = _merge_ln1(x, gated, wo_ref, ln_g_ref, ln_b_ref, alpha)


def _sample_mixer_kernel(sinks_ref, x_ref, rc_ref, rdn_ref, rup_ref, mk_ref, mv_ref, gla_in_ref,
                         swak_in_ref, swav_in_ref, w_ina_ref, w_ga_ref, w_inb_ref, w2_ref,
                         ba_ref, gng_ref, wbg_ref, wbs_ref, wbm_ref, bgate_ref, wo_ref, ln_g_ref, ln_b_ref,
                         out_ref, gla_out_ref, swak_out_ref, swav_out_ref, *, nb, tq, alpha):
    rows = nb * tq
    x = x_ref[...].reshape(rows, D_MODEL)
    xb = _bf(x)

    log_a = _forget_log(xb, w_ga_ref, w2_ref, ba_ref)

    def proj_a(off):
        return _dot(xb, w_ina_ref[:, off:off + GLA_W])

    def state_in(c, h):
        return gla_in_ref[c, h]

    def state_out(c, h, S):
        gla_out_ref[c, h] = S

    gq, gk, gv = proj_a(OFF_GQ), proj_a(OFF_GK), proj_a(OFF_GV)
    dq, dense = _queue_dense(xb, w_ina_ref, w_inb_ref, bgate_ref)
    fill = lambda where: dq.run(SAMPLE_FILL[where])
    o_a = _gla_tile(gq, gk, gv, log_a, tq, nb, state_in, state_out, False, fill)
    o_a = _gla_gate(o_a, dense("gr"), gng_ref[...])
    y_a = _dot(_bf(o_a), wbg_ref[...])

    rc = jnp.concatenate([rc_ref[...]] * nb, axis=0)
    rdn = jnp.concatenate([rdn_ref[...]] * nb, axis=0)
    rup = jnp.concatenate([rup_ref[...]] * nb, axis=0)
    q = _rope(dense("sq"), rc, rdn, rup) * (SWA_HD ** -0.5)
    k = _rope(dense("sk"), rc, rdn, rup)
    v = dense("sv")
    qs = _swa_split_q(q)
    slabs = [[None] * nb for _ in range(SWA_QW // LANES)]
    kks, vvs = [], []
    for bi in range(nb):
        rs = slice(tq * bi, tq * (bi + 1))
        k_all = jnp.concatenate([swak_in_ref[bi], k[rs]], axis=0)
        v_all = jnp.concatenate([swav_in_ref[bi], v[rs]], axis=0)
        swak_out_ref[bi] = k_all[tq:tq + WINDOW]
        swav_out_ref[bi] = v_all[tq:tq + WINDOW]
        kks.append(_swa_dup_k(k_all))
        vvs.append(_swa_place_v(v_all))
    for g in range(SWA_KV_HEADS):
        qrows = []
        for bi in range(nb):
            rs = slice(tq * bi, tq * (bi + 1))
            qrows.append(jnp.concatenate([qs[2 * g][0][rs], qs[2 * g][1][rs],
                                          qs[2 * g + 1][0][rs], qs[2 * g + 1][1][rs]], axis=0))
        outs = _swa_blocks(qrows, [kk[g] for kk in kks], [vv[g][0] for vv in vvs], [vv[g][1] for vv in vvs],
                           _sink_col(sinks_ref, g, tq), None, tq, fill)
        for bi in range(nb):
            slabs[2 * g][bi], slabs[2 * g + 1][bi] = outs[bi]
    o_b = jnp.concatenate([jnp.concatenate(s, axis=0) if nb > 1 else s[0] for s in slabs], axis=1)
    y_b = _dot(_bf(o_b), wbs_ref[...])

    o_c = _mem_attention(dense("mq") * (MEM_HD ** -0.5), [mk_ref.at[bi] for bi in range(nb)],
                         [mv_ref.at[bi] for bi in range(nb)], fill)
    y_c = _dot(_bf(o_c), wbm_ref[...])

    gated = dense("gate0") * y_a + dense("gate1") * y_b + dense("gate2") * y_c
    out = _merge_ln1(x, gated, wo_ref, ln_g_ref, ln_b_ref, alpha)
    out_ref[...] = out.reshape(nb, tq, D_MODEL)


_GELU_K1 = -2.0 * float(np.log2(np.e)) * float(np.sqrt(2.0 / np.pi))
_GELU_K2 = _GELU_K1 * 0.044715


def _conv_geglu(u2, u1, u0, cw_ref, cb_ref):
    c = cb_ref[...] + u2 * cw_ref[0:1, :] + u1 * cw_ref[1:2, :] + u0 * cw_ref[2:3, :]
    g, v = c[:, :D_FF], c[:, D_FF:]
    return g * v / (1.0 + jnp.exp2(g * (g * g * _GELU_K2 + _GELU_K1)))


def _prompt_ffn_kernel(x_ref, wup_ref, cw_ref, cb_ref, wdn_ref, ln_g_ref, ln_b_ref,
                       out_ref, conv_out_ref, u_scr, *, rows, alpha):
    nj = rows // SUBLANES
    base = 2 * SUBLANES
    last = slice(base + rows - SUBLANES, base + rows)
    last2 = slice(base + rows - 2 * SUBLANES, base + rows - SUBLANES)

    @pl.when(pl.program_id(1) == 0)
    def _():
        u_scr[base + rows - 2 * SUBLANES:base + rows, :] = jnp.zeros((2 * SUBLANES, 2 * D_FF), _F32)

    prev_last, prev_last2 = u_scr[last, :], u_scr[last2, :]
    x = jnp.swapaxes(x_ref[0].reshape(SUBLANES, nj, D_MODEL), 0, 1).reshape(rows, D_MODEL)
    u_scr[base:base + rows, :] = _dot(_bf(x), wup_ref[...])
    cur_last, cur_last2 = u_scr[last, :], u_scr[last2, :]
    sub = lax.broadcasted_iota(jnp.int32, (SUBLANES, 2 * D_FF), 0)

    def wrap(prev, cur):
        return pltpu.roll(jnp.where(sub == SUBLANES - 1, prev, cur), 1, axis=0)

    u_scr[SUBLANES:base, :] = wrap(prev_last, cur_last)
    u_scr[0:SUBLANES, :] = wrap(prev_last2, cur_last2)
    h = _conv_geglu(u_scr[0:rows, :], u_scr[SUBLANES:SUBLANES + rows, :], u_scr[base:base + rows, :],
                    cw_ref, cb_ref)
    conv_out_ref[0] = jnp.concatenate([cur_last2[SUBLANES - 1:, :], cur_last[SUBLANES - 1:, :]], axis=0)
    f = _dot(_bf(h), wdn_ref[...])
    y = _layer_norm(alpha * x + f, ln_g_ref[...], ln_b_ref[...])
    out_ref[0] = jnp.swapaxes(y.reshape(nj, SUBLANES, D_MODEL), 0, 1).reshape(rows, D_MODEL)


def _sample_ffn_kernel(x_ref, hist_ref, wup_ref, cw_ref, cb_ref, wdn_ref, ln_g_ref, ln_b_ref,
                       out_ref, conv_out_ref, u_scr, *, nb, tq, alpha):
    rows = nb * tq
    base = (CONV_W - 1) * nb
    x = jnp.swapaxes(x_ref[...], 0, 1).reshape(rows, D_MODEL)
    u_scr[0:base, :] = jnp.swapaxes(hist_ref[...], 0, 1).reshape(base, 2 * D_FF)
    u_scr[base:base + rows, :] = _dot(_bf(x), wup_ref[...])
    h = _conv_geglu(u_scr[0:rows, :], u_scr[nb:nb + rows, :], u_scr[base:base + rows, :], cw_ref, cb_ref)
    conv_out_ref[...] = jnp.swapaxes(u_scr[rows:rows + base, :].reshape(CONV_W - 1, nb, 2 * D_FF), 0, 1)
    f = _dot(_bf(h), wdn_ref[...])
    y = _layer_norm(alpha * x + f, ln_g_ref[...], ln_b_ref[...])
    out_ref[...] = jnp.swapaxes(y.reshape(tq, nb, D_MODEL), 0, 1)


def _mem_kv_kernel(m_ref, w_ref, k_ref, v_ref):
    kv = _dot(_bf(m_ref[0]), w_ref[...])
    for h in range(MEM_HEADS):
        k_ref[0, pl.ds(h, MEM_TOKENS, stride=MEM_HEADS), :] = kv[:, MEM_HD * h:MEM_HD * (h + 1)]
        v_ref[0, pl.ds(h, MEM_TOKENS, stride=MEM_HEADS), :] = kv[:, MEM_W + MEM_HD * h:MEM_W + MEM_HD * (h + 1)]


def _const_spec(shape):
    nd = len(shape)
    return pl.BlockSpec(shape, lambda *_: (0,) * nd, pipeline_mode=pl.Buffered(1))


def _rope_tables(pos):
    half = ROPE_DIM // 2
    inv = np.float32(ROPE_THETA) ** (-np.arange(half, dtype=np.float32) / np.float32(half))
    ang = pos.astype(np.float32)[:, None] * inv[None, :]
    cos, sin = np.cos(ang), np.sin(ang)
    T = pos.shape[0]
    ones = np.ones((T, SWA_HD - ROPE_DIM), np.float32)
    zeros = np.zeros((T, SWA_HD - ROPE_DIM), np.float32)
    zh = np.zeros((T, half), np.float32)
    c = np.concatenate([cos, cos, ones], axis=1)
    dn = np.concatenate([-sin, zh, zeros], axis=1)
    up = np.concatenate([zh, sin, zeros], axis=1)
    rep = LANES // SWA_HD
    return tuple(jnp.asarray(np.concatenate([a] * rep, axis=1), dtype=_F32) for a in (c, dn, up))


def _pack_cols_kernel(wt_ref, o_ref, *, keep):
    t = jnp.transpose(wt_ref[...])
    if keep < t.shape[1]:
        t = jnp.where(lax.broadcasted_iota(jnp.int32, t.shape, 1) < keep, t, 0.0)
    o_ref[...] = _bf(t)


def _pack_cols(wt, col0, ncols, keep, name):
    blk = min(ncols, 2 * LANES)
    k = wt.shape[1]
    return pl.pallas_call(
        functools.partial(_pack_cols_kernel, keep=keep),
        grid=(ncols // blk,),
        in_specs=[pl.BlockSpec((pl.Element(blk), pl.Element(k)),
                               lambda i: (pl.multiple_of(col0 + blk * i, SUBLANES), 0))],
        out_specs=pl.BlockSpec((k, blk), lambda i: (0, i)),
        out_shape=jax.ShapeDtypeStruct((k, ncols), _BF),
        compiler_params=_params(("arbitrary",)),
        name=name,
    )(wt)


def _split_w_in(w_in):
    wt = jnp.transpose(w_in)
    return (_pack_cols(wt, 0, IN_A_COLS, IN_A_COLS, "pack_w_in_a"),
            _pack_cols(wt, IN_A_COLS, LANES, GLA_RANK, "pack_w_in_ga"),
            _pack_cols(wt, IN_A_COLS + GLA_RANK, IN_B_COLS, IN_B_COLS, "pack_w_in_b"))


def _mixer_weight_specs():
    return [
        _const_spec((D_MODEL, IN_A_COLS)),
        _const_spec((D_MODEL, LANES)),
        _const_spec((D_MODEL, IN_B_COLS)),
        _const_spec((LANES, GLA_W)),
        _const_spec((1, GLA_W)),
        _const_spec((1, GLA_DV)),
        _const_spec((GLA_W, D_MODEL)),
        _const_spec((SWA_QW, D_MODEL)),
        _const_spec((MEM_W, D_MODEL)),
        _const_spec((1, GATE_W)),
        _const_spec((D_MODEL, D_MODEL)),
        _const_spec((1, D_MODEL)),
        _const_spec((1, D_MODEL)),
    ]


def _ffn_weight_specs():
    return [
        _const_spec((D_MODEL, 2 * D_FF)),
        _const_spec((CONV_W, 2 * D_FF)),
        _const_spec((1, 2 * D_FF)),
        _const_spec((D_FF, D_MODEL)),
        _const_spec((1, D_MODEL)),
        _const_spec((1, D_MODEL)),
    ]


_SMEM_SPEC = pl.BlockSpec(memory_space=pltpu.SMEM)
_MEM_ROWS = MEM_TOKENS * MEM_HEADS


def _params(sem, flags=None):
    return pltpu.CompilerParams(dimension_semantics=sem, vmem_limit_bytes=V7X_VMEM_LIMIT, flags=flags)


def _prompt_layer(x, mem, sinks, mixer_w, ffn_w, w_mem_kv, alpha):
    B, T, _ = x.shape
    rows = min(PROMPT_ROWS, T)
    nt = T // rows
    assert T % rows == 0 and rows % SWA_PAIR == 0 and rows >= WINDOW
    f32 = jnp.float32
    mk, mv = pl.pallas_call(
        _mem_kv_kernel,
        grid=(B,),
        in_specs=[pl.BlockSpec((1, MEM_TOKENS, D_MODEL), lambda b: (b, 0, 0)),
                  _const_spec((D_MODEL, 2 * MEM_W))],
        out_specs=[pl.BlockSpec((1, _MEM_ROWS, MEM_HD), lambda b: (b, 0, 0))] * 2,
        out_shape=[jax.ShapeDtypeStruct((B, _MEM_ROWS, MEM_HD), f32)] * 2,
        compiler_params=_params(("arbitrary",)),
        name="mem_kv",
    )(mem, w_mem_kv)

    rope = _rope_tables(np.arange(T))
    rope_spec = pl.BlockSpec((rows, LANES), lambda b, t: (t, 0))
    per_batch = lambda *blk: pl.BlockSpec((1,) + blk, lambda b, t: (b,) + (0,) * len(blk))
    x1, gla, swak, swav = pl.pallas_call(
        functools.partial(_prompt_mixer_kernel, rows=rows, alpha=alpha),
        grid=(B, nt),
        in_specs=[_SMEM_SPEC,
                  pl.BlockSpec((1, rows, D_MODEL), lambda b, t: (b, t, 0)),
                  rope_spec, rope_spec, rope_spec,
                  per_batch(_MEM_ROWS, MEM_HD), per_batch(_MEM_ROWS, MEM_HD)] + _mixer_weight_specs(),
        out_specs=[pl.BlockSpec((1, rows, D_MODEL), lambda b, t: (b, t, 0)),
                   per_batch(GLA_HEADS, GLA_DK, GLA_DV),
                   per_batch(WINDOW, SWA_KW), per_batch(WINDOW, SWA_KW)],
        out_shape=[jax.ShapeDtypeStruct((B, T, D_MODEL), f32),
                   jax.ShapeDtypeStruct((B, GLA_HEADS, GLA_DK, GLA_DV), f32),
                   jax.ShapeDtypeStruct((B, WINDOW, SWA_KW), f32),
                   jax.ShapeDtypeStruct((B, WINDOW, SWA_KW), f32)],
        scratch_shapes=[pltpu.VMEM((GLA_HEADS, GLA_DK, GLA_DV), f32),
                        pltpu.VMEM((SWA_KV_HEADS, WINDOW + rows, LANES), _BF),
                        pltpu.VMEM((2 * SWA_KV_HEADS, WINDOW + rows, LANES), _BF)],
        compiler_params=_params(("arbitrary", "arbitrary")),
        name="prompt_mixer",
    )(sinks, x, *rope, mk, mv, *mixer_w)

    frows = min(PROMPT_FFN_ROWS, T)
    assert T % frows == 0 and frows >= CONV_W - 1
    y, conv = pl.pallas_call(
        functools.partial(_prompt_ffn_kernel, rows=frows, alpha=alpha),
        grid=(B, T // frows),
        in_specs=[pl.BlockSpec((1, frows, D_MODEL), lambda b, t: (b, t, 0))] + _ffn_weight_specs(),
        out_specs=[pl.BlockSpec((1, frows, D_MODEL), lambda b, t: (b, t, 0)),
                   per_batch(CONV_W - 1, 2 * D_FF)],
        out_shape=[jax.ShapeDtypeStruct((B, T, D_MODEL), f32),
                   jax.ShapeDtypeStruct((B, CONV_W - 1, 2 * D_FF), f32)],
        scratch_shapes=[pltpu.VMEM((2 * SUBLANES + frows, 2 * D_FF), f32)],
        compiler_params=_params(("arbitrary", "arbitrary")),
        name="prompt_ffn",
    )(x1, *ffn_w)
    return y, gla, swak, swav, mk, mv, conv


def _sample_layer(x, gla0, swak0, swav0, memk, memv, conv0, sinks, mixer_w, ffn_w, alpha):
    B, tq, _ = x.shape
    nb = max(1, min(B, SAMPLE_ROWS // tq))
    assert B % nb == 0 and tq % 16 == 0 and tq >= CONV_W - 1
    f32 = jnp.float32
    rope = _rope_tables(PAST_LEN + np.arange(tq))
    rope_spec = pl.BlockSpec((tq, LANES), lambda i: (0, 0))
    blk = lambda *s: pl.BlockSpec((nb,) + s, lambda i: (i,) + (0,) * len(s))
    blk1 = lambda *s: pl.BlockSpec((nb,) + s, lambda i: (i,) + (0,) * len(s), pipeline_mode=pl.Buffered(1))
    x1, gla, swak, swav = pl.pallas_call(
        functools.partial(_sample_mixer_kernel, nb=nb, tq=tq, alpha=alpha),
        grid=(B // nb,),
        in_specs=[_SMEM_SPEC, blk(tq, D_MODEL), rope_spec, rope_spec, rope_spec,
                  blk(_MEM_ROWS, MEM_HD), blk(_MEM_ROWS, MEM_HD),
                  blk1(GLA_HEADS, GLA_DK, GLA_DV), blk1(WINDOW, SWA_KW), blk1(WINDOW, SWA_KW)]
                 + _mixer_weight_specs(),
        out_specs=[blk(tq, D_MODEL), blk(GLA_HEADS, GLA_DK, GLA_DV), blk(WINDOW, SWA_KW), blk(WINDOW, SWA_KW)],
        out_shape=[jax.ShapeDtypeStruct((B, tq, D_MODEL), f32),
                   jax.ShapeDtypeStruct((B, GLA_HEADS, GLA_DK, GLA_DV), f32),
                   jax.ShapeDtypeStruct((B, WINDOW, SWA_KW), f32),
                   jax.ShapeDtypeStruct((B, WINDOW, SWA_KW), f32)],
        compiler_params=_params(("arbitrary",)),
        name="sample_mixer",
    )(sinks, x, *rope, memk, memv, gla0, swak0, swav0, *mixer_w)

    assert B % SUBLANES == 0
    fblk = lambda *s: pl.BlockSpec((SUBLANES,) + s, lambda i: (i,) + (0,) * len(s))
    y, conv = pl.pallas_call(
        functools.partial(_sample_ffn_kernel, nb=SUBLANES, tq=tq, alpha=alpha),
        grid=(B // SUBLANES,),
        in_specs=[fblk(tq, D_MODEL), fblk(CONV_W - 1, 2 * D_FF)] + _ffn_weight_specs(),
        out_specs=[fblk(tq, D_MODEL), fblk(CONV_W - 1, 2 * D_FF)],
        out_shape=[jax.ShapeDtypeStruct((B, tq, D_MODEL), f32),
                   jax.ShapeDtypeStruct((B, CONV_W - 1, 2 * D_FF), f32)],
        scratch_shapes=[pltpu.VMEM(((CONV_W - 1 + tq) * SUBLANES, 2 * D_FF), f32)],
        compiler_params=_params(("arbitrary",)),
        name="sample_ffn",
    )(x1, conv0, *ffn_w)
    return y, gla, swak, swav, conv


def kernel(x_prompt, x_sample, cache_swa_k, cache_swa_v, state_gla, cache_mem_k, cache_mem_v, cache_ffn_conv, mem_prompt, ln1_g, ln1_b, ln2_g, ln2_b, w_in, b_gate, w_gla_a2, b_gla_a, gla_norm_g, swa_sinks, w_mem_kv, w_br_gla, w_br_swa, w_br_mem, w_o, w_up, conv_w, conv_b, w_down):
    depth = w_in.shape[0]
    alpha = float((2 * depth) ** 0.25)
    Bp = x_prompt.shape[0]
    Bs = x_sample.shape[0]
    hp, hs = x_prompt, x_sample
    outs = [[] for _ in range(10)]
    row = lambda a: a.reshape(1, -1)
    for l in range(depth):
        w2 = jnp.pad(w_gla_a2[l], ((0, LANES - GLA_RANK), (0, 0)))
        mixer_w = _split_w_in(w_in[l]) + (
            _bf(w2), row(b_gla_a[l]), row(gla_norm_g[l]),
            _bf(w_br_gla[l]), _bf(w_br_swa[l]), _bf(w_br_mem[l]), row(b_gate[l]), _bf(w_o[l]),
            row(ln1_g[l]), row(ln1_b[l]))
        ffn_w = (_bf(w_up[l]), conv_w[l], row(conv_b[l]), _bf(w_down[l]), row(ln2_g[l]), row(ln2_b[l]))
        sinks = swa_sinks[l]
        hp, g_p, k_p, v_p, mk_p, mv_p, c_p = _prompt_layer(
            hp, mem_prompt, sinks, mixer_w, ffn_w, _bf(w_mem_kv[l]), alpha)
        hs, g_s, k_s, v_s, c_s = _sample_layer(
            hs, state_gla[l],
            cache_swa_k[l].reshape(Bs, WINDOW, SWA_KW), cache_swa_v[l].reshape(Bs, WINDOW, SWA_KW),
            cache_mem_k[l].reshape(Bs, _MEM_ROWS, MEM_HD), cache_mem_v[l].reshape(Bs, _MEM_ROWS, MEM_HD),
            cache_ffn_conv[l], sinks, mixer_w, ffn_w, alpha)
        kv5 = lambda a, b: a.reshape(b, WINDOW, SWA_KV_HEADS, SWA_HD)
        m5 = lambda a: a.reshape(Bp, MEM_TOKENS, MEM_HEADS, MEM_HD)
        for lst, val in zip(outs, (kv5(k_p, Bp), kv5(v_p, Bp), g_p, m5(mk_p), m5(mv_p), c_p,
                                   kv5(k_s, Bs), kv5(v_s, Bs), g_s, c_s)):
            lst.append(val)
    return (hp, hs) + tuple(jnp.stack(o) for o in outs)
```

```python
import functools

import jax
import jax.numpy as jnp
import numpy as np
from jax import lax
from jax.experimental import pallas as pl
from jax.experimental.pallas import tpu as pltpu

D_MODEL = 1024
CHUNK = 64
GLA_HEADS = 4
GLA_DK = 128
GLA_DV = 128
GLA_RANK = 16
GLA_TAU = 16.0
SWA_HEADS = 8
SWA_KV_HEADS = 2
SWA_HD = 64
WINDOW = 128
ROPE_DIM = 16
ROPE_THETA = 500000.0
MEM_TOKENS = 256
MEM_HEADS = 4
MEM_HD = 128
D_FF = 2816
CONV_W = 3
N_BRANCH = 3
PAST_LEN = 2048
LN_EPS = 1e-5

LANES = 128
SUBLANES = 8
V7X_VMEM_LIMIT = 56 * 1024 * 1024
PROMPT_ROWS = 256
PROMPT_FFN_ROWS = 512
FFN_COLS = 256
SAMPLE_ROWS = 256
GLA_W = GLA_HEADS * GLA_DK
SWA_QW = SWA_HEADS * SWA_HD
SWA_KW = SWA_KV_HEADS * SWA_HD
MEM_W = MEM_HEADS * MEM_HD
GATE_W = N_BRANCH * D_MODEL
SWA_PAIR = 2 * CHUNK

OFF_GQ = 0
OFF_GK = OFF_GQ + GLA_W
OFF_GV = OFF_GK + GLA_W
OFF_GR = OFF_GV + GLA_W
IN_A_COLS = OFF_GR + GLA_W
OFF_SQ = 0
OFF_SK = OFF_SQ + SWA_QW
OFF_SV = OFF_SK + SWA_KW
OFF_MQ = OFF_SV + SWA_KW
OFF_GL = OFF_MQ + MEM_W
IN_B_COLS = OFF_GL + GATE_W

_BF = jnp.bfloat16
_F32 = jnp.float32


def _bf(x):
    return x.astype(_BF)


def _dot(a, b):
    return jnp.dot(a, b, preferred_element_type=_F32)


def _dot_nt(a, b):
    return lax.dot_general(a, b, (((1,), (1,)), ((), ())), preferred_element_type=_F32)


def _layer_norm(h, g, b):
    mu = jnp.mean(h, axis=-1, keepdims=True)
    d = h - mu
    var = jnp.mean(d * d, axis=-1, keepdims=True)
    return d * lax.rsqrt(var + LN_EPS) * g + b


def _log_sigmoid(x):
    return -(jnp.maximum(-x, 0.0) + jnp.log(1.0 + jnp.exp(-jnp.abs(x))))


def _rope(a, c, s_dn, s_up):
    outs = []
    for j in range(a.shape[1] // LANES):
        slab = a[:, LANES * j:LANES * (j + 1)]
        fwd = pltpu.roll(slab, LANES - ROPE_DIM // 2, axis=1)
        bwd = pltpu.roll(slab, ROPE_DIM // 2, axis=1)
        outs.append(slab * c + fwd * s_dn + bwd * s_up)
    return outs[0] if len(outs) == 1 else jnp.concatenate(outs, axis=1)


def _gla_tile(q, k, v, log_a, chunk, nchunk, state_in, state_out, chained):
    R = chunk * nchunk
    row = lax.broadcasted_iota(jnp.int32, (R, R), 0)
    col = lax.broadcasted_iota(jnp.int32, (R, R), 1)
    tril = (row // chunk == col // chunk) & (col <= row)
    ltri = jnp.where(tril, 1.0, 0.0).astype(_BF)
    hi = _bf(log_a)
    lo = _bf(log_a - hi.astype(_F32))
    b = _dot(ltri, hi) + _dot(ltri, lo)
    b_last = [b[chunk * (c + 1) - 1:chunk * (c + 1), :] for c in range(nchunk)]
    bl_rows = jnp.concatenate([jnp.broadcast_to(bl, (chunk, GLA_W)) for bl in b_last], axis=0) \
        if nchunk > 1 else jnp.broadcast_to(b_last[0], (chunk, GLA_W))
    q_dec = _bf(q * (GLA_DK ** -0.5) * jnp.exp(b))
    k_inv = _bf(k * jnp.exp(-b))
    k_tail = _bf(k * jnp.exp(bl_rows - b))
    vb = _bf(v)
    outs = []
    for h in range(GLA_HEADS):
        hs = slice(GLA_DK * h, GLA_DK * (h + 1))
        att = jnp.where(tril, _dot_nt(q_dec[:, hs], k_inv[:, hs]), 0.0)
        o_intra = _dot(_bf(att), vb[:, hs])
        o_inter = []
        S = None
        for c in range(nchunk):
            rs = slice(chunk * c, chunk * (c + 1))
            if c == 0 or not chained:
                S = state_in(c, h)
            o_inter.append(_dot(q_dec[rs, hs], _bf(S)))
            decay = jnp.exp(b_last[c][:, hs])
            dcol = jnp.transpose(jnp.broadcast_to(decay, (GLA_DV, GLA_DK)))
            ds = _dot(jnp.transpose(k_tail[rs, hs].astype(_F32)).astype(_BF), vb[rs, hs])
            S = S * dcol + ds
            if c == nchunk - 1 or not chained:
                state_out(c, h, S)
        o_inter = jnp.concatenate(o_inter, axis=0) if nchunk > 1 else o_inter[0]
        outs.append(o_intra + o_inter)
    return jnp.concatenate(outs, axis=1)


def _gla_gate(o, gr, g):
    outs = []
    for h in range(GLA_HEADS):
        hs = slice(GLA_DV * h, GLA_DV * (h + 1))
        oh = o[:, hs]
        ms = jnp.mean(oh * oh, axis=-1, keepdims=True)
        grh = gr[:, hs]
        outs.append(oh * lax.rsqrt(ms + LN_EPS) * g * (grh * jax.nn.sigmoid(grh)))
    return jnp.concatenate(outs, axis=1)


def _lane_half_masks(shape):
    lane = lax.broadcasted_iota(jnp.int32, shape, 1)
    return lane < SWA_HD, lane >= SWA_HD


def _swa_split_q(q):
    lo_m, hi_m = _lane_half_masks((q.shape[0], LANES))
    out = []
    for j in range(SWA_QW // LANES):
        slab = q[:, LANES * j:LANES * (j + 1)]
        out.append((_bf(jnp.where(lo_m, slab, 0.0)), _bf(jnp.where(hi_m, slab, 0.0))))
    return out


def _swa_dup_k(k):
    lo_m, _ = _lane_half_masks(k.shape)
    kr = pltpu.roll(k, SWA_HD, axis=1)
    return [_bf(jnp.where(lo_m, k, kr)), _bf(jnp.where(lo_m, kr, k))]


def _swa_place_v(v):
    lo_m, hi_m = _lane_half_masks(v.shape)
    vr = pltpu.roll(v, SWA_HD, axis=1)
    return [(_bf(jnp.where(lo_m, v, 0.0)), _bf(jnp.where(hi_m, vr, 0.0))),
            (_bf(jnp.where(lo_m, vr, 0.0)), _bf(jnp.where(hi_m, v, 0.0)))]


def _swa_blocks(qrows, kks, vlos, vhis, sink_col, lowers, upper, tq):
    n = len(kks)
    stack = lambda xs: jnp.concatenate(xs, axis=0) if n > 1 else xs[0]
    s = stack([_dot_nt(qrows[i], kks[i]) for i in range(n)])
    if lowers is not None:
        kcol = lax.broadcasted_iota(jnp.int32, s.shape, 1)
        s = jnp.where(kcol >= stack(lowers), jnp.where(kcol <= stack([upper] * n), s, -jnp.inf), -jnp.inf)
    sink = stack([sink_col] * n)
    m = jnp.maximum(jnp.max(s, axis=-1, keepdims=True), sink)
    p = jnp.exp(s - m)
    den = jnp.sum(p, axis=-1, keepdims=True) + jnp.exp(sink - m)
    p = _bf(p / den)
    outs = []
    for i in range(n):
        r = 4 * tq * i
        outs.append((_dot(p[r:r + tq], vlos[i]) + _dot(p[r + tq:r + 2 * tq], vhis[i]),
                     _dot(p[r + 2 * tq:r + 3 * tq], vlos[i]) + _dot(p[r + 3 * tq:r + 4 * tq], vhis[i])))
    return outs


def _sink_col(sinks_ref, g, tq):
    r = lax.broadcasted_iota(jnp.int32, (4 * tq, 1), 0)
    base = 4 * g
    return jnp.where(r < tq, sinks_ref[base],
                     jnp.where(r < 2 * tq, sinks_ref[base + 1],
                               jnp.where(r < 3 * tq, sinks_ref[base + 2], sinks_ref[base + 3])))


def _mem_head(ref, h):
    return _bf(ref[pl.ds(h, MEM_TOKENS, stride=MEM_HEADS), :])


def _mem_attention(q, mk_refs, mv_refs):
    qb = _bf(q)
    nseg = len(mk_refs)
    tq = q.shape[0] // nseg
    outs = []
    for h in range(MEM_HEADS):
        hs = slice(MEM_HD * h, MEM_HD * (h + 1))
        s = [_dot_nt(qb[tq * i:tq * (i + 1), hs], _mem_head(mk_refs[i], h)) for i in range(nseg)]
        s = jnp.concatenate(s, axis=0) if nseg > 1 else s[0]
        m = jnp.max(s, axis=-1, keepdims=True)
        p = jnp.exp(s - m)
        p = _bf(p / jnp.sum(p, axis=-1, keepdims=True))
        o = [_dot(p[tq * i:tq * (i + 1)], _mem_head(mv_refs[i], h)) for i in range(nseg)]
        outs.append(jnp.concatenate(o, axis=0) if nseg > 1 else o[0])
    return jnp.concatenate(outs, axis=1)


def _gate(xb, i, w_inb_ref, bgate_ref):
    gl = _dot(xb, w_inb_ref[:, OFF_GL + D_MODEL * i:OFF_GL + D_MODEL * (i + 1)])
    return jax.nn.sigmoid(gl + bgate_ref[:, D_MODEL * i:D_MODEL * (i + 1)])


def _merge_ln1(x, gated_sum, wo_ref, g_ref, b_ref, alpha):
    mix = _dot(_bf(gated_sum), wo_ref[...])
    return _layer_norm(alpha * x + mix, g_ref[...], b_ref[...])


def _forget_log(xb, w_ga_ref, w2_ref, ba_ref):
    ga = _dot(xb, w_ga_ref[...])
    return _log_sigmoid(_dot(_bf(ga), w2_ref[...]) + ba_ref[...]) * (1.0 / GLA_TAU)


def _prompt_mixer_kernel(sinks_ref, x_ref, rc_ref, rdn_ref, rup_ref, mk_ref, mv_ref, w_ina_ref, w_ga_ref,
                         w_inb_ref, w2_ref, ba_ref, gng_ref, wbg_ref, wbs_ref, wbm_ref, bgate_ref, wo_ref,
                         ln_g_ref, ln_b_ref,
                         out_ref, gla_out_ref, swak_out_ref, swav_out_ref,
                         s_scr, kk_scr, vv_scr, *, rows, alpha):
    t = pl.program_id(1)
    nchunk = rows // CHUNK

    @pl.when(t == 0)
    def _():
        s_scr[...] = jnp.zeros_like(s_scr)
        kk_scr[:, 0:WINDOW, :] = jnp.zeros((SWA_KV_HEADS, WINDOW, LANES), _BF)
        vv_scr[:, 0:WINDOW, :] = jnp.zeros((2 * SWA_KV_HEADS, WINDOW, LANES), _BF)

    x = x_ref[0]
    xb = _bf(x)

    log_a = _forget_log(xb, w_ga_ref, w2_ref, ba_ref)

    def proj_a(off):
        return _dot(xb, w_ina_ref[:, off:off + GLA_W])

    def state_in(c, h):
        return s_scr[h]

    def state_out(c, h, S):
        s_scr[h] = S
        gla_out_ref[0, h] = S

    gq, gk, gv = proj_a(OFF_GQ), proj_a(OFF_GK), proj_a(OFF_GV)
    gate_a = _gate(xb, 0, w_inb_ref, bgate_ref)
    o_a = _gla_tile(gq, gk, gv, log_a, CHUNK, nchunk, state_in, state_out, chained=True)
    o_a = _gla_gate(o_a, proj_a(OFF_GR), gng_ref[...])
    gated = gate_a * _dot(_bf(o_a), wbg_ref[...])

    rc, rdn, rup = rc_ref[...], rdn_ref[...], rup_ref[...]
    q = _rope(_dot(xb, w_inb_ref[:, OFF_SQ:OFF_SQ + SWA_QW]), rc, rdn, rup) * (SWA_HD ** -0.5)
    k = _rope(_dot(xb, w_inb_ref[:, OFF_SK:OFF_SK + SWA_KW]), rc, rdn, rup)
    v = _dot(xb, w_inb_ref[:, OFF_SV:OFF_SV + SWA_KW])
    swak_out_ref[0] = k[rows - WINDOW:rows, :]
    swav_out_ref[0] = v[rows - WINDOW:rows, :]
    gate_b = _gate(xb, 1, w_inb_ref, bgate_ref)
    kk = _swa_dup_k(k)
    vv = _swa_place_v(v)
    for g in range(SWA_KV_HEADS):
        kk_scr[g, WINDOW:WINDOW + rows, :] = kk[g]
        vv_scr[2 * g, WINDOW:WINDOW + rows, :] = vv[g][0]
        vv_scr[2 * g + 1, WINDOW:WINDOW + rows, :] = vv[g][1]
    qs = _swa_split_q(q)
    nk = WINDOW + SWA_PAIR
    npair = rows // SWA_PAIR
    qchunk = (lax.broadcasted_iota(jnp.int32, (4 * SWA_PAIR, 1), 0) % SWA_PAIR) // CHUNK
    upper = CHUNK * qchunk + (WINDOW + CHUNK - 1)
    slabs = [[None] * npair for _ in range(SWA_QW // LANES)]
    lowers = [jnp.maximum(CHUNK * qchunk, WINDOW - (t * rows + SWA_PAIR * p)) for p in range(npair)]
    for g in range(SWA_KV_HEADS):
        sink_col = _sink_col(sinks_ref, g, SWA_PAIR)
        for p in range(npair):
            rs = slice(SWA_PAIR * p, SWA_PAIR * (p + 1))
            ks = slice(SWA_PAIR * p, SWA_PAIR * p + nk)
            qrows = jnp.concatenate([qs[2 * g][0][rs], qs[2 * g][1][rs],
                                     qs[2 * g + 1][0][rs], qs[2 * g + 1][1][rs]], axis=0)
            (slabs[2 * g][p], slabs[2 * g + 1][p]), = _swa_blocks(
                [qrows], [kk_scr[g, ks, :]], [vv_scr[2 * g, ks, :]], [vv_scr[2 * g + 1, ks, :]],
                sink_col, [lowers[p]], upper, SWA_PAIR)
    o_b = jnp.concatenate([jnp.concatenate(s, axis=0) if npair > 1 else s[0] for s in slabs], axis=1)
    gated = gated + gate_b * _dot(_bf(o_b), wbs_ref[...])
    for g in range(SWA_KV_HEADS):
        kk_scr[g, 0:WINDOW, :] = kk[g][rows - WINDOW:rows]
        vv_scr[2 * g, 0:WINDOW, :] = vv[g][0][rows - WINDOW:rows]
        vv_scr[2 * g + 1, 0:WINDOW, :] = vv[g][1][rows - WINDOW:rows]

    qm = _dot(xb, w_inb_ref[:, OFF_MQ:OFF_MQ + MEM_W]) * (MEM_HD ** -0.5)
    gate_c = _gate(xb, 2, w_inb_ref, bgate_ref)
    o_c = _mem_attention(qm, [mk_ref.at[0]], [mv_ref.at[0]])
    gated = gated + gate_c * _dot(_bf(o_c), wbm_ref[...])

    out_ref[0] = _merge_ln1(x, gated, wo_ref, ln_g_ref, ln_b_ref, alpha)


def _sample_mixer_kernel(sinks_ref, x_ref, rc_ref, rdn_ref, rup_ref, mk_ref, mv_ref, gla_in_ref,
                         swak_in_ref, swav_in_ref, w_ina_ref, w_ga_ref, w_inb_ref, w2_ref,
                         ba_ref, gng_ref, wbg_ref, wbs_ref, wbm_ref, bgate_ref, wo_ref, ln_g_ref, ln_b_ref,
                         out_ref, gla_out_ref, swak_out_ref, swav_out_ref, *, nb, tq, alpha):
    rows = nb * tq
    x = x_ref[...].reshape(rows, D_MODEL)
    xb = _bf(x)

    log_a = _forget_log(xb, w_ga_ref, w2_ref, ba_ref)

    def proj_a(off):
        return _dot(xb, w_ina_ref[:, off:off + GLA_W])

    def state_in(c, h):
        return gla_in_ref[c, h]

    def state_out(c, h, S):
        gla_out_ref[c, h] = S

    gq, gk, gv = proj_a(OFF_GQ), proj_a(OFF_GK), proj_a(OFF_GV)
    gate_a = _gate(xb, 0, w_inb_ref, bgate_ref)
    o_a = _gla_tile(gq, gk, gv, log_a, tq, nb, state_in, state_out, chained=False)
    o_a = _gla_gate(o_a, proj_a(OFF_GR), gng_ref[...])
    gated = gate_a * _dot(_bf(o_a), wbg_ref[...])

    rc = jnp.concatenate([rc_ref[...]] * nb, axis=0)
    rdn = jnp.concatenate([rdn_ref[...]] * nb, axis=0)
    rup = jnp.concatenate([rup_ref[...]] * nb, axis=0)
    q = _rope(_dot(xb, w_inb_ref[:, OFF_SQ:OFF_SQ + SWA_QW]), rc, rdn, rup) * (SWA_HD ** -0.5)
    k = _rope(_dot(xb, w_inb_ref[:, OFF_SK:OFF_SK + SWA_KW]), rc, rdn, rup)
    v = _dot(xb, w_inb_ref[:, OFF_SV:OFF_SV + SWA_KW])
    gate_b = _gate(xb, 1, w_inb_ref, bgate_ref)
    qs = _swa_split_q(q)
    slabs = [[None] * nb for _ in range(SWA_QW // LANES)]
    kks, vvs = [], []
    for bi in range(nb):
        rs = slice(tq * bi, tq * (bi + 1))
        k_all = jnp.concatenate([swak_in_ref[bi], k[rs]], axis=0)
        v_all = jnp.concatenate([swav_in_ref[bi], v[rs]], axis=0)
        swak_out_ref[bi] = k_all[tq:tq + WINDOW]
        swav_out_ref[bi] = v_all[tq:tq + WINDOW]
        kks.append(_swa_dup_k(k_all))
        vvs.append(_swa_place_v(v_all))
    for g in range(SWA_KV_HEADS):
        qrows = []
        for bi in range(nb):
            rs = slice(tq * bi, tq * (bi + 1))
            qrows.append(jnp.concatenate([qs[2 * g][0][rs], qs[2 * g][1][rs],
                                          qs[2 * g + 1][0][rs], qs[2 * g + 1][1][rs]], axis=0))
        outs = _swa_blocks(qrows, [kk[g] for kk in kks], [vv[g][0] for vv in vvs], [vv[g][1] for vv in vvs],
                           _sink_col(sinks_ref, g, tq), None, None, tq)
        for bi in range(nb):
            slabs[2 * g][bi], slabs[2 * g + 1][bi] = outs[bi]
    o_b = jnp.concatenate([jnp.concatenate(s, axis=0) if nb > 1 else s[0] for s in slabs], axis=1)
    gated = gated + gate_b * _dot(_bf(o_b), wbs_ref[...])

    qm = _dot(xb, w_inb_ref[:, OFF_MQ:OFF_MQ + MEM_W]) * (MEM_HD ** -0.5)
    gate_c = _gate(xb, 2, w_inb_ref, bgate_ref)
    o_c = _mem_attention(qm, [mk_ref.at[bi] for bi in range(nb)], [mv_ref.at[bi] for bi in range(nb)])
    gated = gated + gate_c * _dot(_bf(o_c), wbm_ref[...])

    out = _merge_ln1(x, gated, wo_ref, ln_g_ref, ln_b_ref, alpha)
    out_ref[...] = out.reshape(nb, tq, D_MODEL)


_GELU_K1 = -2.0 * float(np.log2(np.e)) * float(np.sqrt(2.0 / np.pi))
_GELU_K2 = _GELU_K1 * 0.044715


def _conv(u2, u1, u0, cw_ref, cb_ref, cs):
    return cb_ref[:, cs] + u2 * cw_ref[0:1, cs] + u1 * cw_ref[1:2, cs] + u0 * cw_ref[2:3, cs]


def _geglu(g, v):
    return g * v / (1.0 + jnp.exp2(g * (g * g * _GELU_K2 + _GELU_K1)))


def _prompt_ffn_kernel(x_ref, wup_ref, cw_ref, cb_ref, wdn_ref, ln_g_ref, ln_b_ref,
                       out_ref, conv_out_ref, u_scr, *, rows, alpha):
    nj = rows // SUBLANES
    base = 2 * SUBLANES
    last = slice(base + rows - SUBLANES, base + rows)
    last2 = slice(base + rows - 2 * SUBLANES, base + rows - SUBLANES)

    @pl.when(pl.program_id(1) == 0)
    def _():
        u_scr[base + rows - 2 * SUBLANES:base + rows, :] = jnp.zeros((2 * SUBLANES, 2 * D_FF), _F32)

    prev_last, prev_last2 = u_scr[last, :], u_scr[last2, :]
    x = jnp.swapaxes(x_ref[0].reshape(SUBLANES, nj, D_MODEL), 0, 1).reshape(rows, D_MODEL)
    u_scr[base:base + rows, :] = _dot(_bf(x), wup_ref[...])
    sub = lax.broadcasted_iota(jnp.int32, (SUBLANES, FFN_COLS), 0)

    def conv_cols(cs):
        wrap = lambda prev, cur: pltpu.roll(jnp.where(sub == SUBLANES - 1, prev, cur), 1, axis=0)
        u_scr[SUBLANES:base, cs] = wrap(prev_last[:, cs], u_scr[last, cs])
        u_scr[0:SUBLANES, cs] = wrap(prev_last2[:, cs], u_scr[last2, cs])
        return _conv(u_scr[0:rows, cs], u_scr[SUBLANES:SUBLANES + rows, cs], u_scr[base:base + rows, cs],
                     cw_ref, cb_ref, cs)

    h = [_bf(_geglu(conv_cols(slice(c0, c0 + FFN_COLS)), conv_cols(slice(D_FF + c0, D_FF + c0 + FFN_COLS))))
         for c0 in range(0, D_FF, FFN_COLS)]
    conv_out_ref[0] = jnp.concatenate([u_scr[base + rows - SUBLANES - 1:base + rows - SUBLANES, :],
                                       u_scr[base + rows - 1:base + rows, :]], axis=0)
    f = _dot(jnp.concatenate(h, axis=1), wdn_ref[...])
    y = _layer_norm(alpha * x + f, ln_g_ref[...], ln_b_ref[...])
    out_ref[0] = jnp.swapaxes(y.reshape(nj, SUBLANES, D_MODEL), 0, 1).reshape(rows, D_MODEL)


def _sample_ffn_kernel(x_ref, hist_ref, wup_ref, cw_ref, cb_ref, wdn_ref, ln_g_ref, ln_b_ref,
                       out_ref, conv_out_ref, u_scr, *, nb, tq, alpha):
    rows = nb * tq
    base = (CONV_W - 1) * nb
    x = jnp.swapaxes(x_ref[...], 0, 1).reshape(rows, D_MODEL)
    u_scr[0:base, :] = jnp.swapaxes(hist_ref[...], 0, 1).reshape(base, 2 * D_FF)
    u_scr[base:base + rows, :] = _dot(_bf(x), wup_ref[...])
    conv = lambda cs: _conv(u_scr[0:rows, cs], u_scr[nb:nb + rows, cs], u_scr[base:base + rows, cs],
                            cw_ref, cb_ref, cs)
    h = _geglu(conv(slice(0, D_FF)), conv(slice(D_FF, 2 * D_FF)))
    conv_out_ref[...] = jnp.swapaxes(u_scr[rows:rows + base, :].reshape(CONV_W - 1, nb, 2 * D_FF), 0, 1)
    f = _dot(_bf(h), wdn_ref[...])
    y = _layer_norm(alpha * x + f, ln_g_ref[...], ln_b_ref[...])
    out_ref[...] = jnp.swapaxes(y.reshape(tq, nb, D_MODEL), 0, 1)


def _mem_kv_kernel(m_ref, w_ref, k_ref, v_ref):
    kv = _dot(_bf(m_ref[0]), w_ref[...])
    for h in range(MEM_HEADS):
        k_ref[0, pl.ds(h, MEM_TOKENS, stride=MEM_HEADS), :] = kv[:, MEM_HD * h:MEM_HD * (h + 1)]
        v_ref[0, pl.ds(h, MEM_TOKENS, stride=MEM_HEADS), :] = kv[:, MEM_W + MEM_HD * h:MEM_W + MEM_HD * (h + 1)]


def _const_spec(shape):
    nd = len(shape)
    return pl.BlockSpec(shape, lambda *_: (0,) * nd, pipeline_mode=pl.Buffered(1))


def _rope_tables(pos):
    half = ROPE_DIM // 2
    inv = np.float32(ROPE_THETA) ** (-np.arange(half, dtype=np.float32) / np.float32(half))
    ang = pos.astype(np.float32)[:, None] * inv[None, :]
    cos, sin = np.cos(ang), np.sin(ang)
    T = pos.shape[0]
    ones = np.ones((T, SWA_HD - ROPE_DIM), np.float32)
    zeros = np.zeros((T, SWA_HD - ROPE_DIM), np.float32)
    zh = np.zeros((T, half), np.float32)
    c = np.concatenate([cos, cos, ones], axis=1)
    dn = np.concatenate([-sin, zh, zeros], axis=1)
    up = np.concatenate([zh, sin, zeros], axis=1)
    rep = LANES // SWA_HD
    return tuple(jnp.asarray(np.concatenate([a] * rep, axis=1), dtype=_F32) for a in (c, dn, up))


def _pack_cols_kernel(wt_ref, o_ref, *, keep):
    t = jnp.transpose(wt_ref[...])
    if keep < t.shape[1]:
        t = jnp.where(lax.broadcasted_iota(jnp.int32, t.shape, 1) < keep, t, 0.0)
    o_ref[...] = _bf(t)


def _pack_cols(wt, col0, ncols, keep, name):
    blk = min(ncols, 2 * LANES)
    k = wt.shape[1]
    return pl.pallas_call(
        functools.partial(_pack_cols_kernel, keep=keep),
        grid=(ncols // blk,),
        in_specs=[pl.BlockSpec((pl.Element(blk), pl.Element(k)),
                               lambda i: (pl.multiple_of(col0 + blk * i, SUBLANES), 0))],
        out_specs=pl.BlockSpec((k, blk), lambda i: (0, i)),
        out_shape=jax.ShapeDtypeStruct((k, ncols), _BF),
        compiler_params=_params(("arbitrary",)),
        name=name,
    )(wt)


def _split_w_in(w_in):
    wt = jnp.transpose(w_in)
    return (_pack_cols(wt, 0, IN_A_COLS, IN_A_COLS, "pack_w_in_a"),
            _pack_cols(wt, IN_A_COLS, LANES, GLA_RANK, "pack_w_in_ga"),
            _pack_cols(wt, IN_A_COLS + GLA_RANK, IN_B_COLS, IN_B_COLS, "pack_w_in_b"))


def _mixer_weight_specs():
    return [
        _const_spec((D_MODEL, IN_A_COLS)),
        _const_spec((D_MODEL, LANES)),
        _const_spec((D_MODEL, IN_B_COLS)),
        _const_spec((LANES, GLA_W)),
        _const_spec((1, GLA_W)),
        _const_spec((1, GLA_DV)),
        _const_spec((GLA_W, D_MODEL)),
        _const_spec((SWA_QW, D_MODEL)),
        _const_spec((MEM_W, D_MODEL)),
        _const_spec((1, GATE_W)),
        _const_spec((D_MODEL, D_MODEL)),
        _const_spec((1, D_MODEL)),
        _const_spec((1, D_MODEL)),
    ]


def _ffn_weight_specs():
    return [
        _const_spec((D_MODEL, 2 * D_FF)),
        _const_spec((CONV_W, 2 * D_FF)),
        _const_spec((1, 2 * D_FF)),
        _const_spec((D_FF, D_MODEL)),
        _const_spec((1, D_MODEL)),
        _const_spec((1, D_MODEL)),
    ]


_SMEM_SPEC = pl.BlockSpec(memory_space=pltpu.SMEM)
_MEM_ROWS = MEM_TOKENS * MEM_HEADS


def _params(sem, flags=None):
    return pltpu.CompilerParams(dimension_semantics=sem, vmem_limit_bytes=V7X_VMEM_LIMIT, flags=flags)


def _prompt_layer(x, mem, sinks, mixer_w, ffn_w, w_mem_kv, alpha):
    B, T, _ = x.shape
    rows = min(PROMPT_ROWS, T)
    nt = T // rows
    assert T % rows == 0 and rows % SWA_PAIR == 0 and rows >= WINDOW
    f32 = jnp.float32
    mk, mv = pl.pallas_call(
        _mem_kv_kernel,
        grid=(B,),
        in_specs=[pl.BlockSpec((1, MEM_TOKENS, D_MODEL), lambda b: (b, 0, 0)),
                  _const_spec((D_MODEL, 2 * MEM_W))],
        out_specs=[pl.BlockSpec((1, _MEM_ROWS, MEM_HD), lambda b: (b, 0, 0))] * 2,
        out_shape=[jax.ShapeDtypeStruct((B, _MEM_ROWS, MEM_HD), f32)] * 2,
        compiler_params=_params(("arbitrary",)),
        name="mem_kv",
    )(mem, w_mem_kv)

    rope = _rope_tables(np.arange(T))
    rope_spec = pl.BlockSpec((rows, LANES), lambda b, t: (t, 0))
    per_batch = lambda *blk: pl.BlockSpec((1,) + blk, lambda b, t: (b,) + (0,) * len(blk))
    x1, gla, swak, swav = pl.pallas_call(
        functools.partial(_prompt_mixer_kernel, rows=rows, alpha=alpha),
        grid=(B, nt),
        in_specs=[_SMEM_SPEC,
                  pl.BlockSpec((1, rows, D_MODEL), lambda b, t: (b, t, 0)),
                  rope_spec, rope_spec, rope_spec,
                  per_batch(_MEM_ROWS, MEM_HD), per_batch(_MEM_ROWS, MEM_HD)] + _mixer_weight_specs(),
        out_specs=[pl.BlockSpec((1, rows, D_MODEL), lambda b, t: (b, t, 0)),
                   per_batch(GLA_HEADS, GLA_DK, GLA_DV),
                   per_batch(WINDOW, SWA_KW), per_batch(WINDOW, SWA_KW)],
        out_shape=[jax.ShapeDtypeStruct((B, T, D_MODEL), f32),
                   jax.ShapeDtypeStruct((B, GLA_HEADS, GLA_DK, GLA_DV), f32),
                   jax.ShapeDtypeStruct((B, WINDOW, SWA_KW), f32),
                   jax.ShapeDtypeStruct((B, WINDOW, SWA_KW), f32)],
        scratch_shapes=[pltpu.VMEM((GLA_HEADS, GLA_DK, GLA_DV), f32),
                        pltpu.VMEM((SWA_KV_HEADS, WINDOW + rows, LANES), _BF),
                        pltpu.VMEM((2 * SWA_KV_HEADS, WINDOW + rows, LANES), _BF)],
        compiler_params=_params(("arbitrary", "arbitrary")),
        name="prompt_mixer",
    )(sinks, x, *rope, mk, mv, *mixer_w)

    frows = min(PROMPT_FFN_ROWS, T)
    assert T % frows == 0 and frows >= CONV_W - 1
    y, conv = pl.pallas_call(
        functools.partial(_prompt_ffn_kernel, rows=frows, alpha=alpha),
        grid=(B, T // frows),
        in_specs=[pl.BlockSpec((1, frows, D_MODEL), lambda b, t: (b, t, 0))] + _ffn_weight_specs(),
        out_specs=[pl.BlockSpec((1, frows, D_MODEL), lambda b, t: (b, t, 0)),
                   per_batch(CONV_W - 1, 2 * D_FF)],
        out_shape=[jax.ShapeDtypeStruct((B, T, D_MODEL), f32),
                   jax.ShapeDtypeStruct((B, CONV_W - 1, 2 * D_FF), f32)],
        scratch_shapes=[pltpu.VMEM((2 * SUBLANES + frows, 2 * D_FF), f32)],
        compiler_params=_params(("arbitrary", "arbitrary")),
        name="prompt_ffn",
    )(x1, *ffn_w)
    return y, gla, swak, swav, mk, mv, conv


def _sample_layer(x, gla0, swak0, swav0, memk, memv, conv0, sinks, mixer_w, ffn_w, alpha):
    B, tq, _ = x.shape
    nb = max(1, min(B, SAMPLE_ROWS // tq))
    assert B % nb == 0 and tq % 16 == 0 and tq >= CONV_W - 1
    f32 = jnp.float32
    rope = _rope_tables(PAST_LEN + np.arange(tq))
    rope_spec = pl.BlockSpec((tq, LANES), lambda i: (0, 0))
    blk = lambda *s: pl.BlockSpec((nb,) + s, lambda i: (i,) + (0,) * len(s))
    blk1 = lambda *s: pl.BlockSpec((nb,) + s, lambda i: (i,) + (0,) * len(s), pipeline_mode=pl.Buffered(1))
    x1, gla, swak, swav = pl.pallas_call(
        functools.partial(_sample_mixer_kernel, nb=nb, tq=tq, alpha=alpha),
        grid=(B // nb,),
        in_specs=[_SMEM_SPEC, blk(tq, D_MODEL), rope_spec, rope_spec, rope_spec,
                  blk(_MEM_ROWS, MEM_HD), blk(_MEM_ROWS, MEM_HD),
                  blk1(GLA_HEADS, GLA_DK, GLA_DV), blk1(WINDOW, SWA_KW), blk1(WINDOW, SWA_KW)]
                 + _mixer_weight_specs(),
        out_specs=[blk(tq, D_MODEL), blk(GLA_HEADS, GLA_DK, GLA_DV), blk(WINDOW, SWA_KW), blk(WINDOW, SWA_KW)],
        out_shape=[jax.ShapeDtypeStruct((B, tq, D_MODEL), f32),
                   jax.ShapeDtypeStruct((B, GLA_HEADS, GLA_DK, GLA_DV), f32),
                   jax.ShapeDtypeStruct((B, WINDOW, SWA_KW), f32),
                   jax.ShapeDtypeStruct((B, WINDOW, SWA_KW), f32)],
        compiler_params=_params(("arbitrary",)),
        name="sample_mixer",
    )(sinks, x, *rope, memk, memv, gla0, swak0, swav0, *mixer_w)

    assert B % SUBLANES == 0
    fblk = lambda *s: pl.BlockSpec((SUBLANES,) + s, lambda i: (i,) + (0,) * len(s))
    y, conv = pl.pallas_call(
        functools.partial(_sample_ffn_kernel, nb=SUBLANES, tq=tq, alpha=alpha),
        grid=(B // SUBLANES,),
        in_specs=[fblk(tq, D_MODEL), fblk(CONV_W - 1, 2 * D_FF)] + _ffn_weight_specs(),
        out_specs=[fblk(tq, D_MODEL), fblk(CONV_W - 1, 2 * D_FF)],
        out_shape=[jax.ShapeDtypeStruct((B, tq, D_MODEL), f32),
                   jax.ShapeDtypeStruct((B, CONV_W - 1, 2 * D_FF), f32)],
        scratch_shapes=[pltpu.VMEM(((CONV_W - 1 + tq) * SUBLANES, 2 * D_FF), f32)],
        compiler_params=_params(("arbitrary",)),
        name="sample_ffn",
    )(x1, conv0, *ffn_w)
    return y, gla, swak, swav, conv


def kernel(x_prompt, x_sample, cache_swa_k, cache_swa_v, state_gla, cache_mem_k, cache_mem_v, cache_ffn_conv, mem_prompt, ln1_g, ln1_b, ln2_g, ln2_b, w_in, b_gate, w_gla_a2, b_gla_a, gla_norm_g, swa_sinks, w_mem_kv, w_br_gla, w_br_swa, w_br_mem, w_o, w_up, conv_w, conv_b, w_down):
    depth = w_in.shape[0]
    alpha = float((2 * depth) ** 0.25)
    Bp = x_prompt.shape[0]
    Bs = x_sample.shape[0]
    hp, hs = x_prompt, x_sample
    outs = [[] for _ in range(10)]
    row = lambda a: a.reshape(1, -1)
    for l in range(depth):
        w2 = jnp.pad(w_gla_a2[l], ((0, LANES - GLA_RANK), (0, 0)))
        mixer_w = _split_w_in(w_in[l]) + (
            _bf(w2), row(b_gla_a[l]), row(gla_norm_g[l]),
            _bf(w_br_gla[l]), _bf(w_br_swa[l]), _bf(w_br_mem[l]), row(b_gate[l]), _bf(w_o[l]),
            row(ln1_g[l]), row(ln1_b[l]))
        ffn_w = (_bf(w_up[l]), conv_w[l], row(conv_b[l]), _bf(w_down[l]), row(ln2_g[l]), row(ln2_b[l]))
        sinks = swa_sinks[l]
        hp, g_p, k_p, v_p, mk_p, mv_p, c_p = _prompt_layer(
            hp, mem_prompt, sinks, mixer_w, ffn_w, _bf(w_mem_kv[l]), alpha)
        hs, g_s, k_s, v_s, c_s = _sample_layer(
            hs, state_gla[l],
            cache_swa_k[l].reshape(Bs, WINDOW, SWA_KW), cache_swa_v[l].reshape(Bs, WINDOW, SWA_KW),
            cache_mem_k[l].reshape(Bs, _MEM_ROWS, MEM_HD), cache_mem_v[l].reshape(Bs, _MEM_ROWS, MEM_HD),
            cache_ffn_conv[l], sinks, mixer_w, ffn_w, alpha)
        kv5 = lambda a, b: a.reshape(b, WINDOW, SWA_KV_HEADS, SWA_HD)
        m5 = lambda a: a.reshape(Bp, MEM_TOKENS, MEM_HEADS, MEM_HD)
        for lst, val in zip(outs, (kv5(k_p, Bp), kv5(v_p, Bp), g_p, m5(mk_p), m5(mv_p), c_p,
                                   kv5(k_s, Bs), kv5(v_s, Bs), g_s, c_s)):
            lst.append(val)
    return (hp, hs) + tuple(jnp.stack(o) for o in outs)
```

```python
import functools

import jax
import jax.numpy as jnp
import numpy as np
from jax import lax
from jax.experimental import pallas as pl
from jax.experimental.pallas import tpu as pltpu

D_MODEL = 1024
CHUNK = 64
GLA_HEADS = 4
GLA_DK = 128
GLA_DV = 128
GLA_RANK = 16
GLA_TAU = 16.0
SWA_HEADS = 8
SWA_KV_HEADS = 2
SWA_HD = 64
WINDOW = 128
ROPE_DIM = 16
ROPE_THETA = 500000.0
MEM_TOKENS = 256
MEM_HEADS = 4
MEM_HD = 128
D_FF = 2816
CONV_W = 3
N_BRANCH = 3
PAST_LEN = 2048
LN_EPS = 1e-5

LANES = 128
SUBLANES = 8
V7X_VMEM_LIMIT = 56 * 1024 * 1024
PROMPT_ROWS = 256
PROMPT_FFN_ROWS = 512
FFN_COLS = 256
SAMPLE_ROWS = 256
GLA_W = GLA_HEADS * GLA_DK
SWA_QW = SWA_HEADS * SWA_HD
SWA_KW = SWA_KV_HEADS * SWA_HD
MEM_W = MEM_HEADS * MEM_HD
GATE_W = N_BRANCH * D_MODEL
SWA_PAIR = 2 * CHUNK

OFF_GQ = 0
OFF_GK = OFF_GQ + GLA_W
OFF_GV = OFF_GK + GLA_W
OFF_GR = OFF_GV + GLA_W
IN_A_COLS = OFF_GR + GLA_W
OFF_SQ = 0
OFF_SK = OFF_SQ + SWA_QW
OFF_SV = OFF_SK + SWA_KW
OFF_MQ = OFF_SV + SWA_KW
OFF_GL = OFF_MQ + MEM_W
IN_B_COLS = OFF_GL + GATE_W

_BF = jnp.bfloat16
_F32 = jnp.float32


def _bf(x):
    return x.astype(_BF)


def _dot(a, b):
    return jnp.dot(a, b, preferred_element_type=_F32)


def _dot_nt(a, b):
    return lax.dot_general(a, b, (((1,), (1,)), ((), ())), preferred_element_type=_F32)


def _dot_tn(a, b):
    return lax.dot_general(a, b, (((0,), (0,)), ((), ())), preferred_element_type=_F32)


def _layer_norm(h, g, b):
    mu = jnp.mean(h, axis=-1, keepdims=True)
    d = h - mu
    var = jnp.mean(d * d, axis=-1, keepdims=True)
    return d * lax.rsqrt(var + LN_EPS) * g + b


def _log_sigmoid(x):
    return -(jnp.maximum(-x, 0.0) + jnp.log(1.0 + jnp.exp(-jnp.abs(x))))


def _rope(a, c, s_dn, s_up):
    outs = []
    for j in range(a.shape[1] // LANES):
        slab = a[:, LANES * j:LANES * (j + 1)]
        fwd = pltpu.roll(slab, LANES - ROPE_DIM // 2, axis=1)
        bwd = pltpu.roll(slab, ROPE_DIM // 2, axis=1)
        outs.append(slab * c + fwd * s_dn + bwd * s_up)
    return outs[0] if len(outs) == 1 else jnp.concatenate(outs, axis=1)


def _gla_tile(q, k, v, log_a, chunk, nchunk, state_in, state_out, chained):
    R = chunk * nchunk
    row = lax.broadcasted_iota(jnp.int32, (R, R), 0)
    col = lax.broadcasted_iota(jnp.int32, (R, R), 1)
    tril = (row // chunk == col // chunk) & (col <= row)
    ltri = jnp.where(tril, 1.0, 0.0).astype(_BF)
    hi = _bf(log_a)
    lo = _bf(log_a - hi.astype(_F32))
    b = _dot(ltri, hi) + _dot(ltri, lo)
    b_last = [b[chunk * (c + 1) - 1:chunk * (c + 1), :] for c in range(nchunk)]
    bl_rows = jnp.concatenate([jnp.broadcast_to(bl, (chunk, GLA_W)) for bl in b_last], axis=0) \
        if nchunk > 1 else jnp.broadcast_to(b_last[0], (chunk, GLA_W))
    q_dec = _bf(q * (GLA_DK ** -0.5) * jnp.exp(b))
    k_inv = _bf(k * jnp.exp(-b))
    k_tail = _bf(k * jnp.exp(bl_rows - b))
    vb = _bf(v)
    outs = []
    for h in range(GLA_HEADS):
        hs = slice(GLA_DK * h, GLA_DK * (h + 1))
        att = jnp.where(tril, _dot_nt(q_dec[:, hs], k_inv[:, hs]), 0.0)
        o_intra = _dot(_bf(att), vb[:, hs])
        o_inter = []
        S = None
        for c in range(nchunk):
            rs = slice(chunk * c, chunk * (c + 1))
            if c == 0 or not chained:
                S = state_in(c, h)
            o_inter.append(_dot(q_dec[rs, hs], _bf(S)))
            decay = jnp.exp(b_last[c][:, hs])
            dcol = jnp.transpose(jnp.broadcast_to(decay, (GLA_DV, GLA_DK)))
            ds = _dot(jnp.transpose(k_tail[rs, hs].astype(_F32)).astype(_BF), vb[rs, hs])
            S = S * dcol + ds
            if c == nchunk - 1 or not chained:
                state_out(c, h, S)
        o_inter = jnp.concatenate(o_inter, axis=0) if nchunk > 1 else o_inter[0]
        outs.append(o_intra + o_inter)
    return jnp.concatenate(outs, axis=1)


def _gla_gate(o, gr, g):
    outs = []
    for h in range(GLA_HEADS):
        hs = slice(GLA_DV * h, GLA_DV * (h + 1))
        oh = o[:, hs]
        ms = jnp.mean(oh * oh, axis=-1, keepdims=True)
        grh = gr[:, hs]
        outs.append(oh * lax.rsqrt(ms + LN_EPS) * g * (grh * jax.nn.sigmoid(grh)))
    return jnp.concatenate(outs, axis=1)


def _lane_half_masks(shape):
    lane = lax.broadcasted_iota(jnp.int32, shape, 1)
    return lane < SWA_HD, lane >= SWA_HD


def _swa_split_q(q):
    lo_m, hi_m = _lane_half_masks((q.shape[0], LANES))
    out = []
    for j in range(SWA_QW // LANES):
        slab = q[:, LANES * j:LANES * (j + 1)]
        out.append((_bf(jnp.where(lo_m, slab, 0.0)), _bf(jnp.where(hi_m, slab, 0.0))))
    return out


def _swa_dup_k(k):
    lo_m, _ = _lane_half_masks(k.shape)
    kr = pltpu.roll(k, SWA_HD, axis=1)
    return [_bf(jnp.where(lo_m, k, kr)), _bf(jnp.where(lo_m, kr, k))]


def _swa_place_v(v):
    lo_m, hi_m = _lane_half_masks(v.shape)
    vr = pltpu.roll(v, SWA_HD, axis=1)
    return [(_bf(jnp.where(lo_m, v, 0.0)), _bf(jnp.where(hi_m, vr, 0.0))),
            (_bf(jnp.where(lo_m, vr, 0.0)), _bf(jnp.where(hi_m, v, 0.0)))]


def _swa_blocks(qrows, kks, vlos, vhis, sink_col, lowers, upper, tq):
    n = len(kks)
    stack = lambda xs: jnp.concatenate(xs, axis=0) if n > 1 else xs[0]
    s = stack([_dot_nt(qrows[i], kks[i]) for i in range(n)])
    if lowers is not None:
        kcol = lax.broadcasted_iota(jnp.int32, s.shape, 1)
        s = jnp.where(kcol >= stack(lowers), jnp.where(kcol <= stack([upper] * n), s, -jnp.inf), -jnp.inf)
    sink = stack([sink_col] * n)
    m = jnp.maximum(jnp.max(s, axis=-1, keepdims=True), sink)
    p = jnp.exp(s - m)
    den = jnp.sum(p, axis=-1, keepdims=True) + jnp.exp(sink - m)
    p = _bf(p / den)
    outs = []
    for i in range(n):
        r = 4 * tq * i
        outs.append((_dot(p[r:r + tq], vlos[i]) + _dot(p[r + tq:r + 2 * tq], vhis[i]),
                     _dot(p[r + 2 * tq:r + 3 * tq], vlos[i]) + _dot(p[r + 3 * tq:r + 4 * tq], vhis[i])))
    return outs


def _sink_col(sinks_ref, g, tq):
    r = lax.broadcasted_iota(jnp.int32, (4 * tq, 1), 0)
    base = 4 * g
    return jnp.where(r < tq, sinks_ref[base],
                     jnp.where(r < 2 * tq, sinks_ref[base + 1],
                               jnp.where(r < 3 * tq, sinks_ref[base + 2], sinks_ref[base + 3])))


def _mem_head(ref, h):
    return _bf(ref[pl.ds(h, MEM_TOKENS, stride=MEM_HEADS), :])


def _mem_attention(q, mk_refs, mv_refs):
    qb = _bf(q)
    nseg = len(mk_refs)
    tq = q.shape[0] // nseg
    outs = []
    for h in range(MEM_HEADS):
        hs = slice(MEM_HD * h, MEM_HD * (h + 1))
        s = [_dot_nt(qb[tq * i:tq * (i + 1), hs], _mem_head(mk_refs[i], h)) for i in range(nseg)]
        s = jnp.concatenate(s, axis=0) if nseg > 1 else s[0]
        m = jnp.max(s, axis=-1, keepdims=True)
        p = jnp.exp(s - m)
        p = _bf(p / jnp.sum(p, axis=-1, keepdims=True))
        o = [_dot(p[tq * i:tq * (i + 1)], _mem_head(mv_refs[i], h)) for i in range(nseg)]
        outs.append(jnp.concatenate(o, axis=0) if nseg > 1 else o[0])
    return jnp.concatenate(outs, axis=1)


def _swa_attention_t(qs, kk_scr, v_scr, sinks_ref, t, rows):
    nk = WINDOW + SWA_PAIR
    npair = rows // SWA_PAIR
    bw = 4 * SWA_PAIR
    blocks = [(g, p) for g in range(SWA_KV_HEADS) for p in range(npair)]
    sts = []
    for g, p in blocks:
        rs = slice(SWA_PAIR * p, SWA_PAIR * (p + 1))
        qrows = jnp.concatenate([qs[2 * g][0][rs], qs[2 * g][1][rs],
                                 qs[2 * g + 1][0][rs], qs[2 * g + 1][1][rs]], axis=0)
        sts.append(_dot_nt(kk_scr[g, SWA_PAIR * p:SWA_PAIR * p + nk, :], qrows))
    st = jnp.concatenate(sts, axis=1)
    width = bw * len(blocks)
    col = lax.broadcasted_iota(jnp.int32, (1, width), 1)
    pair = (col // bw) % npair
    band_lo = CHUNK * ((col % SWA_PAIR) // CHUNK)
    lower = jnp.maximum(band_lo, WINDOW - (t * rows + SWA_PAIR * pair))
    upper = band_lo + (WINDOW + CHUNK - 1)
    krow = lax.broadcasted_iota(jnp.int32, (nk, width), 0)
    st = jnp.where(krow >= lower, jnp.where(krow <= upper, st, -jnp.inf), -jnp.inf)
    head = 4 * (col // (bw * npair)) + (col % bw) // SWA_PAIR
    sink = jnp.zeros((1, width), _F32)
    for h in range(SWA_HEADS):
        sink = jnp.where(head == h, sinks_ref[h], sink)
    m = jnp.maximum(jnp.max(st, axis=0, keepdims=True), sink)
    pt = jnp.exp(st - m)
    inv = 1.0 / (jnp.sum(pt, axis=0, keepdims=True) + jnp.exp(sink - m))
    pb = _bf(pt)
    pieces = [[None] * npair for _ in range(SWA_HEADS)]
    for i, (g, p) in enumerate(blocks):
        cs = slice(bw * i, bw * (i + 1))
        o2 = _dot_tn(v_scr[SWA_PAIR * p:SWA_PAIR * p + nk, :], pb[:, cs]) * inv[:, cs]
        for hh in range(4):
            pieces[4 * g + hh][p] = o2[SWA_HD * g:SWA_HD * (g + 1), SWA_PAIR * hh:SWA_PAIR * (hh + 1)]
    return jnp.concatenate([jnp.concatenate(ps, axis=1) if npair > 1 else ps[0] for ps in pieces], axis=0)


def _mem_attention_t(q, mk_ref, mv_ref):
    qb = _bf(q)
    R = q.shape[0]
    st = jnp.concatenate([_dot_nt(_mem_head(mk_ref, h), qb[:, MEM_HD * h:MEM_HD * (h + 1)])
                          for h in range(MEM_HEADS)], axis=1)
    pt = jnp.exp(st - jnp.max(st, axis=0, keepdims=True))
    inv = 1.0 / jnp.sum(pt, axis=0, keepdims=True)
    pb = _bf(pt)
    outs = [_dot_tn(_mem_head(mv_ref, h), pb[:, R * h:R * (h + 1)]) * inv[:, R * h:R * (h + 1)]
            for h in range(MEM_HEADS)]
    return jnp.concatenate(outs, axis=0)


def _gate(xb, i, w_inb_ref, bgate_ref):
    gl = _dot(xb, w_inb_ref[:, OFF_GL + D_MODEL * i:OFF_GL + D_MODEL * (i + 1)])
    return jax.nn.sigmoid(gl + bgate_ref[:, D_MODEL * i:D_MODEL * (i + 1)])


def _merge_ln1(x, gated_sum, wo_ref, g_ref, b_ref, alpha):
    mix = _dot(_bf(gated_sum), wo_ref[...])
    return _layer_norm(alpha * x + mix, g_ref[...], b_ref[...])


def _forget_log(xb, w_ga_ref, w2_ref, ba_ref):
    ga = _dot(xb, w_ga_ref[...])
    return _log_sigmoid(_dot(_bf(ga), w2_ref[...]) + ba_ref[...]) * (1.0 / GLA_TAU)


def _prompt_mixer_kernel(sinks_ref, x_ref, rc_ref, rdn_ref, rup_ref, mk_ref, mv_ref, w_ina_ref, w_ga_ref,
                         w_inb_ref, w2_ref, ba_ref, gng_ref, wbg_ref, wbs_ref, wbm_ref, bgate_ref, wo_ref,
                         ln_g_ref, ln_b_ref,
                         out_ref, gla_out_ref, swak_out_ref, swav_out_ref,
                         s_scr, kk_scr, vv_scr, *, rows, alpha):
    t = pl.program_id(1)
    nchunk = rows // CHUNK

    @pl.when(t == 0)
    def _():
        s_scr[...] = jnp.zeros_like(s_scr)
        kk_scr[:, 0:WINDOW, :] = jnp.zeros((SWA_KV_HEADS, WINDOW, LANES), _BF)
        vv_scr[0:WINDOW, :] = jnp.zeros((WINDOW, LANES), _BF)

    x = x_ref[0]
    xb = _bf(x)

    log_a = _forget_log(xb, w_ga_ref, w2_ref, ba_ref)

    def proj_a(off):
        return _dot(xb, w_ina_ref[:, off:off + GLA_W])

    def state_in(c, h):
        return s_scr[h]

    def state_out(c, h, S):
        s_scr[h] = S
        gla_out_ref[0, h] = S

    gq, gk, gv = proj_a(OFF_GQ), proj_a(OFF_GK), proj_a(OFF_GV)
    gate_a = _gate(xb, 0, w_inb_ref, bgate_ref)
    o_a = _gla_tile(gq, gk, gv, log_a, CHUNK, nchunk, state_in, state_out, chained=True)
    o_a = _gla_gate(o_a, proj_a(OFF_GR), gng_ref[...])
    gated = gate_a * _dot(_bf(o_a), wbg_ref[...])

    rc, rdn, rup = rc_ref[...], rdn_ref[...], rup_ref[...]
    q = _rope(_dot(xb, w_inb_ref[:, OFF_SQ:OFF_SQ + SWA_QW]), rc, rdn, rup) * (SWA_HD ** -0.5)
    k = _rope(_dot(xb, w_inb_ref[:, OFF_SK:OFF_SK + SWA_KW]), rc, rdn, rup)
    v = _dot(xb, w_inb_ref[:, OFF_SV:OFF_SV + SWA_KW])
    swak_out_ref[0] = k[rows - WINDOW:rows, :]
    swav_out_ref[0] = v[rows - WINDOW:rows, :]
    gate_b = _gate(xb, 1, w_inb_ref, bgate_ref)
    kk = _swa_dup_k(k)
    vb = _bf(v)
    for g in range(SWA_KV_HEADS):
        kk_scr[g, WINDOW:WINDOW + rows, :] = kk[g]
    vv_scr[WINDOW:WINDOW + rows, :] = vb
    o_bt = _swa_attention_t(_swa_split_q(q), kk_scr, vv_scr, sinks_ref, t, rows)
    gated = gated + gate_b * _dot_tn(_bf(o_bt), wbs_ref[...])
    for g in range(SWA_KV_HEADS):
        kk_scr[g, 0:WINDOW, :] = kk[g][rows - WINDOW:rows]
    vv_scr[0:WINDOW, :] = vb[rows - WINDOW:rows]

    qm = _dot(xb, w_inb_ref[:, OFF_MQ:OFF_MQ + MEM_W]) * (MEM_HD ** -0.5)
    gate_c = _gate(xb, 2, w_inb_ref, bgate_ref)
    o_ct = _mem_attention_t(qm, mk_ref.at[0], mv_ref.at[0])
    gated = gated + gate_c * _dot_tn(_bf(o_ct), wbm_ref[...])

    out_ref[0] = _merge_ln1(x, gated, wo_ref, ln_g_ref, ln_b_ref, alpha)


def _sample_mixer_kernel(sinks_ref, x_ref, rc_ref, rdn_ref, rup_ref, mk_ref, mv_ref, gla_in_ref,
                         swak_in_ref, swav_in_ref, w_ina_ref, w_ga_ref, w_inb_ref, w2_ref,
                         ba_ref, gng_ref, wbg_ref, wbs_ref, wbm_ref, bgate_ref, wo_ref, ln_g_ref, ln_b_ref,
                         out_ref, gla_out_ref, swak_out_ref, swav_out_ref, *, nb, tq, alpha):
    rows = nb * tq
    x = x_ref[...].reshape(rows, D_MODEL)
    xb = _bf(x)

    log_a = _forget_log(xb, w_ga_ref, w2_ref, ba_ref)

    def proj_a(off):
        return _dot(xb, w_ina_ref[:, off:off + GLA_W])

    def state_in(c, h):
        return gla_in_ref[c, h]

    def state_out(c, h, S):
        gla_out_ref[c, h] = S

    gq, gk, gv = proj_a(OFF_GQ), proj_a(OFF_GK), proj_a(OFF_GV)
    gate_a = _gate(xb, 0, w_inb_ref, bgate_ref)
    o_a = _gla_tile(gq, gk, gv, log_a, tq, nb, state_in, state_out, chained=False)
    o_a = _gla_gate(o_a, proj_a(OFF_GR), gng_ref[...])
    gated = gate_a * _dot(_bf(o_a), wbg_ref[...])

    rc = jnp.concatenate([rc_ref[...]] * nb, axis=0)
    rdn = jnp.concatenate([rdn_ref[...]] * nb, axis=0)
    rup = jnp.concatenate([rup_ref[...]] * nb, axis=0)
    q = _rope(_dot(xb, w_inb_ref[:, OFF_SQ:OFF_SQ + SWA_QW]), rc, rdn, rup) * (SWA_HD ** -0.5)
    k = _rope(_dot(xb, w_inb_ref[:, OFF_SK:OFF_SK + SWA_KW]), rc, rdn, rup)
    v = _dot(xb, w_inb_ref[:, OFF_SV:OFF_SV + SWA_KW])
    gate_b = _gate(xb, 1, w_inb_ref, bgate_ref)
    qs = _swa_split_q(q)
    slabs = [[None] * nb for _ in range(SWA_QW // LANES)]
    kks, vvs = [], []
    for bi in range(nb):
        rs = slice(tq * bi, tq * (bi + 1))
        k_all = jnp.concatenate([swak_in_ref[bi], k[rs]], axis=0)
        v_all = jnp.concatenate([swav_in_ref[bi], v[rs]], axis=0)
        swak_out_ref[bi] = k_all[tq:tq + WINDOW]
        swav_out_ref[bi] = v_all[tq:tq + WINDOW]
        kks.append(_swa_dup_k(k_all))
        vvs.append(_swa_place_v(v_all))
    for g in range(SWA_KV_HEADS):
        qrows = []
        for bi in range(nb):
            rs = slice(tq * bi, tq * (bi + 1))
            qrows.append(jnp.concatenate([qs[2 * g][0][rs], qs[2 * g][1][rs],
                                          qs[2 * g + 1][0][rs], qs[2 * g + 1][1][rs]], axis=0))
        outs = _swa_blocks(qrows, [kk[g] for kk in kks], [vv[g][0] for vv in vvs], [vv[g][1] for vv in vvs],
                           _sink_col(sinks_ref, g, tq), None, None, tq)
        for bi in range(nb):
            slabs[2 * g][bi], slabs[2 * g + 1][bi] = outs[bi]
    o_b = jnp.concatenate([jnp.concatenate(s, axis=0) if nb > 1 else s[0] for s in slabs], axis=1)
    gated = gated + gate_b * _dot(_bf(o_b), wbs_ref[...])

    qm = _dot(xb, w_inb_ref[:, OFF_MQ:OFF_MQ + MEM_W]) * (MEM_HD ** -0.5)
    gate_c = _gate(xb, 2, w_inb_ref, bgate_ref)
    o_c = _mem_attention(qm, [mk_ref.at[bi] for bi in range(nb)], [mv_ref.at[bi] for bi in range(nb)])
    gated = gated + gate_c * _dot(_bf(o_c), wbm_ref[...])

    out = _merge_ln1(x, gated, wo_ref, ln_g_ref, ln_b_ref, alpha)
    out_ref[...] = out.reshape(nb, tq, D_MODEL)


_GELU_K1 = -2.0 * float(np.log2(np.e)) * float(np.sqrt(2.0 / np.pi))
_GELU_K2 = _GELU_K1 * 0.044715


def _conv(u2, u1, u0, cw_ref, cb_ref, cs):
    return cb_ref[:, cs] + u2 * cw_ref[0:1, cs] + u1 * cw_ref[1:2, cs] + u0 * cw_ref[2:3, cs]


def _geglu(g, v):
    return g * v / (1.0 + jnp.exp2(g * (g * g * _GELU_K2 + _GELU_K1)))


def _prompt_ffn_kernel(x_ref, wup_ref, cw_ref, cb_ref, wdn_ref, ln_g_ref, ln_b_ref,
                       out_ref, conv_out_ref, u_scr, *, rows, alpha):
    nj = rows // SUBLANES
    base = 2 * SUBLANES
    last = slice(base + rows - SUBLANES, base + rows)
    last2 = slice(base + rows - 2 * SUBLANES, base + rows - SUBLANES)

    @pl.when(pl.program_id(1) == 0)
    def _():
        u_scr[base + rows - 2 * SUBLANES:base + rows, :] = jnp.zeros((2 * SUBLANES, 2 * D_FF), _F32)

    prev_last, prev_last2 = u_scr[last, :], u_scr[last2, :]
    x = jnp.swapaxes(x_ref[0].reshape(SUBLANES, nj, D_MODEL), 0, 1).reshape(rows, D_MODEL)
    u_scr[base:base + rows, :] = _dot(_bf(x), wup_ref[...])
    sub = lax.broadcasted_iota(jnp.int32, (SUBLANES, FFN_COLS), 0)

    def conv_cols(cs):
        wrap = lambda prev, cur: pltpu.roll(jnp.where(sub == SUBLANES - 1, prev, cur), 1, axis=0)
        u_scr[SUBLANES:base, cs] = wrap(prev_last[:, cs], u_scr[last, cs])
        u_scr[0:SUBLANES, cs] = wrap(prev_last2[:, cs], u_scr[last2, cs])
        return _conv(u_scr[0:rows, cs], u_scr[SUBLANES:SUBLANES + rows, cs], u_scr[base:base + rows, cs],
                     cw_ref, cb_ref, cs)

    h = [_bf(_geglu(conv_cols(slice(c0, c0 + FFN_COLS)), conv_cols(slice(D_FF + c0, D_FF + c0 + FFN_COLS))))
         for c0 in range(0, D_FF, FFN_COLS)]
    conv_out_ref[0] = jnp.concatenate([u_scr[base + rows - SUBLANES - 1:base + rows - SUBLANES, :],
                                       u_scr[base + rows - 1:base + rows, :]], axis=0)
    f = _dot(jnp.concatenate(h, axis=1), wdn_ref[...])
    y = _layer_norm(alpha * x + f, ln_g_ref[...], ln_b_ref[...])
    out_ref[0] = jnp.swapaxes(y.reshape(nj, SUBLANES, D_MODEL), 0, 1).reshape(rows, D_MODEL)


def _sample_ffn_kernel(x_ref, hist_ref, wup_ref, cw_ref, cb_ref, wdn_ref, ln_g_ref, ln_b_ref,
                       out_ref, conv_out_ref, u_scr, *, nb, tq, alpha):
    rows = nb * tq
    base = (CONV_W - 1) * nb
    x = jnp.swapaxes(x_ref[...], 0, 1).reshape(rows, D_MODEL)
    u_scr[0:base, :] = jnp.swapaxes(hist_ref[...], 0, 1).reshape(base, 2 * D_FF)
    u_scr[base:base + rows, :] = _dot(_bf(x), wup_ref[...])
    conv = lambda cs: _conv(u_scr[0:rows, cs], u_scr[nb:nb + rows, cs], u_scr[base:base + rows, cs],
                            cw_ref, cb_ref, cs)
    h = _geglu(conv(slice(0, D_FF)), conv(slice(D_FF, 2 * D_FF)))
    conv_out_ref[...] = jnp.swapaxes(u_scr[rows:rows + base, :].reshape(CONV_W - 1, nb, 2 * D_FF), 0, 1)
    f = _dot(_bf(h), wdn_ref[...])
    y = _layer_norm(alpha * x + f, ln_g_ref[...], ln_b_ref[...])
    out_ref[...] = jnp.swapaxes(y.reshape(tq, nb, D_MODEL), 0, 1)


def _mem_kv_kernel(m_ref, w_ref, k_ref, v_ref):
    kv = _dot(_bf(m_ref[0]), w_ref[...])
    for h in range(MEM_HEADS):
        k_ref[0, pl.ds(h, MEM_TOKENS, stride=MEM_HEADS), :] = kv[:, MEM_HD * h:MEM_HD * (h + 1)]
        v_ref[0, pl.ds(h, MEM_TOKENS, stride=MEM_HEADS), :] = kv[:, MEM_W + MEM_HD * h:MEM_W + MEM_HD * (h + 1)]


def _const_spec(shape):
    nd = len(shape)
    return pl.BlockSpec(shape, lambda *_: (0,) * nd, pipeline_mode=pl.Buffered(1))


def _rope_tables(pos):
    half = ROPE_DIM // 2
    inv = np.float32(ROPE_THETA) ** (-np.arange(half, dtype=np.float32) / np.float32(half))
    ang = pos.astype(np.float32)[:, None] * inv[None, :]
    cos, sin = np.cos(ang), np.sin(ang)
    T = pos.shape[0]
    ones = np.ones((T, SWA_HD - ROPE_DIM), np.float32)
    zeros = np.zeros((T, SWA_HD - ROPE_DIM), np.float32)
    zh = np.zeros((T, half), np.float32)
    c = np.concatenate([cos, cos, ones], axis=1)
    dn = np.concatenate([-sin, zh, zeros], axis=1)
    up = np.concatenate([zh, sin, zeros], axis=1)
    rep = LANES // SWA_HD
    return tuple(jnp.asarray(np.concatenate([a] * rep, axis=1), dtype=_F32) for a in (c, dn, up))


def _pack_cols_kernel(wt_ref, o_ref, *, keep):
    t = jnp.transpose(wt_ref[...])
    if keep < t.shape[1]:
        t = jnp.where(lax.broadcasted_iota(jnp.int32, t.shape, 1) < keep, t, 0.0)
    o_ref[...] = _bf(t)


def _pack_cols(wt, col0, ncols, keep, name):
    blk = min(ncols, 2 * LANES)
    k = wt.shape[1]
    return pl.pallas_call(
        functools.partial(_pack_cols_kernel, keep=keep),
        grid=(ncols // blk,),
        in_specs=[pl.BlockSpec((pl.Element(blk), pl.Element(k)),
                               lambda i: (pl.multiple_of(col0 + blk * i, SUBLANES), 0))],
        out_specs=pl.BlockSpec((k, blk), lambda i: (0, i)),
        out_shape=jax.ShapeDtypeStruct((k, ncols), _BF),
        compiler_params=_params(("arbitrary",)),
        name=name,
    )(wt)


def _split_w_in(w_in):
    wt = jnp.transpose(w_in)
    return (_pack_cols(wt, 0, IN_A_COLS, IN_A_COLS, "pack_w_in_a"),
            _pack_cols(wt, IN_A_COLS, LANES, GLA_RANK, "pack_w_in_ga"),
            _pack_cols(wt, IN_A_COLS + GLA_RANK, IN_B_COLS, IN_B_COLS, "pack_w_in_b"))


def _mixer_weight_specs():
    return [
        _const_spec((D_MODEL, IN_A_COLS)),
        _const_spec((D_MODEL, LANES)),
        _const_spec((D_MODEL, IN_B_COLS)),
        _const_spec((LANES, GLA_W)),
        _const_spec((1, GLA_W)),
        _const_spec((1, GLA_DV)),
        _const_spec((GLA_W, D_MODEL)),
        _const_spec((SWA_QW, D_MODEL)),
        _const_spec((MEM_W, D_MODEL)),
        _const_spec((1, GATE_W)),
        _const_spec((D_MODEL, D_MODEL)),
        _const_spec((1, D_MODEL)),
        _const_spec((1, D_MODEL)),
    ]


def _ffn_weight_specs():
    return [
        _const_spec((D_MODEL, 2 * D_FF)),
        _const_spec((CONV_W, 2 * D_FF)),
        _const_spec((1, 2 * D_FF)),
        _const_spec((D_FF, D_MODEL)),
        _const_spec((1, D_MODEL)),
        _const_spec((1, D_MODEL)),
    ]


_SMEM_SPEC = pl.BlockSpec(memory_space=pltpu.SMEM)
_MEM_ROWS = MEM_TOKENS * MEM_HEADS


def _params(sem, flags=None):
    return pltpu.CompilerParams(dimension_semantics=sem, vmem_limit_bytes=V7X_VMEM_LIMIT, flags=flags)


def _prompt_layer(x, mem, sinks, mixer_w, ffn_w, w_mem_kv, alpha):
    B, T, _ = x.shape
    rows = min(PROMPT_ROWS, T)
    nt = T // rows
    assert T % rows == 0 and rows % SWA_PAIR == 0 and rows >= WINDOW
    f32 = jnp.float32
    mk, mv = pl.pallas_call(
        _mem_kv_kernel,
        grid=(B,),
        in_specs=[pl.BlockSpec((1, MEM_TOKENS, D_MODEL), lambda b: (b, 0, 0)),
                  _const_spec((D_MODEL, 2 * MEM_W))],
        out_specs=[pl.BlockSpec((1, _MEM_ROWS, MEM_HD), lambda b: (b, 0, 0))] * 2,
        out_shape=[jax.ShapeDtypeStruct((B, _MEM_ROWS, MEM_HD), f32)] * 2,
        compiler_params=_params(("arbitrary",)),
        name="mem_kv",
    )(mem, w_mem_kv)

    rope = _rope_tables(np.arange(T))
    rope_spec = pl.BlockSpec((rows, LANES), lambda b, t: (t, 0))
    per_batch = lambda *blk: pl.BlockSpec((1,) + blk, lambda b, t: (b,) + (0,) * len(blk))
    x1, gla, swak, swav = pl.pallas_call(
        functools.partial(_prompt_mixer_kernel, rows=rows, alpha=alpha),
        grid=(B, nt),
        in_specs=[_SMEM_SPEC,
                  pl.BlockSpec((1, rows, D_MODEL), lambda b, t: (b, t, 0)),
                  rope_spec, rope_spec, rope_spec,
                  per_batch(_MEM_ROWS, MEM_HD), per_batch(_MEM_ROWS, MEM_HD)] + _mixer_weight_specs(),
        out_specs=[pl.BlockSpec((1, rows, D_MODEL), lambda b, t: (b, t, 0)),
                   per_batch(GLA_HEADS, GLA_DK, GLA_DV),
                   per_batch(WINDOW, SWA_KW), per_batch(WINDOW, SWA_KW)],
        out_shape=[jax.ShapeDtypeStruct((B, T, D_MODEL), f32),
                   jax.ShapeDtypeStruct((B, GLA_HEADS, GLA_DK, GLA_DV), f32),
                   jax.ShapeDtypeStruct((B, WINDOW, SWA_KW), f32),
                   jax.ShapeDtypeStruct((B, WINDOW, SWA_KW), f32)],
        scratch_shapes=[pltpu.VMEM((GLA_HEADS, GLA_DK, GLA_DV), f32),
                        pltpu.VMEM((SWA_KV_HEADS, WINDOW + rows, LANES), _BF),
                        pltpu.VMEM((WINDOW + rows, LANES), _BF)],
        compiler_params=_params(("arbitrary", "arbitrary")),
        name="prompt_mixer",
    )(sinks, x, *rope, mk, mv, *mixer_w)

    frows = min(PROMPT_FFN_ROWS, T)
    assert T % frows == 0 and frows >= CONV_W - 1
    y, conv = pl.pallas_call(
        functools.partial(_prompt_ffn_kernel, rows=frows, alpha=alpha),
        grid=(B, T // frows),
        in_specs=[pl.BlockSpec((1, frows, D_MODEL), lambda b, t: (b, t, 0))] + _ffn_weight_specs(),
        out_specs=[pl.BlockSpec((1, frows, D_MODEL), lambda b, t: (b, t, 0)),
                   per_batch(CONV_W - 1, 2 * D_FF)],
        out_shape=[jax.ShapeDtypeStruct((B, T, D_MODEL), f32),
                   jax.ShapeDtypeStruct((B, CONV_W - 1, 2 * D_FF), f32)],
        scratch_shapes=[pltpu.VMEM((2 * SUBLANES + frows, 2 * D_FF), f32)],
        compiler_params=_params(("arbitrary", "arbitrary")),
        name="prompt_ffn",
    )(x1, *ffn_w)
    return y, gla, swak, swav, mk, mv, conv


def _sample_layer(x, gla0, swak0, swav0, memk, memv, conv0, sinks, mixer_w, ffn_w, alpha):
    B, tq, _ = x.shape
    nb = max(1, min(B, SAMPLE_ROWS // tq))
    assert B % nb == 0 and tq % 16 == 0 and tq >= CONV_W - 1
    f32 = jnp.float32
    rope = _rope_tables(PAST_LEN + np.arange(tq))
    rope_spec = pl.BlockSpec((tq, LANES), lambda i: (0, 0))
    blk = lambda *s: pl.BlockSpec((nb,) + s, lambda i: (i,) + (0,) * len(s))
    blk1 = lambda *s: pl.BlockSpec((nb,) + s, lambda i: (i,) + (0,) * len(s), pipeline_mode=pl.Buffered(1))
    x1, gla, swak, swav = pl.pallas_call(
        functools.partial(_sample_mixer_kernel, nb=nb, tq=tq, alpha=alpha),
        grid=(B // nb,),
        in_specs=[_SMEM_SPEC, blk(tq, D_MODEL), rope_spec, rope_spec, rope_spec,
                  blk(_MEM_ROWS, MEM_HD), blk(_MEM_ROWS, MEM_HD),
                  blk1(GLA_HEADS, GLA_DK, GLA_DV), blk1(WINDOW, SWA_KW), blk1(WINDOW, SWA_KW)]
                 + _mixer_weight_specs(),
        out_specs=[blk(tq, D_MODEL), blk(GLA_HEADS, GLA_DK, GLA_DV), blk(WINDOW, SWA_KW), blk(WINDOW, SWA_KW)],
        out_shape=[jax.ShapeDtypeStruct((B, tq, D_MODEL), f32),
                   jax.ShapeDtypeStruct((B, GLA_HEADS, GLA_DK, GLA_DV), f32),
                   jax.ShapeDtypeStruct((B, WINDOW, SWA_KW), f32),
                   jax.ShapeDtypeStruct((B, WINDOW, SWA_KW), f32)],
        compiler_params=_params(("arbitrary",)),
        name="sample_mixer",
    )(sinks, x, *rope, memk, memv, gla0, swak0, swav0, *mixer_w)

    assert B % SUBLANES == 0
    fblk = lambda *s: pl.BlockSpec((SUBLANES,) + s, lambda i: (i,) + (0,) * len(s))
    y, conv = pl.pallas_call(
        functools.partial(_sample_ffn_kernel, nb=SUBLANES, tq=tq, alpha=alpha),
        grid=(B // SUBLANES,),
        in_specs=[fblk(tq, D_MODEL), fblk(CONV_W - 1, 2 * D_FF)] + _ffn_weight_specs(),
        out_specs=[fblk(tq, D_MODEL), fblk(CONV_W - 1, 2 * D_FF)],
        out_shape=[jax.ShapeDtypeStruct((B, tq, D_MODEL), f32),
                   jax.ShapeDtypeStruct((B, CONV_W - 1, 2 * D_FF), f32)],
        scratch_shapes=[pltpu.VMEM(((CONV_W - 1 + tq) * SUBLANES, 2 * D_FF), f32)],
        compiler_params=_params(("arbitrary",)),
        name="sample_ffn",
    )(x1, conv0, *ffn_w)
    return y, gla, swak, swav, conv


def kernel(x_prompt, x_sample, cache_swa_k, cache_swa_v, state_gla, cache_mem_k, cache_mem_v, cache_ffn_conv, mem_prompt, ln1_g, ln1_b, ln2_g, ln2_b, w_in, b_gate, w_gla_a2, b_gla_a, gla_norm_g, swa_sinks, w_mem_kv, w_br_gla, w_br_swa, w_br_mem, w_o, w_up, conv_w, conv_b, w_down):
    depth = w_in.shape[0]
    alpha = float((2 * depth) ** 0.25)
    Bp = x_prompt.shape[0]
    Bs = x_sample.shape[0]
    hp, hs = x_prompt, x_sample
    outs = [[] for _ in range(10)]
    row = lambda a: a.reshape(1, -1)
    for l in range(depth):
        w2 = jnp.pad(w_gla_a2[l], ((0, LANES - GLA_RANK), (0, 0)))
        mixer_w = _split_w_in(w_in[l]) + (
            _bf(w2), row(b_gla_a[l]), row(gla_norm_g[l]),
            _bf(w_br_gla[l]), _bf(w_br_swa[l]), _bf(w_br_mem[l]), row(b_gate[l]), _bf(w_o[l]),
            row(ln1_g[l]), row(ln1_b[l]))
        ffn_w = (_bf(w_up[l]), conv_w[l], row(conv_b[l]), _bf(w_down[l]), row(ln2_g[l]), row(ln2_b[l]))
        sinks = swa_sinks[l]
        hp, g_p, k_p, v_p, mk_p, mv_p, c_p = _prompt_layer(
            hp, mem_prompt, sinks, mixer_w, ffn_w, _bf(w_mem_kv[l]), alpha)
        hs, g_s, k_s, v_s, c_s = _sample_layer(
            hs, state_gla[l],
            cache_swa_k[l].reshape(Bs, WINDOW, SWA_KW), cache_swa_v[l].reshape(Bs, WINDOW, SWA_KW),
            cache_mem_k[l].reshape(Bs, _MEM_ROWS, MEM_HD), cache_mem_v[l].reshape(Bs, _MEM_ROWS, MEM_HD),
            cache_ffn_conv[l], sinks, mixer_w, ffn_w, alpha)
        kv5 = lambda a, b: a.reshape(b, WINDOW, SWA_KV_HEADS, SWA_HD)
        m5 = lambda a: a.reshape(Bp, MEM_TOKENS, MEM_HEADS, MEM_HD)
        for lst, val in zip(outs, (kv5(k_p, Bp), kv5(v_p, Bp), g_p, m5(mk_p), m5(mv_p), c_p,
                                   kv5(k_s, Bs), kv5(v_s, Bs), g_s, c_s)):
            lst.append(val)
    return (hp, hs) + tuple(jnp.stack(o) for o in outs)
```

```python
import functools

import jax
import jax.numpy as jnp
import numpy as np
from jax import lax
from jax.experimental import pallas as pl
from jax.experimental.pallas import tpu as pltpu

D_MODEL = 1024
CHUNK = 64
GLA_HEADS = 4
GLA_DK = 128
GLA_DV = 128
GLA_RANK = 16
GLA_TAU = 16.0
SWA_HEADS = 8
SWA_KV_HEADS = 2
SWA_HD = 64
WINDOW = 128
ROPE_DIM = 16
ROPE_THETA = 500000.0
MEM_TOKENS = 256
MEM_HEADS = 4
MEM_HD = 128
D_FF = 2816
CONV_W = 3
N_BRANCH = 3
PAST_LEN = 2048
LN_EPS = 1e-5

LANES = 128
SUBLANES = 8
V7X_VMEM_LIMIT = 56 * 1024 * 1024
PROMPT_ROWS = 256
PROMPT_FFN_ROWS = 512
FFN_COLS = 256
SAMPLE_ROWS = 256
GLA_W = GLA_HEADS * GLA_DK
SWA_QW = SWA_HEADS * SWA_HD
SWA_KW = SWA_KV_HEADS * SWA_HD
MEM_W = MEM_HEADS * MEM_HD
GATE_W = N_BRANCH * D_MODEL
SWA_PAIR = 2 * CHUNK

OFF_GQ = 0
OFF_GK = OFF_GQ + GLA_W
OFF_GV = OFF_GK + GLA_W
OFF_GR = OFF_GV + GLA_W
IN_A_COLS = OFF_GR + GLA_W
OFF_SQ = 0
OFF_SK = OFF_SQ + SWA_QW
OFF_SV = OFF_SK + SWA_KW
OFF_MQ = OFF_SV + SWA_KW
OFF_GL = OFF_MQ + MEM_W
IN_B_COLS = OFF_GL + GATE_W

_BF = jnp.bfloat16
_F32 = jnp.float32


def _bf(x):
    return x.astype(_BF)


def _dot(a, b):
    return jnp.dot(a, b, preferred_element_type=_F32)


def _dot_nt(a, b):
    return lax.dot_general(a, b, (((1,), (1,)), ((), ())), preferred_element_type=_F32)


def _dot_tn(a, b):
    return lax.dot_general(a, b, (((0,), (0,)), ((), ())), preferred_element_type=_F32)


def _layer_norm(h, g, b):
    mu = jnp.mean(h, axis=-1, keepdims=True)
    d = h - mu
    var = jnp.mean(d * d, axis=-1, keepdims=True)
    return d * lax.rsqrt(var + LN_EPS) * g + b


def _log_sigmoid(x):
    return -(jnp.maximum(-x, 0.0) + jnp.log(1.0 + jnp.exp(-jnp.abs(x))))


def _rope(a, c, s_dn, s_up):
    outs = []
    for j in range(a.shape[1] // LANES):
        slab = a[:, LANES * j:LANES * (j + 1)]
        fwd = pltpu.roll(slab, LANES - ROPE_DIM // 2, axis=1)
        bwd = pltpu.roll(slab, ROPE_DIM // 2, axis=1)
        outs.append(slab * c + fwd * s_dn + bwd * s_up)
    return outs[0] if len(outs) == 1 else jnp.concatenate(outs, axis=1)


def _gla_tile(q, k, v, log_a, chunk, nchunk, state_in, state_out, chained):
    R = chunk * nchunk
    row = lax.broadcasted_iota(jnp.int32, (R, R), 0)
    col = lax.broadcasted_iota(jnp.int32, (R, R), 1)
    tril = (row // chunk == col // chunk) & (col <= row)
    ltri = jnp.where(tril, 1.0, 0.0).astype(_BF)
    hi = _bf(log_a)
    lo = _bf(log_a - hi.astype(_F32))
    b = _dot(ltri, hi) + _dot(ltri, lo)
    b_last = [b[chunk * (c + 1) - 1:chunk * (c + 1), :] for c in range(nchunk)]
    bl_rows = jnp.concatenate([jnp.broadcast_to(bl, (chunk, GLA_W)) for bl in b_last], axis=0) \
        if nchunk > 1 else jnp.broadcast_to(b_last[0], (chunk, GLA_W))
    q_dec = _bf(q * (GLA_DK ** -0.5) * jnp.exp(b))
    k_inv = _bf(k * jnp.exp(-b))
    k_tail = _bf(k * jnp.exp(bl_rows - b))
    vb = _bf(v)
    heads = [slice(GLA_DK * h, GLA_DK * (h + 1)) for h in range(GLA_HEADS)]
    o_intra = []
    for hs in heads:
        att = jnp.where(tril, _dot_nt(q_dec[:, hs], k_inv[:, hs]), 0.0)
        o_intra.append(_dot(_bf(att), vb[:, hs]))
    o_inter = [[None] * nchunk for _ in heads]
    st = [None] * GLA_HEADS
    for c in range(nchunk):
        rs = slice(chunk * c, chunk * (c + 1))
        for h, hs in enumerate(heads):
            if c == 0 or not chained:
                st[h] = state_in(c, h)
            o_inter[h][c] = _dot_nt(q_dec[rs, hs], _bf(st[h]))
            st[h] = st[h] * jnp.exp(b_last[c][:, hs]) + _dot_tn(vb[rs, hs], k_tail[rs, hs])
            if c == nchunk - 1 or not chained:
                state_out(c, h, st[h])
    outs = [o_intra[h] + (jnp.concatenate(o_inter[h], axis=0) if nchunk > 1 else o_inter[h][0])
            for h in range(GLA_HEADS)]
    return jnp.concatenate(outs, axis=1)


def _gla_gate(o, gr, g):
    outs = []
    for h in range(GLA_HEADS):
        hs = slice(GLA_DV * h, GLA_DV * (h + 1))
        oh = o[:, hs]
        ms = jnp.mean(oh * oh, axis=-1, keepdims=True)
        grh = gr[:, hs]
        outs.append(oh * lax.rsqrt(ms + LN_EPS) * g * (grh * jax.nn.sigmoid(grh)))
    return jnp.concatenate(outs, axis=1)


def _lane_half_masks(shape):
    lane = lax.broadcasted_iota(jnp.int32, shape, 1)
    return lane < SWA_HD, lane >= SWA_HD


def _swa_split_q(q):
    lo_m, hi_m = _lane_half_masks((q.shape[0], LANES))
    out = []
    for j in range(SWA_QW // LANES):
        slab = q[:, LANES * j:LANES * (j + 1)]
        out.append((_bf(jnp.where(lo_m, slab, 0.0)), _bf(jnp.where(hi_m, slab, 0.0))))
    return out


def _swa_dup_k(k):
    lo_m, _ = _lane_half_masks(k.shape)
    kr = pltpu.roll(k, SWA_HD, axis=1)
    return [_bf(jnp.where(lo_m, k, kr)), _bf(jnp.where(lo_m, kr, k))]


def _swa_place_v(v):
    lo_m, hi_m = _lane_half_masks(v.shape)
    vr = pltpu.roll(v, SWA_HD, axis=1)
    return [(_bf(jnp.where(lo_m, v, 0.0)), _bf(jnp.where(hi_m, vr, 0.0))),
            (_bf(jnp.where(lo_m, vr, 0.0)), _bf(jnp.where(hi_m, v, 0.0)))]


def _swa_blocks(qrows, kks, vlos, vhis, sink_col, lowers, upper, tq):
    n = len(kks)
    stack = lambda xs: jnp.concatenate(xs, axis=0) if n > 1 else xs[0]
    s = stack([_dot_nt(qrows[i], kks[i]) for i in range(n)])
    if lowers is not None:
        kcol = lax.broadcasted_iota(jnp.int32, s.shape, 1)
        s = jnp.where(kcol >= stack(lowers), jnp.where(kcol <= stack([upper] * n), s, -jnp.inf), -jnp.inf)
    sink = stack([sink_col] * n)
    m = jnp.maximum(jnp.max(s, axis=-1, keepdims=True), sink)
    p = jnp.exp(s - m)
    den = jnp.sum(p, axis=-1, keepdims=True) + jnp.exp(sink - m)
    p = _bf(p / den)
    outs = []
    for i in range(n):
        r = 4 * tq * i
        outs.append((_dot(p[r:r + tq], vlos[i]) + _dot(p[r + tq:r + 2 * tq], vhis[i]),
                     _dot(p[r + 2 * tq:r + 3 * tq], vlos[i]) + _dot(p[r + 3 * tq:r + 4 * tq], vhis[i])))
    return outs


def _sink_col(sinks_ref, g, tq):
    r = lax.broadcasted_iota(jnp.int32, (4 * tq, 1), 0)
    base = 4 * g
    return jnp.where(r < tq, sinks_ref[base],
                     jnp.where(r < 2 * tq, sinks_ref[base + 1],
                               jnp.where(r < 3 * tq, sinks_ref[base + 2], sinks_ref[base + 3])))


def _mem_head(ref, h):
    return _bf(ref[pl.ds(h, MEM_TOKENS, stride=MEM_HEADS), :])


def _mem_attention(q, mk_refs, mv_refs):
    qb = _bf(q)
    nseg = len(mk_refs)
    tq = q.shape[0] // nseg
    outs = []
    for h in range(MEM_HEADS):
        hs = slice(MEM_HD * h, MEM_HD * (h + 1))
        s = [_dot_nt(qb[tq * i:tq * (i + 1), hs], _mem_head(mk_refs[i], h)) for i in range(nseg)]
        s = jnp.concatenate(s, axis=0) if nseg > 1 else s[0]
        m = jnp.max(s, axis=-1, keepdims=True)
        p = jnp.exp(s - m)
        p = _bf(p / jnp.sum(p, axis=-1, keepdims=True))
        o = [_dot(p[tq * i:tq * (i + 1)], _mem_head(mv_refs[i], h)) for i in range(nseg)]
        outs.append(jnp.concatenate(o, axis=0) if nseg > 1 else o[0])
    return jnp.concatenate(outs, axis=1)


def _swa_attention_t(qs, kk_scr, v_scr, sinks_ref, t, rows):
    nk = WINDOW + SWA_PAIR
    npair = rows // SWA_PAIR
    bw = 4 * SWA_PAIR
    blocks = [(g, p) for g in range(SWA_KV_HEADS) for p in range(npair)]
    sts = []
    for g, p in blocks:
        rs = slice(SWA_PAIR * p, SWA_PAIR * (p + 1))
        qrows = jnp.concatenate([qs[2 * g][0][rs], qs[2 * g][1][rs],
                                 qs[2 * g + 1][0][rs], qs[2 * g + 1][1][rs]], axis=0)
        sts.append(_dot_nt(kk_scr[g, SWA_PAIR * p:SWA_PAIR * p + nk, :], qrows))
    st = jnp.concatenate(sts, axis=1)
    width = bw * len(blocks)
    col = lax.broadcasted_iota(jnp.int32, (1, width), 1)
    pair = (col // bw) % npair
    band_lo = CHUNK * ((col % SWA_PAIR) // CHUNK)
    lower = jnp.maximum(band_lo, WINDOW - (t * rows + SWA_PAIR * pair))
    upper = band_lo + (WINDOW + CHUNK - 1)
    krow = lax.broadcasted_iota(jnp.int32, (nk, width), 0)
    st = jnp.where(krow >= lower, jnp.where(krow <= upper, st, -jnp.inf), -jnp.inf)
    head = 4 * (col // (bw * npair)) + (col % bw) // SWA_PAIR
    sink = jnp.zeros((1, width), _F32)
    for h in range(SWA_HEADS):
        sink = jnp.where(head == h, sinks_ref[h], sink)
    m = jnp.maximum(jnp.max(st, axis=0, keepdims=True), sink)
    pt = jnp.exp(st - m)
    inv = 1.0 / (jnp.sum(pt, axis=0, keepdims=True) + jnp.exp(sink - m))
    pb = _bf(pt)
    pieces = [[None] * npair for _ in range(SWA_HEADS)]
    for i, (g, p) in enumerate(blocks):
        cs = slice(bw * i, bw * (i + 1))
        o2 = _dot_tn(v_scr[SWA_PAIR * p:SWA_PAIR * p + nk, :], pb[:, cs]) * inv[:, cs]
        for hh in range(4):
            pieces[4 * g + hh][p] = o2[SWA_HD * g:SWA_HD * (g + 1), SWA_PAIR * hh:SWA_PAIR * (hh + 1)]
    return jnp.concatenate([jnp.concatenate(ps, axis=1) if npair > 1 else ps[0] for ps in pieces], axis=0)


def _mem_attention_t(q, mk_ref, mv_ref):
    qb = _bf(q)
    R = q.shape[0]
    st = jnp.concatenate([_dot_nt(_mem_head(mk_ref, h), qb[:, MEM_HD * h:MEM_HD * (h + 1)])
                          for h in range(MEM_HEADS)], axis=1)
    pt = jnp.exp(st - jnp.max(st, axis=0, keepdims=True))
    inv = 1.0 / jnp.sum(pt, axis=0, keepdims=True)
    pb = _bf(pt)
    outs = [_dot_tn(_mem_head(mv_ref, h), pb[:, R * h:R * (h + 1)]) * inv[:, R * h:R * (h + 1)]
            for h in range(MEM_HEADS)]
    return jnp.concatenate(outs, axis=0)


def _gate(xb, i, w_inb_ref, bgate_ref):
    gl = _dot(xb, w_inb_ref[:, OFF_GL + D_MODEL * i:OFF_GL + D_MODEL * (i + 1)])
    return jax.nn.sigmoid(gl + bgate_ref[:, D_MODEL * i:D_MODEL * (i + 1)])


def _merge_ln1(x, gated_sum, wo_ref, g_ref, b_ref, alpha):
    mix = _dot(_bf(gated_sum), wo_ref[...])
    return _layer_norm(alpha * x + mix, g_ref[...], b_ref[...])


def _forget_log(xb, w_ga_ref, w2_ref, ba_ref):
    ga = _dot(xb, w_ga_ref[...])
    return _log_sigmoid(_dot(_bf(ga), w2_ref[...]) + ba_ref[...]) * (1.0 / GLA_TAU)


def _prompt_mixer_kernel(sinks_ref, x_ref, rc_ref, rdn_ref, rup_ref, mk_ref, mv_ref, w_ina_ref, w_ga_ref,
                         w_inb_ref, w2_ref, ba_ref, gng_ref, wbg_ref, wbs_ref, wbm_ref, bgate_ref, wo_ref,
                         ln_g_ref, ln_b_ref,
                         out_ref, gla_out_ref, swak_out_ref, swav_out_ref,
                         s_scr, kk_scr, vv_scr, *, rows, alpha):
    t = pl.program_id(1)
    nchunk = rows // CHUNK

    @pl.when(t == 0)
    def _():
        s_scr[...] = jnp.zeros_like(s_scr)
        kk_scr[:, 0:WINDOW, :] = jnp.zeros((SWA_KV_HEADS, WINDOW, LANES), _BF)
        vv_scr[0:WINDOW, :] = jnp.zeros((WINDOW, LANES), _BF)

    x = x_ref[0]
    xb = _bf(x)

    log_a = _forget_log(xb, w_ga_ref, w2_ref, ba_ref)

    def proj_a(off):
        return _dot(xb, w_ina_ref[:, off:off + GLA_W])

    def state_in(c, h):
        return s_scr[h]

    def state_out(c, h, st):
        s_scr[h] = st
        gla_out_ref[0, h] = jnp.transpose(st)

    gq, gk, gv = proj_a(OFF_GQ), proj_a(OFF_GK), proj_a(OFF_GV)
    gate_a = _gate(xb, 0, w_inb_ref, bgate_ref)
    o_a = _gla_tile(gq, gk, gv, log_a, CHUNK, nchunk, state_in, state_out, chained=True)
    o_a = _gla_gate(o_a, proj_a(OFF_GR), gng_ref[...])
    gated = gate_a * _dot(_bf(o_a), wbg_ref[...])

    rc, rdn, rup = rc_ref[...], rdn_ref[...], rup_ref[...]
    q = _rope(_dot(xb, w_inb_ref[:, OFF_SQ:OFF_SQ + SWA_QW]), rc, rdn, rup) * (SWA_HD ** -0.5)
    k = _rope(_dot(xb, w_inb_ref[:, OFF_SK:OFF_SK + SWA_KW]), rc, rdn, rup)
    v = _dot(xb, w_inb_ref[:, OFF_SV:OFF_SV + SWA_KW])
    swak_out_ref[0] = k[rows - WINDOW:rows, :]
    swav_out_ref[0] = v[rows - WINDOW:rows, :]
    gate_b = _gate(xb, 1, w_inb_ref, bgate_ref)
    kk = _swa_dup_k(k)
    vb = _bf(v)
    for g in range(SWA_KV_HEADS):
        kk_scr[g, WINDOW:WINDOW + rows, :] = kk[g]
    vv_scr[WINDOW:WINDOW + rows, :] = vb
    o_bt = _swa_attention_t(_swa_split_q(q), kk_scr, vv_scr, sinks_ref, t, rows)
    gated = gated + gate_b * _dot_tn(_bf(o_bt), wbs_ref[...])
    for g in range(SWA_KV_HEADS):
        kk_scr[g, 0:WINDOW, :] = kk[g][rows - WINDOW:rows]
    vv_scr[0:WINDOW, :] = vb[rows - WINDOW:rows]

    qm = _dot(xb, w_inb_ref[:, OFF_MQ:OFF_MQ + MEM_W]) * (MEM_HD ** -0.5)
    gate_c = _gate(xb, 2, w_inb_ref, bgate_ref)
    o_ct = _mem_attention_t(qm, mk_ref.at[0], mv_ref.at[0])
    gated = gated + gate_c * _dot_tn(_bf(o_ct), wbm_ref[...])

    out_ref[0] = _merge_ln1(x, gated, wo_ref, ln_g_ref, ln_b_ref, alpha)


def _sample_mixer_kernel(sinks_ref, x_ref, rc_ref, rdn_ref, rup_ref, mk_ref, mv_ref, gla_in_ref,
                         swak_in_ref, swav_in_ref, w_ina_ref, w_ga_ref, w_inb_ref, w2_ref,
                         ba_ref, gng_ref, wbg_ref, wbs_ref, wbm_ref, bgate_ref, wo_ref, ln_g_ref, ln_b_ref,
                         out_ref, gla_out_ref, swak_out_ref, swav_out_ref, *, nb, tq, alpha):
    rows = nb * tq
    x = x_ref[...].reshape(rows, D_MODEL)
    xb = _bf(x)

    log_a = _forget_log(xb, w_ga_ref, w2_ref, ba_ref)

    def proj_a(off):
        return _dot(xb, w_ina_ref[:, off:off + GLA_W])

    def state_in(c, h):
        return jnp.transpose(gla_in_ref[c, h])

    def state_out(c, h, st):
        gla_out_ref[c, h] = jnp.transpose(st)

    gq, gk, gv = proj_a(OFF_GQ), proj_a(OFF_GK), proj_a(OFF_GV)
    gate_a = _gate(xb, 0, w_inb_ref, bgate_ref)
    o_a = _gla_tile(gq, gk, gv, log_a, tq, nb, state_in, state_out, chained=False)
    o_a = _gla_gate(o_a, proj_a(OFF_GR), gng_ref[...])
    gated = gate_a * _dot(_bf(o_a), wbg_ref[...])

    rc = jnp.concatenate([rc_ref[...]] * nb, axis=0)
    rdn = jnp.concatenate([rdn_ref[...]] * nb, axis=0)
    rup = jnp.concatenate([rup_ref[...]] * nb, axis=0)
    q = _rope(_dot(xb, w_inb_ref[:, OFF_SQ:OFF_SQ + SWA_QW]), rc, rdn, rup) * (SWA_HD ** -0.5)
    k = _rope(_dot(xb, w_inb_ref[:, OFF_SK:OFF_SK + SWA_KW]), rc, rdn, rup)
    v = _dot(xb, w_inb_ref[:, OFF_SV:OFF_SV + SWA_KW])
    gate_b = _gate(xb, 1, w_inb_ref, bgate_ref)
    qs = _swa_split_q(q)
    slabs = [[None] * nb for _ in range(SWA_QW // LANES)]
    kks, vvs = [], []
    for bi in range(nb):
        rs = slice(tq * bi, tq * (bi + 1))
        k_all = jnp.concatenate([swak_in_ref[bi], k[rs]], axis=0)
        v_all = jnp.concatenate([swav_in_ref[bi], v[rs]], axis=0)
        swak_out_ref[bi] = k_all[tq:tq + WINDOW]
        swav_out_ref[bi] = v_all[tq:tq + WINDOW]
        kks.append(_swa_dup_k(k_all))
        vvs.append(_swa_place_v(v_all))
    for g in range(SWA_KV_HEADS):
        qrows = []
        for bi in range(nb):
            rs = slice(tq * bi, tq * (bi + 1))
            qrows.append(jnp.concatenate([qs[2 * g][0][rs], qs[2 * g][1][rs],
                                          qs[2 * g + 1][0][rs], qs[2 * g + 1][1][rs]], axis=0))
        outs = _swa_blocks(qrows, [kk[g] for kk in kks], [vv[g][0] for vv in vvs], [vv[g][1] for vv in vvs],
                           _sink_col(sinks_ref, g, tq), None, None, tq)
        for bi in range(nb):
            slabs[2 * g][bi], slabs[2 * g + 1][bi] = outs[bi]
    o_b = jnp.concatenate([jnp.concatenate(s, axis=0) if nb > 1 else s[0] for s in slabs], axis=1)
    gated = gated + gate_b * _dot(_bf(o_b), wbs_ref[...])

    qm = _dot(xb, w_inb_ref[:, OFF_MQ:OFF_MQ + MEM_W]) * (MEM_HD ** -0.5)
    gate_c = _gate(xb, 2, w_inb_ref, bgate_ref)
    o_c = _mem_attention(qm, [mk_ref.at[bi] for bi in range(nb)], [mv_ref.at[bi] for bi in range(nb)])
    gated = gated + gate_c * _dot(_bf(o_c), wbm_ref[...])

    out = _merge_ln1(x, gated, wo_ref, ln_g_ref, ln_b_ref, alpha)
    out_ref[...] = out.reshape(nb, tq, D_MODEL)


_GELU_K1 = -2.0 * float(np.log2(np.e)) * float(np.sqrt(2.0 / np.pi))
_GELU_K2 = _GELU_K1 * 0.044715


def _conv(u2, u1, u0, cw_ref, cb_ref, cs):
    return cb_ref[:, cs] + u2 * cw_ref[0:1, cs] + u1 * cw_ref[1:2, cs] + u0 * cw_ref[2:3, cs]


def _geglu(g, v):
    return g * v / (1.0 + jnp.exp2(g * (g * g * _GELU_K2 + _GELU_K1)))


def _prompt_ffn_kernel(x_ref, wup_ref, cw_ref, cb_ref, wdn_ref, ln_g_ref, ln_b_ref,
                       out_ref, conv_out_ref, u_scr, *, rows, alpha):
    nj = rows // SUBLANES
    base = 2 * SUBLANES
    last = slice(base + rows - SUBLANES, base + rows)
    last2 = slice(base + rows - 2 * SUBLANES, base + rows - SUBLANES)

    @pl.when(pl.program_id(1) == 0)
    def _():
        u_scr[base + rows - 2 * SUBLANES:base + rows, :] = jnp.zeros((2 * SUBLANES, 2 * D_FF), _F32)

    prev_last, prev_last2 = u_scr[last, :], u_scr[last2, :]
    x = jnp.swapaxes(x_ref[0].reshape(SUBLANES, nj, D_MODEL), 0, 1).reshape(rows, D_MODEL)
    u_scr[base:base + rows, :] = _dot(_bf(x), wup_ref[...])
    sub = lax.broadcasted_iota(jnp.int32, (SUBLANES, FFN_COLS), 0)

    def conv_cols(cs):
        wrap = lambda prev, cur: pltpu.roll(jnp.where(sub == SUBLANES - 1, prev, cur), 1, axis=0)
        u_scr[SUBLANES:base, cs] = wrap(prev_last[:, cs], u_scr[last, cs])
        u_scr[0:SUBLANES, cs] = wrap(prev_last2[:, cs], u_scr[last2, cs])
        return _conv(u_scr[0:rows, cs], u_scr[SUBLANES:SUBLANES + rows, cs], u_scr[base:base + rows, cs],
                     cw_ref, cb_ref, cs)

    h = [_bf(_geglu(conv_cols(slice(c0, c0 + FFN_COLS)), conv_cols(slice(D_FF + c0, D_FF + c0 + FFN_COLS))))
         for c0 in range(0, D_FF, FFN_COLS)]
    conv_out_ref[0] = jnp.concatenate([u_scr[base + rows - SUBLANES - 1:base + rows - SUBLANES, :],
                                       u_scr[base + rows - 1:base + rows, :]], axis=0)
    f = _dot(jnp.concatenate(h, axis=1), wdn_ref[...])
    y = _layer_norm(alpha * x + f, ln_g_ref[...], ln_b_ref[...])
    out_ref[0] = jnp.swapaxes(y.reshape(nj, SUBLANES, D_MODEL), 0, 1).reshape(rows, D_MODEL)


def _sample_ffn_kernel(x_ref, hist_ref, wup_ref, cw_ref, cb_ref, wdn_ref, ln_g_ref, ln_b_ref,
                       out_ref, conv_out_ref, u_scr, *, nb, tq, alpha):
    rows = nb * tq
    base = (CONV_W - 1) * nb
    x = jnp.swapaxes(x_ref[...], 0, 1).reshape(rows, D_MODEL)
    u_scr[0:base, :] = jnp.swapaxes(hist_ref[...], 0, 1).reshape(base, 2 * D_FF)
    u_scr[base:base + rows, :] = _dot(_bf(x), wup_ref[...])
    conv = lambda cs: _conv(u_scr[0:rows, cs], u_scr[nb:nb + rows, cs], u_scr[base:base + rows, cs],
                            cw_ref, cb_ref, cs)
    h = _geglu(conv(slice(0, D_FF)), conv(slice(D_FF, 2 * D_FF)))
    conv_out_ref[...] = jnp.swapaxes(u_scr[rows:rows + base, :].reshape(CONV_W - 1, nb, 2 * D_FF), 0, 1)
    f = _dot(_bf(h), wdn_ref[...])
    y = _layer_norm(alpha * x + f, ln_g_ref[...], ln_b_ref[...])
    out_ref[...] = jnp.swapaxes(y.reshape(tq, nb, D_MODEL), 0, 1)


def _mem_kv_kernel(m_ref, w_ref, k_ref, v_ref):
    kv = _dot(_bf(m_ref[0]), w_ref[...])
    for h in range(MEM_HEADS):
        k_ref[0, pl.ds(h, MEM_TOKENS, stride=MEM_HEADS), :] = kv[:, MEM_HD * h:MEM_HD * (h + 1)]
        v_ref[0, pl.ds(h, MEM_TOKENS, stride=MEM_HEADS), :] = kv[:, MEM_W + MEM_HD * h:MEM_W + MEM_HD * (h + 1)]


def _const_spec(shape):
    nd = len(shape)
    return pl.BlockSpec(shape, lambda *_: (0,) * nd, pipeline_mode=pl.Buffered(1))


def _rope_tables(pos):
    half = ROPE_DIM // 2
    inv = np.float32(ROPE_THETA) ** (-np.arange(half, dtype=np.float32) / np.float32(half))
    ang = pos.astype(np.float32)[:, None] * inv[None, :]
    cos, sin = np.cos(ang), np.sin(ang)
    T = pos.shape[0]
    ones = np.ones((T, SWA_HD - ROPE_DIM), np.float32)
    zeros = np.zeros((T, SWA_HD - ROPE_DIM), np.float32)
    zh = np.zeros((T, half), np.float32)
    c = np.concatenate([cos, cos, ones], axis=1)
    dn = np.concatenate([-sin, zh, zeros], axis=1)
    up = np.concatenate([zh, sin, zeros], axis=1)
    rep = LANES // SWA_HD
    return tuple(jnp.asarray(np.concatenate([a] * rep, axis=1), dtype=_F32) for a in (c, dn, up))


def _pack_cols_kernel(wt_ref, o_ref, *, keep):
    t = jnp.transpose(wt_ref[...])
    if keep < t.shape[1]:
        t = jnp.where(lax.broadcasted_iota(jnp.int32, t.shape, 1) < keep, t, 0.0)
    o_ref[...] = _bf(t)


def _pack_cols(wt, col0, ncols, keep, name):
    blk = min(ncols, 2 * LANES)
    k = wt.shape[1]
    return pl.pallas_call(
        functools.partial(_pack_cols_kernel, keep=keep),
        grid=(ncols // blk,),
        in_specs=[pl.BlockSpec((pl.Element(blk), pl.Element(k)),
                               lambda i: (pl.multiple_of(col0 + blk * i, SUBLANES), 0))],
        out_specs=pl.BlockSpec((k, blk), lambda i: (0, i)),
        out_shape=jax.ShapeDtypeStruct((k, ncols), _BF),
        compiler_params=_params(("arbitrary",)),
        name=name,
    )(wt)


def _split_w_in(w_in):
    wt = jnp.transpose(w_in)
    return (_pack_cols(wt, 0, IN_A_COLS, IN_A_COLS, "pack_w_in_a"),
            _pack_cols(wt, IN_A_COLS, LANES, GLA_RANK, "pack_w_in_ga"),
            _pack_cols(wt, IN_A_COLS + GLA_RANK, IN_B_COLS, IN_B_COLS, "pack_w_in_b"))


def _mixer_weight_specs():
    return [
        _const_spec((D_MODEL, IN_A_COLS)),
        _const_spec((D_MODEL, LANES)),
        _const_spec((D_MODEL, IN_B_COLS)),
        _const_spec((LANES, GLA_W)),
        _const_spec((1, GLA_W)),
        _const_spec((1, GLA_DV)),
        _const_spec((GLA_W, D_MODEL)),
        _const_spec((SWA_QW, D_MODEL)),
        _const_spec((MEM_W, D_MODEL)),
        _const_spec((1, GATE_W)),
        _const_spec((D_MODEL, D_MODEL)),
        _const_spec((1, D_MODEL)),
        _const_spec((1, D_MODEL)),
    ]


def _ffn_weight_specs():
    return [
        _const_spec((D_MODEL, 2 * D_FF)),
        _const_spec((CONV_W, 2 * D_FF)),
        _const_spec((1, 2 * D_FF)),
        _const_spec((D_FF, D_MODEL)),
        _const_spec((1, D_MODEL)),
        _const_spec((1, D_MODEL)),
    ]


_SMEM_SPEC = pl.BlockSpec(memory_space=pltpu.SMEM)
_MEM_ROWS = MEM_TOKENS * MEM_HEADS


def _params(sem, flags=None):
    return pltpu.CompilerParams(dimension_semantics=sem, vmem_limit_bytes=V7X_VMEM_LIMIT, flags=flags)


def _prompt_layer(x, mem, sinks, mixer_w, ffn_w, w_mem_kv, alpha):
    B, T, _ = x.shape
    rows = min(PROMPT_ROWS, T)
    nt = T // rows
    assert T % rows == 0 and rows % SWA_PAIR == 0 and rows >= WINDOW
    f32 = jnp.float32
    mk, mv = pl.pallas_call(
        _mem_kv_kernel,
        grid=(B,),
        in_specs=[pl.BlockSpec((1, MEM_TOKENS, D_MODEL), lambda b: (b, 0, 0)),
                  _const_spec((D_MODEL, 2 * MEM_W))],
        out_specs=[pl.BlockSpec((1, _MEM_ROWS, MEM_HD), lambda b: (b, 0, 0))] * 2,
        out_shape=[jax.ShapeDtypeStruct((B, _MEM_ROWS, MEM_HD), f32)] * 2,
        compiler_params=_params(("arbitrary",)),
        name="mem_kv",
    )(mem, w_mem_kv)

    rope = _rope_tables(np.arange(T))
    rope_spec = pl.BlockSpec((rows, LANES), lambda b, t: (t, 0))
    per_batch = lambda *blk: pl.BlockSpec((1,) + blk, lambda b, t: (b,) + (0,) * len(blk))
    x1, gla, swak, swav = pl.pallas_call(
        functools.partial(_prompt_mixer_kernel, rows=rows, alpha=alpha),
        grid=(B, nt),
        in_specs=[_SMEM_SPEC,
                  pl.BlockSpec((1, rows, D_MODEL), lambda b, t: (b, t, 0)),
                  rope_spec, rope_spec, rope_spec,
                  per_batch(_MEM_ROWS, MEM_HD), per_batch(_MEM_ROWS, MEM_HD)] + _mixer_weight_specs(),
        out_specs=[pl.BlockSpec((1, rows, D_MODEL), lambda b, t: (b, t, 0)),
                   per_batch(GLA_HEADS, GLA_DK, GLA_DV),
                   per_batch(WINDOW, SWA_KW), per_batch(WINDOW, SWA_KW)],
        out_shape=[jax.ShapeDtypeStruct((B, T, D_MODEL), f32),
                   jax.ShapeDtypeStruct((B, GLA_HEADS, GLA_DK, GLA_DV), f32),
                   jax.ShapeDtypeStruct((B, WINDOW, SWA_KW), f32),
                   jax.ShapeDtypeStruct((B, WINDOW, SWA_KW), f32)],
        scratch_shapes=[pltpu.VMEM((GLA_HEADS, GLA_DK, GLA_DV), f32),
                        pltpu.VMEM((SWA_KV_HEADS, WINDOW + rows, LANES), _BF),
                        pltpu.VMEM((WINDOW + rows, LANES), _BF)],
        compiler_params=_params(("arbitrary", "arbitrary")),
        name="prompt_mixer",
    )(sinks, x, *rope, mk, mv, *mixer_w)

    frows = min(PROMPT_FFN_ROWS, T)
    assert T % frows == 0 and frows >= CONV_W - 1
    y, conv = pl.pallas_call(
        functools.partial(_prompt_ffn_kernel, rows=frows, alpha=alpha),
        grid=(B, T // frows),
        in_specs=[pl.BlockSpec((1, frows, D_MODEL), lambda b, t: (b, t, 0))] + _ffn_weight_specs(),
        out_specs=[pl.BlockSpec((1, frows, D_MODEL), lambda b, t: (b, t, 0)),
                   per_batch(CONV_W - 1, 2 * D_FF)],
        out_shape=[jax.ShapeDtypeStruct((B, T, D_MODEL), f32),
                   jax.ShapeDtypeStruct((B, CONV_W - 1, 2 * D_FF), f32)],
        scratch_shapes=[pltpu.VMEM((2 * SUBLANES + frows, 2 * D_FF), f32)],
        compiler_params=_params(("arbitrary", "arbitrary")),
        name="prompt_ffn",
    )(x1, *ffn_w)
    return y, gla, swak, swav, mk, mv, conv


def _sample_layer(x, gla0, swak0, swav0, memk, memv, conv0, sinks, mixer_w, ffn_w, alpha):
    B, tq, _ = x.shape
    nb = max(1, min(B, SAMPLE_ROWS // tq))
    assert B % nb == 0 and tq % 16 == 0 and tq >= CONV_W - 1
    f32 = jnp.float32
    rope = _rope_tables(PAST_LEN + np.arange(tq))
    rope_spec = pl.BlockSpec((tq, LANES), lambda i: (0, 0))
    blk = lambda *s: pl.BlockSpec((nb,) + s, lambda i: (i,) + (0,) * len(s))
    blk1 = lambda *s: pl.BlockSpec((nb,) + s, lambda i: (i,) + (0,) * len(s), pipeline_mode=pl.Buffered(1))
    x1, gla, swak, swav = pl.pallas_call(
        functools.partial(_sample_mixer_kernel, nb=nb, tq=tq, alpha=alpha),
        grid=(B // nb,),
        in_specs=[_SMEM_SPEC, blk(tq, D_MODEL), rope_spec, rope_spec, rope_spec,
                  blk(_MEM_ROWS, MEM_HD), blk(_MEM_ROWS, MEM_HD),
                  blk1(GLA_HEADS, GLA_DK, GLA_DV), blk1(WINDOW, SWA_KW), blk1(WINDOW, SWA_KW)]
                 + _mixer_weight_specs(),
        out_specs=[blk(tq, D_MODEL), blk(GLA_HEADS, GLA_DK, GLA_DV), blk(WINDOW, SWA_KW), blk(WINDOW, SWA_KW)],
        out_shape=[jax.ShapeDtypeStruct((B, tq, D_MODEL), f32),
                   jax.ShapeDtypeStruct((B, GLA_HEADS, GLA_DK, GLA_DV), f32),
                   jax.ShapeDtypeStruct((B, WINDOW, SWA_KW), f32),
                   jax.ShapeDtypeStruct((B, WINDOW, SWA_KW), f32)],
        compiler_params=_params(("arbitrary",)),
        name="sample_mixer",
    )(sinks, x, *rope, memk, memv, gla0, swak0, swav0, *mixer_w)

    assert B % SUBLANES == 0
    fblk = lambda *s: pl.BlockSpec((SUBLANES,) + s, lambda i: (i,) + (0,) * len(s))
    y, conv = pl.pallas_call(
        functools.partial(_sample_ffn_kernel, nb=SUBLANES, tq=tq, alpha=alpha),
        grid=(B // SUBLANES,),
        in_specs=[fblk(tq, D_MODEL), fblk(CONV_W - 1, 2 * D_FF)] + _ffn_weight_specs(),
        out_specs=[fblk(tq, D_MODEL), fblk(CONV_W - 1, 2 * D_FF)],
        out_shape=[jax.ShapeDtypeStruct((B, tq, D_MODEL), f32),
                   jax.ShapeDtypeStruct((B, CONV_W - 1, 2 * D_FF), f32)],
        scratch_shapes=[pltpu.VMEM(((CONV_W - 1 + tq) * SUBLANES, 2 * D_FF), f32)],
        compiler_params=_params(("arbitrary",)),
        name="sample_ffn",
    )(x1, conv0, *ffn_w)
    return y, gla, swak, swav, conv


def kernel(x_prompt, x_sample, cache_swa_k, cache_swa_v, state_gla, cache_mem_k, cache_mem_v, cache_ffn_conv, mem_prompt, ln1_g, ln1_b, ln2_g, ln2_b, w_in, b_gate, w_gla_a2, b_gla_a, gla_norm_g, swa_sinks, w_mem_kv, w_br_gla, w_br_swa, w_br_mem, w_o, w_up, conv_w, conv_b, w_down):
    depth = w_in.shape[0]
    alpha = float((2 * depth) ** 0.25)
    Bp = x_prompt.shape[0]
    Bs = x_sample.shape[0]
    hp, hs = x_prompt, x_sample
    outs = [[] for _ in range(10)]
    row = lambda a: a.reshape(1, -1)
    for l in range(depth):
        w2 = jnp.pad(w_gla_a2[l], ((0, LANES - GLA_RANK), (0, 0)))
        mixer_w = _split_w_in(w_in[l]) + (
            _bf(w2), row(b_gla_a[l]), row(gla_norm_g[l]),
            _bf(w_br_gla[l]), _bf(w_br_swa[l]), _bf(w_br_mem[l]), row(b_gate[l]), _bf(w_o[l]),
            row(ln1_g[l]), row(ln1_b[l]))
        ffn_w = (_bf(w_up[l]), conv_w[l], row(conv_b[l]), _bf(w_down[l]), row(ln2_g[l]), row(ln2_b[l]))
        sinks = swa_sinks[l]
        hp, g_p, k_p, v_p, mk_p, mv_p, c_p = _prompt_layer(
            hp, mem_prompt, sinks, mixer_w, ffn_w, _bf(w_mem_kv[l]), alpha)
        hs, g_s, k_s, v_s, c_s = _sample_layer(
            hs, state_gla[l],
            cache_swa_k[l].reshape(Bs, WINDOW, SWA_KW), cache_swa_v[l].reshape(Bs, WINDOW, SWA_KW),
            cache_mem_k[l].reshape(Bs, _MEM_ROWS, MEM_HD), cache_mem_v[l].reshape(Bs, _MEM_ROWS, MEM_HD),
            cache_ffn_conv[l], sinks, mixer_w, ffn_w, alpha)
        kv5 = lambda a, b: a.reshape(b, WINDOW, SWA_KV_HEADS, SWA_HD)
        m5 = lambda a: a.reshape(Bp, MEM_TOKENS, MEM_HEADS, MEM_HD)
        for lst, val in zip(outs, (kv5(k_p, Bp), kv5(v_p, Bp), g_p, m5(mk_p), m5(mv_p), c_p,
                                   kv5(k_s, Bs), kv5(v_s, Bs), g_s, c_s)):
            lst.append(val)
    return (hp, hs) + tuple(jnp.stack(o) for o in outs)
```

```python
import functools

import jax
import jax.numpy as jnp
import numpy as np
from jax import lax
from jax.experimental import pallas as pl
from jax.experimental.pallas import tpu as pltpu

D_MODEL = 1024
CHUNK = 64
GLA_HEADS = 4
GLA_DK = 128
GLA_DV = 128
GLA_RANK = 16
GLA_TAU = 16.0
SWA_HEADS = 8
SWA_KV_HEADS = 2
SWA_HD = 64
WINDOW = 128
ROPE_DIM = 16
ROPE_THETA = 500000.0
MEM_TOKENS = 256
MEM_HEADS = 4
MEM_HD = 128
D_FF = 2816
CONV_W = 3
N_BRANCH = 3
PAST_LEN = 2048
LN_EPS = 1e-5

LANES = 128
SUBLANES = 8
V7X_VMEM_LIMIT = 56 * 1024 * 1024
PROMPT_ROWS = 256
PROMPT_FFN_ROWS = 512
FFN_COLS = 256
SAMPLE_ROWS = 256
GLA_W = GLA_HEADS * GLA_DK
SWA_QW = SWA_HEADS * SWA_HD
SWA_KW = SWA_KV_HEADS * SWA_HD
MEM_W = MEM_HEADS * MEM_HD
GATE_W = N_BRANCH * D_MODEL
SWA_PAIR = 2 * CHUNK

OFF_GQ = 0
OFF_GK = OFF_GQ + GLA_W
OFF_GV = OFF_GK + GLA_W
OFF_GR = OFF_GV + GLA_W
IN_A_COLS = OFF_GR + GLA_W
OFF_SQ = 0
OFF_SK = OFF_SQ + SWA_QW
OFF_SV = OFF_SK + SWA_KW
OFF_MQ = OFF_SV + SWA_KW
OFF_GL = OFF_MQ + MEM_W
IN_B_COLS = OFF_GL + GATE_W

_BF = jnp.bfloat16
_F32 = jnp.float32


def _bf(x):
    return x.astype(_BF)


def _dot(a, b):
    return jnp.dot(a, b, preferred_element_type=_F32)


def _dot_nt(a, b):
    return lax.dot_general(a, b, (((1,), (1,)), ((), ())), preferred_element_type=_F32)


def _dot_tn(a, b):
    return lax.dot_general(a, b, (((0,), (0,)), ((), ())), preferred_element_type=_F32)


def _layer_norm(h, g, b):
    mu = jnp.mean(h, axis=-1, keepdims=True)
    d = h - mu
    var = jnp.mean(d * d, axis=-1, keepdims=True)
    return d * lax.rsqrt(var + LN_EPS) * g + b


def _log_sigmoid(x):
    return -(jnp.maximum(-x, 0.0) + jnp.log(1.0 + jnp.exp(-jnp.abs(x))))


def _rope(a, c, s_dn, s_up):
    outs = []
    for j in range(a.shape[1] // LANES):
        slab = a[:, LANES * j:LANES * (j + 1)]
        fwd = pltpu.roll(slab, LANES - ROPE_DIM // 2, axis=1)
        bwd = pltpu.roll(slab, ROPE_DIM // 2, axis=1)
        outs.append(slab * c + fwd * s_dn + bwd * s_up)
    return outs[0] if len(outs) == 1 else jnp.concatenate(outs, axis=1)


def _gla_tile(q, k, v, log_a, chunk, nchunk, state_in, state_out, chained):
    R = chunk * nchunk
    row = lax.broadcasted_iota(jnp.int32, (R, R), 0)
    col = lax.broadcasted_iota(jnp.int32, (R, R), 1)
    tril = (row // chunk == col // chunk) & (col <= row)
    ltri = jnp.where(tril, 1.0, 0.0).astype(_BF)
    hi = _bf(log_a)
    lo = _bf(log_a - hi.astype(_F32))
    b = _dot(ltri, hi) + _dot(ltri, lo)
    b_last = [b[chunk * (c + 1) - 1:chunk * (c + 1), :] for c in range(nchunk)]
    bl_rows = jnp.concatenate([jnp.broadcast_to(bl, (chunk, GLA_W)) for bl in b_last], axis=0) \
        if nchunk > 1 else jnp.broadcast_to(b_last[0], (chunk, GLA_W))
    q_dec = _bf(q * (GLA_DK ** -0.5) * jnp.exp(b))
    k_inv = _bf(k * jnp.exp(-b))
    k_tail = _bf(k * jnp.exp(bl_rows - b))
    vb = _bf(v)
    heads = [slice(GLA_DK * h, GLA_DK * (h + 1)) for h in range(GLA_HEADS)]
    o_intra = []
    for hs in heads:
        att = jnp.where(tril, _dot_nt(q_dec[:, hs], k_inv[:, hs]), 0.0)
        o_intra.append(_dot(_bf(att), vb[:, hs]))
    o_inter = [[None] * nchunk for _ in heads]
    st = [None] * GLA_HEADS
    for c in range(nchunk):
        rs = slice(chunk * c, chunk * (c + 1))
        for h, hs in enumerate(heads):
            if c == 0 or not chained:
                st[h] = state_in(c, h)
            o_inter[h][c] = _dot_nt(q_dec[rs, hs], _bf(st[h]))
            st[h] = st[h] * jnp.exp(b_last[c][:, hs]) + _dot_tn(vb[rs, hs], k_tail[rs, hs])
            if c == nchunk - 1 or not chained:
                state_out(c, h, st[h])
    outs = [o_intra[h] + (jnp.concatenate(o_inter[h], axis=0) if nchunk > 1 else o_inter[h][0])
            for h in range(GLA_HEADS)]
    return jnp.concatenate(outs, axis=1)


def _gla_gate(o, gr, g):
    outs = []
    for h in range(GLA_HEADS):
        hs = slice(GLA_DV * h, GLA_DV * (h + 1))
        oh = o[:, hs]
        ms = jnp.mean(oh * oh, axis=-1, keepdims=True)
        grh = gr[:, hs]
        outs.append(oh * lax.rsqrt(ms + LN_EPS) * g * (grh * jax.nn.sigmoid(grh)))
    return jnp.concatenate(outs, axis=1)


def _lane_half_masks(shape):
    lane = lax.broadcasted_iota(jnp.int32, shape, 1)
    return lane < SWA_HD, lane >= SWA_HD


def _swa_split_q(q):
    lo_m, hi_m = _lane_half_masks((q.shape[0], LANES))
    out = []
    for j in range(SWA_QW // LANES):
        slab = q[:, LANES * j:LANES * (j + 1)]
        out.append((_bf(jnp.where(lo_m, slab, 0.0)), _bf(jnp.where(hi_m, slab, 0.0))))
    return out


def _swa_dup_k(k):
    lo_m, _ = _lane_half_masks(k.shape)
    kr = pltpu.roll(k, SWA_HD, axis=1)
    return [_bf(jnp.where(lo_m, k, kr)), _bf(jnp.where(lo_m, kr, k))]


def _swa_place_v(v):
    lo_m, hi_m = _lane_half_masks(v.shape)
    vr = pltpu.roll(v, SWA_HD, axis=1)
    return [(_bf(jnp.where(lo_m, v, 0.0)), _bf(jnp.where(hi_m, vr, 0.0))),
            (_bf(jnp.where(lo_m, vr, 0.0)), _bf(jnp.where(hi_m, v, 0.0)))]


def _swa_blocks(qrows, kks, vlos, vhis, sink_col, lowers, upper, tq):
    n = len(kks)
    stack = lambda xs: jnp.concatenate(xs, axis=0) if n > 1 else xs[0]
    s = stack([_dot_nt(qrows[i], kks[i]) for i in range(n)])
    if lowers is not None:
        kcol = lax.broadcasted_iota(jnp.int32, s.shape, 1)
        s = jnp.where(kcol >= stack(lowers), jnp.where(kcol <= stack([upper] * n), s, -jnp.inf), -jnp.inf)
    sink = stack([sink_col] * n)
    m = jnp.maximum(jnp.max(s, axis=-1, keepdims=True), sink)
    p = jnp.exp(s - m)
    den = jnp.sum(p, axis=-1, keepdims=True) + jnp.exp(sink - m)
    p = _bf(p / den)
    outs = []
    for i in range(n):
        r = 4 * tq * i
        outs.append((_dot(p[r:r + tq], vlos[i]) + _dot(p[r + tq:r + 2 * tq], vhis[i]),
                     _dot(p[r + 2 * tq:r + 3 * tq], vlos[i]) + _dot(p[r + 3 * tq:r + 4 * tq], vhis[i])))
    return outs


def _sink_col(sinks_ref, g, tq):
    r = lax.broadcasted_iota(jnp.int32, (4 * tq, 1), 0)
    base = 4 * g
    return jnp.where(r < tq, sinks_ref[base],
                     jnp.where(r < 2 * tq, sinks_ref[base + 1],
                               jnp.where(r < 3 * tq, sinks_ref[base + 2], sinks_ref[base + 3])))


def _mem_head(ref, h):
    return _bf(ref[pl.ds(h, MEM_TOKENS, stride=MEM_HEADS), :])


def _mem_attention(q, mk_refs, mv_refs):
    qb = _bf(q)
    nseg = len(mk_refs)
    tq = q.shape[0] // nseg
    outs = []
    for h in range(MEM_HEADS):
        hs = slice(MEM_HD * h, MEM_HD * (h + 1))
        s = [_dot_nt(qb[tq * i:tq * (i + 1), hs], _mem_head(mk_refs[i], h)) for i in range(nseg)]
        s = jnp.concatenate(s, axis=0) if nseg > 1 else s[0]
        m = jnp.max(s, axis=-1, keepdims=True)
        p = jnp.exp(s - m)
        p = _bf(p / jnp.sum(p, axis=-1, keepdims=True))
        o = [_dot(p[tq * i:tq * (i + 1)], _mem_head(mv_refs[i], h)) for i in range(nseg)]
        outs.append(jnp.concatenate(o, axis=0) if nseg > 1 else o[0])
    return jnp.concatenate(outs, axis=1)


SWA_BLOCK_LANES = 4 * SWA_PAIR
LOG2E = float(np.log2(np.e))


def _swa_bias_t(npair, first_tile):
    nk = WINDOW + SWA_PAIR
    width = SWA_BLOCK_LANES * SWA_KV_HEADS * npair
    col = lax.broadcasted_iota(jnp.int32, (1, width), 1)
    lower = CHUNK * ((col % SWA_PAIR) // CHUNK)
    upper = lower + (WINDOW + CHUNK - 1)
    if first_tile:
        lower = jnp.maximum(lower, WINDOW - SWA_PAIR * ((col // SWA_BLOCK_LANES) % npair))
    krow = lax.broadcasted_iota(jnp.int32, (nk, width), 0)
    return jnp.where(krow >= lower, jnp.where(krow <= upper, 0.0, -jnp.inf), -jnp.inf)


def _swa_attention_t(qs, kk_scr, v_scr, sinks_ref, bias, rows):
    nk = WINDOW + SWA_PAIR
    npair = rows // SWA_PAIR
    bw = SWA_BLOCK_LANES
    blocks = [(g, p) for g in range(SWA_KV_HEADS) for p in range(npair)]
    sts = []
    for g, p in blocks:
        rs = slice(SWA_PAIR * p, SWA_PAIR * (p + 1))
        qrows = jnp.concatenate([qs[2 * g][0][rs], qs[2 * g][1][rs],
                                 qs[2 * g + 1][0][rs], qs[2 * g + 1][1][rs]], axis=0)
        sts.append(_dot_nt(kk_scr[g, SWA_PAIR * p:SWA_PAIR * p + nk, :], qrows))
    st = jnp.concatenate(sts, axis=1) + bias
    width = bw * len(blocks)
    col = lax.broadcasted_iota(jnp.int32, (1, width), 1)
    head = 4 * (col // (bw * npair)) + (col % bw) // SWA_PAIR
    sink = jnp.zeros((1, width), _F32)
    for h in range(SWA_HEADS):
        sink = jnp.where(head == h, sinks_ref[h] * LOG2E, sink)
    m = jnp.maximum(jnp.max(st, axis=0, keepdims=True), sink)
    pt = jnp.exp2(st - m)
    inv = 1.0 / (jnp.sum(pt, axis=0, keepdims=True) + jnp.exp2(sink - m))
    pb = _bf(pt)
    pieces = [[None] * npair for _ in range(SWA_HEADS)]
    for i, (g, p) in enumerate(blocks):
        cs = slice(bw * i, bw * (i + 1))
        o2 = _dot_tn(v_scr[SWA_PAIR * p:SWA_PAIR * p + nk, :], pb[:, cs]) * inv[:, cs]
        for hh in range(4):
            pieces[4 * g + hh][p] = o2[SWA_HD * g:SWA_HD * (g + 1), SWA_PAIR * hh:SWA_PAIR * (hh + 1)]
    return jnp.concatenate([jnp.concatenate(ps, axis=1) if npair > 1 else ps[0] for ps in pieces], axis=0)


def _mem_attention_t(q, mk_ref, mv_ref):
    qb = _bf(q)
    R = q.shape[0]
    st = jnp.concatenate([_dot_nt(_mem_head(mk_ref, h), qb[:, MEM_HD * h:MEM_HD * (h + 1)])
                          for h in range(MEM_HEADS)], axis=1)
    pt = jnp.exp2(st - jnp.max(st, axis=0, keepdims=True))
    inv = 1.0 / jnp.sum(pt, axis=0, keepdims=True)
    pb = _bf(pt)
    outs = [_dot_tn(_mem_head(mv_ref, h), pb[:, R * h:R * (h + 1)]) * inv[:, R * h:R * (h + 1)]
            for h in range(MEM_HEADS)]
    return jnp.concatenate(outs, axis=0)


def _gate(xb, i, w_inb_ref, bgate_ref):
    gl = _dot(xb, w_inb_ref[:, OFF_GL + D_MODEL * i:OFF_GL + D_MODEL * (i + 1)])
    return jax.nn.sigmoid(gl + bgate_ref[:, D_MODEL * i:D_MODEL * (i + 1)])


def _merge_ln1(x, gated_sum, wo_ref, g_ref, b_ref, alpha):
    mix = _dot(_bf(gated_sum), wo_ref[...])
    return _layer_norm(alpha * x + mix, g_ref[...], b_ref[...])


def _forget_log(xb, w_ga_ref, w2_ref, ba_ref):
    ga = _dot(xb, w_ga_ref[...])
    return _log_sigmoid(_dot(_bf(ga), w2_ref[...]) + ba_ref[...]) * (1.0 / GLA_TAU)


def _prompt_mixer_kernel(sinks_ref, x_ref, rc_ref, rdn_ref, rup_ref, mk_ref, mv_ref, w_ina_ref, w_ga_ref,
                         w_inb_ref, w2_ref, ba_ref, gng_ref, wbg_ref, wbs_ref, wbm_ref, bgate_ref, wo_ref,
                         ln_g_ref, ln_b_ref,
                         out_ref, gla_out_ref, swak_out_ref, swav_out_ref,
                         s_scr, kk_scr, vv_scr, bias_scr, *, rows, alpha):
    t = pl.program_id(1)
    nchunk = rows // CHUNK

    @pl.when(t == 0)
    def _():
        s_scr[...] = jnp.zeros_like(s_scr)
        kk_scr[:, 0:WINDOW, :] = jnp.zeros((SWA_KV_HEADS, WINDOW, LANES), _BF)
        vv_scr[0:WINDOW, :] = jnp.zeros((WINDOW, LANES), _BF)
        bias_scr[0] = _swa_bias_t(rows // SWA_PAIR, first_tile=False)
        bias_scr[1] = _swa_bias_t(rows // SWA_PAIR, first_tile=True)

    x = x_ref[0]
    xb = _bf(x)

    log_a = _forget_log(xb, w_ga_ref, w2_ref, ba_ref)

    def proj_a(off):
        return _dot(xb, w_ina_ref[:, off:off + GLA_W])

    def state_in(c, h):
        return s_scr[h]

    def state_out(c, h, st):
        s_scr[h] = st
        gla_out_ref[0, h] = jnp.transpose(st)

    gq, gk, gv = proj_a(OFF_GQ), proj_a(OFF_GK), proj_a(OFF_GV)
    gate_a = _gate(xb, 0, w_inb_ref, bgate_ref)
    o_a = _gla_tile(gq, gk, gv, log_a, CHUNK, nchunk, state_in, state_out, chained=True)
    o_a = _gla_gate(o_a, proj_a(OFF_GR), gng_ref[...])
    gated = gate_a * _dot(_bf(o_a), wbg_ref[...])

    rc, rdn, rup = rc_ref[...], rdn_ref[...], rup_ref[...]
    q = _rope(_dot(xb, w_inb_ref[:, OFF_SQ:OFF_SQ + SWA_QW]), rc, rdn, rup) * (SWA_HD ** -0.5 * LOG2E)
    k = _rope(_dot(xb, w_inb_ref[:, OFF_SK:OFF_SK + SWA_KW]), rc, rdn, rup)
    v = _dot(xb, w_inb_ref[:, OFF_SV:OFF_SV + SWA_KW])
    swak_out_ref[0] = k[rows - WINDOW:rows, :]
    swav_out_ref[0] = v[rows - WINDOW:rows, :]
    gate_b = _gate(xb, 1, w_inb_ref, bgate_ref)
    kk = _swa_dup_k(k)
    vb = _bf(v)
    for g in range(SWA_KV_HEADS):
        kk_scr[g, WINDOW:WINDOW + rows, :] = kk[g]
    vv_scr[WINDOW:WINDOW + rows, :] = vb
    o_bt = _swa_attention_t(_swa_split_q(q), kk_scr, vv_scr, sinks_ref, bias_scr[jnp.where(t == 0, 1, 0)], rows)
    gated = gated + gate_b * _dot_tn(_bf(o_bt), wbs_ref[...])
    for g in range(SWA_KV_HEADS):
        kk_scr[g, 0:WINDOW, :] = kk[g][rows - WINDOW:rows]
    vv_scr[0:WINDOW, :] = vb[rows - WINDOW:rows]

    qm = _dot(xb, w_inb_ref[:, OFF_MQ:OFF_MQ + MEM_W]) * (MEM_HD ** -0.5 * LOG2E)
    gate_c = _gate(xb, 2, w_inb_ref, bgate_ref)
    o_ct = _mem_attention_t(qm, mk_ref.at[0], mv_ref.at[0])
    gated = gated + gate_c * _dot_tn(_bf(o_ct), wbm_ref[...])

    out_ref[0] = _merge_ln1(x, gated, wo_ref, ln_g_ref, ln_b_ref, alpha)


def _sample_mixer_kernel(sinks_ref, x_ref, rc_ref, rdn_ref, rup_ref, mk_ref, mv_ref, gla_in_ref,
                         swak_in_ref, swav_in_ref, w_ina_ref, w_ga_ref, w_inb_ref, w2_ref,
                         ba_ref, gng_ref, wbg_ref, wbs_ref, wbm_ref, bgate_ref, wo_ref, ln_g_ref, ln_b_ref,
                         out_ref, gla_out_ref, swak_out_ref, swav_out_ref, *, nb, tq, alpha):
    rows = nb * tq
    x = x_ref[...].reshape(rows, D_MODEL)
    xb = _bf(x)

    log_a = _forget_log(xb, w_ga_ref, w2_ref, ba_ref)

    def proj_a(off):
        return _dot(xb, w_ina_ref[:, off:off + GLA_W])

    def state_in(c, h):
        return jnp.transpose(gla_in_ref[c, h])

    def state_out(c, h, st):
        gla_out_ref[c, h] = jnp.transpose(st)

    gq, gk, gv = proj_a(OFF_GQ), proj_a(OFF_GK), proj_a(OFF_GV)
    gate_a = _gate(xb, 0, w_inb_ref, bgate_ref)
    o_a = _gla_tile(gq, gk, gv, log_a, tq, nb, state_in, state_out, chained=False)
    o_a = _gla_gate(o_a, proj_a(OFF_GR), gng_ref[...])
    gated = gate_a * _dot(_bf(o_a), wbg_ref[...])

    rc = jnp.concatenate([rc_ref[...]] * nb, axis=0)
    rdn = jnp.concatenate([rdn_ref[...]] * nb, axis=0)
    rup = jnp.concatenate([rup_ref[...]] * nb, axis=0)
    q = _rope(_dot(xb, w_inb_ref[:, OFF_SQ:OFF_SQ + SWA_QW]), rc, rdn, rup) * (SWA_HD ** -0.5)
    k = _rope(_dot(xb, w_inb_ref[:, OFF_SK:OFF_SK + SWA_KW]), rc, rdn, rup)
    v = _dot(xb, w_inb_ref[:, OFF_SV:OFF_SV + SWA_KW])
    gate_b = _gate(xb, 1, w_inb_ref, bgate_ref)
    qs = _swa_split_q(q)
    slabs = [[None] * nb for _ in range(SWA_QW // LANES)]
    kks, vvs = [], []
    for bi in range(nb):
        rs = slice(tq * bi, tq * (bi + 1))
        k_all = jnp.concatenate([swak_in_ref[bi], k[rs]], axis=0)
        v_all = jnp.concatenate([swav_in_ref[bi], v[rs]], axis=0)
        swak_out_ref[bi] = k_all[tq:tq + WINDOW]
        swav_out_ref[bi] = v_all[tq:tq + WINDOW]
        kks.append(_swa_dup_k(k_all))
        vvs.append(_swa_place_v(v_all))
    for g in range(SWA_KV_HEADS):
        qrows = []
        for bi in range(nb):
            rs = slice(tq * bi, tq * (bi + 1))
            qrows.append(jnp.concatenate([qs[2 * g][0][rs], qs[2 * g][1][rs],
                                          qs[2 * g + 1][0][rs], qs[2 * g + 1][1][rs]], axis=0))
        outs = _swa_blocks(qrows, [kk[g] for kk in kks], [vv[g][0] for vv in vvs], [vv[g][1] for vv in vvs],
                           _sink_col(sinks_ref, g, tq), None, None, tq)
        for bi in range(nb):
            slabs[2 * g][bi], slabs[2 * g + 1][bi] = outs[bi]
    o_b = jnp.concatenate([jnp.concatenate(s, axis=0) if nb > 1 else s[0] for s in slabs], axis=1)
    gated = gated + gate_b * _dot(_bf(o_b), wbs_ref[...])

    qm = _dot(xb, w_inb_ref[:, OFF_MQ:OFF_MQ + MEM_W]) * (MEM_HD ** -0.5)
    gate_c = _gate(xb, 2, w_inb_ref, bgate_ref)
    o_c = _mem_attention(qm, [mk_ref.at[bi] for bi in range(nb)], [mv_ref.at[bi] for bi in range(nb)])
    gated = gated + gate_c * _dot(_bf(o_c), wbm_ref[...])

    out = _merge_ln1(x, gated, wo_ref, ln_g_ref, ln_b_ref, alpha)
    out_ref[...] = out.reshape(nb, tq, D_MODEL)


_GELU_K1 = -2.0 * float(np.log2(np.e)) * float(np.sqrt(2.0 / np.pi))
_GELU_K2 = _GELU_K1 * 0.044715


def _conv(u2, u1, u0, cw_ref, cb_ref, cs):
    return cb_ref[:, cs] + u2 * cw_ref[0:1, cs] + u1 * cw_ref[1:2, cs] + u0 * cw_ref[2:3, cs]


def _geglu(g, v):
    return g * v / (1.0 + jnp.exp2(g * (g * g * _GELU_K2 + _GELU_K1)))


def _prompt_ffn_kernel(x_ref, wup_ref, cw_ref, cb_ref, wdn_ref, ln_g_ref, ln_b_ref,
                       out_ref, conv_out_ref, u_scr, *, rows, alpha):
    nj = rows // SUBLANES
    base = 2 * SUBLANES
    last = slice(base + rows - SUBLANES, base + rows)
    last2 = slice(base + rows - 2 * SUBLANES, base + rows - SUBLANES)

    @pl.when(pl.program_id(1) == 0)
    def _():
        u_scr[base + rows - 2 * SUBLANES:base + rows, :] = jnp.zeros((2 * SUBLANES, 2 * D_FF), _F32)

    prev_last, prev_last2 = u_scr[last, :], u_scr[last2, :]
    x = jnp.swapaxes(x_ref[0].reshape(SUBLANES, nj, D_MODEL), 0, 1).reshape(rows, D_MODEL)
    u_scr[base:base + rows, :] = _dot(_bf(x), wup_ref[...])
    sub = lax.broadcasted_iota(jnp.int32, (SUBLANES, FFN_COLS), 0)

    def conv_cols(cs):
        wrap = lambda prev, cur: pltpu.roll(jnp.where(sub == SUBLANES - 1, prev, cur), 1, axis=0)
        u_scr[SUBLANES:base, cs] = wrap(prev_last[:, cs], u_scr[last, cs])
        u_scr[0:SUBLANES, cs] = wrap(prev_last2[:, cs], u_scr[last2, cs])
        return _conv(u_scr[0:rows, cs], u_scr[SUBLANES:SUBLANES + rows, cs], u_scr[base:base + rows, cs],
                     cw_ref, cb_ref, cs)

    h = [_bf(_geglu(conv_cols(slice(c0, c0 + FFN_COLS)), conv_cols(slice(D_FF + c0, D_FF + c0 + FFN_COLS))))
         for c0 in range(0, D_FF, FFN_COLS)]
    conv_out_ref[0] = jnp.concatenate([u_scr[base + rows - SUBLANES - 1:base + rows - SUBLANES, :],
                                       u_scr[base + rows - 1:base + rows, :]], axis=0)
    f = _dot(jnp.concatenate(h, axis=1), wdn_ref[...])
    y = _layer_norm(alpha * x + f, ln_g_ref[...], ln_b_ref[...])
    out_ref[0] = jnp.swapaxes(y.reshape(nj, SUBLANES, D_MODEL), 0, 1).reshape(rows, D_MODEL)


def _sample_ffn_kernel(x_ref, hist_ref, wup_ref, cw_ref, cb_ref, wdn_ref, ln_g_ref, ln_b_ref,
                       out_ref, conv_out_ref, u_scr, *, nb, tq, alpha):
    rows = nb * tq
    base = (CONV_W - 1) * nb
    x = jnp.swapaxes(x_ref[...], 0, 1).reshape(rows, D_MODEL)
    u_scr[0:base, :] = jnp.swapaxes(hist_ref[...], 0, 1).reshape(base, 2 * D_FF)
    u_scr[base:base + rows, :] = _dot(_bf(x), wup_ref[...])
    conv = lambda cs: _conv(u_scr[0:rows, cs], u_scr[nb:nb + rows, cs], u_scr[base:base + rows, cs],
                            cw_ref, cb_ref, cs)
    h = _geglu(conv(slice(0, D_FF)), conv(slice(D_FF, 2 * D_FF)))
    conv_out_ref[...] = jnp.swapaxes(u_scr[rows:rows + base, :].reshape(CONV_W - 1, nb, 2 * D_FF), 0, 1)
    f = _dot(_bf(h), wdn_ref[...])
    y = _layer_norm(alpha * x + f, ln_g_ref[...], ln_b_ref[...])
    out_ref[...] = jnp.swapaxes(y.reshape(tq, nb, D_MODEL), 0, 1)


def _mem_kv_kernel(m_ref, w_ref, k_ref, v_ref):
    kv = _dot(_bf(m_ref[0]), w_ref[...])
    for h in range(MEM_HEADS):
        k_ref[0, pl.ds(h, MEM_TOKENS, stride=MEM_HEADS), :] = kv[:, MEM_HD * h:MEM_HD * (h + 1)]
        v_ref[0, pl.ds(h, MEM_TOKENS, stride=MEM_HEADS), :] = kv[:, MEM_W + MEM_HD * h:MEM_W + MEM_HD * (h + 1)]


def _const_spec(shape):
    nd = len(shape)
    return pl.BlockSpec(shape, lambda *_: (0,) * nd, pipeline_mode=pl.Buffered(1))


def _rope_tables(pos):
    half = ROPE_DIM // 2
    inv = np.float32(ROPE_THETA) ** (-np.arange(half, dtype=np.float32) / np.float32(half))
    ang = pos.astype(np.float32)[:, None] * inv[None, :]
    cos, sin = np.cos(ang), np.sin(ang)
    T = pos.shape[0]
    ones = np.ones((T, SWA_HD - ROPE_DIM), np.float32)
    zeros = np.zeros((T, SWA_HD - ROPE_DIM), np.float32)
    zh = np.zeros((T, half), np.float32)
    c = np.concatenate([cos, cos, ones], axis=1)
    dn = np.concatenate([-sin, zh, zeros], axis=1)
    up = np.concatenate([zh, sin, zeros], axis=1)
    rep = LANES // SWA_HD
    return tuple(jnp.asarray(np.concatenate([a] * rep, axis=1), dtype=_F32) for a in (c, dn, up))


def _pack_cols_kernel(wt_ref, o_ref, *, keep):
    t = jnp.transpose(wt_ref[...])
    if keep < t.shape[1]:
        t = jnp.where(lax.broadcasted_iota(jnp.int32, t.shape, 1) < keep, t, 0.0)
    o_ref[...] = _bf(t)


def _pack_cols(wt, col0, ncols, keep, name):
    blk = min(ncols, 2 * LANES)
    k = wt.shape[1]
    return pl.pallas_call(
        functools.partial(_pack_cols_kernel, keep=keep),
        grid=(ncols // blk,),
        in_specs=[pl.BlockSpec((pl.Element(blk), pl.Element(k)),
                               lambda i: (pl.multiple_of(col0 + blk * i, SUBLANES), 0))],
        out_specs=pl.BlockSpec((k, blk), lambda i: (0, i)),
        out_shape=jax.ShapeDtypeStruct((k, ncols), _BF),
        compiler_params=_params(("arbitrary",)),
        name=name,
    )(wt)


def _split_w_in(w_in):
    wt = jnp.transpose(w_in)
    return (_pack_cols(wt, 0, IN_A_COLS, IN_A_COLS, "pack_w_in_a"),
            _pack_cols(wt, IN_A_COLS, LANES, GLA_RANK, "pack_w_in_ga"),
            _pack_cols(wt, IN_A_COLS + GLA_RANK, IN_B_COLS, IN_B_COLS, "pack_w_in_b"))


def _mixer_weight_specs():
    return [
        _const_spec((D_MODEL, IN_A_COLS)),
        _const_spec((D_MODEL, LANES)),
        _const_spec((D_MODEL, IN_B_COLS)),
        _const_spec((LANES, GLA_W)),
        _const_spec((1, GLA_W)),
        _const_spec((1, GLA_DV)),
        _const_spec((GLA_W, D_MODEL)),
        _const_spec((SWA_QW, D_MODEL)),
        _const_spec((MEM_W, D_MODEL)),
        _const_spec((1, GATE_W)),
        _const_spec((D_MODEL, D_MODEL)),
        _const_spec((1, D_MODEL)),
        _const_spec((1, D_MODEL)),
    ]


def _ffn_weight_specs():
    return [
        _const_spec((D_MODEL, 2 * D_FF)),
        _const_spec((CONV_W, 2 * D_FF)),
        _const_spec((1, 2 * D_FF)),
        _const_spec((D_FF, D_MODEL)),
        _const_spec((1, D_MODEL)),
        _const_spec((1, D_MODEL)),
    ]


_SMEM_SPEC = pl.BlockSpec(memory_space=pltpu.SMEM)
_MEM_ROWS = MEM_TOKENS * MEM_HEADS


def _params(sem, flags=None):
    return pltpu.CompilerParams(dimension_semantics=sem, vmem_limit_bytes=V7X_VMEM_LIMIT, flags=flags)


def _prompt_layer(x, mem, sinks, mixer_w, ffn_w, w_mem_kv, alpha):
    B, T, _ = x.shape
    rows = min(PROMPT_ROWS, T)
    nt = T // rows
    assert T % rows == 0 and rows % SWA_PAIR == 0 and rows >= WINDOW
    f32 = jnp.float32
    mk, mv = pl.pallas_call(
        _mem_kv_kernel,
        grid=(B,),
        in_specs=[pl.BlockSpec((1, MEM_TOKENS, D_MODEL), lambda b: (b, 0, 0)),
                  _const_spec((D_MODEL, 2 * MEM_W))],
        out_specs=[pl.BlockSpec((1, _MEM_ROWS, MEM_HD), lambda b: (b, 0, 0))] * 2,
        out_shape=[jax.ShapeDtypeStruct((B, _MEM_ROWS, MEM_HD), f32)] * 2,
        compiler_params=_params(("arbitrary",)),
        name="mem_kv",
    )(mem, w_mem_kv)

    rope = _rope_tables(np.arange(T))
    rope_spec = pl.BlockSpec((rows, LANES), lambda b, t: (t, 0))
    per_batch = lambda *blk: pl.BlockSpec((1,) + blk, lambda b, t: (b,) + (0,) * len(blk))
    x1, gla, swak, swav = pl.pallas_call(
        functools.partial(_prompt_mixer_kernel, rows=rows, alpha=alpha),
        grid=(B, nt),
        in_specs=[_SMEM_SPEC,
                  pl.BlockSpec((1, rows, D_MODEL), lambda b, t: (b, t, 0)),
                  rope_spec, rope_spec, rope_spec,
                  per_batch(_MEM_ROWS, MEM_HD), per_batch(_MEM_ROWS, MEM_HD)] + _mixer_weight_specs(),
        out_specs=[pl.BlockSpec((1, rows, D_MODEL), lambda b, t: (b, t, 0)),
                   per_batch(GLA_HEADS, GLA_DK, GLA_DV),
                   per_batch(WINDOW, SWA_KW), per_batch(WINDOW, SWA_KW)],
        out_shape=[jax.ShapeDtypeStruct((B, T, D_MODEL), f32),
                   jax.ShapeDtypeStruct((B, GLA_HEADS, GLA_DK, GLA_DV), f32),
                   jax.ShapeDtypeStruct((B, WINDOW, SWA_KW), f32),
                   jax.ShapeDtypeStruct((B, WINDOW, SWA_KW), f32)],
        scratch_shapes=[pltpu.VMEM((GLA_HEADS, GLA_DK, GLA_DV), f32),
                        pltpu.VMEM((SWA_KV_HEADS, WINDOW + rows, LANES), _BF),
                        pltpu.VMEM((WINDOW + rows, LANES), _BF),
                        pltpu.VMEM((2, WINDOW + SWA_PAIR, SWA_BLOCK_LANES * SWA_KV_HEADS * (rows // SWA_PAIR)),
                                   f32)],
        compiler_params=_params(("arbitrary", "arbitrary")),
        name="prompt_mixer",
    )(sinks, x, *rope, mk, mv, *mixer_w)

    frows = min(PROMPT_FFN_ROWS, T)
    assert T % frows == 0 and frows >= CONV_W - 1
    y, conv = pl.pallas_call(
        functools.partial(_prompt_ffn_kernel, rows=frows, alpha=alpha),
        grid=(B, T // frows),
        in_specs=[pl.BlockSpec((1, frows, D_MODEL), lambda b, t: (b, t, 0))] + _ffn_weight_specs(),
        out_specs=[pl.BlockSpec((1, frows, D_MODEL), lambda b, t: (b, t, 0)),
                   per_batch(CONV_W - 1, 2 * D_FF)],
        out_shape=[jax.ShapeDtypeStruct((B, T, D_MODEL), f32),
                   jax.ShapeDtypeStruct((B, CONV_W - 1, 2 * D_FF), f32)],
        scratch_shapes=[pltpu.VMEM((2 * SUBLANES + frows, 2 * D_FF), f32)],
        compiler_params=_params(("arbitrary", "arbitrary")),
        name="prompt_ffn",
    )(x1, *ffn_w)
    return y, gla, swak, swav, mk, mv, conv


def _sample_layer(x, gla0, swak0, swav0, memk, memv, conv0, sinks, mixer_w, ffn_w, alpha):
    B, tq, _ = x.shape
    nb = max(1, min(B, SAMPLE_ROWS // tq))
    assert B % nb == 0 and tq % 16 == 0 and tq >= CONV_W - 1
    f32 = jnp.float32
    rope = _rope_tables(PAST_LEN + np.arange(tq))
    rope_spec = pl.BlockSpec((tq, LANES), lambda i: (0, 0))
    blk = lambda *s: pl.BlockSpec((nb,) + s, lambda i: (i,) + (0,) * len(s))
    blk1 = lambda *s: pl.BlockSpec((nb,) + s, lambda i: (i,) + (0,) * len(s), pipeline_mode=pl.Buffered(1))
    x1, gla, swak, swav = pl.pallas_call(
        functools.partial(_sample_mixer_kernel, nb=nb, tq=tq, alpha=alpha),
        grid=(B // nb,),
        in_specs=[_SMEM_SPEC, blk(tq, D_MODEL), rope_spec, rope_spec, rope_spec,
                  blk(_MEM_ROWS, MEM_HD), blk(_MEM_ROWS, MEM_HD),
                  blk1(GLA_HEADS, GLA_DK, GLA_DV), blk1(WINDOW, SWA_KW), blk1(WINDOW, SWA_KW)]
                 + _mixer_weight_specs(),
        out_specs=[blk(tq, D_MODEL), blk(GLA_HEADS, GLA_DK, GLA_DV), blk(WINDOW, SWA_KW), blk(WINDOW, SWA_KW)],
        out_shape=[jax.ShapeDtypeStruct((B, tq, D_MODEL), f32),
                   jax.ShapeDtypeStruct((B, GLA_HEADS, GLA_DK, GLA_DV), f32),
                   jax.ShapeDtypeStruct((B, WINDOW, SWA_KW), f32),
                   jax.ShapeDtypeStruct((B, WINDOW, SWA_KW), f32)],
        compiler_params=_params(("arbitrary",)),
        name="sample_mixer",
    )(sinks, x, *rope, memk, memv, gla0, swak0, swav0, *mixer_w)

    assert B % SUBLANES == 0
    fblk = lambda *s: pl.BlockSpec((SUBLANES,) + s, lambda i: (i,) + (0,) * len(s))
    y, conv = pl.pallas_call(
        functools.partial(_sample_ffn_kernel, nb=SUBLANES, tq=tq, alpha=alpha),
        grid=(B // SUBLANES,),
        in_specs=[fblk(tq, D_MODEL), fblk(CONV_W - 1, 2 * D_FF)] + _ffn_weight_specs(),
        out_specs=[fblk(tq, D_MODEL), fblk(CONV_W - 1, 2 * D_FF)],
        out_shape=[jax.ShapeDtypeStruct((B, tq, D_MODEL), f32),
                   jax.ShapeDtypeStruct((B, CONV_W - 1, 2 * D_FF), f32)],
        scratch_shapes=[pltpu.VMEM(((CONV_W - 1 + tq) * SUBLANES, 2 * D_FF), f32)],
        compiler_params=_params(("arbitrary",)),
        name="sample_ffn",
    )(x1, conv0, *ffn_w)
    return y, gla, swak, swav, conv


def kernel(x_prompt, x_sample, cache_swa_k, cache_swa_v, state_gla, cache_mem_k, cache_mem_v, cache_ffn_conv, mem_prompt, ln1_g, ln1_b, ln2_g, ln2_b, w_in, b_gate, w_gla_a2, b_gla_a, gla_norm_g, swa_sinks, w_mem_kv, w_br_gla, w_br_swa, w_br_mem, w_o, w_up, conv_w, conv_b, w_down):
    depth = w_in.shape[0]
    alpha = float((2 * depth) ** 0.25)
    Bp = x_prompt.shape[0]
    Bs = x_sample.shape[0]
    hp, hs = x_prompt, x_sample
    outs = [[] for _ in range(10)]
    row = lambda a: a.reshape(1, -1)
    for l in range(depth):
        w2 = jnp.pad(w_gla_a2[l], ((0, LANES - GLA_RANK), (0, 0)))
        mixer_w = _split_w_in(w_in[l]) + (
            _bf(w2), row(b_gla_a[l]), row(gla_norm_g[l]),
            _bf(w_br_gla[l]), _bf(w_br_swa[l]), _bf(w_br_mem[l]), row(b_gate[l]), _bf(w_o[l]),
            row(ln1_g[l]), row(ln1_b[l]))
        ffn_w = (_bf(w_up[l]), conv_w[l], row(conv_b[l]), _bf(w_down[l]), row(ln2_g[l]), row(ln2_b[l]))
        sinks = swa_sinks[l]
        hp, g_p, k_p, v_p, mk_p, mv_p, c_p = _prompt_layer(
            hp, mem_prompt, sinks, mixer_w, ffn_w, _bf(w_mem_kv[l]), alpha)
        hs, g_s, k_s, v_s, c_s = _sample_layer(
            hs, state_gla[l],
            cache_swa_k[l].reshape(Bs, WINDOW, SWA_KW), cache_swa_v[l].reshape(Bs, WINDOW, SWA_KW),
            cache_mem_k[l].reshape(Bs, _MEM_ROWS, MEM_HD), cache_mem_v[l].reshape(Bs, _MEM_ROWS, MEM_HD),
            cache_ffn_conv[l], sinks, mixer_w, ffn_w, alpha)
        kv5 = lambda a, b: a.reshape(b, WINDOW, SWA_KV_HEADS, SWA_HD)
        m5 = lambda a: a.reshape(Bp, MEM_TOKENS, MEM_HEADS, MEM_HD)
        for lst, val in zip(outs, (kv5(k_p, Bp), kv5(v_p, Bp), g_p, m5(mk_p), m5(mv_p), c_p,
                                   kv5(k_s, Bs), kv5(v_s, Bs), g_s, c_s)):
            lst.append(val)
    return (hp, hs) + tuple(jnp.stack(o) for o in outs)
```

```python
import functools

import jax
import jax.numpy as jnp
import numpy as np
from jax import lax
from jax.experimental import pallas as pl
from jax.experimental.pallas import tpu as pltpu

D_MODEL = 1024
CHUNK = 64
GLA_HEADS = 4
GLA_DK = 128
GLA_DV = 128
GLA_RANK = 16
GLA_TAU = 16.0
SWA_HEADS = 8
SWA_KV_HEADS = 2
SWA_HD = 64
WINDOW = 128
ROPE_DIM = 16
ROPE_THETA = 500000.0
MEM_TOKENS = 256
MEM_HEADS = 4
MEM_HD = 128
D_FF = 2816
CONV_W = 3
N_BRANCH = 3
PAST_LEN = 2048
LN_EPS = 1e-5

LANES = 128
SUBLANES = 8
V7X_VMEM_LIMIT = 56 * 1024 * 1024
PROMPT_ROWS = 256
PROMPT_FFN_ROWS = 512
FFN_COLS = 256
SAMPLE_ROWS = 256
GLA_W = GLA_HEADS * GLA_DK
SWA_QW = SWA_HEADS * SWA_HD
SWA_KW = SWA_KV_HEADS * SWA_HD
MEM_W = MEM_HEADS * MEM_HD
GATE_W = N_BRANCH * D_MODEL
SWA_PAIR = 2 * CHUNK

OFF_GQ = 0
OFF_GK = OFF_GQ + GLA_W
OFF_GV = OFF_GK + GLA_W
OFF_GR = OFF_GV + GLA_W
IN_A_COLS = OFF_GR + GLA_W
OFF_SQ = 0
OFF_SK = OFF_SQ + SWA_QW
OFF_SV = OFF_SK + SWA_KW
OFF_MQ = OFF_SV + SWA_KW
OFF_GL = OFF_MQ + MEM_W
IN_B_COLS = OFF_GL + GATE_W

_BF = jnp.bfloat16
_F32 = jnp.float32


def _bf(x):
    return x.astype(_BF)


def _dot(a, b):
    return jnp.dot(a, b, preferred_element_type=_F32)


def _dot_nt(a, b):
    return lax.dot_general(a, b, (((1,), (1,)), ((), ())), preferred_element_type=_F32)


def _dot_tn(a, b):
    return lax.dot_general(a, b, (((0,), (0,)), ((), ())), preferred_element_type=_F32)


def _layer_norm(h, g, b):
    mu = jnp.mean(h, axis=-1, keepdims=True)
    d = h - mu
    var = jnp.mean(d * d, axis=-1, keepdims=True)
    return d * lax.rsqrt(var + LN_EPS) * g + b


def _log_sigmoid(x):
    return -(jnp.maximum(-x, 0.0) + jnp.log(1.0 + jnp.exp(-jnp.abs(x))))


def _rope(a, c, s_dn, s_up):
    outs = []
    for j in range(a.shape[1] // LANES):
        slab = a[:, LANES * j:LANES * (j + 1)]
        fwd = pltpu.roll(slab, LANES - ROPE_DIM // 2, axis=1)
        bwd = pltpu.roll(slab, ROPE_DIM // 2, axis=1)
        outs.append(slab * c + fwd * s_dn + bwd * s_up)
    return outs[0] if len(outs) == 1 else jnp.concatenate(outs, axis=1)


def _gla_tile(q, k, v, log_a, chunk, nchunk, state_in, state_out, chained, between=lambda: None):
    R = chunk * nchunk
    row = lax.broadcasted_iota(jnp.int32, (R, R), 0)
    col = lax.broadcasted_iota(jnp.int32, (R, R), 1)
    tril = (row // chunk == col // chunk) & (col <= row)
    ltri = jnp.where(tril, 1.0, 0.0).astype(_BF)
    hi = _bf(log_a)
    lo = _bf(log_a - hi.astype(_F32))
    b = _dot(ltri, hi) + _dot(ltri, lo)
    between()
    b_last = [b[chunk * (c + 1) - 1:chunk * (c + 1), :] for c in range(nchunk)]
    bl_rows = jnp.concatenate([jnp.broadcast_to(bl, (chunk, GLA_W)) for bl in b_last], axis=0) \
        if nchunk > 1 else jnp.broadcast_to(b_last[0], (chunk, GLA_W))
    q_dec = _bf(q * (GLA_DK ** -0.5) * jnp.exp(b))
    k_inv = _bf(k * jnp.exp(-b))
    k_tail = _bf(k * jnp.exp(bl_rows - b))
    vb = _bf(v)
    heads = [slice(GLA_DK * h, GLA_DK * (h + 1)) for h in range(GLA_HEADS)]
    o_intra = []
    for hs in heads:
        att = jnp.where(tril, _dot_nt(q_dec[:, hs], k_inv[:, hs]), 0.0)
        o_intra.append(_dot(_bf(att), vb[:, hs]))
    o_inter = [[None] * nchunk for _ in heads]
    st = [None] * GLA_HEADS
    for c in range(nchunk):
        rs = slice(chunk * c, chunk * (c + 1))
        for h, hs in enumerate(heads):
            if c == 0 or not chained:
                st[h] = state_in(c, h)
            o_inter[h][c] = _dot_nt(q_dec[rs, hs], _bf(st[h]))
            st[h] = st[h] * jnp.exp(b_last[c][:, hs]) + _dot_tn(vb[rs, hs], k_tail[rs, hs])
            if c == nchunk - 1 or not chained:
                state_out(c, h, st[h])
    outs = [o_intra[h] + (jnp.concatenate(o_inter[h], axis=0) if nchunk > 1 else o_inter[h][0])
            for h in range(GLA_HEADS)]
    return jnp.concatenate(outs, axis=1)


def _gla_gate(o, gr, g):
    outs = []
    for h in range(GLA_HEADS):
        hs = slice(GLA_DV * h, GLA_DV * (h + 1))
        oh = o[:, hs]
        ms = jnp.mean(oh * oh, axis=-1, keepdims=True)
        grh = gr[:, hs]
        outs.append(oh * lax.rsqrt(ms + LN_EPS) * g * (grh * jax.nn.sigmoid(grh)))
    return jnp.concatenate(outs, axis=1)


def _lane_half_masks(shape):
    lane = lax.broadcasted_iota(jnp.int32, shape, 1)
    return lane < SWA_HD, lane >= SWA_HD


def _swa_split_q(q):
    lo_m, hi_m = _lane_half_masks((q.shape[0], LANES))
    out = []
    for j in range(SWA_QW // LANES):
        slab = q[:, LANES * j:LANES * (j + 1)]
        out.append((_bf(jnp.where(lo_m, slab, 0.0)), _bf(jnp.where(hi_m, slab, 0.0))))
    return out


def _swa_dup_k(k):
    lo_m, _ = _lane_half_masks(k.shape)
    kr = pltpu.roll(k, SWA_HD, axis=1)
    return [_bf(jnp.where(lo_m, k, kr)), _bf(jnp.where(lo_m, kr, k))]


def _swa_place_v(v):
    lo_m, hi_m = _lane_half_masks(v.shape)
    vr = pltpu.roll(v, SWA_HD, axis=1)
    return [(_bf(jnp.where(lo_m, v, 0.0)), _bf(jnp.where(hi_m, vr, 0.0))),
            (_bf(jnp.where(lo_m, vr, 0.0)), _bf(jnp.where(hi_m, v, 0.0)))]


def _swa_blocks(qrows, kks, vlos, vhis, sink_col, lowers, upper, tq):
    n = len(kks)
    stack = lambda xs: jnp.concatenate(xs, axis=0) if n > 1 else xs[0]
    s = stack([_dot_nt(qrows[i], kks[i]) for i in range(n)])
    if lowers is not None:
        kcol = lax.broadcasted_iota(jnp.int32, s.shape, 1)
        s = jnp.where(kcol >= stack(lowers), jnp.where(kcol <= stack([upper] * n), s, -jnp.inf), -jnp.inf)
    sink = stack([sink_col] * n)
    m = jnp.maximum(jnp.max(s, axis=-1, keepdims=True), sink)
    p = jnp.exp(s - m)
    den = jnp.sum(p, axis=-1, keepdims=True) + jnp.exp(sink - m)
    p = _bf(p / den)
    outs = []
    for i in range(n):
        r = 4 * tq * i
        outs.append((_dot(p[r:r + tq], vlos[i]) + _dot(p[r + tq:r + 2 * tq], vhis[i]),
                     _dot(p[r + 2 * tq:r + 3 * tq], vlos[i]) + _dot(p[r + 3 * tq:r + 4 * tq], vhis[i])))
    return outs


def _sink_col(sinks_ref, g, tq):
    r = lax.broadcasted_iota(jnp.int32, (4 * tq, 1), 0)
    base = 4 * g
    return jnp.where(r < tq, sinks_ref[base],
                     jnp.where(r < 2 * tq, sinks_ref[base + 1],
                               jnp.where(r < 3 * tq, sinks_ref[base + 2], sinks_ref[base + 3])))


def _mem_head(ref, h):
    return _bf(ref[pl.ds(h, MEM_TOKENS, stride=MEM_HEADS), :])


def _mem_attention(q, mk_refs, mv_refs):
    qb = _bf(q)
    nseg = len(mk_refs)
    tq = q.shape[0] // nseg
    outs = []
    for h in range(MEM_HEADS):
        hs = slice(MEM_HD * h, MEM_HD * (h + 1))
        s = [_dot_nt(qb[tq * i:tq * (i + 1), hs], _mem_head(mk_refs[i], h)) for i in range(nseg)]
        s = jnp.concatenate(s, axis=0) if nseg > 1 else s[0]
        m = jnp.max(s, axis=-1, keepdims=True)
        p = jnp.exp(s - m)
        p = _bf(p / jnp.sum(p, axis=-1, keepdims=True))
        o = [_dot(p[tq * i:tq * (i + 1)], _mem_head(mv_refs[i], h)) for i in range(nseg)]
        outs.append(jnp.concatenate(o, axis=0) if nseg > 1 else o[0])
    return jnp.concatenate(outs, axis=1)


SWA_BLOCK_LANES = 4 * SWA_PAIR
LOG2E = float(np.log2(np.e))


def _swa_bias_t(npair, first_tile):
    nk = WINDOW + SWA_PAIR
    width = SWA_BLOCK_LANES * SWA_KV_HEADS * npair
    col = lax.broadcasted_iota(jnp.int32, (1, width), 1)
    lower = CHUNK * ((col % SWA_PAIR) // CHUNK)
    upper = lower + (WINDOW + CHUNK - 1)
    if first_tile:
        lower = jnp.maximum(lower, WINDOW - SWA_PAIR * ((col // SWA_BLOCK_LANES) % npair))
    krow = lax.broadcasted_iota(jnp.int32, (nk, width), 0)
    return jnp.where(krow >= lower, jnp.where(krow <= upper, 0.0, -jnp.inf), -jnp.inf)


def _swa_attention_t(qs, kk_scr, v_scr, sinks_ref, bias, rows, between=lambda: None):
    nk = WINDOW + SWA_PAIR
    npair = rows // SWA_PAIR
    bw = SWA_BLOCK_LANES
    blocks = [(g, p) for g in range(SWA_KV_HEADS) for p in range(npair)]
    sts = []
    for g, p in blocks:
        rs = slice(SWA_PAIR * p, SWA_PAIR * (p + 1))
        qrows = jnp.concatenate([qs[2 * g][0][rs], qs[2 * g][1][rs],
                                 qs[2 * g + 1][0][rs], qs[2 * g + 1][1][rs]], axis=0)
        sts.append(_dot_nt(kk_scr[g, SWA_PAIR * p:SWA_PAIR * p + nk, :], qrows))
    st = jnp.concatenate(sts, axis=1) + bias
    between()
    width = bw * len(blocks)
    col = lax.broadcasted_iota(jnp.int32, (1, width), 1)
    head = 4 * (col // (bw * npair)) + (col % bw) // SWA_PAIR
    sink = jnp.zeros((1, width), _F32)
    for h in range(SWA_HEADS):
        sink = jnp.where(head == h, sinks_ref[h] * LOG2E, sink)
    m = jnp.maximum(jnp.max(st, axis=0, keepdims=True), sink)
    pt = jnp.exp2(st - m)
    inv = 1.0 / (jnp.sum(pt, axis=0, keepdims=True) + jnp.exp2(sink - m))
    pb = _bf(pt)
    pieces = [[None] * npair for _ in range(SWA_HEADS)]
    for i, (g, p) in enumerate(blocks):
        cs = slice(bw * i, bw * (i + 1))
        o2 = _dot_tn(v_scr[SWA_PAIR * p:SWA_PAIR * p + nk, :], pb[:, cs]) * inv[:, cs]
        for hh in range(4):
            pieces[4 * g + hh][p] = o2[SWA_HD * g:SWA_HD * (g + 1), SWA_PAIR * hh:SWA_PAIR * (hh + 1)]
    return jnp.concatenate([jnp.concatenate(ps, axis=1) if npair > 1 else ps[0] for ps in pieces], axis=0)


def _mem_attention_t(q, mk_ref, mv_ref, between=lambda: None):
    qb = _bf(q)
    R = q.shape[0]
    st = jnp.concatenate([_dot_nt(_mem_head(mk_ref, h), qb[:, MEM_HD * h:MEM_HD * (h + 1)])
                          for h in range(MEM_HEADS)], axis=1)
    between()
    pt = jnp.exp2(st - jnp.max(st, axis=0, keepdims=True))
    inv = 1.0 / jnp.sum(pt, axis=0, keepdims=True)
    pb = _bf(pt)
    outs = [_dot_tn(_mem_head(mv_ref, h), pb[:, R * h:R * (h + 1)]) * inv[:, R * h:R * (h + 1)]
            for h in range(MEM_HEADS)]
    return jnp.concatenate(outs, axis=0)


def _gate(xb, i, w_inb_ref, bgate_ref):
    gl = _dot(xb, w_inb_ref[:, OFF_GL + D_MODEL * i:OFF_GL + D_MODEL * (i + 1)])
    return jax.nn.sigmoid(gl + bgate_ref[:, D_MODEL * i:D_MODEL * (i + 1)])


def _merge_ln1(x, gated_sum, wo_ref, g_ref, b_ref, alpha):
    mix = _dot(_bf(gated_sum), wo_ref[...])
    return _layer_norm(alpha * x + mix, g_ref[...], b_ref[...])


def _forget_log(xb, w_ga_ref, w2_ref, ba_ref):
    ga = _dot(xb, w_ga_ref[...])
    return _log_sigmoid(_dot(_bf(ga), w2_ref[...]) + ba_ref[...]) * (1.0 / GLA_TAU)


def _prompt_mixer_kernel(sinks_ref, x_ref, rc_ref, rdn_ref, rup_ref, mk_ref, mv_ref, w_ina_ref, w_ga_ref,
                         w_inb_ref, w2_ref, ba_ref, gng_ref, wbg_ref, wbs_ref, wbm_ref, bgate_ref, wo_ref,
                         ln_g_ref, ln_b_ref,
                         out_ref, gla_out_ref, swak_out_ref, swav_out_ref,
                         s_scr, kk_scr, vv_scr, bias_scr, *, rows, alpha):
    t = pl.program_id(1)
    nchunk = rows // CHUNK

    @pl.when(t == 0)
    def _():
        s_scr[...] = jnp.zeros_like(s_scr)
        kk_scr[:, 0:WINDOW, :] = jnp.zeros((SWA_KV_HEADS, WINDOW, LANES), _BF)
        vv_scr[0:WINDOW, :] = jnp.zeros((WINDOW, LANES), _BF)
        bias_scr[0] = _swa_bias_t(rows // SWA_PAIR, first_tile=False)
        bias_scr[1] = _swa_bias_t(rows // SWA_PAIR, first_tile=True)

    x = x_ref[0]
    xb = _bf(x)

    log_a = _forget_log(xb, w_ga_ref, w2_ref, ba_ref)

    def proj_a(off):
        return _dot(xb, w_ina_ref[:, off:off + GLA_W])

    def state_in(c, h):
        return s_scr[h]

    def state_out(c, h, st):
        s_scr[h] = st
        gla_out_ref[0, h] = jnp.transpose(st)

    gq, gk, gv = proj_a(OFF_GQ), proj_a(OFF_GK), proj_a(OFF_GV)
    gate_a = _gate(xb, 0, w_inb_ref, bgate_ref)
    late = {}

    def dense_a():
        late["gr"] = proj_a(OFF_GR)
        late["sq"] = _dot(xb, w_inb_ref[:, OFF_SQ:OFF_SQ + SWA_QW])
        late["sk"] = _dot(xb, w_inb_ref[:, OFF_SK:OFF_SK + SWA_KW])
        late["sv"] = _dot(xb, w_inb_ref[:, OFF_SV:OFF_SV + SWA_KW])

    def dense_b():
        late["gate_b"] = _gate(xb, 1, w_inb_ref, bgate_ref)

    def dense_c():
        late["gate_c"] = _gate(xb, 2, w_inb_ref, bgate_ref)

    o_a = _gla_tile(gq, gk, gv, log_a, CHUNK, nchunk, state_in, state_out, True, dense_a)
    o_a = _gla_gate(o_a, late["gr"], gng_ref[...])
    gated = gate_a * _dot(_bf(o_a), wbg_ref[...])

    rc, rdn, rup = rc_ref[...], rdn_ref[...], rup_ref[...]
    q = _rope(late["sq"], rc, rdn, rup) * (SWA_HD ** -0.5 * LOG2E)
    k = _rope(late["sk"], rc, rdn, rup)
    v = late["sv"]
    swak_out_ref[0] = k[rows - WINDOW:rows, :]
    swav_out_ref[0] = v[rows - WINDOW:rows, :]
    kk = _swa_dup_k(k)
    vb = _bf(v)
    for g in range(SWA_KV_HEADS):
        kk_scr[g, WINDOW:WINDOW + rows, :] = kk[g]
    vv_scr[WINDOW:WINDOW + rows, :] = vb
    o_bt = _swa_attention_t(_swa_split_q(q), kk_scr, vv_scr, sinks_ref, bias_scr[jnp.where(t == 0, 1, 0)], rows,
                            dense_b)
    gated = gated + late["gate_b"] * _dot_tn(_bf(o_bt), wbs_ref[...])
    for g in range(SWA_KV_HEADS):
        kk_scr[g, 0:WINDOW, :] = kk[g][rows - WINDOW:rows]
    vv_scr[0:WINDOW, :] = vb[rows - WINDOW:rows]

    qm = _dot(xb, w_inb_ref[:, OFF_MQ:OFF_MQ + MEM_W]) * (MEM_HD ** -0.5 * LOG2E)
    o_ct = _mem_attention_t(qm, mk_ref.at[0], mv_ref.at[0], dense_c)
    gated = gated + late["gate_c"] * _dot_tn(_bf(o_ct), wbm_ref[...])

    out_ref[0] = _merge_ln1(x, gated, wo_ref, ln_g_ref, ln_b_ref, alpha)


def _sample_mixer_kernel(sinks_ref, x_ref, rc_ref, rdn_ref, rup_ref, mk_ref, mv_ref, gla_in_ref,
                         swak_in_ref, swav_in_ref, w_ina_ref, w_ga_ref, w_inb_ref, w2_ref,
                         ba_ref, gng_ref, wbg_ref, wbs_ref, wbm_ref, bgate_ref, wo_ref, ln_g_ref, ln_b_ref,
                         out_ref, gla_out_ref, swak_out_ref, swav_out_ref, *, nb, tq, alpha):
    rows = nb * tq
    x = x_ref[...].reshape(rows, D_MODEL)
    xb = _bf(x)

    log_a = _forget_log(xb, w_ga_ref, w2_ref, ba_ref)

    def proj_a(off):
        return _dot(xb, w_ina_ref[:, off:off + GLA_W])

    def state_in(c, h):
        return jnp.transpose(gla_in_ref[c, h])

    def state_out(c, h, st):
        gla_out_ref[c, h] = jnp.transpose(st)

    gq, gk, gv = proj_a(OFF_GQ), proj_a(OFF_GK), proj_a(OFF_GV)
    gate_a = _gate(xb, 0, w_inb_ref, bgate_ref)
    o_a = _gla_tile(gq, gk, gv, log_a, tq, nb, state_in, state_out, chained=False)
    o_a = _gla_gate(o_a, proj_a(OFF_GR), gng_ref[...])
    gated = gate_a * _dot(_bf(o_a), wbg_ref[...])

    rc = jnp.concatenate([rc_ref[...]] * nb, axis=0)
    rdn = jnp.concatenate([rdn_ref[...]] * nb, axis=0)
    rup = jnp.concatenate([rup_ref[...]] * nb, axis=0)
    q = _rope(_dot(xb, w_inb_ref[:, OFF_SQ:OFF_SQ + SWA_QW]), rc, rdn, rup) * (SWA_HD ** -0.5)
    k = _rope(_dot(xb, w_inb_ref[:, OFF_SK:OFF_SK + SWA_KW]), rc, rdn, rup)
    v = _dot(xb, w_inb_ref[:, OFF_SV:OFF_SV + SWA_KW])
    gate_b = _gate(xb, 1, w_inb_ref, bgate_ref)
    qs = _swa_split_q(q)
    slabs = [[None] * nb for _ in range(SWA_QW // LANES)]
    kks, vvs = [], []
    for bi in range(nb):
        rs = slice(tq * bi, tq * (bi + 1))
        k_all = jnp.concatenate([swak_in_ref[bi], k[rs]], axis=0)
        v_all = jnp.concatenate([swav_in_ref[bi], v[rs]], axis=0)
        swak_out_ref[bi] = k_all[tq:tq + WINDOW]
        swav_out_ref[bi] = v_all[tq:tq + WINDOW]
        kks.append(_swa_dup_k(k_all))
        vvs.append(_swa_place_v(v_all))
    for g in range(SWA_KV_HEADS):
        qrows = []
        for bi in range(nb):
            rs = slice(tq * bi, tq * (bi + 1))
            qrows.append(jnp.concatenate([qs[2 * g][0][rs], qs[2 * g][1][rs],
                                          qs[2 * g + 1][0][rs], qs[2 * g + 1][1][rs]], axis=0))
        outs = _swa_blocks(qrows, [kk[g] for kk in kks], [vv[g][0] for vv in vvs], [vv[g][1] for vv in vvs],
                           _sink_col(sinks_ref, g, tq), None, None, tq)
        for bi in range(nb):
            slabs[2 * g][bi], slabs[2 * g + 1][bi] = outs[bi]
    o_b = jnp.concatenate([jnp.concatenate(s, axis=0) if nb > 1 else s[0] for s in slabs], axis=1)
    gated = gated + gate_b * _dot(_bf(o_b), wbs_ref[...])

    qm = _dot(xb, w_inb_ref[:, OFF_MQ:OFF_MQ + MEM_W]) * (MEM_HD ** -0.5)
    gate_c = _gate(xb, 2, w_inb_ref, bgate_ref)
    o_c = _mem_attention(qm, [mk_ref.at[bi] for bi in range(nb)], [mv_ref.at[bi] for bi in range(nb)])
    gated = gated + gate_c * _dot(_bf(o_c), wbm_ref[...])

    out = _merge_ln1(x, gated, wo_ref, ln_g_ref, ln_b_ref, alpha)
    out_ref[...] = out.reshape(nb, tq, D_MODEL)


_GELU_K1 = -2.0 * float(np.log2(np.e)) * float(np.sqrt(2.0 / np.pi))
_GELU_K2 = _GELU_K1 * 0.044715


def _conv(u2, u1, u0, cw_ref, cb_ref, cs):
    return cb_ref[:, cs] + u2 * cw_ref[0:1, cs] + u1 * cw_ref[1:2, cs] + u0 * cw_ref[2:3, cs]


def _geglu(g, v):
    return g * v / (1.0 + jnp.exp2(g * (g * g * _GELU_K2 + _GELU_K1)))


def _prompt_ffn_kernel(x_ref, wup_ref, cw_ref, cb_ref, wdn_ref, ln_g_ref, ln_b_ref,
                       out_ref, conv_out_ref, u_scr, *, rows, alpha):
    nj = rows // SUBLANES
    base = 2 * SUBLANES
    last = slice(base + rows - SUBLANES, base + rows)
    last2 = slice(base + rows - 2 * SUBLANES, base + rows - SUBLANES)

    @pl.when(pl.program_id(1) == 0)
    def _():
        u_scr[base + rows - 2 * SUBLANES:base + rows, :] = jnp.zeros((2 * SUBLANES, 2 * D_FF), _F32)

    prev_last, prev_last2 = u_scr[last, :], u_scr[last2, :]
    x = jnp.swapaxes(x_ref[0].reshape(SUBLANES, nj, D_MODEL), 0, 1).reshape(rows, D_MODEL)
    u_scr[base:base + rows, :] = _dot(_bf(x), wup_ref[...])
    sub = lax.broadcasted_iota(jnp.int32, (SUBLANES, FFN_COLS), 0)

    def conv_cols(cs):
        wrap = lambda prev, cur: pltpu.roll(jnp.where(sub == SUBLANES - 1, prev, cur), 1, axis=0)
        u_scr[SUBLANES:base, cs] = wrap(prev_last[:, cs], u_scr[last, cs])
        u_scr[0:SUBLANES, cs] = wrap(prev_last2[:, cs], u_scr[last2, cs])
        return _conv(u_scr[0:rows, cs], u_scr[SUBLANES:SUBLANES + rows, cs], u_scr[base:base + rows, cs],
                     cw_ref, cb_ref, cs)

    h = [_bf(_geglu(conv_cols(slice(c0, c0 + FFN_COLS)), conv_cols(slice(D_FF + c0, D_FF + c0 + FFN_COLS))))
         for c0 in range(0, D_FF, FFN_COLS)]
    conv_out_ref[0] = jnp.concatenate([u_scr[base + rows - SUBLANES - 1:base + rows - SUBLANES, :],
                                       u_scr[base + rows - 1:base + rows, :]], axis=0)
    f = _dot(jnp.concatenate(h, axis=1), wdn_ref[...])
    y = _layer_norm(alpha * x + f, ln_g_ref[...], ln_b_ref[...])
    out_ref[0] = jnp.swapaxes(y.reshape(nj, SUBLANES, D_MODEL), 0, 1).reshape(rows, D_MODEL)


def _sample_ffn_kernel(x_ref, hist_ref, wup_ref, cw_ref, cb_ref, wdn_ref, ln_g_ref, ln_b_ref,
                       out_ref, conv_out_ref, u_scr, *, nb, tq, alpha):
    rows = nb * tq
    base = (CONV_W - 1) * nb
    x = jnp.swapaxes(x_ref[...], 0, 1).reshape(rows, D_MODEL)
    u_scr[0:base, :] = jnp.swapaxes(hist_ref[...], 0, 1).reshape(base, 2 * D_FF)
    u_scr[base:base + rows, :] = _dot(_bf(x), wup_ref[...])
    conv = lambda cs: _conv(u_scr[0:rows, cs], u_scr[nb:nb + rows, cs], u_scr[base:base + rows, cs],
                            cw_ref, cb_ref, cs)
    h = _geglu(conv(slice(0, D_FF)), conv(slice(D_FF, 2 * D_FF)))
    conv_out_ref[...] = jnp.swapaxes(u_scr[rows:rows + base, :].reshape(CONV_W - 1, nb, 2 * D_FF), 0, 1)
    f = _dot(_bf(h), wdn_ref[...])
    y = _layer_norm(alpha * x + f, ln_g_ref[...], ln_b_ref[...])
    out_ref[...] = jnp.swapaxes(y.reshape(tq, nb, D_MODEL), 0, 1)


def _mem_kv_kernel(m_ref, w_ref, k_ref, v_ref):
    kv = _dot(_bf(m_ref[0]), w_ref[...])
    for h in range(MEM_HEADS):
        k_ref[0, pl.ds(h, MEM_TOKENS, stride=MEM_HEADS), :] = kv[:, MEM_HD * h:MEM_HD * (h + 1)]
        v_ref[0, pl.ds(h, MEM_TOKENS, stride=MEM_HEADS), :] = kv[:, MEM_W + MEM_HD * h:MEM_W + MEM_HD * (h + 1)]


def _const_spec(shape):
    nd = len(shape)
    return pl.BlockSpec(shape, lambda *_: (0,) * nd, pipeline_mode=pl.Buffered(1))


def _rope_tables(pos):
    half = ROPE_DIM // 2
    inv = np.float32(ROPE_THETA) ** (-np.arange(half, dtype=np.float32) / np.float32(half))
    ang = pos.astype(np.float32)[:, None] * inv[None, :]
    cos, sin = np.cos(ang), np.sin(ang)
    T = pos.shape[0]
    ones = np.ones((T, SWA_HD - ROPE_DIM), np.float32)
    zeros = np.zeros((T, SWA_HD - ROPE_DIM), np.float32)
    zh = np.zeros((T, half), np.float32)
    c = np.concatenate([cos, cos, ones], axis=1)
    dn = np.concatenate([-sin, zh, zeros], axis=1)
    up = np.concatenate([zh, sin, zeros], axis=1)
    rep = LANES // SWA_HD
    return tuple(jnp.asarray(np.concatenate([a] * rep, axis=1), dtype=_F32) for a in (c, dn, up))


def _pack_cols_kernel(wt_ref, o_ref, *, keep):
    t = jnp.transpose(wt_ref[...])
    if keep < t.shape[1]:
        t = jnp.where(lax.broadcasted_iota(jnp.int32, t.shape, 1) < keep, t, 0.0)
    o_ref[...] = _bf(t)


def _pack_cols(wt, col0, ncols, keep, name):
    blk = min(ncols, 2 * LANES)
    k = wt.shape[1]
    return pl.pallas_call(
        functools.partial(_pack_cols_kernel, keep=keep),
        grid=(ncols // blk,),
        in_specs=[pl.BlockSpec((pl.Element(blk), pl.Element(k)),
                               lambda i: (pl.multiple_of(col0 + blk * i, SUBLANES), 0))],
        out_specs=pl.BlockSpec((k, blk), lambda i: (0, i)),
        out_shape=jax.ShapeDtypeStruct((k, ncols), _BF),
        compiler_params=_params(("arbitrary",)),
        name=name,
    )(wt)


def _split_w_in(w_in):
    wt = jnp.transpose(w_in)
    return (_pack_cols(wt, 0, IN_A_COLS, IN_A_COLS, "pack_w_in_a"),
            _pack_cols(wt, IN_A_COLS, LANES, GLA_RANK, "pack_w_in_ga"),
            _pack_cols(wt, IN_A_COLS + GLA_RANK, IN_B_COLS, IN_B_COLS, "pack_w_in_b"))


def _mixer_weight_specs():
    return [
        _const_spec((D_MODEL, IN_A_COLS)),
        _const_spec((D_MODEL, LANES)),
        _const_spec((D_MODEL, IN_B_COLS)),
        _const_spec((LANES, GLA_W)),
        _const_spec((1, GLA_W)),
        _const_spec((1, GLA_DV)),
        _const_spec((GLA_W, D_MODEL)),
        _const_spec((SWA_QW, D_MODEL)),
        _const_spec((MEM_W, D_MODEL)),
        _const_spec((1, GATE_W)),
        _const_spec((D_MODEL, D_MODEL)),
        _const_spec((1, D_MODEL)),
        _const_spec((1, D_MODEL)),
    ]


def _ffn_weight_specs():
    return [
        _const_spec((D_MODEL, 2 * D_FF)),
        _const_spec((CONV_W, 2 * D_FF)),
        _const_spec((1, 2 * D_FF)),
        _const_spec((D_FF, D_MODEL)),
        _const_spec((1, D_MODEL)),
        _const_spec((1, D_MODEL)),
    ]


_SMEM_SPEC = pl.BlockSpec(memory_space=pltpu.SMEM)
_MEM_ROWS = MEM_TOKENS * MEM_HEADS


def _params(sem, flags=None):
    return pltpu.CompilerParams(dimension_semantics=sem, vmem_limit_bytes=V7X_VMEM_LIMIT, flags=flags)


def _prompt_layer(x, mem, sinks, mixer_w, ffn_w, w_mem_kv, alpha):
    B, T, _ = x.shape
    rows = min(PROMPT_ROWS, T)
    nt = T // rows
    assert T % rows == 0 and rows % SWA_PAIR == 0 and rows >= WINDOW
    f32 = jnp.float32
    mk, mv = pl.pallas_call(
        _mem_kv_kernel,
        grid=(B,),
        in_specs=[pl.BlockSpec((1, MEM_TOKENS, D_MODEL), lambda b: (b, 0, 0)),
                  _const_spec((D_MODEL, 2 * MEM_W))],
        out_specs=[pl.BlockSpec((1, _MEM_ROWS, MEM_HD), lambda b: (b, 0, 0))] * 2,
        out_shape=[jax.ShapeDtypeStruct((B, _MEM_ROWS, MEM_HD), f32)] * 2,
        compiler_params=_params(("arbitrary",)),
        name="mem_kv",
    )(mem, w_mem_kv)

    rope = _rope_tables(np.arange(T))
    rope_spec = pl.BlockSpec((rows, LANES), lambda b, t: (t, 0))
    per_batch = lambda *blk: pl.BlockSpec((1,) + blk, lambda b, t: (b,) + (0,) * len(blk))
    x1, gla, swak, swav = pl.pallas_call(
        functools.partial(_prompt_mixer_kernel, rows=rows, alpha=alpha),
        grid=(B, nt),
        in_specs=[_SMEM_SPEC,
                  pl.BlockSpec((1, rows, D_MODEL), lambda b, t: (b, t, 0)),
                  rope_spec, rope_spec, rope_spec,
                  per_batch(_MEM_ROWS, MEM_HD), per_batch(_MEM_ROWS, MEM_HD)] + _mixer_weight_specs(),
        out_specs=[pl.BlockSpec((1, rows, D_MODEL), lambda b, t: (b, t, 0)),
                   per_batch(GLA_HEADS, GLA_DK, GLA_DV),
                   per_batch(WINDOW, SWA_KW), per_batch(WINDOW, SWA_KW)],
        out_shape=[jax.ShapeDtypeStruct((B, T, D_MODEL), f32),
                   jax.ShapeDtypeStruct((B, GLA_HEADS, GLA_DK, GLA_DV), f32),
                   jax.ShapeDtypeStruct((B, WINDOW, SWA_KW), f32),
                   jax.ShapeDtypeStruct((B, WINDOW, SWA_KW), f32)],
        scratch_shapes=[pltpu.VMEM((GLA_HEADS, GLA_DK, GLA_DV), f32),
                        pltpu.VMEM((SWA_KV_HEADS, WINDOW + rows, LANES), _BF),
                        pltpu.VMEM((WINDOW + rows, LANES), _BF),
                        pltpu.VMEM((2, WINDOW + SWA_PAIR, SWA_BLOCK_LANES * SWA_KV_HEADS * (rows // SWA_PAIR)),
                                   f32)],
        compiler_params=_params(("arbitrary", "arbitrary")),
        name="prompt_mixer",
    )(sinks, x, *rope, mk, mv, *mixer_w)

    frows = min(PROMPT_FFN_ROWS, T)
    assert T % frows == 0 and frows >= CONV_W - 1
    y, conv = pl.pallas_call(
        functools.partial(_prompt_ffn_kernel, rows=frows, alpha=alpha),
        grid=(B, T // frows),
        in_specs=[pl.BlockSpec((1, frows, D_MODEL), lambda b, t: (b, t, 0))] + _ffn_weight_specs(),
        out_specs=[pl.BlockSpec((1, frows, D_MODEL), lambda b, t: (b, t, 0)),
                   per_batch(CONV_W - 1, 2 * D_FF)],
        out_shape=[jax.ShapeDtypeStruct((B, T, D_MODEL), f32),
                   jax.ShapeDtypeStruct((B, CONV_W - 1, 2 * D_FF), f32)],
        scratch_shapes=[pltpu.VMEM((2 * SUBLANES + frows, 2 * D_FF), f32)],
        compiler_params=_params(("arbitrary", "arbitrary")),
        name="prompt_ffn",
    )(x1, *ffn_w)
    return y, gla, swak, swav, mk, mv, conv


def _sample_layer(x, gla0, swak0, swav0, memk, memv, conv0, sinks, mixer_w, ffn_w, alpha):
    B, tq, _ = x.shape
    nb = max(1, min(B, SAMPLE_ROWS // tq))
    assert B % nb == 0 and tq % 16 == 0 and tq >= CONV_W - 1
    f32 = jnp.float32
    rope = _rope_tables(PAST_LEN + np.arange(tq))
    rope_spec = pl.BlockSpec((tq, LANES), lambda i: (0, 0))
    blk = lambda *s: pl.BlockSpec((nb,) + s, lambda i: (i,) + (0,) * len(s))
    blk1 = lambda *s: pl.BlockSpec((nb,) + s, lambda i: (i,) + (0,) * len(s), pipeline_mode=pl.Buffered(1))
    x1, gla, swak, swav = pl.pallas_call(
        functools.partial(_sample_mixer_kernel, nb=nb, tq=tq, alpha=alpha),
        grid=(B // nb,),
        in_specs=[_SMEM_SPEC, blk(tq, D_MODEL), rope_spec, rope_spec, rope_spec,
                  blk(_MEM_ROWS, MEM_HD), blk(_MEM_ROWS, MEM_HD),
                  blk1(GLA_HEADS, GLA_DK, GLA_DV), blk1(WINDOW, SWA_KW), blk1(WINDOW, SWA_KW)]
                 + _mixer_weight_specs(),
        out_specs=[blk(tq, D_MODEL), blk(GLA_HEADS, GLA_DK, GLA_DV), blk(WINDOW, SWA_KW), blk(WINDOW, SWA_KW)],
        out_shape=[jax.ShapeDtypeStruct((B, tq, D_MODEL), f32),
                   jax.ShapeDtypeStruct((B, GLA_HEADS, GLA_DK, GLA_DV), f32),
                   jax.ShapeDtypeStruct((B, WINDOW, SWA_KW), f32),
                   jax.ShapeDtypeStruct((B, WINDOW, SWA_KW), f32)],
        compiler_params=_params(("arbitrary",)),
        name="sample_mixer",
    )(sinks, x, *rope, memk, memv, gla0, swak0, swav0, *mixer_w)

    assert B % SUBLANES == 0
    fblk = lambda *s: pl.BlockSpec((SUBLANES,) + s, lambda i: (i,) + (0,) * len(s))
    y, conv = pl.pallas_call(
        functools.partial(_sample_ffn_kernel, nb=SUBLANES, tq=tq, alpha=alpha),
        grid=(B // SUBLANES,),
        in_specs=[fblk(tq, D_MODEL), fblk(CONV_W - 1, 2 * D_FF)] + _ffn_weight_specs(),
        out_specs=[fblk(tq, D_MODEL), fblk(CONV_W - 1, 2 * D_FF)],
        out_shape=[jax.ShapeDtypeStruct((B, tq, D_MODEL), f32),
                   jax.ShapeDtypeStruct((B, CONV_W - 1, 2 * D_FF), f32)],
        scratch_shapes=[pltpu.VMEM(((CONV_W - 1 + tq) * SUBLANES, 2 * D_FF), f32)],
        compiler_params=_params(("arbitrary",)),
        name="sample_ffn",
    )(x1, conv0, *ffn_w)
    return y, gla, swak, swav, conv


def kernel(x_prompt, x_sample, cache_swa_k, cache_swa_v, state_gla, cache_mem_k, cache_mem_v, cache_ffn_conv, mem_prompt, ln1_g, ln1_b, ln2_g, ln2_b, w_in, b_gate, w_gla_a2, b_gla_a, gla_norm_g, swa_sinks, w_mem_kv, w_br_gla, w_br_swa, w_br_mem, w_o, w_up, conv_w, conv_b, w_down):
    depth = w_in.shape[0]
    alpha = float((2 * depth) ** 0.25)
    Bp = x_prompt.shape[0]
    Bs = x_sample.shape[0]
    hp, hs = x_prompt, x_sample
    outs = [[] for _ in range(10)]
    row = lambda a: a.reshape(1, -1)
    for l in range(depth):
        w2 = jnp.pad(w_gla_a2[l], ((0, LANES - GLA_RANK), (0, 0)))
        mixer_w = _split_w_in(w_in[l]) + (
            _bf(w2), row(b_gla_a[l]), row(gla_norm_g[l]),
            _bf(w_br_gla[l]), _bf(w_br_swa[l]), _bf(w_br_mem[l]), row(b_gate[l]), _bf(w_o[l]),
            row(ln1_g[l]), row(ln1_b[l]))
        ffn_w = (_bf(w_up[l]), conv_w[l], row(conv_b[l]), _bf(w_down[l]), row(ln2_g[l]), row(ln2_b[l]))
        sinks = swa_sinks[l]
        hp, g_p, k_p, v_p, mk_p, mv_p, c_p = _prompt_layer(
            hp, mem_prompt, sinks, mixer_w, ffn_w, _bf(w_mem_kv[l]), alpha)
        hs, g_s, k_s, v_s, c_s = _sample_layer(
            hs, state_gla[l],
            cache_swa_k[l].reshape(Bs, WINDOW, SWA_KW), cache_swa_v[l].reshape(Bs, WINDOW, SWA_KW),
            cache_mem_k[l].reshape(Bs, _MEM_ROWS, MEM_HD), cache_mem_v[l].reshape(Bs, _MEM_ROWS, MEM_HD),
            cache_ffn_conv[l], sinks, mixer_w, ffn_w, alpha)
        kv5 = lambda a, b: a.reshape(b, WINDOW, SWA_KV_HEADS, SWA_HD)
        m5 = lambda a: a.reshape(Bp, MEM_TOKENS, MEM_HEADS, MEM_HD)
        for lst, val in zip(outs, (kv5(k_p, Bp), kv5(v_p, Bp), g_p, m5(mk_p), m5(mv_p), c_p,
                                   kv5(k_s, Bs), kv5(v_s, Bs), g_s, c_s)):
            lst.append(val)
    return (hp, hs) + tuple(jnp.stack(o) for o in outs)
```

```python
import functools

import jax
import jax.numpy as jnp
import numpy as np
from jax import lax
from jax.experimental import pallas as pl
from jax.experimental.pallas import tpu as pltpu

D_MODEL = 1024
CHUNK = 64
GLA_HEADS = 4
GLA_DK = 128
GLA_DV = 128
GLA_RANK = 16
GLA_TAU = 16.0
SWA_HEADS = 8
SWA_KV_HEADS = 2
SWA_HD = 64
WINDOW = 128
ROPE_DIM = 16
ROPE_THETA = 500000.0
MEM_TOKENS = 256
MEM_HEADS = 4
MEM_HD = 128
D_FF = 2816
CONV_W = 3
N_BRANCH = 3
PAST_LEN = 2048
LN_EPS = 1e-5

LANES = 128
SUBLANES = 8
V7X_VMEM_LIMIT = 56 * 1024 * 1024
PROMPT_ROWS = 256
PROMPT_FFN_ROWS = 512
FFN_COLS = 256
SAMPLE_ROWS = 256
GLA_W = GLA_HEADS * GLA_DK
SWA_QW = SWA_HEADS * SWA_HD
SWA_KW = SWA_KV_HEADS * SWA_HD
MEM_W = MEM_HEADS * MEM_HD
GATE_W = N_BRANCH * D_MODEL
SWA_PAIR = 2 * CHUNK

OFF_GQ = 0
OFF_GK = OFF_GQ + GLA_W
OFF_GV = OFF_GK + GLA_W
OFF_GR = OFF_GV + GLA_W
IN_A_COLS = OFF_GR + GLA_W
OFF_SQ = 0
OFF_SK = OFF_SQ + SWA_QW
OFF_SV = OFF_SK + SWA_KW
OFF_MQ = OFF_SV + SWA_KW
OFF_GL = OFF_MQ + MEM_W
IN_B_COLS = OFF_GL + GATE_W

_BF = jnp.bfloat16
_F32 = jnp.float32


def _bf(x):
    return x.astype(_BF)


def _dot(a, b):
    return jnp.dot(a, b, preferred_element_type=_F32)


def _dot_nt(a, b):
    return lax.dot_general(a, b, (((1,), (1,)), ((), ())), preferred_element_type=_F32)


def _dot_tn(a, b):
    return lax.dot_general(a, b, (((0,), (0,)), ((), ())), preferred_element_type=_F32)


def _layer_norm(h, g, b):
    mu = jnp.mean(h, axis=-1, keepdims=True)
    d = h - mu
    var = jnp.mean(d * d, axis=-1, keepdims=True)
    return d * lax.rsqrt(var + LN_EPS) * g + b


def _log_sigmoid(x):
    return -(jnp.maximum(-x, 0.0) + jnp.log(1.0 + jnp.exp(-jnp.abs(x))))


def _rope(a, c, s_dn, s_up):
    outs = []
    for j in range(a.shape[1] // LANES):
        slab = a[:, LANES * j:LANES * (j + 1)]
        fwd = pltpu.roll(slab, LANES - ROPE_DIM // 2, axis=1)
        bwd = pltpu.roll(slab, ROPE_DIM // 2, axis=1)
        outs.append(slab * c + fwd * s_dn + bwd * s_up)
    return outs[0] if len(outs) == 1 else jnp.concatenate(outs, axis=1)


def _gla_tile(q, k, v, log2_a, chunk, nchunk, state_in, state_out, chained, between=lambda: None):
    R = chunk * nchunk
    row = lax.broadcasted_iota(jnp.int32, (R, R), 0)
    col = lax.broadcasted_iota(jnp.int32, (R, R), 1)
    tril = (row // chunk == col // chunk) & (col <= row)
    ltri = jnp.where(tril, 1.0, 0.0).astype(_BF)
    hi = _bf(log2_a)
    lo = _bf(log2_a - hi.astype(_F32))
    b = _dot(ltri, hi) + _dot(ltri, lo)
    between()
    decay = [jnp.exp2(b[chunk * (c + 1) - 1:chunk * (c + 1), :]) for c in range(nchunk)]
    q_dec = _bf(q * (GLA_DK ** -0.5) * jnp.exp2(b))
    k_inv = k * jnp.exp2(-b)
    k_tail = _bf(jnp.concatenate([k_inv[chunk * c:chunk * (c + 1)] * decay[c] for c in range(nchunk)], axis=0)
                 if nchunk > 1 else k_inv * decay[0])
    k_inv = _bf(k_inv)
    vb = _bf(v)
    heads = [slice(GLA_DK * h, GLA_DK * (h + 1)) for h in range(GLA_HEADS)]
    o_intra = []
    for hs in heads:
        att = jnp.where(tril, _dot_nt(q_dec[:, hs], k_inv[:, hs]), 0.0)
        o_intra.append(_dot(_bf(att), vb[:, hs]))
    o_inter = [[None] * nchunk for _ in heads]
    st = [None] * GLA_HEADS
    for c in range(nchunk):
        rs = slice(chunk * c, chunk * (c + 1))
        for h, hs in enumerate(heads):
            if c == 0 or not chained:
                st[h] = state_in(c, h)
            o_inter[h][c] = _dot_nt(q_dec[rs, hs], _bf(st[h]))
            st[h] = st[h] * decay[c][:, hs] + _dot_tn(vb[rs, hs], k_tail[rs, hs])
            if c == nchunk - 1 or not chained:
                state_out(c, h, st[h])
    outs = [o_intra[h] + (jnp.concatenate(o_inter[h], axis=0) if nchunk > 1 else o_inter[h][0])
            for h in range(GLA_HEADS)]
    return jnp.concatenate(outs, axis=1)


def _gla_gate(o, gr, g):
    outs = []
    for h in range(GLA_HEADS):
        hs = slice(GLA_DV * h, GLA_DV * (h + 1))
        oh = o[:, hs]
        ms = jnp.mean(oh * oh, axis=-1, keepdims=True)
        grh = gr[:, hs]
        outs.append(oh * lax.rsqrt(ms + LN_EPS) * g * (grh * jax.nn.sigmoid(grh)))
    return jnp.concatenate(outs, axis=1)


def _lane_half_masks(shape):
    lane = lax.broadcasted_iota(jnp.int32, shape, 1)
    return lane < SWA_HD, lane >= SWA_HD


def _swa_split_q(q):
    lo_m, hi_m = _lane_half_masks((q.shape[0], LANES))
    out = []
    for j in range(SWA_QW // LANES):
        slab = q[:, LANES * j:LANES * (j + 1)]
        out.append((_bf(jnp.where(lo_m, slab, 0.0)), _bf(jnp.where(hi_m, slab, 0.0))))
    return out


def _swa_dup_k(k):
    lo_m, _ = _lane_half_masks(k.shape)
    kr = pltpu.roll(k, SWA_HD, axis=1)
    return [_bf(jnp.where(lo_m, k, kr)), _bf(jnp.where(lo_m, kr, k))]


def _swa_place_v(v):
    lo_m, hi_m = _lane_half_masks(v.shape)
    vr = pltpu.roll(v, SWA_HD, axis=1)
    return [(_bf(jnp.where(lo_m, v, 0.0)), _bf(jnp.where(hi_m, vr, 0.0))),
            (_bf(jnp.where(lo_m, vr, 0.0)), _bf(jnp.where(hi_m, v, 0.0)))]


def _swa_blocks(qrows, kks, vlos, vhis, sink_col, lowers, upper, tq):
    n = len(kks)
    stack = lambda xs: jnp.concatenate(xs, axis=0) if n > 1 else xs[0]
    s = stack([_dot_nt(qrows[i], kks[i]) for i in range(n)])
    if lowers is not None:
        kcol = lax.broadcasted_iota(jnp.int32, s.shape, 1)
        s = jnp.where(kcol >= stack(lowers), jnp.where(kcol <= stack([upper] * n), s, -jnp.inf), -jnp.inf)
    sink = stack([sink_col] * n)
    m = jnp.maximum(jnp.max(s, axis=-1, keepdims=True), sink)
    p = jnp.exp(s - m)
    den = jnp.sum(p, axis=-1, keepdims=True) + jnp.exp(sink - m)
    p = _bf(p / den)
    outs = []
    for i in range(n):
        r = 4 * tq * i
        outs.append((_dot(p[r:r + tq], vlos[i]) + _dot(p[r + tq:r + 2 * tq], vhis[i]),
                     _dot(p[r + 2 * tq:r + 3 * tq], vlos[i]) + _dot(p[r + 3 * tq:r + 4 * tq], vhis[i])))
    return outs


def _sink_col(sinks_ref, g, tq):
    r = lax.broadcasted_iota(jnp.int32, (4 * tq, 1), 0)
    base = 4 * g
    return jnp.where(r < tq, sinks_ref[base],
                     jnp.where(r < 2 * tq, sinks_ref[base + 1],
                               jnp.where(r < 3 * tq, sinks_ref[base + 2], sinks_ref[base + 3])))


def _mem_head(ref, h):
    return _bf(ref[pl.ds(h, MEM_TOKENS, stride=MEM_HEADS), :])


def _mem_attention(q, mk_refs, mv_refs):
    qb = _bf(q)
    nseg = len(mk_refs)
    tq = q.shape[0] // nseg
    outs = []
    for h in range(MEM_HEADS):
        hs = slice(MEM_HD * h, MEM_HD * (h + 1))
        s = [_dot_nt(qb[tq * i:tq * (i + 1), hs], _mem_head(mk_refs[i], h)) for i in range(nseg)]
        s = jnp.concatenate(s, axis=0) if nseg > 1 else s[0]
        m = jnp.max(s, axis=-1, keepdims=True)
        p = jnp.exp(s - m)
        p = _bf(p / jnp.sum(p, axis=-1, keepdims=True))
        o = [_dot(p[tq * i:tq * (i + 1)], _mem_head(mv_refs[i], h)) for i in range(nseg)]
        outs.append(jnp.concatenate(o, axis=0) if nseg > 1 else o[0])
    return jnp.concatenate(outs, axis=1)


SWA_BLOCK_LANES = 4 * SWA_PAIR
LOG2E = float(np.log2(np.e))


def _swa_bias_t(npair, first_tile):
    nk = WINDOW + SWA_PAIR
    width = SWA_BLOCK_LANES * SWA_KV_HEADS * npair
    col = lax.broadcasted_iota(jnp.int32, (1, width), 1)
    lower = CHUNK * ((col % SWA_PAIR) // CHUNK)
    upper = lower + (WINDOW + CHUNK - 1)
    if first_tile:
        lower = jnp.maximum(lower, WINDOW - SWA_PAIR * ((col // SWA_BLOCK_LANES) % npair))
    krow = lax.broadcasted_iota(jnp.int32, (nk, width), 0)
    return jnp.where(krow >= lower, jnp.where(krow <= upper, 0.0, -jnp.inf), -jnp.inf)


def _swa_attention_t(qs, kk_scr, v_scr, sinks_ref, bias, rows, between=lambda: None):
    nk = WINDOW + SWA_PAIR
    npair = rows // SWA_PAIR
    bw = SWA_BLOCK_LANES
    blocks = [(g, p) for g in range(SWA_KV_HEADS) for p in range(npair)]
    sts = []
    for g, p in blocks:
        rs = slice(SWA_PAIR * p, SWA_PAIR * (p + 1))
        qrows = jnp.concatenate([qs[2 * g][0][rs], qs[2 * g][1][rs],
                                 qs[2 * g + 1][0][rs], qs[2 * g + 1][1][rs]], axis=0)
        sts.append(_dot_nt(kk_scr[g, SWA_PAIR * p:SWA_PAIR * p + nk, :], qrows))
    st = jnp.concatenate(sts, axis=1) + bias
    between()
    width = bw * len(blocks)
    col = lax.broadcasted_iota(jnp.int32, (1, width), 1)
    head = 4 * (col // (bw * npair)) + (col % bw) // SWA_PAIR
    sink = jnp.zeros((1, width), _F32)
    for h in range(SWA_HEADS):
        sink = jnp.where(head == h, sinks_ref[h] * LOG2E, sink)
    m = jnp.maximum(jnp.max(st, axis=0, keepdims=True), sink)
    pt = jnp.exp2(st - m)
    inv = 1.0 / (jnp.sum(pt, axis=0, keepdims=True) + jnp.exp2(sink - m))
    pb = _bf(pt)
    pieces = [[None] * npair for _ in range(SWA_HEADS)]
    for i, (g, p) in enumerate(blocks):
        cs = slice(bw * i, bw * (i + 1))
        o2 = _dot_tn(v_scr[SWA_PAIR * p:SWA_PAIR * p + nk, :], pb[:, cs]) * inv[:, cs]
        for hh in range(4):
            pieces[4 * g + hh][p] = o2[SWA_HD * g:SWA_HD * (g + 1), SWA_PAIR * hh:SWA_PAIR * (hh + 1)]
    return jnp.concatenate([jnp.concatenate(ps, axis=1) if npair > 1 else ps[0] for ps in pieces], axis=0)


def _mem_attention_t(q, mk_ref, mv_ref, between=lambda: None):
    qb = _bf(q)
    R = q.shape[0]
    st = jnp.concatenate([_dot_nt(_mem_head(mk_ref, h), qb[:, MEM_HD * h:MEM_HD * (h + 1)])
                          for h in range(MEM_HEADS)], axis=1)
    between()
    pt = jnp.exp2(st - jnp.max(st, axis=0, keepdims=True))
    inv = 1.0 / jnp.sum(pt, axis=0, keepdims=True)
    pb = _bf(pt)
    outs = [_dot_tn(_mem_head(mv_ref, h), pb[:, R * h:R * (h + 1)]) * inv[:, R * h:R * (h + 1)]
            for h in range(MEM_HEADS)]
    return jnp.concatenate(outs, axis=0)


def _gate(xb, i, w_inb_ref, bgate_ref):
    gl = _dot(xb, w_inb_ref[:, OFF_GL + D_MODEL * i:OFF_GL + D_MODEL * (i + 1)])
    return jax.nn.sigmoid(gl + bgate_ref[:, D_MODEL * i:D_MODEL * (i + 1)])


def _merge_ln1(x, gated_sum, wo_ref, g_ref, b_ref, alpha):
    mix = _dot(_bf(gated_sum), wo_ref[...])
    return _layer_norm(alpha * x + mix, g_ref[...], b_ref[...])


def _forget_log2(xb, w_ga_ref, w2_ref, ba_ref):
    ga = _dot(xb, w_ga_ref[...])
    return _log_sigmoid(_dot(_bf(ga), w2_ref[...]) + ba_ref[...]) * (LOG2E / GLA_TAU)


def _prompt_mixer_kernel(sinks_ref, x_ref, rc_ref, rdn_ref, rup_ref, mk_ref, mv_ref, w_ina_ref, w_ga_ref,
                         w_inb_ref, w2_ref, ba_ref, gng_ref, wbg_ref, wbs_ref, wbm_ref, bgate_ref, wo_ref,
                         ln_g_ref, ln_b_ref,
                         out_ref, gla_out_ref, swak_out_ref, swav_out_ref,
                         s_scr, kk_scr, vv_scr, bias_scr, *, rows, alpha):
    t = pl.program_id(1)
    nchunk = rows // CHUNK

    @pl.when(t == 0)
    def _():
        s_scr[...] = jnp.zeros_like(s_scr)
        kk_scr[:, 0:WINDOW, :] = jnp.zeros((SWA_KV_HEADS, WINDOW, LANES), _BF)
        vv_scr[0:WINDOW, :] = jnp.zeros((WINDOW, LANES), _BF)
        bias_scr[0] = _swa_bias_t(rows // SWA_PAIR, first_tile=False)
        bias_scr[1] = _swa_bias_t(rows // SWA_PAIR, first_tile=True)

    x = x_ref[0]
    xb = _bf(x)

    log_a = _forget_log2(xb, w_ga_ref, w2_ref, ba_ref)

    def proj_a(off):
        return _dot(xb, w_ina_ref[:, off:off + GLA_W])

    def state_in(c, h):
        return s_scr[h]

    def state_out(c, h, st):
        s_scr[h] = st
        gla_out_ref[0, h] = jnp.transpose(st)

    gq, gk, gv = proj_a(OFF_GQ), proj_a(OFF_GK), proj_a(OFF_GV)
    gate_a = _gate(xb, 0, w_inb_ref, bgate_ref)
    late = {}

    def dense_a():
        late["gr"] = proj_a(OFF_GR)
        late["sq"] = _dot(xb, w_inb_ref[:, OFF_SQ:OFF_SQ + SWA_QW])
        late["sk"] = _dot(xb, w_inb_ref[:, OFF_SK:OFF_SK + SWA_KW])
        late["sv"] = _dot(xb, w_inb_ref[:, OFF_SV:OFF_SV + SWA_KW])

    def dense_b():
        late["gate_b"] = _gate(xb, 1, w_inb_ref, bgate_ref)

    def dense_c():
        late["gate_c"] = _gate(xb, 2, w_inb_ref, bgate_ref)

    o_a = _gla_tile(gq, gk, gv, log_a, CHUNK, nchunk, state_in, state_out, True, dense_a)
    o_a = _gla_gate(o_a, late["gr"], gng_ref[...])
    gated = gate_a * _dot(_bf(o_a), wbg_ref[...])

    rc, rdn, rup = rc_ref[...], rdn_ref[...], rup_ref[...]
    q = _rope(late["sq"], rc, rdn, rup) * (SWA_HD ** -0.5 * LOG2E)
    k = _rope(late["sk"], rc, rdn, rup)
    v = late["sv"]
    swak_out_ref[0] = k[rows - WINDOW:rows, :]
    swav_out_ref[0] = v[rows - WINDOW:rows, :]
    kk = _swa_dup_k(k)
    vb = _bf(v)
    for g in range(SWA_KV_HEADS):
        kk_scr[g, WINDOW:WINDOW + rows, :] = kk[g]
    vv_scr[WINDOW:WINDOW + rows, :] = vb
    o_bt = _swa_attention_t(_swa_split_q(q), kk_scr, vv_scr, sinks_ref, bias_scr[jnp.where(t == 0, 1, 0)], rows,
                            dense_b)
    gated = gated + late["gate_b"] * _dot_tn(_bf(o_bt), wbs_ref[...])
    for g in range(SWA_KV_HEADS):
        kk_scr[g, 0:WINDOW, :] = kk[g][rows - WINDOW:rows]
    vv_scr[0:WINDOW, :] = vb[rows - WINDOW:rows]

    qm = _dot(xb, w_inb_ref[:, OFF_MQ:OFF_MQ + MEM_W]) * (MEM_HD ** -0.5 * LOG2E)
    o_ct = _mem_attention_t(qm, mk_ref.at[0], mv_ref.at[0], dense_c)
    gated = gated + late["gate_c"] * _dot_tn(_bf(o_ct), wbm_ref[...])

    out_ref[0] = _merge_ln1(x, gated, wo_ref, ln_g_ref, ln_b_ref, alpha)


def _sample_mixer_kernel(sinks_ref, x_ref, rc_ref, rdn_ref, rup_ref, mk_ref, mv_ref, gla_in_ref,
                         swak_in_ref, swav_in_ref, w_ina_ref, w_ga_ref, w_inb_ref, w2_ref,
                         ba_ref, gng_ref, wbg_ref, wbs_ref, wbm_ref, bgate_ref, wo_ref, ln_g_ref, ln_b_ref,
                         out_ref, gla_out_ref, swak_out_ref, swav_out_ref, *, nb, tq, alpha):
    rows = nb * tq
    x = x_ref[...].reshape(rows, D_MODEL)
    xb = _bf(x)

    log_a = _forget_log2(xb, w_ga_ref, w2_ref, ba_ref)

    def proj_a(off):
        return _dot(xb, w_ina_ref[:, off:off + GLA_W])

    def state_in(c, h):
        return jnp.transpose(gla_in_ref[c, h])

    def state_out(c, h, st):
        gla_out_ref[c, h] = jnp.transpose(st)

    gq, gk, gv = proj_a(OFF_GQ), proj_a(OFF_GK), proj_a(OFF_GV)
    gate_a = _gate(xb, 0, w_inb_ref, bgate_ref)
    o_a = _gla_tile(gq, gk, gv, log_a, tq, nb, state_in, state_out, chained=False)
    o_a = _gla_gate(o_a, proj_a(OFF_GR), gng_ref[...])
    gated = gate_a * _dot(_bf(o_a), wbg_ref[...])

    rc = jnp.concatenate([rc_ref[...]] * nb, axis=0)
    rdn = jnp.concatenate([rdn_ref[...]] * nb, axis=0)
    rup = jnp.concatenate([rup_ref[...]] * nb, axis=0)
    q = _rope(_dot(xb, w_inb_ref[:, OFF_SQ:OFF_SQ + SWA_QW]), rc, rdn, rup) * (SWA_HD ** -0.5)
    k = _rope(_dot(xb, w_inb_ref[:, OFF_SK:OFF_SK + SWA_KW]), rc, rdn, rup)
    v = _dot(xb, w_inb_ref[:, OFF_SV:OFF_SV + SWA_KW])
    gate_b = _gate(xb, 1, w_inb_ref, bgate_ref)
    qs = _swa_split_q(q)
    slabs = [[None] * nb for _ in range(SWA_QW // LANES)]
    kks, vvs = [], []
    for bi in range(nb):
        rs = slice(tq * bi, tq * (bi + 1))
        k_all = jnp.concatenate([swak_in_ref[bi], k[rs]], axis=0)
        v_all = jnp.concatenate([swav_in_ref[bi], v[rs]], axis=0)
        swak_out_ref[bi] = k_all[tq:tq + WINDOW]
        swav_out_ref[bi] = v_all[tq:tq + WINDOW]
        kks.append(_swa_dup_k(k_all))
        vvs.append(_swa_place_v(v_all))
    for g in range(SWA_KV_HEADS):
        qrows = []
        for bi in range(nb):
            rs = slice(tq * bi, tq * (bi + 1))
            qrows.append(jnp.concatenate([qs[2 * g][0][rs], qs[2 * g][1][rs],
                                          qs[2 * g + 1][0][rs], qs[2 * g + 1][1][rs]], axis=0))
        outs = _swa_blocks(qrows, [kk[g] for kk in kks], [vv[g][0] for vv in vvs], [vv[g][1] for vv in vvs],
                           _sink_col(sinks_ref, g, tq), None, None, tq)
        for bi in range(nb):
            slabs[2 * g][bi], slabs[2 * g + 1][bi] = outs[bi]
    o_b = jnp.concatenate([jnp.concatenate(s, axis=0) if nb > 1 else s[0] for s in slabs], axis=1)
    gated = gated + gate_b * _dot(_bf(o_b), wbs_ref[...])

    qm = _dot(xb, w_inb_ref[:, OFF_MQ:OFF_MQ + MEM_W]) * (MEM_HD ** -0.5)
    gate_c = _gate(xb, 2, w_inb_ref, bgate_ref)
    o_c = _mem_attention(qm, [mk_ref.at[bi] for bi in range(nb)], [mv_ref.at[bi] for bi in range(nb)])
    gated = gated + gate_c * _dot(_bf(o_c), wbm_ref[...])

    out = _merge_ln1(x, gated, wo_ref, ln_g_ref, ln_b_ref, alpha)
    out_ref[...] = out.reshape(nb, tq, D_MODEL)


_GELU_K1 = -2.0 * float(np.log2(np.e)) * float(np.sqrt(2.0 / np.pi))
_GELU_K2 = _GELU_K1 * 0.044715


def _conv(u2, u1, u0, cw_ref, cb_ref, cs):
    return cb_ref[:, cs] + u2 * cw_ref[0:1, cs] + u1 * cw_ref[1:2, cs] + u0 * cw_ref[2:3, cs]


def _geglu(g, v):
    return g * v / (1.0 + jnp.exp2(g * (g * g * _GELU_K2 + _GELU_K1)))


def _prompt_ffn_kernel(x_ref, wup_ref, cw_ref, cb_ref, wdn_ref, ln_g_ref, ln_b_ref,
                       out_ref, conv_out_ref, u_scr, *, rows, alpha):
    nj = rows // SUBLANES
    base = 2 * SUBLANES
    last = slice(base + rows - SUBLANES, base + rows)
    last2 = slice(base + rows - 2 * SUBLANES, base + rows - SUBLANES)

    @pl.when(pl.program_id(1) == 0)
    def _():
        u_scr[base + rows - 2 * SUBLANES:base + rows, :] = jnp.zeros((2 * SUBLANES, 2 * D_FF), _F32)

    prev_last, prev_last2 = u_scr[last, :], u_scr[last2, :]
    x = jnp.swapaxes(x_ref[0].reshape(SUBLANES, nj, D_MODEL), 0, 1).reshape(rows, D_MODEL)
    u_scr[base:base + rows, :] = _dot(_bf(x), wup_ref[...])
    sub = lax.broadcasted_iota(jnp.int32, (SUBLANES, FFN_COLS), 0)

    def conv_cols(cs):
        wrap = lambda prev, cur: pltpu.roll(jnp.where(sub == SUBLANES - 1, prev, cur), 1, axis=0)
        u_scr[SUBLANES:base, cs] = wrap(prev_last[:, cs], u_scr[last, cs])
        u_scr[0:SUBLANES, cs] = wrap(prev_last2[:, cs], u_scr[last2, cs])
        return _conv(u_scr[0:rows, cs], u_scr[SUBLANES:SUBLANES + rows, cs], u_scr[base:base + rows, cs],
                     cw_ref, cb_ref, cs)

    h = [_bf(_geglu(conv_cols(slice(c0, c0 + FFN_COLS)), conv_cols(slice(D_FF + c0, D_FF + c0 + FFN_COLS))))
         for c0 in range(0, D_FF, FFN_COLS)]
    conv_out_ref[0] = jnp.concatenate([u_scr[base + rows - SUBLANES - 1:base + rows - SUBLANES, :],
                                       u_scr[base + rows - 1:base + rows, :]], axis=0)
    f = _dot(jnp.concatenate(h, axis=1), wdn_ref[...])
    y = _layer_norm(alpha * x + f, ln_g_ref[...], ln_b_ref[...])
    out_ref[0] = jnp.swapaxes(y.reshape(nj, SUBLANES, D_MODEL), 0, 1).reshape(rows, D_MODEL)


def _sample_ffn_kernel(x_ref, hist_ref, wup_ref, cw_ref, cb_ref, wdn_ref, ln_g_ref, ln_b_ref,
                       out_ref, conv_out_ref, u_scr, *, nb, tq, alpha):
    rows = nb * tq
    base = (CONV_W - 1) * nb
    x = jnp.swapaxes(x_ref[...], 0, 1).reshape(rows, D_MODEL)
    u_scr[0:base, :] = jnp.swapaxes(hist_ref[...], 0, 1).reshape(base, 2 * D_FF)
    u_scr[base:base + rows, :] = _dot(_bf(x), wup_ref[...])
    conv = lambda cs: _conv(u_scr[0:rows, cs], u_scr[nb:nb + rows, cs], u_scr[base:base + rows, cs],
                            cw_ref, cb_ref, cs)
    h = _geglu(conv(slice(0, D_FF)), conv(slice(D_FF, 2 * D_FF)))
    conv_out_ref[...] = jnp.swapaxes(u_scr[rows:rows + base, :].reshape(CONV_W - 1, nb, 2 * D_FF), 0, 1)
    f = _dot(_bf(h), wdn_ref[...])
    y = _layer_norm(alpha * x + f, ln_g_ref[...], ln_b_ref[...])
    out_ref[...] = jnp.swapaxes(y.reshape(tq, nb, D_MODEL), 0, 1)


def _mem_kv_kernel(m_ref, w_ref, k_ref, v_ref):
    kv = _dot(_bf(m_ref[0]), w_ref[...])
    for h in range(MEM_HEADS):
        k_ref[0, pl.ds(h, MEM_TOKENS, stride=MEM_HEADS), :] = kv[:, MEM_HD * h:MEM_HD * (h + 1)]
        v_ref[0, pl.ds(h, MEM_TOKENS, stride=MEM_HEADS), :] = kv[:, MEM_W + MEM_HD * h:MEM_W + MEM_HD * (h + 1)]


def _const_spec(shape):
    nd = len(shape)
    return pl.BlockSpec(shape, lambda *_: (0,) * nd, pipeline_mode=pl.Buffered(1))


def _rope_tables(pos):
    half = ROPE_DIM // 2
    inv = np.float32(ROPE_THETA) ** (-np.arange(half, dtype=np.float32) / np.float32(half))
    ang = pos.astype(np.float32)[:, None] * inv[None, :]
    cos, sin = np.cos(ang), np.sin(ang)
    T = pos.shape[0]
    ones = np.ones((T, SWA_HD - ROPE_DIM), np.float32)
    zeros = np.zeros((T, SWA_HD - ROPE_DIM), np.float32)
    zh = np.zeros((T, half), np.float32)
    c = np.concatenate([cos, cos, ones], axis=1)
    dn = np.concatenate([-sin, zh, zeros], axis=1)
    up = np.concatenate([zh, sin, zeros], axis=1)
    rep = LANES // SWA_HD
    return tuple(jnp.asarray(np.concatenate([a] * rep, axis=1), dtype=_F32) for a in (c, dn, up))


def _pack_cols_kernel(wt_ref, o_ref, *, keep):
    t = jnp.transpose(wt_ref[...])
    if keep < t.shape[1]:
        t = jnp.where(lax.broadcasted_iota(jnp.int32, t.shape, 1) < keep, t, 0.0)
    o_ref[...] = _bf(t)


def _pack_cols(wt, col0, ncols, keep, name):
    blk = min(ncols, 2 * LANES)
    k = wt.shape[1]
    return pl.pallas_call(
        functools.partial(_pack_cols_kernel, keep=keep),
        grid=(ncols // blk,),
        in_specs=[pl.BlockSpec((pl.Element(blk), pl.Element(k)),
                               lambda i: (pl.multiple_of(col0 + blk * i, SUBLANES), 0))],
        out_specs=pl.BlockSpec((k, blk), lambda i: (0, i)),
        out_shape=jax.ShapeDtypeStruct((k, ncols), _BF),
        compiler_params=_params(("arbitrary",)),
        name=name,
    )(wt)


def _split_w_in(w_in):
    wt = jnp.transpose(w_in)
    return (_pack_cols(wt, 0, IN_A_COLS, IN_A_COLS, "pack_w_in_a"),
            _pack_cols(wt, IN_A_COLS, LANES, GLA_RANK, "pack_w_in_ga"),
            _pack_cols(wt, IN_A_COLS + GLA_RANK, IN_B_COLS, IN_B_COLS, "pack_w_in_b"))


def _mixer_weight_specs():
    return [
        _const_spec((D_MODEL, IN_A_COLS)),
        _const_spec((D_MODEL, LANES)),
        _const_spec((D_MODEL, IN_B_COLS)),
        _const_spec((LANES, GLA_W)),
        _const_spec((1, GLA_W)),
        _const_spec((1, GLA_DV)),
        _const_spec((GLA_W, D_MODEL)),
        _const_spec((SWA_QW, D_MODEL)),
        _const_spec((MEM_W, D_MODEL)),
        _const_spec((1, GATE_W)),
        _const_spec((D_MODEL, D_MODEL)),
        _const_spec((1, D_MODEL)),
        _const_spec((1, D_MODEL)),
    ]


def _ffn_weight_specs():
    return [
        _const_spec((D_MODEL, 2 * D_FF)),
        _const_spec((CONV_W, 2 * D_FF)),
        _const_spec((1, 2 * D_FF)),
        _const_spec((D_FF, D_MODEL)),
        _const_spec((1, D_MODEL)),
        _const_spec((1, D_MODEL)),
    ]


_SMEM_SPEC = pl.BlockSpec(memory_space=pltpu.SMEM)
_MEM_ROWS = MEM_TOKENS * MEM_HEADS


def _params(sem, flags=None):
    return pltpu.CompilerParams(dimension_semantics=sem, vmem_limit_bytes=V7X_VMEM_LIMIT, flags=flags)


def _prompt_layer(x, mem, sinks, mixer_w, ffn_w, w_mem_kv, alpha):
    B, T, _ = x.shape
    rows = min(PROMPT_ROWS, T)
    nt = T // rows
    assert T % rows == 0 and rows % SWA_PAIR == 0 and rows >= WINDOW
    f32 = jnp.float32
    mk, mv = pl.pallas_call(
        _mem_kv_kernel,
        grid=(B,),
        in_specs=[pl.BlockSpec((1, MEM_TOKENS, D_MODEL), lambda b: (b, 0, 0)),
                  _const_spec((D_MODEL, 2 * MEM_W))],
        out_specs=[pl.BlockSpec((1, _MEM_ROWS, MEM_HD), lambda b: (b, 0, 0))] * 2,
        out_shape=[jax.ShapeDtypeStruct((B, _MEM_ROWS, MEM_HD), f32)] * 2,
        compiler_params=_params(("arbitrary",)),
        name="mem_kv",
    )(mem, w_mem_kv)

    rope = _rope_tables(np.arange(T))
    rope_spec = pl.BlockSpec((rows, LANES), lambda b, t: (t, 0))
    per_batch = lambda *blk: pl.BlockSpec((1,) + blk, lambda b, t: (b,) + (0,) * len(blk))
    x1, gla, swak, swav = pl.pallas_call(
        functools.partial(_prompt_mixer_kernel, rows=rows, alpha=alpha),
        grid=(B, nt),
        in_specs=[_SMEM_SPEC,
                  pl.BlockSpec((1, rows, D_MODEL), lambda b, t: (b, t, 0)),
                  rope_spec, rope_spec, rope_spec,
                  per_batch(_MEM_ROWS, MEM_HD), per_batch(_MEM_ROWS, MEM_HD)] + _mixer_weight_specs(),
        out_specs=[pl.BlockSpec((1, rows, D_MODEL), lambda b, t: (b, t, 0)),
                   per_batch(GLA_HEADS, GLA_DK, GLA_DV),
                   per_batch(WINDOW, SWA_KW), per_batch(WINDOW, SWA_KW)],
        out_shape=[jax.ShapeDtypeStruct((B, T, D_MODEL), f32),
                   jax.ShapeDtypeStruct((B, GLA_HEADS, GLA_DK, GLA_DV), f32),
                   jax.ShapeDtypeStruct((B, WINDOW, SWA_KW), f32),
                   jax.ShapeDtypeStruct((B, WINDOW, SWA_KW), f32)],
        scratch_shapes=[pltpu.VMEM((GLA_HEADS, GLA_DK, GLA_DV), f32),
                        pltpu.VMEM((SWA_KV_HEADS, WINDOW + rows, LANES), _BF),
                        pltpu.VMEM((WINDOW + rows, LANES), _BF),
                        pltpu.VMEM((2, WINDOW + SWA_PAIR, SWA_BLOCK_LANES * SWA_KV_HEADS * (rows // SWA_PAIR)),
                                   f32)],
        compiler_params=_params(("arbitrary", "arbitrary")),
        name="prompt_mixer",
    )(sinks, x, *rope, mk, mv, *mixer_w)

    frows = min(PROMPT_FFN_ROWS, T)
    assert T % frows == 0 and frows >= CONV_W - 1
    y, conv = pl.pallas_call(
        functools.partial(_prompt_ffn_kernel, rows=frows, alpha=alpha),
        grid=(B, T // frows),
        in_specs=[pl.BlockSpec((1, frows, D_MODEL), lambda b, t: (b, t, 0))] + _ffn_weight_specs(),
        out_specs=[pl.BlockSpec((1, frows, D_MODEL), lambda b, t: (b, t, 0)),
                   per_batch(CONV_W - 1, 2 * D_FF)],
        out_shape=[jax.ShapeDtypeStruct((B, T, D_MODEL), f32),
                   jax.ShapeDtypeStruct((B, CONV_W - 1, 2 * D_FF), f32)],
        scratch_shapes=[pltpu.VMEM((2 * SUBLANES + frows, 2 * D_FF), f32)],
        compiler_params=_params(("arbitrary", "arbitrary")),
        name="prompt_ffn",
    )(x1, *ffn_w)
    return y, gla, swak, swav, mk, mv, conv


def _sample_layer(x, gla0, swak0, swav0, memk, memv, conv0, sinks, mixer_w, ffn_w, alpha):
    B, tq, _ = x.shape
    nb = max(1, min(B, SAMPLE_ROWS // tq))
    assert B % nb == 0 and tq % 16 == 0 and tq >= CONV_W - 1
    f32 = jnp.float32
    rope = _rope_tables(PAST_LEN + np.arange(tq))
    rope_spec = pl.BlockSpec((tq, LANES), lambda i: (0, 0))
    blk = lambda *s: pl.BlockSpec((nb,) + s, lambda i: (i,) + (0,) * len(s))
    blk1 = lambda *s: pl.BlockSpec((nb,) + s, lambda i: (i,) + (0,) * len(s), pipeline_mode=pl.Buffered(1))
    x1, gla, swak, swav = pl.pallas_call(
        functools.partial(_sample_mixer_kernel, nb=nb, tq=tq, alpha=alpha),
        grid=(B // nb,),
        in_specs=[_SMEM_SPEC, blk(tq, D_MODEL), rope_spec, rope_spec, rope_spec,
                  blk(_MEM_ROWS, MEM_HD), blk(_MEM_ROWS, MEM_HD),
                  blk1(GLA_HEADS, GLA_DK, GLA_DV), blk1(WINDOW, SWA_KW), blk1(WINDOW, SWA_KW)]
                 + _mixer_weight_specs(),
        out_specs=[blk(tq, D_MODEL), blk(GLA_HEADS, GLA_DK, GLA_DV), blk(WINDOW, SWA_KW), blk(WINDOW, SWA_KW)],
        out_shape=[jax.ShapeDtypeStruct((B, tq, D_MODEL), f32),
                   jax.ShapeDtypeStruct((B, GLA_HEADS, GLA_DK, GLA_DV), f32),
                   jax.ShapeDtypeStruct((B, WINDOW, SWA_KW), f32),
                   jax.ShapeDtypeStruct((B, WINDOW, SWA_KW), f32)],
        compiler_params=_params(("arbitrary",)),
        name="sample_mixer",
    )(sinks, x, *rope, memk, memv, gla0, swak0, swav0, *mixer_w)

    assert B % SUBLANES == 0
    fblk = lambda *s: pl.BlockSpec((SUBLANES,) + s, lambda i: (i,) + (0,) * len(s))
    y, conv = pl.pallas_call(
        functools.partial(_sample_ffn_kernel, nb=SUBLANES, tq=tq, alpha=alpha),
        grid=(B // SUBLANES,),
        in_specs=[fblk(tq, D_MODEL), fblk(CONV_W - 1, 2 * D_FF)] + _ffn_weight_specs(),
        out_specs=[fblk(tq, D_MODEL), fblk(CONV_W - 1, 2 * D_FF)],
        out_shape=[jax.ShapeDtypeStruct((B, tq, D_MODEL), f32),
                   jax.ShapeDtypeStruct((B, CONV_W - 1, 2 * D_FF), f32)],
        scratch_shapes=[pltpu.VMEM(((CONV_W - 1 + tq) * SUBLANES, 2 * D_FF), f32)],
        compiler_params=_params(("arbitrary",)),
        name="sample_ffn",
    )(x1, conv0, *ffn_w)
    return y, gla, swak, swav, conv


def kernel(x_prompt, x_sample, cache_swa_k, cache_swa_v, state_gla, cache_mem_k, cache_mem_v, cache_ffn_conv, mem_prompt, ln1_g, ln1_b, ln2_g, ln2_b, w_in, b_gate, w_gla_a2, b_gla_a, gla_norm_g, swa_sinks, w_mem_kv, w_br_gla, w_br_swa, w_br_mem, w_o, w_up, conv_w, conv_b, w_down):
    depth = w_in.shape[0]
    alpha = float((2 * depth) ** 0.25)
    Bp = x_prompt.shape[0]
    Bs = x_sample.shape[0]
    hp, hs = x_prompt, x_sample
    outs = [[] for _ in range(10)]
    row = lambda a: a.reshape(1, -1)
    for l in range(depth):
        w2 = jnp.pad(w_gla_a2[l], ((0, LANES - GLA_RANK), (0, 0)))
        mixer_w = _split_w_in(w_in[l]) + (
            _bf(w2), row(b_gla_a[l]), row(gla_norm_g[l]),
            _bf(w_br_gla[l]), _bf(w_br_swa[l]), _bf(w_br_mem[l]), row(b_gate[l]), _bf(w_o[l]),
            row(ln1_g[l]), row(ln1_b[l]))
        ffn_w = (_bf(w_up[l]), conv_w[l], row(conv_b[l]), _bf(w_down[l]), row(ln2_g[l]), row(ln2_b[l]))
        sinks = swa_sinks[l]
        hp, g_p, k_p, v_p, mk_p, mv_p, c_p = _prompt_layer(
            hp, mem_prompt, sinks, mixer_w, ffn_w, _bf(w_mem_kv[l]), alpha)
        hs, g_s, k_s, v_s, c_s = _sample_layer(
            hs, state_gla[l],
            cache_swa_k[l].reshape(Bs, WINDOW, SWA_KW), cache_swa_v[l].reshape(Bs, WINDOW, SWA_KW),
            cache_mem_k[l].reshape(Bs, _MEM_ROWS, MEM_HD), cache_mem_v[l].reshape(Bs, _MEM_ROWS, MEM_HD),
            cache_ffn_conv[l], sinks, mixer_w, ffn_w, alpha)
        kv5 = lambda a, b: a.reshape(b, WINDOW, SWA_KV_HEADS, SWA_HD)
        m5 = lambda a: a.reshape(Bp, MEM_TOKENS, MEM_HEADS, MEM_HD)
        for lst, val in zip(outs, (kv5(k_p, Bp), kv5(v_p, Bp), g_p, m5(mk_p), m5(mv_p), c_p,
                                   kv5(k_s, Bs), kv5(v_s, Bs), g_s, c_s)):
            lst.append(val)
    return (hp, hs) + tuple(jnp.stack(o) for o in outs)
```

```python
import functools

import jax
import jax.numpy as jnp
import numpy as np
from jax import lax
from jax.experimental import pallas as pl
from jax.experimental.pallas import tpu as pltpu

D_MODEL = 1024
CHUNK = 64
GLA_HEADS = 4
GLA_DK = 128
GLA_DV = 128
GLA_RANK = 16
GLA_TAU = 16.0
SWA_HEADS = 8
SWA_KV_HEADS = 2
SWA_HD = 64
WINDOW = 128
ROPE_DIM = 16
ROPE_THETA = 500000.0
MEM_TOKENS = 256
MEM_HEADS = 4
MEM_HD = 128
D_FF = 2816
CONV_W = 3
N_BRANCH = 3
PAST_LEN = 2048
LN_EPS = 1e-5

LANES = 128
SUBLANES = 8
V7X_VMEM_LIMIT = 56 * 1024 * 1024
PROMPT_ROWS = 256
PROMPT_FFN_ROWS = 512
FFN_COLS = 256
SAMPLE_ROWS = 256
GLA_W = GLA_HEADS * GLA_DK
SWA_QW = SWA_HEADS * SWA_HD
SWA_KW = SWA_KV_HEADS * SWA_HD
MEM_W = MEM_HEADS * MEM_HD
GATE_W = N_BRANCH * D_MODEL
SWA_PAIR = 2 * CHUNK

OFF_GQ = 0
OFF_GK = OFF_GQ + GLA_W
OFF_GV = OFF_GK + GLA_W
OFF_GR = OFF_GV + GLA_W
IN_A_COLS = OFF_GR + GLA_W
OFF_SQ = 0
OFF_SK = OFF_SQ + SWA_QW
OFF_SV = OFF_SK + SWA_KW
OFF_MQ = OFF_SV + SWA_KW
OFF_GL = OFF_MQ + MEM_W
IN_B_COLS = OFF_GL + GATE_W

_BF = jnp.bfloat16
_F32 = jnp.float32


def _bf(x):
    return x.astype(_BF)


def _dot(a, b):
    return jnp.dot(a, b, preferred_element_type=_F32)


def _dot_nt(a, b):
    return lax.dot_general(a, b, (((1,), (1,)), ((), ())), preferred_element_type=_F32)


def _dot_tn(a, b):
    return lax.dot_general(a, b, (((0,), (0,)), ((), ())), preferred_element_type=_F32)


def _layer_norm(h, g, b):
    mu = jnp.mean(h, axis=-1, keepdims=True)
    d = h - mu
    var = jnp.mean(d * d, axis=-1, keepdims=True)
    return d * lax.rsqrt(var + LN_EPS) * g + b


def _log_sigmoid(x):
    return -(jnp.maximum(-x, 0.0) + jnp.log(1.0 + jnp.exp(-jnp.abs(x))))


def _rope(a, c, s_dn, s_up):
    outs = []
    for j in range(a.shape[1] // LANES):
        slab = a[:, LANES * j:LANES * (j + 1)]
        fwd = pltpu.roll(slab, LANES - ROPE_DIM // 2, axis=1)
        bwd = pltpu.roll(slab, ROPE_DIM // 2, axis=1)
        outs.append(slab * c + fwd * s_dn + bwd * s_up)
    return outs[0] if len(outs) == 1 else jnp.concatenate(outs, axis=1)


def _gla_tile(q, k, v, log2_a, chunk, nchunk, state_in, state_out, chained, between=lambda: None):
    R = chunk * nchunk
    row = lax.broadcasted_iota(jnp.int32, (R, R), 0)
    col = lax.broadcasted_iota(jnp.int32, (R, R), 1)
    tril = (row // chunk == col // chunk) & (col <= row)
    ltri = jnp.where(tril, 1.0, 0.0).astype(_BF)
    hi = _bf(log2_a)
    lo = _bf(log2_a - hi.astype(_F32))
    b = _dot(ltri, hi) + _dot(ltri, lo)
    between()
    decay = [jnp.exp2(b[chunk * (c + 1) - 1:chunk * (c + 1), :]) for c in range(nchunk)]
    q_dec = _bf(q * (GLA_DK ** -0.5) * jnp.exp2(b))
    k_inv = k * jnp.exp2(-b)
    k_tail = _bf(jnp.concatenate([k_inv[chunk * c:chunk * (c + 1)] * decay[c] for c in range(nchunk)], axis=0)
                 if nchunk > 1 else k_inv * decay[0])
    k_inv = _bf(k_inv)
    vb = _bf(v)
    heads = [slice(GLA_DK * h, GLA_DK * (h + 1)) for h in range(GLA_HEADS)]
    o_intra = []
    for hs in heads:
        att = jnp.where(tril, _dot_nt(q_dec[:, hs], k_inv[:, hs]), 0.0)
        o_intra.append(_dot(_bf(att), vb[:, hs]))
    o_inter = [[None] * nchunk for _ in heads]
    st = [None] * GLA_HEADS
    for c in range(nchunk):
        rs = slice(chunk * c, chunk * (c + 1))
        for h, hs in enumerate(heads):
            if c == 0 or not chained:
                st[h] = state_in(c, h)
            o_inter[h][c] = _dot_nt(q_dec[rs, hs], _bf(st[h]))
            st[h] = st[h] * decay[c][:, hs] + _dot_tn(vb[rs, hs], k_tail[rs, hs])
            if c == nchunk - 1 or not chained:
                state_out(c, h, st[h])
    outs = [o_intra[h] + (jnp.concatenate(o_inter[h], axis=0) if nchunk > 1 else o_inter[h][0])
            for h in range(GLA_HEADS)]
    return jnp.concatenate(outs, axis=1)


def _gla_gate(o, gr, g):
    outs = []
    for h in range(GLA_HEADS):
        hs = slice(GLA_DV * h, GLA_DV * (h + 1))
        oh = o[:, hs]
        ms = jnp.mean(oh * oh, axis=-1, keepdims=True)
        grh = gr[:, hs]
        outs.append(oh * lax.rsqrt(ms + LN_EPS) * g * (grh * jax.nn.sigmoid(grh)))
    return jnp.concatenate(outs, axis=1)


def _lane_half_masks(shape):
    lane = lax.broadcasted_iota(jnp.int32, shape, 1)
    return lane < SWA_HD, lane >= SWA_HD


def _swa_split_q(q):
    lo_m, hi_m = _lane_half_masks((q.shape[0], LANES))
    out = []
    for j in range(SWA_QW // LANES):
        slab = q[:, LANES * j:LANES * (j + 1)]
        out.append((_bf(jnp.where(lo_m, slab, 0.0)), _bf(jnp.where(hi_m, slab, 0.0))))
    return out


def _swa_dup_k(k):
    lo_m, _ = _lane_half_masks(k.shape)
    kr = pltpu.roll(k, SWA_HD, axis=1)
    return [_bf(jnp.where(lo_m, k, kr)), _bf(jnp.where(lo_m, kr, k))]


def _swa_place_v(v):
    lo_m, hi_m = _lane_half_masks(v.shape)
    vr = pltpu.roll(v, SWA_HD, axis=1)
    return [(_bf(jnp.where(lo_m, v, 0.0)), _bf(jnp.where(hi_m, vr, 0.0))),
            (_bf(jnp.where(lo_m, vr, 0.0)), _bf(jnp.where(hi_m, v, 0.0)))]


def _swa_blocks(qrows, kks, vlos, vhis, sink_col, lowers, upper, tq):
    n = len(kks)
    stack = lambda xs: jnp.concatenate(xs, axis=0) if n > 1 else xs[0]
    s = stack([_dot_nt(qrows[i], kks[i]) for i in range(n)])
    if lowers is not None:
        kcol = lax.broadcasted_iota(jnp.int32, s.shape, 1)
        s = jnp.where(kcol >= stack(lowers), jnp.where(kcol <= stack([upper] * n), s, -jnp.inf), -jnp.inf)
    sink = stack([sink_col] * n)
    m = jnp.maximum(jnp.max(s, axis=-1, keepdims=True), sink)
    p = jnp.exp(s - m)
    den = jnp.sum(p, axis=-1, keepdims=True) + jnp.exp(sink - m)
    p = _bf(p / den)
    outs = []
    for i in range(n):
        r = 4 * tq * i
        outs.append((_dot(p[r:r + tq], vlos[i]) + _dot(p[r + tq:r + 2 * tq], vhis[i]),
                     _dot(p[r + 2 * tq:r + 3 * tq], vlos[i]) + _dot(p[r + 3 * tq:r + 4 * tq], vhis[i])))
    return outs


def _sink_col(sinks_ref, g, tq):
    r = lax.broadcasted_iota(jnp.int32, (4 * tq, 1), 0)
    base = 4 * g
    return jnp.where(r < tq, sinks_ref[base],
                     jnp.where(r < 2 * tq, sinks_ref[base + 1],
                               jnp.where(r < 3 * tq, sinks_ref[base + 2], sinks_ref[base + 3])))


def _mem_head(ref, h):
    return _bf(ref[pl.ds(h, MEM_TOKENS, stride=MEM_HEADS), :])


def _mem_attention(q, mk_refs, mv_refs):
    qb = _bf(q)
    nseg = len(mk_refs)
    tq = q.shape[0] // nseg
    outs = []
    for h in range(MEM_HEADS):
        hs = slice(MEM_HD * h, MEM_HD * (h + 1))
        s = [_dot_nt(qb[tq * i:tq * (i + 1), hs], _mem_head(mk_refs[i], h)) for i in range(nseg)]
        s = jnp.concatenate(s, axis=0) if nseg > 1 else s[0]
        m = jnp.max(s, axis=-1, keepdims=True)
        p = jnp.exp(s - m)
        p = _bf(p / jnp.sum(p, axis=-1, keepdims=True))
        o = [_dot(p[tq * i:tq * (i + 1)], _mem_head(mv_refs[i], h)) for i in range(nseg)]
        outs.append(jnp.concatenate(o, axis=0) if nseg > 1 else o[0])
    return jnp.concatenate(outs, axis=1)


SWA_BLOCK_LANES = 4 * SWA_PAIR
LOG2E = float(np.log2(np.e))


def _swa_bias_t(npair, first_tile):
    nk = WINDOW + SWA_PAIR
    width = SWA_BLOCK_LANES * SWA_KV_HEADS * npair
    col = lax.broadcasted_iota(jnp.int32, (1, width), 1)
    lower = CHUNK * ((col % SWA_PAIR) // CHUNK)
    upper = lower + (WINDOW + CHUNK - 1)
    if first_tile:
        lower = jnp.maximum(lower, WINDOW - SWA_PAIR * ((col // SWA_BLOCK_LANES) % npair))
    krow = lax.broadcasted_iota(jnp.int32, (nk, width), 0)
    return jnp.where(krow >= lower, jnp.where(krow <= upper, 0.0, -jnp.inf), -jnp.inf)


def _swa_attention_t(qs, kk_scr, v_scr, sinks_ref, bias, rows, between=lambda: None):
    nk = WINDOW + SWA_PAIR
    npair = rows // SWA_PAIR
    bw = SWA_BLOCK_LANES
    blocks = [(g, p) for g in range(SWA_KV_HEADS) for p in range(npair)]
    sts = []
    for g, p in blocks:
        rs = slice(SWA_PAIR * p, SWA_PAIR * (p + 1))
        qrows = jnp.concatenate([qs[2 * g][0][rs], qs[2 * g][1][rs],
                                 qs[2 * g + 1][0][rs], qs[2 * g + 1][1][rs]], axis=0)
        sts.append(_dot_nt(kk_scr[g, SWA_PAIR * p:SWA_PAIR * p + nk, :], qrows))
    st = jnp.concatenate(sts, axis=1) + bias
    between()
    width = bw * len(blocks)
    col = lax.broadcasted_iota(jnp.int32, (1, width), 1)
    head = 4 * (col // (bw * npair)) + (col % bw) // SWA_PAIR
    sink = jnp.zeros((1, width), _F32)
    for h in range(SWA_HEADS):
        sink = jnp.where(head == h, sinks_ref[h] * LOG2E, sink)
    m = jnp.maximum(jnp.max(st, axis=0, keepdims=True), sink)
    pt = jnp.exp2(st - m)
    inv = 1.0 / (jnp.sum(pt, axis=0, keepdims=True) + jnp.exp2(sink - m))
    pb = _bf(pt)
    pieces = [[None] * npair for _ in range(SWA_HEADS)]
    for i, (g, p) in enumerate(blocks):
        cs = slice(bw * i, bw * (i + 1))
        o2 = _dot_tn(v_scr[SWA_PAIR * p:SWA_PAIR * p + nk, :], pb[:, cs]) * inv[:, cs]
        for hh in range(4):
            pieces[4 * g + hh][p] = o2[SWA_HD * g:SWA_HD * (g + 1), SWA_PAIR * hh:SWA_PAIR * (hh + 1)]
    return jnp.concatenate([jnp.concatenate(ps, axis=1) if npair > 1 else ps[0] for ps in pieces], axis=0)


def _mem_attention_t(q, mk_ref, mv_ref, between=lambda: None):
    qb = _bf(q)
    R = q.shape[0]
    st = jnp.concatenate([_dot_nt(_mem_head(mk_ref, h), qb[:, MEM_HD * h:MEM_HD * (h + 1)])
                          for h in range(MEM_HEADS)], axis=1)
    between()
    pt = jnp.exp2(st - jnp.max(st, axis=0, keepdims=True))
    inv = 1.0 / jnp.sum(pt, axis=0, keepdims=True)
    pb = _bf(pt)
    outs = [_dot_tn(_mem_head(mv_ref, h), pb[:, R * h:R * (h + 1)]) * inv[:, R * h:R * (h + 1)]
            for h in range(MEM_HEADS)]
    return jnp.concatenate(outs, axis=0)


def _gate(xb, i, w_inb_ref, bgate_ref):
    gl = _dot(xb, w_inb_ref[:, OFF_GL + D_MODEL * i:OFF_GL + D_MODEL * (i + 1)])
    return jax.nn.sigmoid(gl + bgate_ref[:, D_MODEL * i:D_MODEL * (i + 1)])


def _merge_ln1(x, gated_sum, wo_ref, g_ref, b_ref, alpha):
    mix = _dot(_bf(gated_sum), wo_ref[...])
    return _layer_norm(alpha * x + mix, g_ref[...], b_ref[...])


def _forget_log2(xb, w_ga_ref, w2_ref, ba_ref):
    ga = _dot(xb, w_ga_ref[...])
    return _log_sigmoid(_dot(_bf(ga), w2_ref[...]) + ba_ref[...]) * (LOG2E / GLA_TAU)


def _prompt_mixer_kernel(sinks_ref, x_ref, rc_ref, rdn_ref, rup_ref, mk_ref, mv_ref, w_ina_ref, w_ga_ref,
                         w_inb_ref, w2_ref, ba_ref, gng_ref, wbg_ref, wbs_ref, wbm_ref, bgate_ref, wo_ref,
                         ln_g_ref, ln_b_ref,
                         out_ref, gla_out_ref, swak_out_ref, swav_out_ref,
                         s_scr, kk_scr, vv_scr, bias_scr, *, rows, alpha):
    t = pl.program_id(1)
    nchunk = rows // CHUNK

    @pl.when(t == 0)
    def _():
        s_scr[...] = jnp.zeros_like(s_scr)
        kk_scr[:, 0:WINDOW, :] = jnp.zeros((SWA_KV_HEADS, WINDOW, LANES), _BF)
        vv_scr[0:WINDOW, :] = jnp.zeros((WINDOW, LANES), _BF)
        bias_scr[0] = _swa_bias_t(rows // SWA_PAIR, first_tile=False)
        bias_scr[1] = _swa_bias_t(rows // SWA_PAIR, first_tile=True)

    x = x_ref[0]
    xb = _bf(x)

    log_a = _forget_log2(xb, w_ga_ref, w2_ref, ba_ref)

    def proj_a(off):
        return _dot(xb, w_ina_ref[:, off:off + GLA_W])

    def state_in(c, h):
        return s_scr[h]

    def state_out(c, h, st):
        s_scr[h] = st
        gla_out_ref[0, h] = jnp.transpose(st)

    gq, gk, gv = proj_a(OFF_GQ), proj_a(OFF_GK), proj_a(OFF_GV)
    gate_a = _gate(xb, 0, w_inb_ref, bgate_ref)
    late = {}

    def dense_a():
        late["gr"] = proj_a(OFF_GR)
        late["sq"] = _dot(xb, w_inb_ref[:, OFF_SQ:OFF_SQ + SWA_QW])
        late["sk"] = _dot(xb, w_inb_ref[:, OFF_SK:OFF_SK + SWA_KW])
        late["sv"] = _dot(xb, w_inb_ref[:, OFF_SV:OFF_SV + SWA_KW])

    def dense_b():
        late["gate_b"] = _gate(xb, 1, w_inb_ref, bgate_ref)

    def dense_c():
        late["gate_c"] = _gate(xb, 2, w_inb_ref, bgate_ref)

    o_a = _gla_tile(gq, gk, gv, log_a, CHUNK, nchunk, state_in, state_out, True, dense_a)
    o_a = _gla_gate(o_a, late["gr"], gng_ref[...])
    gated = gate_a * _dot(_bf(o_a), wbg_ref[...])

    rc, rdn, rup = rc_ref[...], rdn_ref[...], rup_ref[...]
    q = _rope(late["sq"], rc, rdn, rup) * (SWA_HD ** -0.5 * LOG2E)
    k = _rope(late["sk"], rc, rdn, rup)
    v = late["sv"]
    swak_out_ref[0] = k[rows - WINDOW:rows, :]
    swav_out_ref[0] = v[rows - WINDOW:rows, :]
    kk = _swa_dup_k(k)
    vb = _bf(v)
    for g in range(SWA_KV_HEADS):
        kk_scr[g, WINDOW:WINDOW + rows, :] = kk[g]
    vv_scr[WINDOW:WINDOW + rows, :] = vb
    o_bt = _swa_attention_t(_swa_split_q(q), kk_scr, vv_scr, sinks_ref, bias_scr[jnp.where(t == 0, 1, 0)], rows,
                            dense_b)
    gated = gated + late["gate_b"] * _dot_tn(_bf(o_bt), wbs_ref[...])
    for g in range(SWA_KV_HEADS):
        kk_scr[g, 0:WINDOW, :] = kk[g][rows - WINDOW:rows]
    vv_scr[0:WINDOW, :] = vb[rows - WINDOW:rows]

    qm = _dot(xb, w_inb_ref[:, OFF_MQ:OFF_MQ + MEM_W]) * (MEM_HD ** -0.5 * LOG2E)
    o_ct = _mem_attention_t(qm, mk_ref.at[0], mv_ref.at[0], dense_c)
    gated = gated + late["gate_c"] * _dot_tn(_bf(o_ct), wbm_ref[...])

    out_ref[0] = _merge_ln1(x, gated, wo_ref, ln_g_ref, ln_b_ref, alpha)


def _sample_mixer_kernel(sinks_ref, x_ref, rc_ref, rdn_ref, rup_ref, mk_ref, mv_ref, gla_in_ref,
                         swak_in_ref, swav_in_ref, w_ina_ref, w_ga_ref, w_inb_ref, w2_ref,
                         ba_ref, gng_ref, wbg_ref, wbs_ref, wbm_ref, bgate_ref, wo_ref, ln_g_ref, ln_b_ref,
                         out_ref, gla_out_ref, swak_out_ref, swav_out_ref, *, nb, tq, alpha):
    rows = nb * tq
    x = x_ref[...].reshape(rows, D_MODEL)
    xb = _bf(x)

    log_a = _forget_log2(xb, w_ga_ref, w2_ref, ba_ref)

    def proj_a(off):
        return _dot(xb, w_ina_ref[:, off:off + GLA_W])

    def state_in(c, h):
        return jnp.transpose(gla_in_ref[c, h])

    def state_out(c, h, st):
        gla_out_ref[c, h] = jnp.transpose(st)

    gq, gk, gv = proj_a(OFF_GQ), proj_a(OFF_GK), proj_a(OFF_GV)
    gate_a = _gate(xb, 0, w_inb_ref, bgate_ref)
    o_a = _gla_tile(gq, gk, gv, log_a, tq, nb, state_in, state_out, chained=False)
    o_a = _gla_gate(o_a, proj_a(OFF_GR), gng_ref[...])
    gated = gate_a * _dot(_bf(o_a), wbg_ref[...])

    rc = jnp.concatenate([rc_ref[...]] * nb, axis=0)
    rdn = jnp.concatenate([rdn_ref[...]] * nb, axis=0)
    rup = jnp.concatenate([rup_ref[...]] * nb, axis=0)
    q = _rope(_dot(xb, w_inb_ref[:, OFF_SQ:OFF_SQ + SWA_QW]), rc, rdn, rup) * (SWA_HD ** -0.5)
    k = _rope(_dot(xb, w_inb_ref[:, OFF_SK:OFF_SK + SWA_KW]), rc, rdn, rup)
    v = _dot(xb, w_inb_ref[:, OFF_SV:OFF_SV + SWA_KW])
    gate_b = _gate(xb, 1, w_inb_ref, bgate_ref)
    qs = _swa_split_q(q)
    slabs = [[None] * nb for _ in range(SWA_QW // LANES)]
    kks, vvs = [], []
    for bi in range(nb):
        rs = slice(tq * bi, tq * (bi + 1))
        k_all = jnp.concatenate([swak_in_ref[bi], k[rs]], axis=0)
        v_all = jnp.concatenate([swav_in_ref[bi], v[rs]], axis=0)
        swak_out_ref[bi] = k_all[tq:tq + WINDOW]
        swav_out_ref[bi] = v_all[tq:tq + WINDOW]
        kks.append(_swa_dup_k(k_all))
        vvs.append(_swa_place_v(v_all))
    for g in range(SWA_KV_HEADS):
        qrows = []
        for bi in range(nb):
            rs = slice(tq * bi, tq * (bi + 1))
            qrows.append(jnp.concatenate([qs[2 * g][0][rs], qs[2 * g][1][rs],
                                          qs[2 * g + 1][0][rs], qs[2 * g + 1][1][rs]], axis=0))
        outs = _swa_blocks(qrows, [kk[g] for kk in kks], [vv[g][0] for vv in vvs], [vv[g][1] for vv in vvs],
                           _sink_col(sinks_ref, g, tq), None, None, tq)
        for bi in range(nb):
            slabs[2 * g][bi], slabs[2 * g + 1][bi] = outs[bi]
    o_b = jnp.concatenate([jnp.concatenate(s, axis=0) if nb > 1 else s[0] for s in slabs], axis=1)
    gated = gated + gate_b * _dot(_bf(o_b), wbs_ref[...])

    qm = _dot(xb, w_inb_ref[:, OFF_MQ:OFF_MQ + MEM_W]) * (MEM_HD ** -0.5)
    gate_c = _gate(xb, 2, w_inb_ref, bgate_ref)
    o_c = _mem_attention(qm, [mk_ref.at[bi] for bi in range(nb)], [mv_ref.at[bi] for bi in range(nb)])
    gated = gated + gate_c * _dot(_bf(o_c), wbm_ref[...])

    out = _merge_ln1(x, gated, wo_ref, ln_g_ref, ln_b_ref, alpha)
    out_ref[...] = out.reshape(nb, tq, D_MODEL)


_GELU_K1 = -2.0 * float(np.log2(np.e)) * float(np.sqrt(2.0 / np.pi))
_GELU_K2 = _GELU_K1 * 0.044715


def _conv(u2, u1, u0, cw_ref, cb_ref, cs):
    return cb_ref[:, cs] + u2 * cw_ref[0:1, cs] + u1 * cw_ref[1:2, cs] + u0 * cw_ref[2:3, cs]


def _geglu(g, v):
    return g * v / (1.0 + jnp.exp2(g * (g * g * _GELU_K2 + _GELU_K1)))


def _prompt_ffn_kernel(x_ref, wup_ref, cw_ref, cb_ref, wdn_ref, ln_g_ref, ln_b_ref,
                       out_ref, conv_out_ref, u_scr, *, rows, alpha):
    nj = rows // SUBLANES
    base = 2 * SUBLANES
    last = slice(base + rows - SUBLANES, base + rows)
    last2 = slice(base + rows - 2 * SUBLANES, base + rows - SUBLANES)

    @pl.when(pl.program_id(1) == 0)
    def _():
        u_scr[base + rows - 2 * SUBLANES:base + rows, :] = jnp.zeros((2 * SUBLANES, 2 * D_FF), _F32)

    prev_last, prev_last2 = u_scr[last, :], u_scr[last2, :]
    x = jnp.swapaxes(x_ref[0].reshape(SUBLANES, nj, D_MODEL), 0, 1).reshape(rows, D_MODEL)
    u_scr[base:base + rows, :] = _dot(_bf(x), wup_ref[...])
    sub = lax.broadcasted_iota(jnp.int32, (SUBLANES, FFN_COLS), 0)

    def conv_cols(cs):
        wrap = lambda prev, cur: pltpu.roll(jnp.where(sub == SUBLANES - 1, prev, cur), 1, axis=0)
        u_scr[SUBLANES:base, cs] = wrap(prev_last[:, cs], u_scr[last, cs])
        u_scr[0:SUBLANES, cs] = wrap(prev_last2[:, cs], u_scr[last2, cs])
        return _conv(u_scr[0:rows, cs], u_scr[SUBLANES:SUBLANES + rows, cs], u_scr[base:base + rows, cs],
                     cw_ref, cb_ref, cs)

    h = [_bf(_geglu(conv_cols(slice(c0, c0 + FFN_COLS)), conv_cols(slice(D_FF + c0, D_FF + c0 + FFN_COLS))))
         for c0 in range(0, D_FF, FFN_COLS)]
    conv_out_ref[0] = jnp.concatenate([u_scr[base + rows - SUBLANES - 1:base + rows - SUBLANES, :],
                                       u_scr[base + rows - 1:base + rows, :]], axis=0)
    f = _dot(jnp.concatenate(h, axis=1), wdn_ref[...])
    y = _layer_norm(alpha * x + f, ln_g_ref[...], ln_b_ref[...])
    out_ref[0] = jnp.swapaxes(y.reshape(nj, SUBLANES, D_MODEL), 0, 1).reshape(rows, D_MODEL)


def _sample_ffn_kernel(x_ref, hist_ref, wup_ref, cw_ref, cb_ref, wdn_ref, ln_g_ref, ln_b_ref,
                       out_ref, conv_out_ref, u_scr, *, nb, tq, alpha):
    rows = nb * tq
    base = (CONV_W - 1) * nb
    x = jnp.swapaxes(x_ref[...], 0, 1).reshape(rows, D_MODEL)
    u_scr[0:base, :] = jnp.swapaxes(hist_ref[...], 0, 1).reshape(base, 2 * D_FF)
    u_scr[base:base + rows, :] = _dot(_bf(x), wup_ref[...])
    conv = lambda cs: _conv(u_scr[0:rows, cs], u_scr[nb:nb + rows, cs], u_scr[base:base + rows, cs],
                            cw_ref, cb_ref, cs)
    h = _geglu(conv(slice(0, D_FF)), conv(slice(D_FF, 2 * D_FF)))
    conv_out_ref[...] = jnp.swapaxes(u_scr[rows:rows + base, :].reshape(CONV_W - 1, nb, 2 * D_FF), 0, 1)
    f = _dot(_bf(h), wdn_ref[...])
    y = _layer_norm(alpha * x + f, ln_g_ref[...], ln_b_ref[...])
    out_ref[...] = jnp.swapaxes(y.reshape(tq, nb, D_MODEL), 0, 1)


def _mem_kv_kernel(m_ref, w_ref, k_ref, v_ref):
    kv = _dot(_bf(m_ref[0]), w_ref[...])
    for h in range(MEM_HEADS):
        k_ref[0, pl.ds(h, MEM_TOKENS, stride=MEM_HEADS), :] = kv[:, MEM_HD * h:MEM_HD * (h + 1)]
        v_ref[0, pl.ds(h, MEM_TOKENS, stride=MEM_HEADS), :] = kv[:, MEM_W + MEM_HD * h:MEM_W + MEM_HD * (h + 1)]


def _const_spec(shape):
    nd = len(shape)
    return pl.BlockSpec(shape, lambda *_: (0,) * nd, pipeline_mode=pl.Buffered(1))


def _rope_tables(pos):
    half = ROPE_DIM // 2
    inv = np.float32(ROPE_THETA) ** (-np.arange(half, dtype=np.float32) / np.float32(half))
    ang = pos.astype(np.float32)[:, None] * inv[None, :]
    cos, sin = np.cos(ang), np.sin(ang)
    T = pos.shape[0]
    ones = np.ones((T, SWA_HD - ROPE_DIM), np.float32)
    zeros = np.zeros((T, SWA_HD - ROPE_DIM), np.float32)
    zh = np.zeros((T, half), np.float32)
    c = np.concatenate([cos, cos, ones], axis=1)
    dn = np.concatenate([-sin, zh, zeros], axis=1)
    up = np.concatenate([zh, sin, zeros], axis=1)
    rep = LANES // SWA_HD
    return tuple(jnp.asarray(np.concatenate([a] * rep, axis=1), dtype=_F32) for a in (c, dn, up))


def _pack_cols_kernel(wt_ref, o_ref, *, keep):
    t = jnp.transpose(wt_ref[...])
    if keep < t.shape[1]:
        t = jnp.where(lax.broadcasted_iota(jnp.int32, t.shape, 1) < keep, t, 0.0)
    o_ref[...] = _bf(t)


def _pack_cols(wt, col0, ncols, keep, name):
    blk = ncols // 2 if ncols % (2 * LANES) == 0 else ncols
    k = wt.shape[1]
    return pl.pallas_call(
        functools.partial(_pack_cols_kernel, keep=keep),
        grid=(ncols // blk,),
        in_specs=[pl.BlockSpec((pl.Element(blk), pl.Element(k)),
                               lambda i: (pl.multiple_of(col0 + blk * i, SUBLANES), 0))],
        out_specs=pl.BlockSpec((k, blk), lambda i: (0, i)),
        out_shape=jax.ShapeDtypeStruct((k, ncols), _BF),
        compiler_params=_params(("arbitrary",)),
        name=name,
    )(wt)


def _split_w_in(w_in):
    wt = jnp.transpose(w_in)
    return (_pack_cols(wt, 0, IN_A_COLS, IN_A_COLS, "pack_w_in_a"),
            _pack_cols(wt, IN_A_COLS, LANES, GLA_RANK, "pack_w_in_ga"),
            _pack_cols(wt, IN_A_COLS + GLA_RANK, IN_B_COLS, IN_B_COLS, "pack_w_in_b"))


def _mixer_weight_specs():
    return [
        _const_spec((D_MODEL, IN_A_COLS)),
        _const_spec((D_MODEL, LANES)),
        _const_spec((D_MODEL, IN_B_COLS)),
        _const_spec((LANES, GLA_W)),
        _const_spec((1, GLA_W)),
        _const_spec((1, GLA_DV)),
        _const_spec((GLA_W, D_MODEL)),
        _const_spec((SWA_QW, D_MODEL)),
        _const_spec((MEM_W, D_MODEL)),
        _const_spec((1, GATE_W)),
        _const_spec((D_MODEL, D_MODEL)),
        _const_spec((1, D_MODEL)),
        _const_spec((1, D_MODEL)),
    ]


def _ffn_weight_specs():
    return [
        _const_spec((D_MODEL, 2 * D_FF)),
        _const_spec((CONV_W, 2 * D_FF)),
        _const_spec((1, 2 * D_FF)),
        _const_spec((D_FF, D_MODEL)),
        _const_spec((1, D_MODEL)),
        _const_spec((1, D_MODEL)),
    ]


_SMEM_SPEC = pl.BlockSpec(memory_space=pltpu.SMEM)
_MEM_ROWS = MEM_TOKENS * MEM_HEADS


def _params(sem, flags=None):
    return pltpu.CompilerParams(dimension_semantics=sem, vmem_limit_bytes=V7X_VMEM_LIMIT, flags=flags)


def _prompt_layer(x, mem, sinks, mixer_w, ffn_w, w_mem_kv, alpha):
    B, T, _ = x.shape
    rows = min(PROMPT_ROWS, T)
    nt = T // rows
    assert T % rows == 0 and rows % SWA_PAIR == 0 and rows >= WINDOW
    f32 = jnp.float32
    mk, mv = pl.pallas_call(
        _mem_kv_kernel,
        grid=(B,),
        in_specs=[pl.BlockSpec((1, MEM_TOKENS, D_MODEL), lambda b: (b, 0, 0)),
                  _const_spec((D_MODEL, 2 * MEM_W))],
        out_specs=[pl.BlockSpec((1, _MEM_ROWS, MEM_HD), lambda b: (b, 0, 0))] * 2,
        out_shape=[jax.ShapeDtypeStruct((B, _MEM_ROWS, MEM_HD), f32)] * 2,
        compiler_params=_params(("arbitrary",)),
        name="mem_kv",
    )(mem, w_mem_kv)

    rope = _rope_tables(np.arange(T))
    rope_spec = pl.BlockSpec((rows, LANES), lambda b, t: (t, 0))
    per_batch = lambda *blk: pl.BlockSpec((1,) + blk, lambda b, t: (b,) + (0,) * len(blk))
    x1, gla, swak, swav = pl.pallas_call(
        functools.partial(_prompt_mixer_kernel, rows=rows, alpha=alpha),
        grid=(B, nt),
        in_specs=[_SMEM_SPEC,
                  pl.BlockSpec((1, rows, D_MODEL), lambda b, t: (b, t, 0)),
                  rope_spec, rope_spec, rope_spec,
                  per_batch(_MEM_ROWS, MEM_HD), per_batch(_MEM_ROWS, MEM_HD)] + _mixer_weight_specs(),
        out_specs=[pl.BlockSpec((1, rows, D_MODEL), lambda b, t: (b, t, 0)),
                   per_batch(GLA_HEADS, GLA_DK, GLA_DV),
                   per_batch(WINDOW, SWA_KW), per_batch(WINDOW, SWA_KW)],
        out_shape=[jax.ShapeDtypeStruct((B, T, D_MODEL), f32),
                   jax.ShapeDtypeStruct((B, GLA_HEADS, GLA_DK, GLA_DV), f32),
                   jax.ShapeDtypeStruct((B, WINDOW, SWA_KW), f32),
                   jax.ShapeDtypeStruct((B, WINDOW, SWA_KW), f32)],
        scratch_shapes=[pltpu.VMEM((GLA_HEADS, GLA_DK, GLA_DV), f32),
                        pltpu.VMEM((SWA_KV_HEADS, WINDOW + rows, LANES), _BF),
                        pltpu.VMEM((WINDOW + rows, LANES), _BF),
                        pltpu.VMEM((2, WINDOW + SWA_PAIR, SWA_BLOCK_LANES * SWA_KV_HEADS * (rows // SWA_PAIR)),
                                   f32)],
        compiler_params=_params(("arbitrary", "arbitrary")),
        name="prompt_mixer",
    )(sinks, x, *rope, mk, mv, *mixer_w)

    frows = min(PROMPT_FFN_ROWS, T)
    assert T % frows == 0 and frows >= CONV_W - 1
    y, conv = pl.pallas_call(
        functools.partial(_prompt_ffn_kernel, rows=frows, alpha=alpha),
        grid=(B, T // frows),
        in_specs=[pl.BlockSpec((1, frows, D_MODEL), lambda b, t: (b, t, 0))] + _ffn_weight_specs(),
        out_specs=[pl.BlockSpec((1, frows, D_MODEL), lambda b, t: (b, t, 0)),
                   per_batch(CONV_W - 1, 2 * D_FF)],
        out_shape=[jax.ShapeDtypeStruct((B, T, D_MODEL), f32),
                   jax.ShapeDtypeStruct((B, CONV_W - 1, 2 * D_FF), f32)],
        scratch_shapes=[pltpu.VMEM((2 * SUBLANES + frows, 2 * D_FF), f32)],
        compiler_params=_params(("arbitrary", "arbitrary")),
        name="prompt_ffn",
    )(x1, *ffn_w)
    return y, gla, swak, swav, mk, mv, conv


def _sample_layer(x, gla0, swak0, swav0, memk, memv, conv0, sinks, mixer_w, ffn_w, alpha):
    B, tq, _ = x.shape
    nb = max(1, min(B, SAMPLE_ROWS // tq))
    assert B % nb == 0 and tq % 16 == 0 and tq >= CONV_W - 1
    f32 = jnp.float32
    rope = _rope_tables(PAST_LEN + np.arange(tq))
    rope_spec = pl.BlockSpec((tq, LANES), lambda i: (0, 0))
    blk = lambda *s: pl.BlockSpec((nb,) + s, lambda i: (i,) + (0,) * len(s))
    blk1 = lambda *s: pl.BlockSpec((nb,) + s, lambda i: (i,) + (0,) * len(s), pipeline_mode=pl.Buffered(1))
    x1, gla, swak, swav = pl.pallas_call(
        functools.partial(_sample_mixer_kernel, nb=nb, tq=tq, alpha=alpha),
        grid=(B // nb,),
        in_specs=[_SMEM_SPEC, blk(tq, D_MODEL), rope_spec, rope_spec, rope_spec,
                  blk(_MEM_ROWS, MEM_HD), blk(_MEM_ROWS, MEM_HD),
                  blk1(GLA_HEADS, GLA_DK, GLA_DV), blk1(WINDOW, SWA_KW), blk1(WINDOW, SWA_KW)]
                 + _mixer_weight_specs(),
        out_specs=[blk(tq, D_MODEL), blk(GLA_HEADS, GLA_DK, GLA_DV), blk(WINDOW, SWA_KW), blk(WINDOW, SWA_KW)],
        out_shape=[jax.ShapeDtypeStruct((B, tq, D_MODEL), f32),
                   jax.ShapeDtypeStruct((B, GLA_HEADS, GLA_DK, GLA_DV), f32),
                   jax.ShapeDtypeStruct((B, WINDOW, SWA_KW), f32),
                   jax.ShapeDtypeStruct((B, WINDOW, SWA_KW), f32)],
        compiler_params=_params(("arbitrary",)),
        name="sample_mixer",
    )(sinks, x, *rope, memk, memv, gla0, swak0, swav0, *mixer_w)

    assert B % SUBLANES == 0
    fblk = lambda *s: pl.BlockSpec((SUBLANES,) + s, lambda i: (i,) + (0,) * len(s))
    y, conv = pl.pallas_call(
        functools.partial(_sample_ffn_kernel, nb=SUBLANES, tq=tq, alpha=alpha),
        grid=(B // SUBLANES,),
        in_specs=[fblk(tq, D_MODEL), fblk(CONV_W - 1, 2 * D_FF)] + _ffn_weight_specs(),
        out_specs=[fblk(tq, D_MODEL), fblk(CONV_W - 1, 2 * D_FF)],
        out_shape=[jax.ShapeDtypeStruct((B, tq, D_MODEL), f32),
                   jax.ShapeDtypeStruct((B, CONV_W - 1, 2 * D_FF), f32)],
        scratch_shapes=[pltpu.VMEM(((CONV_W - 1 + tq) * SUBLANES, 2 * D_FF), f32)],
        compiler_params=_params(("arbitrary",)),
        name="sample_ffn",
    )(x1, conv0, *ffn_w)
    return y, gla, swak, swav, conv


def kernel(x_prompt, x_sample, cache_swa_k, cache_swa_v, state_gla, cache_mem_k, cache_mem_v, cache_ffn_conv, mem_prompt, ln1_g, ln1_b, ln2_g, ln2_b, w_in, b_gate, w_gla_a2, b_gla_a, gla_norm_g, swa_sinks, w_mem_kv, w_br_gla, w_br_swa, w_br_mem, w_o, w_up, conv_w, conv_b, w_down):
    depth = w_in.shape[0]
    alpha = float((2 * depth) ** 0.25)
    Bp = x_prompt.shape[0]
    Bs = x_sample.shape[0]
    hp, hs = x_prompt, x_sample
    outs = [[] for _ in range(10)]
    row = lambda a: a.reshape(1, -1)
    for l in range(depth):
        w2 = jnp.pad(w_gla_a2[l], ((0, LANES - GLA_RANK), (0, 0)))
        mixer_w = _split_w_in(w_in[l]) + (
            _bf(w2), row(b_gla_a[l]), row(gla_norm_g[l]),
            _bf(w_br_gla[l]), _bf(w_br_swa[l]), _bf(w_br_mem[l]), row(b_gate[l]), _bf(w_o[l]),
            row(ln1_g[l]), row(ln1_b[l]))
        ffn_w = (_bf(w_up[l]), conv_w[l], row(conv_b[l]), _bf(w_down[l]), row(ln2_g[l]), row(ln2_b[l]))
        sinks = swa_sinks[l]
        hp, g_p, k_p, v_p, mk_p, mv_p, c_p = _prompt_layer(
            hp, mem_prompt, sinks, mixer_w, ffn_w, _bf(w_mem_kv[l]), alpha)
        hs, g_s, k_s, v_s, c_s = _sample_layer(
            hs, state_gla[l],
            cache_swa_k[l].reshape(Bs, WINDOW, SWA_KW), cache_swa_v[l].reshape(Bs, WINDOW, SWA_KW),
            cache_mem_k[l].reshape(Bs, _MEM_ROWS, MEM_HD), cache_mem_v[l].reshape(Bs, _MEM_ROWS, MEM_HD),
            cache_ffn_conv[l], sinks, mixer_w, ffn_w, alpha)
        kv5 = lambda a, b: a.reshape(b, WINDOW, SWA_KV_HEADS, SWA_HD)
        m5 = lambda a: a.reshape(Bp, MEM_TOKENS, MEM_HEADS, MEM_HD)
        for lst, val in zip(outs, (kv5(k_p, Bp), kv5(v_p, Bp), g_p, m5(mk_p), m5(mv_p), c_p,
                                   kv5(k_s, Bs), kv5(v_s, Bs), g_s, c_s)):
            lst.append(val)
    return (hp, hs) + tuple(jnp.stack(o) for o in outs)
```

```python
import functools

import jax
import jax.numpy as jnp
import numpy as np
from jax import lax
from jax.experimental import pallas as pl
from jax.experimental.pallas import tpu as pltpu

D_MODEL = 1024
CHUNK = 64
GLA_HEADS = 4
GLA_DK = 128
GLA_DV = 128
GLA_RANK = 16
GLA_TAU = 16.0
SWA_HEADS = 8
SWA_KV_HEADS = 2
SWA_HD = 64
WINDOW = 128
ROPE_DIM = 16
ROPE_THETA = 500000.0
MEM_TOKENS = 256
MEM_HEADS = 4
MEM_HD = 128
D_FF = 2816
CONV_W = 3
N_BRANCH = 3
PAST_LEN = 2048
LN_EPS = 1e-5

LANES = 128
SUBLANES = 8
V7X_VMEM_LIMIT = 56 * 1024 * 1024
PROMPT_ROWS = 512
PROMPT_FFN_ROWS = 512
FFN_COLS = 256
SAMPLE_ROWS = 256
GLA_W = GLA_HEADS * GLA_DK
SWA_QW = SWA_HEADS * SWA_HD
SWA_KW = SWA_KV_HEADS * SWA_HD
MEM_W = MEM_HEADS * MEM_HD
GATE_W = N_BRANCH * D_MODEL
SWA_PAIR = 2 * CHUNK

OFF_GQ = 0
OFF_GK = OFF_GQ + GLA_W
OFF_GV = OFF_GK + GLA_W
OFF_GR = OFF_GV + GLA_W
IN_A_COLS = OFF_GR + GLA_W
OFF_SQ = 0
OFF_SK = OFF_SQ + SWA_QW
OFF_SV = OFF_SK + SWA_KW
OFF_MQ = OFF_SV + SWA_KW
OFF_GL = OFF_MQ + MEM_W
IN_B_COLS = OFF_GL + GATE_W

_BF = jnp.bfloat16
_F32 = jnp.float32


def _bf(x):
    return x.astype(_BF)


def _dot(a, b):
    return jnp.dot(a, b, preferred_element_type=_F32)


def _dot_nt(a, b):
    return lax.dot_general(a, b, (((1,), (1,)), ((), ())), preferred_element_type=_F32)


def _dot_tn(a, b):
    return lax.dot_general(a, b, (((0,), (0,)), ((), ())), preferred_element_type=_F32)


def _layer_norm(h, g, b):
    mu = jnp.mean(h, axis=-1, keepdims=True)
    d = h - mu
    var = jnp.mean(d * d, axis=-1, keepdims=True)
    return d * lax.rsqrt(var + LN_EPS) * g + b


def _log_sigmoid(x):
    return -(jnp.maximum(-x, 0.0) + jnp.log(1.0 + jnp.exp(-jnp.abs(x))))


def _rope(a, c, s_dn, s_up):
    outs = []
    for j in range(a.shape[1] // LANES):
        slab = a[:, LANES * j:LANES * (j + 1)]
        fwd = pltpu.roll(slab, LANES - ROPE_DIM // 2, axis=1)
        bwd = pltpu.roll(slab, ROPE_DIM // 2, axis=1)
        outs.append(slab * c + fwd * s_dn + bwd * s_up)
    return outs[0] if len(outs) == 1 else jnp.concatenate(outs, axis=1)


def _gla_tile(q, k, v, log2_a, chunk, nchunk, state_in, state_out, chained, between=lambda: None):
    R = chunk * nchunk
    row = lax.broadcasted_iota(jnp.int32, (R, R), 0)
    col = lax.broadcasted_iota(jnp.int32, (R, R), 1)
    tril = (row // chunk == col // chunk) & (col <= row)
    ltri = jnp.where(tril, 1.0, 0.0).astype(_BF)
    hi = _bf(log2_a)
    lo = _bf(log2_a - hi.astype(_F32))
    b = _dot(ltri, hi) + _dot(ltri, lo)
    between()
    decay = [jnp.exp2(b[chunk * (c + 1) - 1:chunk * (c + 1), :]) for c in range(nchunk)]
    q_dec = _bf(q * (GLA_DK ** -0.5) * jnp.exp2(b))
    k_inv = k * jnp.exp2(-b)
    k_tail = _bf(jnp.concatenate([k_inv[chunk * c:chunk * (c + 1)] * decay[c] for c in range(nchunk)], axis=0)
                 if nchunk > 1 else k_inv * decay[0])
    k_inv = _bf(k_inv)
    vb = _bf(v)
    heads = [slice(GLA_DK * h, GLA_DK * (h + 1)) for h in range(GLA_HEADS)]
    o_intra = []
    for hs in heads:
        att = jnp.where(tril, _dot_nt(q_dec[:, hs], k_inv[:, hs]), 0.0)
        o_intra.append(_dot(_bf(att), vb[:, hs]))
    o_inter = [[None] * nchunk for _ in heads]
    st = [None] * GLA_HEADS
    for c in range(nchunk):
        rs = slice(chunk * c, chunk * (c + 1))
        for h, hs in enumerate(heads):
            if c == 0 or not chained:
                st[h] = state_in(c, h)
            o_inter[h][c] = _dot_nt(q_dec[rs, hs], _bf(st[h]))
            st[h] = st[h] * decay[c][:, hs] + _dot_tn(vb[rs, hs], k_tail[rs, hs])
            if c == nchunk - 1 or not chained:
                state_out(c, h, st[h])
    outs = [o_intra[h] + (jnp.concatenate(o_inter[h], axis=0) if nchunk > 1 else o_inter[h][0])
            for h in range(GLA_HEADS)]
    return jnp.concatenate(outs, axis=1)


def _gla_gate(o, gr, g):
    outs = []
    for h in range(GLA_HEADS):
        hs = slice(GLA_DV * h, GLA_DV * (h + 1))
        oh = o[:, hs]
        ms = jnp.mean(oh * oh, axis=-1, keepdims=True)
        grh = gr[:, hs]
        outs.append(oh * lax.rsqrt(ms + LN_EPS) * g * (grh * jax.nn.sigmoid(grh)))
    return jnp.concatenate(outs, axis=1)


def _lane_half_masks(shape):
    lane = lax.broadcasted_iota(jnp.int32, shape, 1)
    return lane < SWA_HD, lane >= SWA_HD


def _swa_split_q(q):
    lo_m, hi_m = _lane_half_masks((q.shape[0], LANES))
    out = []
    for j in range(SWA_QW // LANES):
        slab = q[:, LANES * j:LANES * (j + 1)]
        out.append((_bf(jnp.where(lo_m, slab, 0.0)), _bf(jnp.where(hi_m, slab, 0.0))))
    return out


def _swa_dup_k(k):
    lo_m, _ = _lane_half_masks(k.shape)
    kr = pltpu.roll(k, SWA_HD, axis=1)
    return [_bf(jnp.where(lo_m, k, kr)), _bf(jnp.where(lo_m, kr, k))]


def _swa_place_v(v):
    lo_m, hi_m = _lane_half_masks(v.shape)
    vr = pltpu.roll(v, SWA_HD, axis=1)
    return [(_bf(jnp.where(lo_m, v, 0.0)), _bf(jnp.where(hi_m, vr, 0.0))),
            (_bf(jnp.where(lo_m, vr, 0.0)), _bf(jnp.where(hi_m, v, 0.0)))]


def _swa_blocks(qrows, kks, vlos, vhis, sink_col, lowers, upper, tq):
    n = len(kks)
    stack = lambda xs: jnp.concatenate(xs, axis=0) if n > 1 else xs[0]
    s = stack([_dot_nt(qrows[i], kks[i]) for i in range(n)])
    if lowers is not None:
        kcol = lax.broadcasted_iota(jnp.int32, s.shape, 1)
        s = jnp.where(kcol >= stack(lowers), jnp.where(kcol <= stack([upper] * n), s, -jnp.inf), -jnp.inf)
    sink = stack([sink_col] * n)
    m = jnp.maximum(jnp.max(s, axis=-1, keepdims=True), sink)
    p = jnp.exp(s - m)
    den = jnp.sum(p, axis=-1, keepdims=True) + jnp.exp(sink - m)
    p = _bf(p / den)
    outs = []
    for i in range(n):
        r = 4 * tq * i
        outs.append((_dot(p[r:r + tq], vlos[i]) + _dot(p[r + tq:r + 2 * tq], vhis[i]),
                     _dot(p[r + 2 * tq:r + 3 * tq], vlos[i]) + _dot(p[r + 3 * tq:r + 4 * tq], vhis[i])))
    return outs


def _sink_col(sinks_ref, g, tq):
    r = lax.broadcasted_iota(jnp.int32, (4 * tq, 1), 0)
    base = 4 * g
    return jnp.where(r < tq, sinks_ref[base],
                     jnp.where(r < 2 * tq, sinks_ref[base + 1],
                               jnp.where(r < 3 * tq, sinks_ref[base + 2], sinks_ref[base + 3])))


def _mem_head(ref, h):
    return _bf(ref[pl.ds(h, MEM_TOKENS, stride=MEM_HEADS), :])


def _mem_attention(q, mk_refs, mv_refs):
    qb = _bf(q)
    nseg = len(mk_refs)
    tq = q.shape[0] // nseg
    outs = []
    for h in range(MEM_HEADS):
        hs = slice(MEM_HD * h, MEM_HD * (h + 1))
        s = [_dot_nt(qb[tq * i:tq * (i + 1), hs], _mem_head(mk_refs[i], h)) for i in range(nseg)]
        s = jnp.concatenate(s, axis=0) if nseg > 1 else s[0]
        m = jnp.max(s, axis=-1, keepdims=True)
        p = jnp.exp(s - m)
        p = _bf(p / jnp.sum(p, axis=-1, keepdims=True))
        o = [_dot(p[tq * i:tq * (i + 1)], _mem_head(mv_refs[i], h)) for i in range(nseg)]
        outs.append(jnp.concatenate(o, axis=0) if nseg > 1 else o[0])
    return jnp.concatenate(outs, axis=1)


SWA_BLOCK_LANES = 4 * SWA_PAIR
LOG2E = float(np.log2(np.e))


def _swa_bias_t(npair, first_tile):
    nk = WINDOW + SWA_PAIR
    width = SWA_BLOCK_LANES * SWA_KV_HEADS * npair
    col = lax.broadcasted_iota(jnp.int32, (1, width), 1)
    lower = CHUNK * ((col % SWA_PAIR) // CHUNK)
    upper = lower + (WINDOW + CHUNK - 1)
    if first_tile:
        lower = jnp.maximum(lower, WINDOW - SWA_PAIR * ((col // SWA_BLOCK_LANES) % npair))
    krow = lax.broadcasted_iota(jnp.int32, (nk, width), 0)
    return jnp.where(krow >= lower, jnp.where(krow <= upper, 0.0, -jnp.inf), -jnp.inf)


def _swa_attention_t(qs, kk_scr, v_scr, sinks_ref, bias, rows, between=lambda: None):
    nk = WINDOW + SWA_PAIR
    npair = rows // SWA_PAIR
    bw = SWA_BLOCK_LANES
    blocks = [(g, p) for g in range(SWA_KV_HEADS) for p in range(npair)]
    sts = []
    for g, p in blocks:
        rs = slice(SWA_PAIR * p, SWA_PAIR * (p + 1))
        qrows = jnp.concatenate([qs[2 * g][0][rs], qs[2 * g][1][rs],
                                 qs[2 * g + 1][0][rs], qs[2 * g + 1][1][rs]], axis=0)
        sts.append(_dot_nt(kk_scr[g, SWA_PAIR * p:SWA_PAIR * p + nk, :], qrows))
    st = jnp.concatenate(sts, axis=1) + bias
    between()
    width = bw * len(blocks)
    col = lax.broadcasted_iota(jnp.int32, (1, width), 1)
    head = 4 * (col // (bw * npair)) + (col % bw) // SWA_PAIR
    sink = jnp.zeros((1, width), _F32)
    for h in range(SWA_HEADS):
        sink = jnp.where(head == h, sinks_ref[h] * LOG2E, sink)
    m = jnp.maximum(jnp.max(st, axis=0, keepdims=True), sink)
    pt = jnp.exp2(st - m)
    inv = 1.0 / (jnp.sum(pt, axis=0, keepdims=True) + jnp.exp2(sink - m))
    pb = _bf(pt)
    pieces = [[None] * npair for _ in range(SWA_HEADS)]
    for i, (g, p) in enumerate(blocks):
        cs = slice(bw * i, bw * (i + 1))
        o2 = _dot_tn(v_scr[SWA_PAIR * p:SWA_PAIR * p + nk, :], pb[:, cs]) * inv[:, cs]
        for hh in range(4):
            pieces[4 * g + hh][p] = o2[SWA_HD * g:SWA_HD * (g + 1), SWA_PAIR * hh:SWA_PAIR * (hh + 1)]
    return jnp.concatenate([jnp.concatenate(ps, axis=1) if npair > 1 else ps[0] for ps in pieces], axis=0)


def _mem_attention_t(q, mk_ref, mv_ref, between=lambda: None):
    qb = _bf(q)
    R = q.shape[0]
    st = jnp.concatenate([_dot_nt(_mem_head(mk_ref, h), qb[:, MEM_HD * h:MEM_HD * (h + 1)])
                          for h in range(MEM_HEADS)], axis=1)
    between()
    pt = jnp.exp2(st - jnp.max(st, axis=0, keepdims=True))
    inv = 1.0 / jnp.sum(pt, axis=0, keepdims=True)
    pb = _bf(pt)
    outs = [_dot_tn(_mem_head(mv_ref, h), pb[:, R * h:R * (h + 1)]) * inv[:, R * h:R * (h + 1)]
            for h in range(MEM_HEADS)]
    return jnp.concatenate(outs, axis=0)


def _gate(xb, i, w_inb_ref, bgate_ref):
    gl = _dot(xb, w_inb_ref[:, OFF_GL + D_MODEL * i:OFF_GL + D_MODEL * (i + 1)])
    return jax.nn.sigmoid(gl + bgate_ref[:, D_MODEL * i:D_MODEL * (i + 1)])


def _merge_ln1(x, gated_sum, wo_ref, g_ref, b_ref, alpha):
    mix = _dot(_bf(gated_sum), wo_ref[...])
    return _layer_norm(alpha * x + mix, g_ref[...], b_ref[...])


def _forget_log2(xb, w_ga_ref, w2_ref, ba_ref):
    ga = _dot(xb, w_ga_ref[...])
    return _log_sigmoid(_dot(_bf(ga), w2_ref[...]) + ba_ref[...]) * (LOG2E / GLA_TAU)


def _prompt_mixer_kernel(sinks_ref, x_ref, rc_ref, rdn_ref, rup_ref, mk_ref, mv_ref, w_ina_ref, w_ga_ref,
                         w_inb_ref, w2_ref, ba_ref, gng_ref, wbg_ref, wbs_ref, wbm_ref, bgate_ref, wo_ref,
                         ln_g_ref, ln_b_ref,
                         out_ref, gla_out_ref, swak_out_ref, swav_out_ref,
                         s_scr, kk_scr, vv_scr, bias_scr, *, rows, alpha):
    t = pl.program_id(1)
    nchunk = rows // CHUNK

    @pl.when(t == 0)
    def _():
        s_scr[...] = jnp.zeros_like(s_scr)
        kk_scr[:, 0:WINDOW, :] = jnp.zeros((SWA_KV_HEADS, WINDOW, LANES), _BF)
        vv_scr[0:WINDOW, :] = jnp.zeros((WINDOW, LANES), _BF)
        bias_scr[0] = _swa_bias_t(rows // SWA_PAIR, first_tile=False)
        bias_scr[1] = _swa_bias_t(rows // SWA_PAIR, first_tile=True)

    x = x_ref[0]
    xb = _bf(x)

    log_a = _forget_log2(xb, w_ga_ref, w2_ref, ba_ref)

    def proj_a(off):
        return _dot(xb, w_ina_ref[:, off:off + GLA_W])

    def state_in(c, h):
        return s_scr[h]

    def state_out(c, h, st):
        s_scr[h] = st
        gla_out_ref[0, h] = jnp.transpose(st)

    gq, gk, gv = proj_a(OFF_GQ), proj_a(OFF_GK), proj_a(OFF_GV)
    gate_a = _gate(xb, 0, w_inb_ref, bgate_ref)
    late = {}

    def dense_a():
        late["gr"] = proj_a(OFF_GR)
        late["sq"] = _dot(xb, w_inb_ref[:, OFF_SQ:OFF_SQ + SWA_QW])
        late["sk"] = _dot(xb, w_inb_ref[:, OFF_SK:OFF_SK + SWA_KW])
        late["sv"] = _dot(xb, w_inb_ref[:, OFF_SV:OFF_SV + SWA_KW])

    def dense_b():
        late["gate_b"] = _gate(xb, 1, w_inb_ref, bgate_ref)

    def dense_c():
        late["gate_c"] = _gate(xb, 2, w_inb_ref, bgate_ref)

    o_a = _gla_tile(gq, gk, gv, log_a, CHUNK, nchunk, state_in, state_out, True, dense_a)
    o_a = _gla_gate(o_a, late["gr"], gng_ref[...])
    gated = gate_a * _dot(_bf(o_a), wbg_ref[...])

    rc, rdn, rup = rc_ref[...], rdn_ref[...], rup_ref[...]
    q = _rope(late["sq"], rc, rdn, rup) * (SWA_HD ** -0.5 * LOG2E)
    k = _rope(late["sk"], rc, rdn, rup)
    v = late["sv"]
    swak_out_ref[0] = k[rows - WINDOW:rows, :]
    swav_out_ref[0] = v[rows - WINDOW:rows, :]
    kk = _swa_dup_k(k)
    vb = _bf(v)
    for g in range(SWA_KV_HEADS):
        kk_scr[g, WINDOW:WINDOW + rows, :] = kk[g]
    vv_scr[WINDOW:WINDOW + rows, :] = vb
    o_bt = _swa_attention_t(_swa_split_q(q), kk_scr, vv_scr, sinks_ref, bias_scr[jnp.where(t == 0, 1, 0)], rows,
                            dense_b)
    gated = gated + late["gate_b"] * _dot_tn(_bf(o_bt), wbs_ref[...])
    for g in range(SWA_KV_HEADS):
        kk_scr[g, 0:WINDOW, :] = kk[g][rows - WINDOW:rows]
    vv_scr[0:WINDOW, :] = vb[rows - WINDOW:rows]

    qm = _dot(xb, w_inb_ref[:, OFF_MQ:OFF_MQ + MEM_W]) * (MEM_HD ** -0.5 * LOG2E)
    o_ct = _mem_attention_t(qm, mk_ref.at[0], mv_ref.at[0], dense_c)
    gated = gated + late["gate_c"] * _dot_tn(_bf(o_ct), wbm_ref[...])

    out_ref[0] = _merge_ln1(x, gated, wo_ref, ln_g_ref, ln_b_ref, alpha)


def _sample_mixer_kernel(sinks_ref, x_ref, rc_ref, rdn_ref, rup_ref, mk_ref, mv_ref, gla_in_ref,
                         swak_in_ref, swav_in_ref, w_ina_ref, w_ga_ref, w_inb_ref, w2_ref,
                         ba_ref, gng_ref, wbg_ref, wbs_ref, wbm_ref, bgate_ref, wo_ref, ln_g_ref, ln_b_ref,
                         out_ref, gla_out_ref, swak_out_ref, swav_out_ref, *, nb, tq, alpha):
    rows = nb * tq
    x = x_ref[...].reshape(rows, D_MODEL)
    xb = _bf(x)

    log_a = _forget_log2(xb, w_ga_ref, w2_ref, ba_ref)

    def proj_a(off):
        return _dot(xb, w_ina_ref[:, off:off + GLA_W])

    def state_in(c, h):
        return jnp.transpose(gla_in_ref[c, h])

    def state_out(c, h, st):
        gla_out_ref[c, h] = jnp.transpose(st)

    gq, gk, gv = proj_a(OFF_GQ), proj_a(OFF_GK), proj_a(OFF_GV)
    gate_a = _gate(xb, 0, w_inb_ref, bgate_ref)
    o_a = _gla_tile(gq, gk, gv, log_a, tq, nb, state_in, state_out, chained=False)
    o_a = _gla_gate(o_a, proj_a(OFF_GR), gng_ref[...])
    gated = gate_a * _dot(_bf(o_a), wbg_ref[...])

    rc = jnp.concatenate([rc_ref[...]] * nb, axis=0)
    rdn = jnp.concatenate([rdn_ref[...]] * nb, axis=0)
    rup = jnp.concatenate([rup_ref[...]] * nb, axis=0)
    q = _rope(_dot(xb, w_inb_ref[:, OFF_SQ:OFF_SQ + SWA_QW]), rc, rdn, rup) * (SWA_HD ** -0.5)
    k = _rope(_dot(xb, w_inb_ref[:, OFF_SK:OFF_SK + SWA_KW]), rc, rdn, rup)
    v = _dot(xb, w_inb_ref[:, OFF_SV:OFF_SV + SWA_KW])
    gate_b = _gate(xb, 1, w_inb_ref, bgate_ref)
    qs = _swa_split_q(q)
    slabs = [[None] * nb for _ in range(SWA_QW // LANES)]
    kks, vvs = [], []
    for bi in range(nb):
        rs = slice(tq * bi, tq * (bi + 1))
        k_all = jnp.concatenate([swak_in_ref[bi], k[rs]], axis=0)
        v_all = jnp.concatenate([swav_in_ref[bi], v[rs]], axis=0)
        swak_out_ref[bi] = k_all[tq:tq + WINDOW]
        swav_out_ref[bi] = v_all[tq:tq + WINDOW]
        kks.append(_swa_dup_k(k_all))
        vvs.append(_swa_place_v(v_all))
    for g in range(SWA_KV_HEADS):
        qrows = []
        for bi in range(nb):
            rs = slice(tq * bi, tq * (bi + 1))
            qrows.append(jnp.concatenate([qs[2 * g][0][rs], qs[2 * g][1][rs],
                                          qs[2 * g + 1][0][rs], qs[2 * g + 1][1][rs]], axis=0))
        outs = _swa_blocks(qrows, [kk[g] for kk in kks], [vv[g][0] for vv in vvs], [vv[g][1] for vv in vvs],
                           _sink_col(sinks_ref, g, tq), None, None, tq)
        for bi in range(nb):
            slabs[2 * g][bi], slabs[2 * g + 1][bi] = outs[bi]
    o_b = jnp.concatenate([jnp.concatenate(s, axis=0) if nb > 1 else s[0] for s in slabs], axis=1)
    gated = gated + gate_b * _dot(_bf(o_b), wbs_ref[...])

    qm = _dot(xb, w_inb_ref[:, OFF_MQ:OFF_MQ + MEM_W]) * (MEM_HD ** -0.5)
    gate_c = _gate(xb, 2, w_inb_ref, bgate_ref)
    o_c = _mem_attention(qm, [mk_ref.at[bi] for bi in range(nb)], [mv_ref.at[bi] for bi in range(nb)])
    gated = gated + gate_c * _dot(_bf(o_c), wbm_ref[...])

    out = _merge_ln1(x, gated, wo_ref, ln_g_ref, ln_b_ref, alpha)
    out_ref[...] = out.reshape(nb, tq, D_MODEL)


_GELU_K1 = -2.0 * float(np.log2(np.e)) * float(np.sqrt(2.0 / np.pi))
_GELU_K2 = _GELU_K1 * 0.044715


def _conv(u2, u1, u0, cw_ref, cb_ref, cs):
    return cb_ref[:, cs] + u2 * cw_ref[0:1, cs] + u1 * cw_ref[1:2, cs] + u0 * cw_ref[2:3, cs]


def _geglu(g, v):
    return g * v / (1.0 + jnp.exp2(g * (g * g * _GELU_K2 + _GELU_K1)))


def _prompt_ffn_kernel(x_ref, wup_ref, cw_ref, cb_ref, wdn_ref, ln_g_ref, ln_b_ref,
                       out_ref, conv_out_ref, u_scr, *, rows, alpha):
    nj = rows // SUBLANES
    base = 2 * SUBLANES
    last = slice(base + rows - SUBLANES, base + rows)
    last2 = slice(base + rows - 2 * SUBLANES, base + rows - SUBLANES)

    @pl.when(pl.program_id(1) == 0)
    def _():
        u_scr[base + rows - 2 * SUBLANES:base + rows, :] = jnp.zeros((2 * SUBLANES, 2 * D_FF), _F32)

    prev_last, prev_last2 = u_scr[last, :], u_scr[last2, :]
    x = jnp.swapaxes(x_ref[0].reshape(SUBLANES, nj, D_MODEL), 0, 1).reshape(rows, D_MODEL)
    u_scr[base:base + rows, :] = _dot(_bf(x), wup_ref[...])
    sub = lax.broadcasted_iota(jnp.int32, (SUBLANES, FFN_COLS), 0)

    def conv_cols(cs):
        wrap = lambda prev, cur: pltpu.roll(jnp.where(sub == SUBLANES - 1, prev, cur), 1, axis=0)
        u_scr[SUBLANES:base, cs] = wrap(prev_last[:, cs], u_scr[last, cs])
        u_scr[0:SUBLANES, cs] = wrap(prev_last2[:, cs], u_scr[last2, cs])
        return _conv(u_scr[0:rows, cs], u_scr[SUBLANES:SUBLANES + rows, cs], u_scr[base:base + rows, cs],
                     cw_ref, cb_ref, cs)

    h = [_bf(_geglu(conv_cols(slice(c0, c0 + FFN_COLS)), conv_cols(slice(D_FF + c0, D_FF + c0 + FFN_COLS))))
         for c0 in range(0, D_FF, FFN_COLS)]
    conv_out_ref[0] = jnp.concatenate([u_scr[base + rows - SUBLANES - 1:base + rows - SUBLANES, :],
                                       u_scr[base + rows - 1:base + rows, :]], axis=0)
    f = _dot(jnp.concatenate(h, axis=1), wdn_ref[...])
    y = _layer_norm(alpha * x + f, ln_g_ref[...], ln_b_ref[...])
    out_ref[0] = jnp.swapaxes(y.reshape(nj, SUBLANES, D_MODEL), 0, 1).reshape(rows, D_MODEL)


def _sample_ffn_kernel(x_ref, hist_ref, wup_ref, cw_ref, cb_ref, wdn_ref, ln_g_ref, ln_b_ref,
                       out_ref, conv_out_ref, u_scr, *, nb, tq, alpha):
    rows = nb * tq
    base = (CONV_W - 1) * nb
    x = jnp.swapaxes(x_ref[...], 0, 1).reshape(rows, D_MODEL)
    u_scr[0:base, :] = jnp.swapaxes(hist_ref[...], 0, 1).reshape(base, 2 * D_FF)
    u_scr[base:base + rows, :] = _dot(_bf(x), wup_ref[...])
    conv = lambda cs: _conv(u_scr[0:rows, cs], u_scr[nb:nb + rows, cs], u_scr[base:base + rows, cs],
                            cw_ref, cb_ref, cs)
    h = _geglu(conv(slice(0, D_FF)), conv(slice(D_FF, 2 * D_FF)))
    conv_out_ref[...] = jnp.swapaxes(u_scr[rows:rows + base, :].reshape(CONV_W - 1, nb, 2 * D_FF), 0, 1)
    f = _dot(_bf(h), wdn_ref[...])
    y = _layer_norm(alpha * x + f, ln_g_ref[...], ln_b_ref[...])
    out_ref[...] = jnp.swapaxes(y.reshape(tq, nb, D_MODEL), 0, 1)


def _mem_kv_kernel(m_ref, w_ref, k_ref, v_ref):
    kv = _dot(_bf(m_ref[0]), w_ref[...])
    for h in range(MEM_HEADS):
        k_ref[0, pl.ds(h, MEM_TOKENS, stride=MEM_HEADS), :] = kv[:, MEM_HD * h:MEM_HD * (h + 1)]
        v_ref[0, pl.ds(h, MEM_TOKENS, stride=MEM_HEADS), :] = kv[:, MEM_W + MEM_HD * h:MEM_W + MEM_HD * (h + 1)]


def _const_spec(shape):
    nd = len(shape)
    return pl.BlockSpec(shape, lambda *_: (0,) * nd, pipeline_mode=pl.Buffered(1))


def _rope_tables(pos):
    half = ROPE_DIM // 2
    inv = np.float32(ROPE_THETA) ** (-np.arange(half, dtype=np.float32) / np.float32(half))
    ang = pos.astype(np.float32)[:, None] * inv[None, :]
    cos, sin = np.cos(ang), np.sin(ang)
    T = pos.shape[0]
    ones = np.ones((T, SWA_HD - ROPE_DIM), np.float32)
    zeros = np.zeros((T, SWA_HD - ROPE_DIM), np.float32)
    zh = np.zeros((T, half), np.float32)
    c = np.concatenate([cos, cos, ones], axis=1)
    dn = np.concatenate([-sin, zh, zeros], axis=1)
    up = np.concatenate([zh, sin, zeros], axis=1)
    rep = LANES // SWA_HD
    return tuple(jnp.asarray(np.concatenate([a] * rep, axis=1), dtype=_F32) for a in (c, dn, up))


def _pack_cols_kernel(wt_ref, o_ref, *, keep):
    t = jnp.transpose(wt_ref[...])
    if keep < t.shape[1]:
        t = jnp.where(lax.broadcasted_iota(jnp.int32, t.shape, 1) < keep, t, 0.0)
    o_ref[...] = _bf(t)


def _pack_cols(wt, col0, ncols, keep, name):
    blk = ncols // 2 if ncols % (2 * LANES) == 0 else ncols
    k = wt.shape[1]
    return pl.pallas_call(
        functools.partial(_pack_cols_kernel, keep=keep),
        grid=(ncols // blk,),
        in_specs=[pl.BlockSpec((pl.Element(blk), pl.Element(k)),
                               lambda i: (pl.multiple_of(col0 + blk * i, SUBLANES), 0))],
        out_specs=pl.BlockSpec((k, blk), lambda i: (0, i)),
        out_shape=jax.ShapeDtypeStruct((k, ncols), _BF),
        compiler_params=_params(("arbitrary",)),
        name=name,
    )(wt)


def _split_w_in(w_in):
    wt = jnp.transpose(w_in)
    return (_pack_cols(wt, 0, IN_A_COLS, IN_A_COLS, "pack_w_in_a"),
            _pack_cols(wt, IN_A_COLS, LANES, GLA_RANK, "pack_w_in_ga"),
            _pack_cols(wt, IN_A_COLS + GLA_RANK, IN_B_COLS, IN_B_COLS, "pack_w_in_b"))


def _mixer_weight_specs():
    return [
        _const_spec((D_MODEL, IN_A_COLS)),
        _const_spec((D_MODEL, LANES)),
        _const_spec((D_MODEL, IN_B_COLS)),
        _const_spec((LANES, GLA_W)),
        _const_spec((1, GLA_W)),
        _const_spec((1, GLA_DV)),
        _const_spec((GLA_W, D_MODEL)),
        _const_spec((SWA_QW, D_MODEL)),
        _const_spec((MEM_W, D_MODEL)),
        _const_spec((1, GATE_W)),
        _const_spec((D_MODEL, D_MODEL)),
        _const_spec((1, D_MODEL)),
        _const_spec((1, D_MODEL)),
    ]


def _ffn_weight_specs():
    return [
        _const_spec((D_MODEL, 2 * D_FF)),
        _const_spec((CONV_W, 2 * D_FF)),
        _const_spec((1, 2 * D_FF)),
        _const_spec((D_FF, D_MODEL)),
        _const_spec((1, D_MODEL)),
        _const_spec((1, D_MODEL)),
    ]


_SMEM_SPEC = pl.BlockSpec(memory_space=pltpu.SMEM)
_MEM_ROWS = MEM_TOKENS * MEM_HEADS


def _params(sem, flags=None):
    return pltpu.CompilerParams(dimension_semantics=sem, vmem_limit_bytes=V7X_VMEM_LIMIT, flags=flags)


def _prompt_layer(x, mem, sinks, mixer_w, ffn_w, w_mem_kv, alpha):
    B, T, _ = x.shape
    rows = min(PROMPT_ROWS, T)
    nt = T // rows
    assert T % rows == 0 and rows % SWA_PAIR == 0 and rows >= WINDOW
    f32 = jnp.float32
    mk, mv = pl.pallas_call(
        _mem_kv_kernel,
        grid=(B,),
        in_specs=[pl.BlockSpec((1, MEM_TOKENS, D_MODEL), lambda b: (b, 0, 0)),
                  _const_spec((D_MODEL, 2 * MEM_W))],
        out_specs=[pl.BlockSpec((1, _MEM_ROWS, MEM_HD), lambda b: (b, 0, 0))] * 2,
        out_shape=[jax.ShapeDtypeStruct((B, _MEM_ROWS, MEM_HD), f32)] * 2,
        compiler_params=_params(("arbitrary",)),
        name="mem_kv",
    )(mem, w_mem_kv)

    rope = _rope_tables(np.arange(T))
    rope_spec = pl.BlockSpec((rows, LANES), lambda b, t: (t, 0))
    per_batch = lambda *blk: pl.BlockSpec((1,) + blk, lambda b, t: (b,) + (0,) * len(blk))
    x1, gla, swak, swav = pl.pallas_call(
        functools.partial(_prompt_mixer_kernel, rows=rows, alpha=alpha),
        grid=(B, nt),
        in_specs=[_SMEM_SPEC,
                  pl.BlockSpec((1, rows, D_MODEL), lambda b, t: (b, t, 0)),
                  rope_spec, rope_spec, rope_spec,
                  per_batch(_MEM_ROWS, MEM_HD), per_batch(_MEM_ROWS, MEM_HD)] + _mixer_weight_specs(),
        out_specs=[pl.BlockSpec((1, rows, D_MODEL), lambda b, t: (b, t, 0)),
                   per_batch(GLA_HEADS, GLA_DK, GLA_DV),
                   per_batch(WINDOW, SWA_KW), per_batch(WINDOW, SWA_KW)],
        out_shape=[jax.ShapeDtypeStruct((B, T, D_MODEL), f32),
                   jax.ShapeDtypeStruct((B, GLA_HEADS, GLA_DK, GLA_DV), f32),
                   jax.ShapeDtypeStruct((B, WINDOW, SWA_KW), f32),
                   jax.ShapeDtypeStruct((B, WINDOW, SWA_KW), f32)],
        scratch_shapes=[pltpu.VMEM((GLA_HEADS, GLA_DK, GLA_DV), f32),
                        pltpu.VMEM((SWA_KV_HEADS, WINDOW + rows, LANES), _BF),
                        pltpu.VMEM((WINDOW + rows, LANES), _BF),
                        pltpu.VMEM((2, WINDOW + SWA_PAIR, SWA_BLOCK_LANES * SWA_KV_HEADS * (rows // SWA_PAIR)),
                                   f32)],
        compiler_params=_params(("arbitrary", "arbitrary")),
        name="prompt_mixer",
    )(sinks, x, *rope, mk, mv, *mixer_w)

    frows = min(PROMPT_FFN_ROWS, T)
    assert T % frows == 0 and frows >= CONV_W - 1
    y, conv = pl.pallas_call(
        functools.partial(_prompt_ffn_kernel, rows=frows, alpha=alpha),
        grid=(B, T // frows),
        in_specs=[pl.BlockSpec((1, frows, D_MODEL), lambda b, t: (b, t, 0))] + _ffn_weight_specs(),
        out_specs=[pl.BlockSpec((1, frows, D_MODEL), lambda b, t: (b, t, 0)),
                   per_batch(CONV_W - 1, 2 * D_FF)],
        out_shape=[jax.ShapeDtypeStruct((B, T, D_MODEL), f32),
                   jax.ShapeDtypeStruct((B, CONV_W - 1, 2 * D_FF), f32)],
        scratch_shapes=[pltpu.VMEM((2 * SUBLANES + frows, 2 * D_FF), f32)],
        compiler_params=_params(("arbitrary", "arbitrary")),
        name="prompt_ffn",
    )(x1, *ffn_w)
    return y, gla, swak, swav, mk, mv, conv


def _sample_layer(x, gla0, swak0, swav0, memk, memv, conv0, sinks, mixer_w, ffn_w, alpha):
    B, tq, _ = x.shape
    nb = max(1, min(B, SAMPLE_ROWS // tq))
    assert B % nb == 0 and tq % 16 == 0 and tq >= CONV_W - 1
    f32 = jnp.float32
    rope = _rope_tables(PAST_LEN + np.arange(tq))
    rope_spec = pl.BlockSpec((tq, LANES), lambda i: (0, 0))
    blk = lambda *s: pl.BlockSpec((nb,) + s, lambda i: (i,) + (0,) * len(s))
    blk1 = lambda *s: pl.BlockSpec((nb,) + s, lambda i: (i,) + (0,) * len(s), pipeline_mode=pl.Buffered(1))
    x1, gla, swak, swav = pl.pallas_call(
        functools.partial(_sample_mixer_kernel, nb=nb, tq=tq, alpha=alpha),
        grid=(B // nb,),
        in_specs=[_SMEM_SPEC, blk(tq, D_MODEL), rope_spec, rope_spec, rope_spec,
                  blk(_MEM_ROWS, MEM_HD), blk(_MEM_ROWS, MEM_HD),
                  blk1(GLA_HEADS, GLA_DK, GLA_DV), blk1(WINDOW, SWA_KW), blk1(WINDOW, SWA_KW)]
                 + _mixer_weight_specs(),
        out_specs=[blk(tq, D_MODEL), blk(GLA_HEADS, GLA_DK, GLA_DV), blk(WINDOW, SWA_KW), blk(WINDOW, SWA_KW)],
        out_shape=[jax.ShapeDtypeStruct((B, tq, D_MODEL), f32),
                   jax.ShapeDtypeStruct((B, GLA_HEADS, GLA_DK, GLA_DV), f32),
                   jax.ShapeDtypeStruct((B, WINDOW, SWA_KW), f32),
                   jax.ShapeDtypeStruct((B, WINDOW, SWA_KW), f32)],
        compiler_params=_params(("arbitrary",)),
        name="sample_mixer",
    )(sinks, x, *rope, memk, memv, gla0, swak0, swav0, *mixer_w)

    assert B % SUBLANES == 0
    fblk = lambda *s: pl.BlockSpec((SUBLANES,) + s, lambda i: (i,) + (0,) * len(s))
    y, conv = pl.pallas_call(
        functools.partial(_sample_ffn_kernel, nb=SUBLANES, tq=tq, alpha=alpha),
        grid=(B // SUBLANES,),
        in_specs=[fblk(tq, D_MODEL), fblk(CONV_W - 1, 2 * D_FF)] + _ffn_weight_specs(),
        out_specs=[fblk(tq, D_MODEL), fblk(CONV_W - 1, 2 * D_FF)],
        out_shape=[jax.ShapeDtypeStruct((B, tq, D_MODEL), f32),
                   jax.ShapeDtypeStruct((B, CONV_W - 1, 2 * D_FF), f32)],
        scratch_shapes=[pltpu.VMEM(((CONV_W - 1 + tq) * SUBLANES, 2 * D_FF), f32)],
        compiler_params=_params(("arbitrary",)),
        name="sample_ffn",
    )(x1, conv0, *ffn_w)
    return y, gla, swak, swav, conv


def kernel(x_prompt, x_sample, cache_swa_k, cache_swa_v, state_gla, cache_mem_k, cache_mem_v, cache_ffn_conv, mem_prompt, ln1_g, ln1_b, ln2_g, ln2_b, w_in, b_gate, w_gla_a2, b_gla_a, gla_norm_g, swa_sinks, w_mem_kv, w_br_gla, w_br_swa, w_br_mem, w_o, w_up, conv_w, conv_b, w_down):
    depth = w_in.shape[0]
    alpha = float((2 * depth) ** 0.25)
    Bp = x_prompt.shape[0]
    Bs = x_sample.shape[0]
    hp, hs = x_prompt, x_sample
    outs = [[] for _ in range(10)]
    row = lambda a: a.reshape(1, -1)
    for l in range(depth):
        w2 = jnp.pad(w_gla_a2[l], ((0, LANES - GLA_RANK), (0, 0)))
        mixer_w = _split_w_in(w_in[l]) + (
            _bf(w2), row(b_gla_a[l]), row(gla_norm_g[l]),
            _bf(w_br_gla[l]), _bf(w_br_swa[l]), _bf(w_br_mem[l]), row(b_gate[l]), _bf(w_o[l]),
            row(ln1_g[l]), row(ln1_b[l]))
        ffn_w = (_bf(w_up[l]), conv_w[l], row(conv_b[l]), _bf(w_down[l]), row(ln2_g[l]), row(ln2_b[l]))
        sinks = swa_sinks[l]
        hp, g_p, k_p, v_p, mk_p, mv_p, c_p = _prompt_layer(
            hp, mem_prompt, sinks, mixer_w, ffn_w, _bf(w_mem_kv[l]), alpha)
        hs, g_s, k_s, v_s, c_s = _sample_layer(
            hs, state_gla[l],
            cache_swa_k[l].reshape(Bs, WINDOW, SWA_KW), cache_swa_v[l].reshape(Bs, WINDOW, SWA_KW),
            cache_mem_k[l].reshape(Bs, _MEM_ROWS, MEM_HD), cache_mem_v[l].reshape(Bs, _MEM_ROWS, MEM_HD),
            cache_ffn_conv[l], sinks, mixer_w, ffn_w, alpha)
        kv5 = lambda a, b: a.reshape(b, WINDOW, SWA_KV_HEADS, SWA_HD)
        m5 = lambda a: a.reshape(Bp, MEM_TOKENS, MEM_HEADS, MEM_HD)
        for lst, val in zip(outs, (kv5(k_p, Bp), kv5(v_p, Bp), g_p, m5(mk_p), m5(mv_p), c_p,
                                   kv5(k_s, Bs), kv5(v_s, Bs), g_s, c_s)):
            lst.append(val)
    return (hp, hs) + tuple(jnp.stack(o) for o in outs)
```

```python
import functools

import jax
import jax.numpy as jnp
import numpy as np
from jax import lax
from jax.experimental import pallas as pl
from jax.experimental.pallas import tpu as pltpu

D_MODEL = 1024
CHUNK = 64
GLA_HEADS = 4
GLA_DK = 128
GLA_DV = 128
GLA_RANK = 16
GLA_TAU = 16.0
SWA_HEADS = 8
SWA_KV_HEADS = 2
SWA_HD = 64
WINDOW = 128
ROPE_DIM = 16
ROPE_THETA = 500000.0
MEM_TOKENS = 256
MEM_HEADS = 4
MEM_HD = 128
D_FF = 2816
CONV_W = 3
N_BRANCH = 3
PAST_LEN = 2048
LN_EPS = 1e-5

LANES = 128
SUBLANES = 8
V7X_VMEM_LIMIT = 56 * 1024 * 1024
PROMPT_ROWS = 512
PROMPT_FFN_ROWS = 512
FFN_COLS = 256
GLA_BLOCK_ROWS = 256
SAMPLE_ROWS = 256
GLA_W = GLA_HEADS * GLA_DK
SWA_QW = SWA_HEADS * SWA_HD
SWA_KW = SWA_KV_HEADS * SWA_HD
MEM_W = MEM_HEADS * MEM_HD
GATE_W = N_BRANCH * D_MODEL
SWA_PAIR = 2 * CHUNK

OFF_GQ = 0
OFF_GK = OFF_GQ + GLA_W
OFF_GV = OFF_GK + GLA_W
OFF_GR = OFF_GV + GLA_W
IN_A_COLS = OFF_GR + GLA_W
OFF_SQ = 0
OFF_SK = OFF_SQ + SWA_QW
OFF_SV = OFF_SK + SWA_KW
OFF_MQ = OFF_SV + SWA_KW
OFF_GL = OFF_MQ + MEM_W
IN_B_COLS = OFF_GL + GATE_W

_BF = jnp.bfloat16
_F32 = jnp.float32


def _bf(x):
    return x.astype(_BF)


def _dot(a, b):
    return jnp.dot(a, b, preferred_element_type=_F32)


def _dot_nt(a, b):
    return lax.dot_general(a, b, (((1,), (1,)), ((), ())), preferred_element_type=_F32)


def _dot_tn(a, b):
    return lax.dot_general(a, b, (((0,), (0,)), ((), ())), preferred_element_type=_F32)


def _layer_norm(h, g, b):
    mu = jnp.mean(h, axis=-1, keepdims=True)
    d = h - mu
    var = jnp.mean(d * d, axis=-1, keepdims=True)
    return d * lax.rsqrt(var + LN_EPS) * g + b


def _log_sigmoid(x):
    return -(jnp.maximum(-x, 0.0) + jnp.log(1.0 + jnp.exp(-jnp.abs(x))))


def _rope(a, c, s_dn, s_up):
    outs = []
    for j in range(a.shape[1] // LANES):
        slab = a[:, LANES * j:LANES * (j + 1)]
        fwd = pltpu.roll(slab, LANES - ROPE_DIM // 2, axis=1)
        bwd = pltpu.roll(slab, ROPE_DIM // 2, axis=1)
        outs.append(slab * c + fwd * s_dn + bwd * s_up)
    return outs[0] if len(outs) == 1 else jnp.concatenate(outs, axis=1)


def _gla_tile(q, k, v, log2_a, chunk, nchunk, state_in, state_out, chained, between=lambda: None):
    R = chunk * nchunk
    blk = min(R, GLA_BLOCK_ROWS)
    blocks = [slice(r0, r0 + blk) for r0 in range(0, R, blk)]
    row = lax.broadcasted_iota(jnp.int32, (blk, blk), 0)
    col = lax.broadcasted_iota(jnp.int32, (blk, blk), 1)
    tril = (row // chunk == col // chunk) & (col <= row)
    ltri = jnp.where(tril, 1.0, 0.0).astype(_BF)
    hi = _bf(log2_a)
    lo = _bf(log2_a - hi.astype(_F32))
    b = [_dot(ltri, hi[rs]) + _dot(ltri, lo[rs]) for rs in blocks]
    b = jnp.concatenate(b, axis=0) if len(b) > 1 else b[0]
    between()
    decay = [jnp.exp2(b[chunk * (c + 1) - 1:chunk * (c + 1), :]) for c in range(nchunk)]
    q_dec = _bf(q * (GLA_DK ** -0.5) * jnp.exp2(b))
    k_inv = k * jnp.exp2(-b)
    k_tail = _bf(jnp.concatenate([k_inv[chunk * c:chunk * (c + 1)] * decay[c] for c in range(nchunk)], axis=0)
                 if nchunk > 1 else k_inv * decay[0])
    k_inv = _bf(k_inv)
    vb = _bf(v)
    heads = [slice(GLA_DK * h, GLA_DK * (h + 1)) for h in range(GLA_HEADS)]
    o_intra = []
    for hs in heads:
        o = [_dot(_bf(jnp.where(tril, _dot_nt(q_dec[rs, hs], k_inv[rs, hs]), 0.0)), vb[rs, hs]) for rs in blocks]
        o_intra.append(jnp.concatenate(o, axis=0) if len(o) > 1 else o[0])
    o_inter = [[None] * nchunk for _ in heads]
    st = [None] * GLA_HEADS
    for c in range(nchunk):
        rs = slice(chunk * c, chunk * (c + 1))
        for h, hs in enumerate(heads):
            if c == 0 or not chained:
                st[h] = state_in(c, h)
            o_inter[h][c] = _dot_nt(q_dec[rs, hs], _bf(st[h]))
            st[h] = st[h] * decay[c][:, hs] + _dot_tn(vb[rs, hs], k_tail[rs, hs])
            if c == nchunk - 1 or not chained:
                state_out(c, h, st[h])
    outs = [o_intra[h] + (jnp.concatenate(o_inter[h], axis=0) if nchunk > 1 else o_inter[h][0])
            for h in range(GLA_HEADS)]
    return jnp.concatenate(outs, axis=1)


def _gla_gate(o, gr, g):
    outs = []
    for h in range(GLA_HEADS):
        hs = slice(GLA_DV * h, GLA_DV * (h + 1))
        oh = o[:, hs]
        ms = jnp.mean(oh * oh, axis=-1, keepdims=True)
        grh = gr[:, hs]
        outs.append(oh * lax.rsqrt(ms + LN_EPS) * g * (grh * jax.nn.sigmoid(grh)))
    return jnp.concatenate(outs, axis=1)


def _lane_half_masks(shape):
    lane = lax.broadcasted_iota(jnp.int32, shape, 1)
    return lane < SWA_HD, lane >= SWA_HD


def _swa_split_q(q):
    lo_m, hi_m = _lane_half_masks((q.shape[0], LANES))
    out = []
    for j in range(SWA_QW // LANES):
        slab = q[:, LANES * j:LANES * (j + 1)]
        out.append((_bf(jnp.where(lo_m, slab, 0.0)), _bf(jnp.where(hi_m, slab, 0.0))))
    return out


def _swa_dup_k(k):
    lo_m, _ = _lane_half_masks(k.shape)
    kr = pltpu.roll(k, SWA_HD, axis=1)
    return [_bf(jnp.where(lo_m, k, kr)), _bf(jnp.where(lo_m, kr, k))]


def _swa_place_v(v):
    lo_m, hi_m = _lane_half_masks(v.shape)
    vr = pltpu.roll(v, SWA_HD, axis=1)
    return [(_bf(jnp.where(lo_m, v, 0.0)), _bf(jnp.where(hi_m, vr, 0.0))),
            (_bf(jnp.where(lo_m, vr, 0.0)), _bf(jnp.where(hi_m, v, 0.0)))]


def _swa_blocks(qrows, kks, vlos, vhis, sink_col, lowers, upper, tq):
    n = len(kks)
    stack = lambda xs: jnp.concatenate(xs, axis=0) if n > 1 else xs[0]
    s = stack([_dot_nt(qrows[i], kks[i]) for i in range(n)])
    if lowers is not None:
        kcol = lax.broadcasted_iota(jnp.int32, s.shape, 1)
        s = jnp.where(kcol >= stack(lowers), jnp.where(kcol <= stack([upper] * n), s, -jnp.inf), -jnp.inf)
    sink = stack([sink_col] * n)
    m = jnp.maximum(jnp.max(s, axis=-1, keepdims=True), sink)
    p = jnp.exp(s - m)
    den = jnp.sum(p, axis=-1, keepdims=True) + jnp.exp(sink - m)
    p = _bf(p / den)
    outs = []
    for i in range(n):
        r = 4 * tq * i
        outs.append((_dot(p[r:r + tq], vlos[i]) + _dot(p[r + tq:r + 2 * tq], vhis[i]),
                     _dot(p[r + 2 * tq:r + 3 * tq], vlos[i]) + _dot(p[r + 3 * tq:r + 4 * tq], vhis[i])))
    return outs


def _sink_col(sinks_ref, g, tq):
    r = lax.broadcasted_iota(jnp.int32, (4 * tq, 1), 0)
    base = 4 * g
    return jnp.where(r < tq, sinks_ref[base],
                     jnp.where(r < 2 * tq, sinks_ref[base + 1],
                               jnp.where(r < 3 * tq, sinks_ref[base + 2], sinks_ref[base + 3])))


def _mem_head(ref, h):
    return _bf(ref[pl.ds(h, MEM_TOKENS, stride=MEM_HEADS), :])


def _mem_attention(q, mk_refs, mv_refs):
    qb = _bf(q)
    nseg = len(mk_refs)
    tq = q.shape[0] // nseg
    outs = []
    for h in range(MEM_HEADS):
        hs = slice(MEM_HD * h, MEM_HD * (h + 1))
        s = [_dot_nt(qb[tq * i:tq * (i + 1), hs], _mem_head(mk_refs[i], h)) for i in range(nseg)]
        s = jnp.concatenate(s, axis=0) if nseg > 1 else s[0]
        m = jnp.max(s, axis=-1, keepdims=True)
        p = jnp.exp(s - m)
        p = _bf(p / jnp.sum(p, axis=-1, keepdims=True))
        o = [_dot(p[tq * i:tq * (i + 1)], _mem_head(mv_refs[i], h)) for i in range(nseg)]
        outs.append(jnp.concatenate(o, axis=0) if nseg > 1 else o[0])
    return jnp.concatenate(outs, axis=1)


SWA_BLOCK_LANES = 4 * SWA_PAIR
LOG2E = float(np.log2(np.e))


def _swa_bias_t(npair, first_tile):
    nk = WINDOW + SWA_PAIR
    width = SWA_BLOCK_LANES * SWA_KV_HEADS * npair
    col = lax.broadcasted_iota(jnp.int32, (1, width), 1)
    lower = CHUNK * ((col % SWA_PAIR) // CHUNK)
    upper = lower + (WINDOW + CHUNK - 1)
    if first_tile:
        lower = jnp.maximum(lower, WINDOW - SWA_PAIR * ((col // SWA_BLOCK_LANES) % npair))
    krow = lax.broadcasted_iota(jnp.int32, (nk, width), 0)
    return jnp.where(krow >= lower, jnp.where(krow <= upper, 0.0, -jnp.inf), -jnp.inf)


def _swa_attention_t(qs, kk_scr, v_scr, sinks_ref, bias, rows, between=lambda: None):
    nk = WINDOW + SWA_PAIR
    npair = rows // SWA_PAIR
    bw = SWA_BLOCK_LANES
    blocks = [(g, p) for g in range(SWA_KV_HEADS) for p in range(npair)]
    sts = []
    for g, p in blocks:
        rs = slice(SWA_PAIR * p, SWA_PAIR * (p + 1))
        qrows = jnp.concatenate([qs[2 * g][0][rs], qs[2 * g][1][rs],
                                 qs[2 * g + 1][0][rs], qs[2 * g + 1][1][rs]], axis=0)
        sts.append(_dot_nt(kk_scr[g, SWA_PAIR * p:SWA_PAIR * p + nk, :], qrows))
    st = jnp.concatenate(sts, axis=1) + bias
    between()
    width = bw * len(blocks)
    col = lax.broadcasted_iota(jnp.int32, (1, width), 1)
    head = 4 * (col // (bw * npair)) + (col % bw) // SWA_PAIR
    sink = jnp.zeros((1, width), _F32)
    for h in range(SWA_HEADS):
        sink = jnp.where(head == h, sinks_ref[h] * LOG2E, sink)
    m = jnp.maximum(jnp.max(st, axis=0, keepdims=True), sink)
    pt = jnp.exp2(st - m)
    inv = 1.0 / (jnp.sum(pt, axis=0, keepdims=True) + jnp.exp2(sink - m))
    pb = _bf(pt)
    pieces = [[None] * npair for _ in range(SWA_HEADS)]
    for i, (g, p) in enumerate(blocks):
        cs = slice(bw * i, bw * (i + 1))
        o2 = _dot_tn(v_scr[SWA_PAIR * p:SWA_PAIR * p + nk, :], pb[:, cs]) * inv[:, cs]
        for hh in range(4):
            pieces[4 * g + hh][p] = o2[SWA_HD * g:SWA_HD * (g + 1), SWA_PAIR * hh:SWA_PAIR * (hh + 1)]
    return jnp.concatenate([jnp.concatenate(ps, axis=1) if npair > 1 else ps[0] for ps in pieces], axis=0)


def _mem_attention_t(q, mk_ref, mv_ref, between=lambda: None):
    qb = _bf(q)
    R = q.shape[0]
    st = jnp.concatenate([_dot_nt(_mem_head(mk_ref, h), qb[:, MEM_HD * h:MEM_HD * (h + 1)])
                          for h in range(MEM_HEADS)], axis=1)
    between()
    pt = jnp.exp2(st - jnp.max(st, axis=0, keepdims=True))
    inv = 1.0 / jnp.sum(pt, axis=0, keepdims=True)
    pb = _bf(pt)
    outs = [_dot_tn(_mem_head(mv_ref, h), pb[:, R * h:R * (h + 1)]) * inv[:, R * h:R * (h + 1)]
            for h in range(MEM_HEADS)]
    return jnp.concatenate(outs, axis=0)


def _gate(xb, i, w_inb_ref, bgate_ref):
    gl = _dot(xb, w_inb_ref[:, OFF_GL + D_MODEL * i:OFF_GL + D_MODEL * (i + 1)])
    return jax.nn.sigmoid(gl + bgate_ref[:, D_MODEL * i:D_MODEL * (i + 1)])


def _merge_ln1(x, gated_sum, wo_ref, g_ref, b_ref, alpha):
    mix = _dot(_bf(gated_sum), wo_ref[...])
    return _layer_norm(alpha * x + mix, g_ref[...], b_ref[...])


def _forget_log2(xb, w_ga_ref, w2_ref, ba_ref):
    ga = _dot(xb, w_ga_ref[...])
    return _log_sigmoid(_dot(_bf(ga), w2_ref[...]) + ba_ref[...]) * (LOG2E / GLA_TAU)


def _prompt_mixer_kernel(sinks_ref, x_ref, rc_ref, rdn_ref, rup_ref, mk_ref, mv_ref, w_ina_ref, w_ga_ref,
                         w_inb_ref, w2_ref, ba_ref, gng_ref, wbg_ref, wbs_ref, wbm_ref, bgate_ref, wo_ref,
                         ln_g_ref, ln_b_ref,
                         out_ref, gla_out_ref, swak_out_ref, swav_out_ref,
                         s_scr, kk_scr, vv_scr, bias_scr, *, rows, alpha):
    t = pl.program_id(1)
    nchunk = rows // CHUNK

    @pl.when(t == 0)
    def _():
        s_scr[...] = jnp.zeros_like(s_scr)
        kk_scr[:, 0:WINDOW, :] = jnp.zeros((SWA_KV_HEADS, WINDOW, LANES), _BF)
        vv_scr[0:WINDOW, :] = jnp.zeros((WINDOW, LANES), _BF)
        bias_scr[0] = _swa_bias_t(rows // SWA_PAIR, first_tile=False)
        bias_scr[1] = _swa_bias_t(rows // SWA_PAIR, first_tile=True)

    x = x_ref[0]
    xb = _bf(x)

    log_a = _forget_log2(xb, w_ga_ref, w2_ref, ba_ref)

    def proj_a(off):
        return _dot(xb, w_ina_ref[:, off:off + GLA_W])

    def state_in(c, h):
        return s_scr[h]

    def state_out(c, h, st):
        s_scr[h] = st
        gla_out_ref[0, h] = jnp.transpose(st)

    gq, gk, gv = proj_a(OFF_GQ), proj_a(OFF_GK), proj_a(OFF_GV)
    gate_a = _gate(xb, 0, w_inb_ref, bgate_ref)
    late = {}

    def dense_a():
        late["gr"] = proj_a(OFF_GR)
        late["sq"] = _dot(xb, w_inb_ref[:, OFF_SQ:OFF_SQ + SWA_QW])
        late["sk"] = _dot(xb, w_inb_ref[:, OFF_SK:OFF_SK + SWA_KW])
        late["sv"] = _dot(xb, w_inb_ref[:, OFF_SV:OFF_SV + SWA_KW])

    def dense_b():
        late["gate_b"] = _gate(xb, 1, w_inb_ref, bgate_ref)

    def dense_c():
        late["gate_c"] = _gate(xb, 2, w_inb_ref, bgate_ref)

    o_a = _gla_tile(gq, gk, gv, log_a, CHUNK, nchunk, state_in, state_out, True, dense_a)
    o_a = _gla_gate(o_a, late["gr"], gng_ref[...])
    gated = gate_a * _dot(_bf(o_a), wbg_ref[...])

    rc, rdn, rup = rc_ref[...], rdn_ref[...], rup_ref[...]
    q = _rope(late["sq"], rc, rdn, rup) * (SWA_HD ** -0.5 * LOG2E)
    k = _rope(late["sk"], rc, rdn, rup)
    v = late["sv"]
    swak_out_ref[0] = k[rows - WINDOW:rows, :]
    swav_out_ref[0] = v[rows - WINDOW:rows, :]
    kk = _swa_dup_k(k)
    vb = _bf(v)
    for g in range(SWA_KV_HEADS):
        kk_scr[g, WINDOW:WINDOW + rows, :] = kk[g]
    vv_scr[WINDOW:WINDOW + rows, :] = vb
    o_bt = _swa_attention_t(_swa_split_q(q), kk_scr, vv_scr, sinks_ref, bias_scr[jnp.where(t == 0, 1, 0)], rows,
                            dense_b)
    gated = gated + late["gate_b"] * _dot_tn(_bf(o_bt), wbs_ref[...])
    for g in range(SWA_KV_HEADS):
        kk_scr[g, 0:WINDOW, :] = kk[g][rows - WINDOW:rows]
    vv_scr[0:WINDOW, :] = vb[rows - WINDOW:rows]

    qm = _dot(xb, w_inb_ref[:, OFF_MQ:OFF_MQ + MEM_W]) * (MEM_HD ** -0.5 * LOG2E)
    o_ct = _mem_attention_t(qm, mk_ref.at[0], mv_ref.at[0], dense_c)
    gated = gated + late["gate_c"] * _dot_tn(_bf(o_ct), wbm_ref[...])

    out_ref[0] = _merge_ln1(x, gated, wo_ref, ln_g_ref, ln_b_ref, alpha)


def _sample_mixer_kernel(sinks_ref, x_ref, rc_ref, rdn_ref, rup_ref, mk_ref, mv_ref, gla_in_ref,
                         swak_in_ref, swav_in_ref, w_ina_ref, w_ga_ref, w_inb_ref, w2_ref,
                         ba_ref, gng_ref, wbg_ref, wbs_ref, wbm_ref, bgate_ref, wo_ref, ln_g_ref, ln_b_ref,
                         out_ref, gla_out_ref, swak_out_ref, swav_out_ref, *, nb, tq, alpha):
    rows = nb * tq
    x = x_ref[...].reshape(rows, D_MODEL)
    xb = _bf(x)

    log_a = _forget_log2(xb, w_ga_ref, w2_ref, ba_ref)

    def proj_a(off):
        return _dot(xb, w_ina_ref[:, off:off + GLA_W])

    def state_in(c, h):
        return jnp.transpose(gla_in_ref[c, h])

    def state_out(c, h, st):
        gla_out_ref[c, h] = jnp.transpose(st)

    gq, gk, gv = proj_a(OFF_GQ), proj_a(OFF_GK), proj_a(OFF_GV)
    gate_a = _gate(xb, 0, w_inb_ref, bgate_ref)
    o_a = _gla_tile(gq, gk, gv, log_a, tq, nb, state_in, state_out, chained=False)
    o_a = _gla_gate(o_a, proj_a(OFF_GR), gng_ref[...])
    gated = gate_a * _dot(_bf(o_a), wbg_ref[...])

    rc = jnp.concatenate([rc_ref[...]] * nb, axis=0)
    rdn = jnp.concatenate([rdn_ref[...]] * nb, axis=0)
    rup = jnp.concatenate([rup_ref[...]] * nb, axis=0)
    q = _rope(_dot(xb, w_inb_ref[:, OFF_SQ:OFF_SQ + SWA_QW]), rc, rdn, rup) * (SWA_HD ** -0.5)
    k = _rope(_dot(xb, w_inb_ref[:, OFF_SK:OFF_SK + SWA_KW]), rc, rdn, rup)
    v = _dot(xb, w_inb_ref[:, OFF_SV:OFF_SV + SWA_KW])
    gate_b = _gate(xb, 1, w_inb_ref, bgate_ref)
    qs = _swa_split_q(q)
    slabs = [[None] * nb for _ in range(SWA_QW // LANES)]
    kks, vvs = [], []
    for bi in range(nb):
        rs = slice(tq * bi, tq * (bi + 1))
        k_all = jnp.concatenate([swak_in_ref[bi], k[rs]], axis=0)
        v_all = jnp.concatenate([swav_in_ref[bi], v[rs]], axis=0)
        swak_out_ref[bi] = k_all[tq:tq + WINDOW]
        swav_out_ref[bi] = v_all[tq:tq + WINDOW]
        kks.append(_swa_dup_k(k_all))
        vvs.append(_swa_place_v(v_all))
    for g in range(SWA_KV_HEADS):
        qrows = []
        for bi in range(nb):
            rs = slice(tq * bi, tq * (bi + 1))
            qrows.append(jnp.concatenate([qs[2 * g][0][rs], qs[2 * g][1][rs],
                                          qs[2 * g + 1][0][rs], qs[2 * g + 1][1][rs]], axis=0))
        outs = _swa_blocks(qrows, [kk[g] for kk in kks], [vv[g][0] for vv in vvs], [vv[g][1] for vv in vvs],
                           _sink_col(sinks_ref, g, tq), None, None, tq)
        for bi in range(nb):
            slabs[2 * g][bi], slabs[2 * g + 1][bi] = outs[bi]
    o_b = jnp.concatenate([jnp.concatenate(s, axis=0) if nb > 1 else s[0] for s in slabs], axis=1)
    gated = gated + gate_b * _dot(_bf(o_b), wbs_ref[...])

    qm = _dot(xb, w_inb_ref[:, OFF_MQ:OFF_MQ + MEM_W]) * (MEM_HD ** -0.5)
    gate_c = _gate(xb, 2, w_inb_ref, bgate_ref)
    o_c = _mem_attention(qm, [mk_ref.at[bi] for bi in range(nb)], [mv_ref.at[bi] for bi in range(nb)])
    gated = gated + gate_c * _dot(_bf(o_c), wbm_ref[...])

    out = _merge_ln1(x, gated, wo_ref, ln_g_ref, ln_b_ref, alpha)
    out_ref[...] = out.reshape(nb, tq, D_MODEL)


_GELU_K1 = -2.0 * float(np.log2(np.e)) * float(np.sqrt(2.0 / np.pi))
_GELU_K2 = _GELU_K1 * 0.044715


def _conv(u2, u1, u0, cw_ref, cb_ref, cs):
    return cb_ref[:, cs] + u2 * cw_ref[0:1, cs] + u1 * cw_ref[1:2, cs] + u0 * cw_ref[2:3, cs]


def _geglu(g, v):
    return g * v / (1.0 + jnp.exp2(g * (g * g * _GELU_K2 + _GELU_K1)))


def _prompt_ffn_kernel(x_ref, wup_ref, cw_ref, cb_ref, wdn_ref, ln_g_ref, ln_b_ref,
                       out_ref, conv_out_ref, u_scr, *, rows, alpha):
    nj = rows // SUBLANES
    base = 2 * SUBLANES
    last = slice(base + rows - SUBLANES, base + rows)
    last2 = slice(base + rows - 2 * SUBLANES, base + rows - SUBLANES)

    @pl.when(pl.program_id(1) == 0)
    def _():
        u_scr[base + rows - 2 * SUBLANES:base + rows, :] = jnp.zeros((2 * SUBLANES, 2 * D_FF), _F32)

    prev_last, prev_last2 = u_scr[last, :], u_scr[last2, :]
    x = jnp.swapaxes(x_ref[0].reshape(SUBLANES, nj, D_MODEL), 0, 1).reshape(rows, D_MODEL)
    u_scr[base:base + rows, :] = _dot(_bf(x), wup_ref[...])
    sub = lax.broadcasted_iota(jnp.int32, (SUBLANES, FFN_COLS), 0)

    def conv_cols(cs):
        wrap = lambda prev, cur: pltpu.roll(jnp.where(sub == SUBLANES - 1, prev, cur), 1, axis=0)
        u_scr[SUBLANES:base, cs] = wrap(prev_last[:, cs], u_scr[last, cs])
        u_scr[0:SUBLANES, cs] = wrap(prev_last2[:, cs], u_scr[last2, cs])
        return _conv(u_scr[0:rows, cs], u_scr[SUBLANES:SUBLANES + rows, cs], u_scr[base:base + rows, cs],
                     cw_ref, cb_ref, cs)

    h = [_bf(_geglu(conv_cols(slice(c0, c0 + FFN_COLS)), conv_cols(slice(D_FF + c0, D_FF + c0 + FFN_COLS))))
         for c0 in range(0, D_FF, FFN_COLS)]
    conv_out_ref[0] = jnp.concatenate([u_scr[base + rows - SUBLANES - 1:base + rows - SUBLANES, :],
                                       u_scr[base + rows - 1:base + rows, :]], axis=0)
    f = _dot(jnp.concatenate(h, axis=1), wdn_ref[...])
    y = _layer_norm(alpha * x + f, ln_g_ref[...], ln_b_ref[...])
    out_ref[0] = jnp.swapaxes(y.reshape(nj, SUBLANES, D_MODEL), 0, 1).reshape(rows, D_MODEL)


def _sample_ffn_kernel(x_ref, hist_ref, wup_ref, cw_ref, cb_ref, wdn_ref, ln_g_ref, ln_b_ref,
                       out_ref, conv_out_ref, u_scr, *, nb, tq, alpha):
    rows = nb * tq
    base = (CONV_W - 1) * nb
    x = jnp.swapaxes(x_ref[...], 0, 1).reshape(rows, D_MODEL)
    u_scr[0:base, :] = jnp.swapaxes(hist_ref[...], 0, 1).reshape(base, 2 * D_FF)
    u_scr[base:base + rows, :] = _dot(_bf(x), wup_ref[...])
    conv = lambda cs: _conv(u_scr[0:rows, cs], u_scr[nb:nb + rows, cs], u_scr[base:base + rows, cs],
                            cw_ref, cb_ref, cs)
    h = _geglu(conv(slice(0, D_FF)), conv(slice(D_FF, 2 * D_FF)))
    conv_out_ref[...] = jnp.swapaxes(u_scr[rows:rows + base, :].reshape(CONV_W - 1, nb, 2 * D_FF), 0, 1)
    f = _dot(_bf(h), wdn_ref[...])
    y = _layer_norm(alpha * x + f, ln_g_ref[...], ln_b_ref[...])
    out_ref[...] = jnp.swapaxes(y.reshape(tq, nb, D_MODEL), 0, 1)


def _mem_kv_kernel(m_ref, w_ref, k_ref, v_ref):
    kv = _dot(_bf(m_ref[0]), w_ref[...])
    for h in range(MEM_HEADS):
        k_ref[0, pl.ds(h, MEM_TOKENS, stride=MEM_HEADS), :] = kv[:, MEM_HD * h:MEM_HD * (h + 1)]
        v_ref[0, pl.ds(h, MEM_TOKENS, stride=MEM_HEADS), :] = kv[:, MEM_W + MEM_HD * h:MEM_W + MEM_HD * (h + 1)]


def _const_spec(shape):
    nd = len(shape)
    return pl.BlockSpec(shape, lambda *_: (0,) * nd, pipeline_mode=pl.Buffered(1))


def _rope_tables(pos):
    half = ROPE_DIM // 2
    inv = np.float32(ROPE_THETA) ** (-np.arange(half, dtype=np.float32) / np.float32(half))
    ang = pos.astype(np.float32)[:, None] * inv[None, :]
    cos, sin = np.cos(ang), np.sin(ang)
    T = pos.shape[0]
    ones = np.ones((T, SWA_HD - ROPE_DIM), np.float32)
    zeros = np.zeros((T, SWA_HD - ROPE_DIM), np.float32)
    zh = np.zeros((T, half), np.float32)
    c = np.concatenate([cos, cos, ones], axis=1)
    dn = np.concatenate([-sin, zh, zeros], axis=1)
    up = np.concatenate([zh, sin, zeros], axis=1)
    rep = LANES // SWA_HD
    return tuple(jnp.asarray(np.concatenate([a] * rep, axis=1), dtype=_F32) for a in (c, dn, up))


def _pack_cols_kernel(wt_ref, o_ref, *, keep):
    t = jnp.transpose(wt_ref[...])
    if keep < t.shape[1]:
        t = jnp.where(lax.broadcasted_iota(jnp.int32, t.shape, 1) < keep, t, 0.0)
    o_ref[...] = _bf(t)


def _pack_cols(wt, col0, ncols, keep, name):
    blk = ncols // 2 if ncols % (2 * LANES) == 0 else ncols
    k = wt.shape[1]
    return pl.pallas_call(
        functools.partial(_pack_cols_kernel, keep=keep),
        grid=(ncols // blk,),
        in_specs=[pl.BlockSpec((pl.Element(blk), pl.Element(k)),
                               lambda i: (pl.multiple_of(col0 + blk * i, SUBLANES), 0))],
        out_specs=pl.BlockSpec((k, blk), lambda i: (0, i)),
        out_shape=jax.ShapeDtypeStruct((k, ncols), _BF),
        compiler_params=_params(("arbitrary",)),
        name=name,
    )(wt)


def _split_w_in(w_in):
    wt = jnp.transpose(w_in)
    return (_pack_cols(wt, 0, IN_A_COLS, IN_A_COLS, "pack_w_in_a"),
            _pack_cols(wt, IN_A_COLS, LANES, GLA_RANK, "pack_w_in_ga"),
            _pack_cols(wt, IN_A_COLS + GLA_RANK, IN_B_COLS, IN_B_COLS, "pack_w_in_b"))


def _mixer_weight_specs():
    return [
        _const_spec((D_MODEL, IN_A_COLS)),
        _const_spec((D_MODEL, LANES)),
        _const_spec((D_MODEL, IN_B_COLS)),
        _const_spec((LANES, GLA_W)),
        _const_spec((1, GLA_W)),
        _const_spec((1, GLA_DV)),
        _const_spec((GLA_W, D_MODEL)),
        _const_spec((SWA_QW, D_MODEL)),
        _const_spec((MEM_W, D_MODEL)),
        _const_spec((1, GATE_W)),
        _const_spec((D_MODEL, D_MODEL)),
        _const_spec((1, D_MODEL)),
        _const_spec((1, D_MODEL)),
    ]


def _ffn_weight_specs():
    return [
        _const_spec((D_MODEL, 2 * D_FF)),
        _const_spec((CONV_W, 2 * D_FF)),
        _const_spec((1, 2 * D_FF)),
        _const_spec((D_FF, D_MODEL)),
        _const_spec((1, D_MODEL)),
        _const_spec((1, D_MODEL)),
    ]


_SMEM_SPEC = pl.BlockSpec(memory_space=pltpu.SMEM)
_MEM_ROWS = MEM_TOKENS * MEM_HEADS


def _params(sem, flags=None):
    return pltpu.CompilerParams(dimension_semantics=sem, vmem_limit_bytes=V7X_VMEM_LIMIT, flags=flags)


def _prompt_layer(x, mem, sinks, mixer_w, ffn_w, w_mem_kv, alpha):
    B, T, _ = x.shape
    rows = min(PROMPT_ROWS, T)
    nt = T // rows
    assert T % rows == 0 and rows % SWA_PAIR == 0 and rows >= WINDOW
    f32 = jnp.float32
    mk, mv = pl.pallas_call(
        _mem_kv_kernel,
        grid=(B,),
        in_specs=[pl.BlockSpec((1, MEM_TOKENS, D_MODEL), lambda b: (b, 0, 0)),
                  _const_spec((D_MODEL, 2 * MEM_W))],
        out_specs=[pl.BlockSpec((1, _MEM_ROWS, MEM_HD), lambda b: (b, 0, 0))] * 2,
        out_shape=[jax.ShapeDtypeStruct((B, _MEM_ROWS, MEM_HD), f32)] * 2,
        compiler_params=_params(("arbitrary",)),
        name="mem_kv",
    )(mem, w_mem_kv)

    rope = _rope_tables(np.arange(T))
    rope_spec = pl.BlockSpec((rows, LANES), lambda b, t: (t, 0))
    per_batch = lambda *blk: pl.BlockSpec((1,) + blk, lambda b, t: (b,) + (0,) * len(blk))
    x1, gla, swak, swav = pl.pallas_call(
        functools.partial(_prompt_mixer_kernel, rows=rows, alpha=alpha),
        grid=(B, nt),
        in_specs=[_SMEM_SPEC,
                  pl.BlockSpec((1, rows, D_MODEL), lambda b, t: (b, t, 0)),
                  rope_spec, rope_spec, rope_spec,
                  per_batch(_MEM_ROWS, MEM_HD), per_batch(_MEM_ROWS, MEM_HD)] + _mixer_weight_specs(),
        out_specs=[pl.BlockSpec((1, rows, D_MODEL), lambda b, t: (b, t, 0)),
                   per_batch(GLA_HEADS, GLA_DK, GLA_DV),
                   per_batch(WINDOW, SWA_KW), per_batch(WINDOW, SWA_KW)],
        out_shape=[jax.ShapeDtypeStruct((B, T, D_MODEL), f32),
                   jax.ShapeDtypeStruct((B, GLA_HEADS, GLA_DK, GLA_DV), f32),
                   jax.ShapeDtypeStruct((B, WINDOW, SWA_KW), f32),
                   jax.ShapeDtypeStruct((B, WINDOW, SWA_KW), f32)],
        scratch_shapes=[pltpu.VMEM((GLA_HEADS, GLA_DK, GLA_DV), f32),
                        pltpu.VMEM((SWA_KV_HEADS, WINDOW + rows, LANES), _BF),
                        pltpu.VMEM((WINDOW + rows, LANES), _BF),
                        pltpu.VMEM((2, WINDOW + SWA_PAIR, SWA_BLOCK_LANES * SWA_KV_HEADS * (rows // SWA_PAIR)),
                                   f32)],
        compiler_params=_params(("arbitrary", "arbitrary")),
        name="prompt_mixer",
    )(sinks, x, *rope, mk, mv, *mixer_w)

    frows = min(PROMPT_FFN_ROWS, T)
    assert T % frows == 0 and frows >= CONV_W - 1
    y, conv = pl.pallas_call(
        functools.partial(_prompt_ffn_kernel, rows=frows, alpha=alpha),
        grid=(B, T // frows),
        in_specs=[pl.BlockSpec((1, frows, D_MODEL), lambda b, t: (b, t, 0))] + _ffn_weight_specs(),
        out_specs=[pl.BlockSpec((1, frows, D_MODEL), lambda b, t: (b, t, 0)),
                   per_batch(CONV_W - 1, 2 * D_FF)],
        out_shape=[jax.ShapeDtypeStruct((B, T, D_MODEL), f32),
                   jax.ShapeDtypeStruct((B, CONV_W - 1, 2 * D_FF), f32)],
        scratch_shapes=[pltpu.VMEM((2 * SUBLANES + frows, 2 * D_FF), f32)],
        compiler_params=_params(("arbitrary", "arbitrary")),
        name="prompt_ffn",
    )(x1, *ffn_w)
    return y, gla, swak, swav, mk, mv, conv


def _sample_layer(x, gla0, swak0, swav0, memk, memv, conv0, sinks, mixer_w, ffn_w, alpha):
    B, tq, _ = x.shape
    nb = max(1, min(B, SAMPLE_ROWS // tq))
    assert B % nb == 0 and tq % 16 == 0 and tq >= CONV_W - 1
    f32 = jnp.float32
    rope = _rope_tables(PAST_LEN + np.arange(tq))
    rope_spec = pl.BlockSpec((tq, LANES), lambda i: (0, 0))
    blk = lambda *s: pl.BlockSpec((nb,) + s, lambda i: (i,) + (0,) * len(s))
    blk1 = lambda *s: pl.BlockSpec((nb,) + s, lambda i: (i,) + (0,) * len(s), pipeline_mode=pl.Buffered(1))
    x1, gla, swak, swav = pl.pallas_call(
        functools.partial(_sample_mixer_kernel, nb=nb, tq=tq, alpha=alpha),
        grid=(B // nb,),
        in_specs=[_SMEM_SPEC, blk(tq, D_MODEL), rope_spec, rope_spec, rope_spec,
                  blk(_MEM_ROWS, MEM_HD), blk(_MEM_ROWS, MEM_HD),
                  blk1(GLA_HEADS, GLA_DK, GLA_DV), blk1(WINDOW, SWA_KW), blk1(WINDOW, SWA_KW)]
                 + _mixer_weight_specs(),
        out_specs=[blk(tq, D_MODEL), blk(GLA_HEADS, GLA_DK, GLA_DV), blk(WINDOW, SWA_KW), blk(WINDOW, SWA_KW)],
        out_shape=[jax.ShapeDtypeStruct((B, tq, D_MODEL), f32),
                   jax.ShapeDtypeStruct((B, GLA_HEADS, GLA_DK, GLA_DV), f32),
                   jax.ShapeDtypeStruct((B, WINDOW, SWA_KW), f32),
                   jax.ShapeDtypeStruct((B, WINDOW, SWA_KW), f32)],
        compiler_params=_params(("arbitrary",)),
        name="sample_mixer",
    )(sinks, x, *rope, memk, memv, gla0, swak0, swav0, *mixer_w)

    assert B % SUBLANES == 0
    fblk = lambda *s: pl.BlockSpec((SUBLANES,) + s, lambda i: (i,) + (0,) * len(s))
    y, conv = pl.pallas_call(
        functools.partial(_sample_ffn_kernel, nb=SUBLANES, tq=tq, alpha=alpha),
        grid=(B // SUBLANES,),
        in_specs=[fblk(tq, D_MODEL), fblk(CONV_W - 1, 2 * D_FF)] + _ffn_weight_specs(),
        out_specs=[fblk(tq, D_MODEL), fblk(CONV_W - 1, 2 * D_FF)],
        out_shape=[jax.ShapeDtypeStruct((B, tq, D_MODEL), f32),
                   jax.ShapeDtypeStruct((B, CONV_W - 1, 2 * D_FF), f32)],
        scratch_shapes=[pltpu.VMEM(((CONV_W - 1 + tq) * SUBLANES, 2 * D_FF), f32)],
        compiler_params=_params(("arbitrary",)),
        name="sample_ffn",
    )(x1, conv0, *ffn_w)
    return y, gla, swak, swav, conv


def kernel(x_prompt, x_sample, cache_swa_k, cache_swa_v, state_gla, cache_mem_k, cache_mem_v, cache_ffn_conv, mem_prompt, ln1_g, ln1_b, ln2_g, ln2_b, w_in, b_gate, w_gla_a2, b_gla_a, gla_norm_g, swa_sinks, w_mem_kv, w_br_gla, w_br_swa, w_br_mem, w_o, w_up, conv_w, conv_b, w_down):
    depth = w_in.shape[0]
    alpha = float((2 * depth) ** 0.25)
    Bp = x_prompt.shape[0]
    Bs = x_sample.shape[0]
    hp, hs = x_prompt, x_sample
    outs = [[] for _ in range(10)]
    row = lambda a: a.reshape(1, -1)
    for l in range(depth):
        w2 = jnp.pad(w_gla_a2[l], ((0, LANES - GLA_RANK), (0, 0)))
        mixer_w = _split_w_in(w_in[l]) + (
            _bf(w2), row(b_gla_a[l]), row(gla_norm_g[l]),
            _bf(w_br_gla[l]), _bf(w_br_swa[l]), _bf(w_br_mem[l]), row(b_gate[l]), _bf(w_o[l]),
            row(ln1_g[l]), row(ln1_b[l]))
        ffn_w = (_bf(w_up[l]), conv_w[l], row(conv_b[l]), _bf(w_down[l]), row(ln2_g[l]), row(ln2_b[l]))
        sinks = swa_sinks[l]
        hp, g_p, k_p, v_p, mk_p, mv_p, c_p = _prompt_layer(
            hp, mem_prompt, sinks, mixer_w, ffn_w, _bf(w_mem_kv[l]), alpha)
        hs, g_s, k_s, v_s, c_s = _sample_layer(
            hs, state_gla[l],
            cache_swa_k[l].reshape(Bs, WINDOW, SWA_KW), cache_swa_v[l].reshape(Bs, WINDOW, SWA_KW),
            cache_mem_k[l].reshape(Bs, _MEM_ROWS, MEM_HD), cache_mem_v[l].reshape(Bs, _MEM_ROWS, MEM_HD),
            cache_ffn_conv[l], sinks, mixer_w, ffn_w, alpha)
        kv5 = lambda a, b: a.reshape(b, WINDOW, SWA_KV_HEADS, SWA_HD)
        m5 = lambda a: a.reshape(Bp, MEM_TOKENS, MEM_HEADS, MEM_HD)
        for lst, val in zip(outs, (kv5(k_p, Bp), kv5(v_p, Bp), g_p, m5(mk_p), m5(mv_p), c_p,
                                   kv5(k_s, Bs), kv5(v_s, Bs), g_s, c_s)):
            lst.append(val)
    return (hp, hs) + tuple(jnp.stack(o) for o in outs)
```

```python
import functools

import jax
import jax.numpy as jnp
import numpy as np
from jax import lax
from jax.experimental import pallas as pl
from jax.experimental.pallas import tpu as pltpu

D_MODEL = 1024
CHUNK = 64
GLA_HEADS = 4
GLA_DK = 128
GLA_DV = 128
GLA_RANK = 16
GLA_TAU = 16.0
SWA_HEADS = 8
SWA_KV_HEADS = 2
SWA_HD = 64
WINDOW = 128
ROPE_DIM = 16
ROPE_THETA = 500000.0
MEM_TOKENS = 256
MEM_HEADS = 4
MEM_HD = 128
D_FF = 2816
CONV_W = 3
N_BRANCH = 3
PAST_LEN = 2048
LN_EPS = 1e-5

LANES = 128
SUBLANES = 8
V7X_VMEM_LIMIT = 56 * 1024 * 1024
PROMPT_ROWS = 512
PROMPT_FFN_ROWS = 512
FFN_COLS = 256
GLA_BLOCK_ROWS = 256
SAMPLE_ROWS = 256
GLA_W = GLA_HEADS * GLA_DK
SWA_QW = SWA_HEADS * SWA_HD
SWA_KW = SWA_KV_HEADS * SWA_HD
MEM_W = MEM_HEADS * MEM_HD
GATE_W = N_BRANCH * D_MODEL
SWA_PAIR = 2 * CHUNK

OFF_GQ = 0
OFF_GK = OFF_GQ + GLA_W
OFF_GV = OFF_GK + GLA_W
OFF_GR = OFF_GV + GLA_W
IN_A_COLS = OFF_GR + GLA_W
OFF_SQ = 0
OFF_SK = OFF_SQ + SWA_QW
OFF_SV = OFF_SK + SWA_KW
OFF_MQ = OFF_SV + SWA_KW
OFF_GL = OFF_MQ + MEM_W
IN_B_COLS = OFF_GL + GATE_W

_BF = jnp.bfloat16
_F32 = jnp.float32


def _bf(x):
    return x.astype(_BF)


def _dot(a, b):
    return jnp.dot(a, b, preferred_element_type=_F32)


def _dot_nt(a, b):
    return lax.dot_general(a, b, (((1,), (1,)), ((), ())), preferred_element_type=_F32)


def _dot_tn(a, b):
    return lax.dot_general(a, b, (((0,), (0,)), ((), ())), preferred_element_type=_F32)


def _layer_norm(h, g, b):
    mu = jnp.mean(h, axis=-1, keepdims=True)
    d = h - mu
    var = jnp.mean(d * d, axis=-1, keepdims=True)
    return d * lax.rsqrt(var + LN_EPS) * g + b


def _log_sigmoid(x):
    return -(jnp.maximum(-x, 0.0) + jnp.log(1.0 + jnp.exp(-jnp.abs(x))))


def _rope(a, c, s_dn, s_up):
    outs = []
    for j in range(a.shape[1] // LANES):
        slab = a[:, LANES * j:LANES * (j + 1)]
        fwd = pltpu.roll(slab, LANES - ROPE_DIM // 2, axis=1)
        bwd = pltpu.roll(slab, ROPE_DIM // 2, axis=1)
        outs.append(slab * c + fwd * s_dn + bwd * s_up)
    return outs[0] if len(outs) == 1 else jnp.concatenate(outs, axis=1)


def _gla_tile(q, k, v, log2_a, chunk, nchunk, state_in, state_out, chained, between=lambda: None):
    R = chunk * nchunk
    blk = min(R, GLA_BLOCK_ROWS)
    blocks = [slice(r0, r0 + blk) for r0 in range(0, R, blk)]
    row = lax.broadcasted_iota(jnp.int32, (blk, blk), 0)
    col = lax.broadcasted_iota(jnp.int32, (blk, blk), 1)
    tril = (row // chunk == col // chunk) & (col <= row)
    ltri = jnp.where(tril, 1.0, 0.0).astype(_BF)
    hi = _bf(log2_a)
    lo = _bf(log2_a - hi.astype(_F32))
    b = [_dot(ltri, hi[rs]) + _dot(ltri, lo[rs]) for rs in blocks]
    b = jnp.concatenate(b, axis=0) if len(b) > 1 else b[0]
    between()
    decay = [jnp.exp2(b[chunk * (c + 1) - 1:chunk * (c + 1), :]) for c in range(nchunk)]
    q_dec = _bf(q * (GLA_DK ** -0.5) * jnp.exp2(b))
    k_inv = k * jnp.exp2(-b)
    k_tail = _bf(jnp.concatenate([k_inv[chunk * c:chunk * (c + 1)] * decay[c] for c in range(nchunk)], axis=0)
                 if nchunk > 1 else k_inv * decay[0])
    k_inv = _bf(k_inv)
    vb = _bf(v)
    heads = [slice(GLA_DK * h, GLA_DK * (h + 1)) for h in range(GLA_HEADS)]
    o_intra = []
    for hs in heads:
        o = [_dot(_bf(jnp.where(tril, _dot_nt(q_dec[rs, hs], k_inv[rs, hs]), 0.0)), vb[rs, hs]) for rs in blocks]
        o_intra.append(jnp.concatenate(o, axis=0) if len(o) > 1 else o[0])
    o_inter = [[None] * nchunk for _ in heads]
    st = [None] * GLA_HEADS
    for c in range(nchunk):
        rs = slice(chunk * c, chunk * (c + 1))
        for h, hs in enumerate(heads):
            if c == 0 or not chained:
                st[h] = state_in(c, h)
            o_inter[h][c] = _dot_nt(q_dec[rs, hs], _bf(st[h]))
            st[h] = st[h] * decay[c][:, hs] + _dot_tn(vb[rs, hs], k_tail[rs, hs])
            if c == nchunk - 1 or not chained:
                state_out(c, h, st[h])
    outs = [o_intra[h] + (jnp.concatenate(o_inter[h], axis=0) if nchunk > 1 else o_inter[h][0])
            for h in range(GLA_HEADS)]
    return jnp.concatenate(outs, axis=1)


def _gla_gate(o, gr, g):
    outs = []
    for h in range(GLA_HEADS):
        hs = slice(GLA_DV * h, GLA_DV * (h + 1))
        oh = o[:, hs]
        ms = jnp.mean(oh * oh, axis=-1, keepdims=True)
        grh = gr[:, hs]
        outs.append(oh * lax.rsqrt(ms + LN_EPS) * g * (grh * jax.nn.sigmoid(grh)))
    return jnp.concatenate(outs, axis=1)


def _lane_half_masks(shape):
    lane = lax.broadcasted_iota(jnp.int32, shape, 1)
    return lane < SWA_HD, lane >= SWA_HD


def _swa_split_q(q):
    lo_m, hi_m = _lane_half_masks((q.shape[0], LANES))
    out = []
    for j in range(SWA_QW // LANES):
        slab = q[:, LANES * j:LANES * (j + 1)]
        out.append((_bf(jnp.where(lo_m, slab, 0.0)), _bf(jnp.where(hi_m, slab, 0.0))))
    return out


def _swa_dup_k(k):
    lo_m, _ = _lane_half_masks(k.shape)
    kr = pltpu.roll(k, SWA_HD, axis=1)
    return [_bf(jnp.where(lo_m, k, kr)), _bf(jnp.where(lo_m, kr, k))]


def _swa_place_v(v):
    lo_m, hi_m = _lane_half_masks(v.shape)
    vr = pltpu.roll(v, SWA_HD, axis=1)
    return [(_bf(jnp.where(lo_m, v, 0.0)), _bf(jnp.where(hi_m, vr, 0.0))),
            (_bf(jnp.where(lo_m, vr, 0.0)), _bf(jnp.where(hi_m, v, 0.0)))]


def _swa_blocks(qrows, kks, vlos, vhis, sink_col, lowers, upper, tq):
    n = len(kks)
    stack = lambda xs: jnp.concatenate(xs, axis=0) if n > 1 else xs[0]
    s = stack([_dot_nt(qrows[i], kks[i]) for i in range(n)])
    if lowers is not None:
        kcol = lax.broadcasted_iota(jnp.int32, s.shape, 1)
        s = jnp.where(kcol >= stack(lowers), jnp.where(kcol <= stack([upper] * n), s, -jnp.inf), -jnp.inf)
    sink = stack([sink_col] * n)
    m = jnp.maximum(jnp.max(s, axis=-1, keepdims=True), sink)
    p = jnp.exp(s - m)
    den = jnp.sum(p, axis=-1, keepdims=True) + jnp.exp(sink - m)
    p = _bf(p / den)
    outs = []
    for i in range(n):
        r = 4 * tq * i
        outs.append((_dot(p[r:r + tq], vlos[i]) + _dot(p[r + tq:r + 2 * tq], vhis[i]),
                     _dot(p[r + 2 * tq:r + 3 * tq], vlos[i]) + _dot(p[r + 3 * tq:r + 4 * tq], vhis[i])))
    return outs


def _sink_col(sinks_ref, g, tq):
    r = lax.broadcasted_iota(jnp.int32, (4 * tq, 1), 0)
    base = 4 * g
    return jnp.where(r < tq, sinks_ref[base],
                     jnp.where(r < 2 * tq, sinks_ref[base + 1],
                               jnp.where(r < 3 * tq, sinks_ref[base + 2], sinks_ref[base + 3])))


def _mem_head(ref, h):
    return _bf(ref[pl.ds(h, MEM_TOKENS, stride=MEM_HEADS), :])


def _mem_attention(q, mk_refs, mv_refs):
    qb = _bf(q)
    nseg = len(mk_refs)
    tq = q.shape[0] // nseg
    outs = []
    for h in range(MEM_HEADS):
        hs = slice(MEM_HD * h, MEM_HD * (h + 1))
        s = [_dot_nt(qb[tq * i:tq * (i + 1), hs], _mem_head(mk_refs[i], h)) for i in range(nseg)]
        s = jnp.concatenate(s, axis=0) if nseg > 1 else s[0]
        m = jnp.max(s, axis=-1, keepdims=True)
        p = jnp.exp(s - m)
        p = _bf(p / jnp.sum(p, axis=-1, keepdims=True))
        o = [_dot(p[tq * i:tq * (i + 1)], _mem_head(mv_refs[i], h)) for i in range(nseg)]
        outs.append(jnp.concatenate(o, axis=0) if nseg > 1 else o[0])
    return jnp.concatenate(outs, axis=1)


SWA_BLOCK_LANES = 4 * SWA_PAIR
LOG2E = float(np.log2(np.e))


def _swa_bias_t(sequence_start):
    nk = WINDOW + SWA_PAIR
    col = lax.broadcasted_iota(jnp.int32, (1, SWA_BLOCK_LANES), 1)
    lower = CHUNK * ((col % SWA_PAIR) // CHUNK)
    upper = lower + (WINDOW + CHUNK - 1)
    if sequence_start:
        lower = jnp.maximum(lower, WINDOW)
    krow = lax.broadcasted_iota(jnp.int32, (nk, SWA_BLOCK_LANES), 0)
    return jnp.where(krow >= lower, jnp.where(krow <= upper, 0.0, -jnp.inf), -jnp.inf)


def _swa_attention_t(qs, kk_scr, v_scr, sinks_ref, bias_first, bias_rest, rows, between=lambda: None):
    nk = WINDOW + SWA_PAIR
    npair = rows // SWA_PAIR
    bw = SWA_BLOCK_LANES
    blocks = [(g, p) for g in range(SWA_KV_HEADS) for p in range(npair)]
    sts = []
    for g, p in blocks:
        rs = slice(SWA_PAIR * p, SWA_PAIR * (p + 1))
        qrows = jnp.concatenate([qs[2 * g][0][rs], qs[2 * g][1][rs],
                                 qs[2 * g + 1][0][rs], qs[2 * g + 1][1][rs]], axis=0)
        sts.append(_dot_nt(kk_scr[g, SWA_PAIR * p:SWA_PAIR * p + nk, :], qrows)
                   + (bias_first if p == 0 else bias_rest))
    st = jnp.concatenate(sts, axis=1)
    between()
    width = bw * len(blocks)
    col = lax.broadcasted_iota(jnp.int32, (1, width), 1)
    head = 4 * (col // (bw * npair)) + (col % bw) // SWA_PAIR
    sink = jnp.zeros((1, width), _F32)
    for h in range(SWA_HEADS):
        sink = jnp.where(head == h, sinks_ref[h] * LOG2E, sink)
    m = jnp.maximum(jnp.max(st, axis=0, keepdims=True), sink)
    pt = jnp.exp2(st - m)
    inv = 1.0 / (jnp.sum(pt, axis=0, keepdims=True) + jnp.exp2(sink - m))
    pb = _bf(pt)
    pieces = [[None] * npair for _ in range(SWA_HEADS)]
    for i, (g, p) in enumerate(blocks):
        cs = slice(bw * i, bw * (i + 1))
        o2 = _dot_tn(v_scr[SWA_PAIR * p:SWA_PAIR * p + nk, :], pb[:, cs]) * inv[:, cs]
        for hh in range(4):
            pieces[4 * g + hh][p] = o2[SWA_HD * g:SWA_HD * (g + 1), SWA_PAIR * hh:SWA_PAIR * (hh + 1)]
    return jnp.concatenate([jnp.concatenate(ps, axis=1) if npair > 1 else ps[0] for ps in pieces], axis=0)


def _mem_attention_t(q, mk_ref, mv_ref, between=lambda: None):
    qb = _bf(q)
    R = q.shape[0]
    st = jnp.concatenate([_dot_nt(_mem_head(mk_ref, h), qb[:, MEM_HD * h:MEM_HD * (h + 1)])
                          for h in range(MEM_HEADS)], axis=1)
    between()
    pt = jnp.exp2(st - jnp.max(st, axis=0, keepdims=True))
    inv = 1.0 / jnp.sum(pt, axis=0, keepdims=True)
    pb = _bf(pt)
    outs = [_dot_tn(_mem_head(mv_ref, h), pb[:, R * h:R * (h + 1)]) * inv[:, R * h:R * (h + 1)]
            for h in range(MEM_HEADS)]
    return jnp.concatenate(outs, axis=0)


def _gate(xb, i, w_inb_ref, bgate_ref):
    gl = _dot(xb, w_inb_ref[:, OFF_GL + D_MODEL * i:OFF_GL + D_MODEL * (i + 1)])
    return jax.nn.sigmoid(gl + bgate_ref[:, D_MODEL * i:D_MODEL * (i + 1)])


def _merge_ln1(x, gated_sum, wo_ref, g_ref, b_ref, alpha):
    mix = _dot(_bf(gated_sum), wo_ref[...])
    return _layer_norm(alpha * x + mix, g_ref[...], b_ref[...])


def _forget_log2(xb, w_ga_ref, w2_ref, ba_ref):
    ga = _dot(xb, w_ga_ref[...])
    return _log_sigmoid(_dot(_bf(ga), w2_ref[...]) + ba_ref[...]) * (LOG2E / GLA_TAU)


def _prompt_mixer_kernel(sinks_ref, x_ref, rc_ref, rdn_ref, rup_ref, mk_ref, mv_ref, w_ina_ref, w_ga_ref,
                         w_inb_ref, w2_ref, ba_ref, gng_ref, wbg_ref, wbs_ref, wbm_ref, bgate_ref, wo_ref,
                         ln_g_ref, ln_b_ref,
                         out_ref, gla_out_ref, swak_out_ref, swav_out_ref,
                         s_scr, kk_scr, vv_scr, bias_scr, *, rows, alpha):
    t = pl.program_id(1)
    nchunk = rows // CHUNK

    @pl.when(t == 0)
    def _():
        s_scr[...] = jnp.zeros_like(s_scr)
        kk_scr[:, 0:WINDOW, :] = jnp.zeros((SWA_KV_HEADS, WINDOW, LANES), _BF)
        vv_scr[0:WINDOW, :] = jnp.zeros((WINDOW, LANES), _BF)
        bias_scr[0] = _swa_bias_t(sequence_start=False)
        bias_scr[1] = _swa_bias_t(sequence_start=True)

    x = x_ref[0]
    xb = _bf(x)

    log_a = _forget_log2(xb, w_ga_ref, w2_ref, ba_ref)

    def proj_a(off):
        return _dot(xb, w_ina_ref[:, off:off + GLA_W])

    def state_in(c, h):
        return s_scr[h]

    def state_out(c, h, st):
        s_scr[h] = st
        gla_out_ref[0, h] = jnp.transpose(st)

    gq, gk, gv = proj_a(OFF_GQ), proj_a(OFF_GK), proj_a(OFF_GV)
    gate_a = _gate(xb, 0, w_inb_ref, bgate_ref)
    late = {}

    def dense_a():
        late["gr"] = proj_a(OFF_GR)
        late["sq"] = _dot(xb, w_inb_ref[:, OFF_SQ:OFF_SQ + SWA_QW])
        late["sk"] = _dot(xb, w_inb_ref[:, OFF_SK:OFF_SK + SWA_KW])
        late["sv"] = _dot(xb, w_inb_ref[:, OFF_SV:OFF_SV + SWA_KW])

    def dense_b():
        late["gate_b"] = _gate(xb, 1, w_inb_ref, bgate_ref)

    def dense_c():
        late["gate_c"] = _gate(xb, 2, w_inb_ref, bgate_ref)

    o_a = _gla_tile(gq, gk, gv, log_a, CHUNK, nchunk, state_in, state_out, True, dense_a)
    o_a = _gla_gate(o_a, late["gr"], gng_ref[...])
    gated = gate_a * _dot(_bf(o_a), wbg_ref[...])

    rc, rdn, rup = rc_ref[...], rdn_ref[...], rup_ref[...]
    q = _rope(late["sq"], rc, rdn, rup) * (SWA_HD ** -0.5 * LOG2E)
    k = _rope(late["sk"], rc, rdn, rup)
    v = late["sv"]
    swak_out_ref[0] = k[rows - WINDOW:rows, :]
    swav_out_ref[0] = v[rows - WINDOW:rows, :]
    kk = _swa_dup_k(k)
    vb = _bf(v)
    for g in range(SWA_KV_HEADS):
        kk_scr[g, WINDOW:WINDOW + rows, :] = kk[g]
    vv_scr[WINDOW:WINDOW + rows, :] = vb
    o_bt = _swa_attention_t(_swa_split_q(q), kk_scr, vv_scr, sinks_ref, bias_scr[jnp.where(t == 0, 1, 0)],
                            bias_scr[0], rows, dense_b)
    gated = gated + late["gate_b"] * _dot_tn(_bf(o_bt), wbs_ref[...])
    for g in range(SWA_KV_HEADS):
        kk_scr[g, 0:WINDOW, :] = kk[g][rows - WINDOW:rows]
    vv_scr[0:WINDOW, :] = vb[rows - WINDOW:rows]

    qm = _dot(xb, w_inb_ref[:, OFF_MQ:OFF_MQ + MEM_W]) * (MEM_HD ** -0.5 * LOG2E)
    o_ct = _mem_attention_t(qm, mk_ref.at[0], mv_ref.at[0], dense_c)
    gated = gated + late["gate_c"] * _dot_tn(_bf(o_ct), wbm_ref[...])

    out_ref[0] = _merge_ln1(x, gated, wo_ref, ln_g_ref, ln_b_ref, alpha)


def _sample_mixer_kernel(sinks_ref, x_ref, rc_ref, rdn_ref, rup_ref, mk_ref, mv_ref, gla_in_ref,
                         swak_in_ref, swav_in_ref, w_ina_ref, w_ga_ref, w_inb_ref, w2_ref,
                         ba_ref, gng_ref, wbg_ref, wbs_ref, wbm_ref, bgate_ref, wo_ref, ln_g_ref, ln_b_ref,
                         out_ref, gla_out_ref, swak_out_ref, swav_out_ref, *, nb, tq, alpha):
    rows = nb * tq
    x = x_ref[...].reshape(rows, D_MODEL)
    xb = _bf(x)

    log_a = _forget_log2(xb, w_ga_ref, w2_ref, ba_ref)

    def proj_a(off):
        return _dot(xb, w_ina_ref[:, off:off + GLA_W])

    def state_in(c, h):
        return jnp.transpose(gla_in_ref[c, h])

    def state_out(c, h, st):
        gla_out_ref[c, h] = jnp.transpose(st)

    gq, gk, gv = proj_a(OFF_GQ), proj_a(OFF_GK), proj_a(OFF_GV)
    gate_a = _gate(xb, 0, w_inb_ref, bgate_ref)
    o_a = _gla_tile(gq, gk, gv, log_a, tq, nb, state_in, state_out, chained=False)
    o_a = _gla_gate(o_a, proj_a(OFF_GR), gng_ref[...])
    gated = gate_a * _dot(_bf(o_a), wbg_ref[...])

    rc = jnp.concatenate([rc_ref[...]] * nb, axis=0)
    rdn = jnp.concatenate([rdn_ref[...]] * nb, axis=0)
    rup = jnp.concatenate([rup_ref[...]] * nb, axis=0)
    q = _rope(_dot(xb, w_inb_ref[:, OFF_SQ:OFF_SQ + SWA_QW]), rc, rdn, rup) * (SWA_HD ** -0.5)
    k = _rope(_dot(xb, w_inb_ref[:, OFF_SK:OFF_SK + SWA_KW]), rc, rdn, rup)
    v = _dot(xb, w_inb_ref[:, OFF_SV:OFF_SV + SWA_KW])
    gate_b = _gate(xb, 1, w_inb_ref, bgate_ref)
    qs = _swa_split_q(q)
    slabs = [[None] * nb for _ in range(SWA_QW // LANES)]
    kks, vvs = [], []
    for bi in range(nb):
        rs = slice(tq * bi, tq * (bi + 1))
        k_all = jnp.concatenate([swak_in_ref[bi], k[rs]], axis=0)
        v_all = jnp.concatenate([swav_in_ref[bi], v[rs]], axis=0)
        swak_out_ref[bi] = k_all[tq:tq + WINDOW]
        swav_out_ref[bi] = v_all[tq:tq + WINDOW]
        kks.append(_swa_dup_k(k_all))
        vvs.append(_swa_place_v(v_all))
    for g in range(SWA_KV_HEADS):
        qrows = []
        for bi in range(nb):
            rs = slice(tq * bi, tq * (bi + 1))
            qrows.append(jnp.concatenate([qs[2 * g][0][rs], qs[2 * g][1][rs],
                                          qs[2 * g + 1][0][rs], qs[2 * g + 1][1][rs]], axis=0))
        outs = _swa_blocks(qrows, [kk[g] for kk in kks], [vv[g][0] for vv in vvs], [vv[g][1] for vv in vvs],
                           _sink_col(sinks_ref, g, tq), None, None, tq)
        for bi in range(nb):
            slabs[2 * g][bi], slabs[2 * g + 1][bi] = outs[bi]
    o_b = jnp.concatenate([jnp.concatenate(s, axis=0) if nb > 1 else s[0] for s in slabs], axis=1)
    gated = gated + gate_b * _dot(_bf(o_b), wbs_ref[...])

    qm = _dot(xb, w_inb_ref[:, OFF_MQ:OFF_MQ + MEM_W]) * (MEM_HD ** -0.5)
    gate_c = _gate(xb, 2, w_inb_ref, bgate_ref)
    o_c = _mem_attention(qm, [mk_ref.at[bi] for bi in range(nb)], [mv_ref.at[bi] for bi in range(nb)])
    gated = gated + gate_c * _dot(_bf(o_c), wbm_ref[...])

    out = _merge_ln1(x, gated, wo_ref, ln_g_ref, ln_b_ref, alpha)
    out_ref[...] = out.reshape(nb, tq, D_MODEL)


_GELU_K1 = -2.0 * float(np.log2(np.e)) * float(np.sqrt(2.0 / np.pi))
_GELU_K2 = _GELU_K1 * 0.044715


def _conv(u2, u1, u0, cw_ref, cb_ref, cs):
    return cb_ref[:, cs] + u2 * cw_ref[0:1, cs] + u1 * cw_ref[1:2, cs] + u0 * cw_ref[2:3, cs]


def _geglu(g, v):
    return g * v / (1.0 + jnp.exp2(g * (g * g * _GELU_K2 + _GELU_K1)))


def _prompt_ffn_kernel(x_ref, wup_ref, cw_ref, cb_ref, wdn_ref, ln_g_ref, ln_b_ref,
                       out_ref, conv_out_ref, u_scr, *, rows, alpha):
    nj = rows // SUBLANES
    base = 2 * SUBLANES
    last = slice(base + rows - SUBLANES, base + rows)
    last2 = slice(base + rows - 2 * SUBLANES, base + rows - SUBLANES)

    @pl.when(pl.program_id(1) == 0)
    def _():
        u_scr[base + rows - 2 * SUBLANES:base + rows, :] = jnp.zeros((2 * SUBLANES, 2 * D_FF), _F32)

    prev_last, prev_last2 = u_scr[last, :], u_scr[last2, :]
    x = jnp.swapaxes(x_ref[0].reshape(SUBLANES, nj, D_MODEL), 0, 1).reshape(rows, D_MODEL)
    u_scr[base:base + rows, :] = _dot(_bf(x), wup_ref[...])
    sub = lax.broadcasted_iota(jnp.int32, (SUBLANES, FFN_COLS), 0)

    def conv_cols(cs):
        wrap = lambda prev, cur: pltpu.roll(jnp.where(sub == SUBLANES - 1, prev, cur), 1, axis=0)
        u_scr[SUBLANES:base, cs] = wrap(prev_last[:, cs], u_scr[last, cs])
        u_scr[0:SUBLANES, cs] = wrap(prev_last2[:, cs], u_scr[last2, cs])
        return _conv(u_scr[0:rows, cs], u_scr[SUBLANES:SUBLANES + rows, cs], u_scr[base:base + rows, cs],
                     cw_ref, cb_ref, cs)

    h = [_bf(_geglu(conv_cols(slice(c0, c0 + FFN_COLS)), conv_cols(slice(D_FF + c0, D_FF + c0 + FFN_COLS))))
         for c0 in range(0, D_FF, FFN_COLS)]
    conv_out_ref[0] = jnp.concatenate([u_scr[base + rows - SUBLANES - 1:base + rows - SUBLANES, :],
                                       u_scr[base + rows - 1:base + rows, :]], axis=0)
    f = _dot(jnp.concatenate(h, axis=1), wdn_ref[...])
    y = _layer_norm(alpha * x + f, ln_g_ref[...], ln_b_ref[...])
    out_ref[0] = jnp.swapaxes(y.reshape(nj, SUBLANES, D_MODEL), 0, 1).reshape(rows, D_MODEL)


def _sample_ffn_kernel(x_ref, hist_ref, wup_ref, cw_ref, cb_ref, wdn_ref, ln_g_ref, ln_b_ref,
                       out_ref, conv_out_ref, u_scr, *, nb, tq, alpha):
    rows = nb * tq
    base = (CONV_W - 1) * nb
    x = jnp.swapaxes(x_ref[...], 0, 1).reshape(rows, D_MODEL)
    u_scr[0:base, :] = jnp.swapaxes(hist_ref[...], 0, 1).reshape(base, 2 * D_FF)
    u_scr[base:base + rows, :] = _dot(_bf(x), wup_ref[...])
    conv = lambda cs: _conv(u_scr[0:rows, cs], u_scr[nb:nb + rows, cs], u_scr[base:base + rows, cs],
                            cw_ref, cb_ref, cs)
    h = _geglu(conv(slice(0, D_FF)), conv(slice(D_FF, 2 * D_FF)))
    conv_out_ref[...] = jnp.swapaxes(u_scr[rows:rows + base, :].reshape(CONV_W - 1, nb, 2 * D_FF), 0, 1)
    f = _dot(_bf(h), wdn_ref[...])
    y = _layer_norm(alpha * x + f, ln_g_ref[...], ln_b_ref[...])
    out_ref[...] = jnp.swapaxes(y.reshape(tq, nb, D_MODEL), 0, 1)


def _mem_kv_kernel(m_ref, w_ref, k_ref, v_ref):
    kv = _dot(_bf(m_ref[0]), w_ref[...])
    for h in range(MEM_HEADS):
        k_ref[0, pl.ds(h, MEM_TOKENS, stride=MEM_HEADS), :] = kv[:, MEM_HD * h:MEM_HD * (h + 1)]
        v_ref[0, pl.ds(h, MEM_TOKENS, stride=MEM_HEADS), :] = kv[:, MEM_W + MEM_HD * h:MEM_W + MEM_HD * (h + 1)]


def _const_spec(shape):
    nd = len(shape)
    return pl.BlockSpec(shape, lambda *_: (0,) * nd, pipeline_mode=pl.Buffered(1))


def _rope_tables(pos):
    half = ROPE_DIM // 2
    inv = np.float32(ROPE_THETA) ** (-np.arange(half, dtype=np.float32) / np.float32(half))
    ang = pos.astype(np.float32)[:, None] * inv[None, :]
    cos, sin = np.cos(ang), np.sin(ang)
    T = pos.shape[0]
    ones = np.ones((T, SWA_HD - ROPE_DIM), np.float32)
    zeros = np.zeros((T, SWA_HD - ROPE_DIM), np.float32)
    zh = np.zeros((T, half), np.float32)
    c = np.concatenate([cos, cos, ones], axis=1)
    dn = np.concatenate([-sin, zh, zeros], axis=1)
    up = np.concatenate([zh, sin, zeros], axis=1)
    rep = LANES // SWA_HD
    return tuple(jnp.asarray(np.concatenate([a] * rep, axis=1), dtype=_F32) for a in (c, dn, up))


def _pack_cols_kernel(wt_ref, o_ref, *, keep):
    t = jnp.transpose(wt_ref[...])
    if keep < t.shape[1]:
        t = jnp.where(lax.broadcasted_iota(jnp.int32, t.shape, 1) < keep, t, 0.0)
    o_ref[...] = _bf(t)


def _pack_cols(wt, col0, ncols, keep, name):
    blk = ncols // 2 if ncols % (2 * LANES) == 0 else ncols
    k = wt.shape[1]
    return pl.pallas_call(
        functools.partial(_pack_cols_kernel, keep=keep),
        grid=(ncols // blk,),
        in_specs=[pl.BlockSpec((pl.Element(blk), pl.Element(k)),
                               lambda i: (pl.multiple_of(col0 + blk * i, SUBLANES), 0))],
        out_specs=pl.BlockSpec((k, blk), lambda i: (0, i)),
        out_shape=jax.ShapeDtypeStruct((k, ncols), _BF),
        compiler_params=_params(("arbitrary",)),
        name=name,
    )(wt)


def _split_w_in(w_in):
    wt = jnp.transpose(w_in)
    return (_pack_cols(wt, 0, IN_A_COLS, IN_A_COLS, "pack_w_in_a"),
            _pack_cols(wt, IN_A_COLS, LANES, GLA_RANK, "pack_w_in_ga"),
            _pack_cols(wt, IN_A_COLS + GLA_RANK, IN_B_COLS, IN_B_COLS, "pack_w_in_b"))


def _mixer_weight_specs():
    return [
        _const_spec((D_MODEL, IN_A_COLS)),
        _const_spec((D_MODEL, LANES)),
        _const_spec((D_MODEL, IN_B_COLS)),
        _const_spec((LANES, GLA_W)),
        _const_spec((1, GLA_W)),
        _const_spec((1, GLA_DV)),
        _const_spec((GLA_W, D_MODEL)),
        _const_spec((SWA_QW, D_MODEL)),
        _const_spec((MEM_W, D_MODEL)),
        _const_spec((1, GATE_W)),
        _const_spec((D_MODEL, D_MODEL)),
        _const_spec((1, D_MODEL)),
        _const_spec((1, D_MODEL)),
    ]


def _ffn_weight_specs():
    return [
        _const_spec((D_MODEL, 2 * D_FF)),
        _const_spec((CONV_W, 2 * D_FF)),
        _const_spec((1, 2 * D_FF)),
        _const_spec((D_FF, D_MODEL)),
        _const_spec((1, D_MODEL)),
        _const_spec((1, D_MODEL)),
    ]


_SMEM_SPEC = pl.BlockSpec(memory_space=pltpu.SMEM)
_MEM_ROWS = MEM_TOKENS * MEM_HEADS


def _params(sem):
    return pltpu.CompilerParams(dimension_semantics=sem, vmem_limit_bytes=V7X_VMEM_LIMIT)


def _prompt_layer(x, mem, sinks, mixer_w, ffn_w, w_mem_kv, alpha):
    B, T, _ = x.shape
    rows = min(PROMPT_ROWS, T)
    nt = T // rows
    assert T % rows == 0 and rows % SWA_PAIR == 0 and rows >= WINDOW
    f32 = jnp.float32
    mk, mv = pl.pallas_call(
        _mem_kv_kernel,
        grid=(B,),
        in_specs=[pl.BlockSpec((1, MEM_TOKENS, D_MODEL), lambda b: (b, 0, 0)),
                  _const_spec((D_MODEL, 2 * MEM_W))],
        out_specs=[pl.BlockSpec((1, _MEM_ROWS, MEM_HD), lambda b: (b, 0, 0))] * 2,
        out_shape=[jax.ShapeDtypeStruct((B, _MEM_ROWS, MEM_HD), f32)] * 2,
        compiler_params=_params(("arbitrary",)),
        name="mem_kv",
    )(mem, w_mem_kv)

    rope = _rope_tables(np.arange(T))
    rope_spec = pl.BlockSpec((rows, LANES), lambda b, t: (t, 0))
    per_batch = lambda *blk: pl.BlockSpec((1,) + blk, lambda b, t: (b,) + (0,) * len(blk))
    x1, gla, swak, swav = pl.pallas_call(
        functools.partial(_prompt_mixer_kernel, rows=rows, alpha=alpha),
        grid=(B, nt),
        in_specs=[_SMEM_SPEC,
                  pl.BlockSpec((1, rows, D_MODEL), lambda b, t: (b, t, 0)),
                  rope_spec, rope_spec, rope_spec,
                  per_batch(_MEM_ROWS, MEM_HD), per_batch(_MEM_ROWS, MEM_HD)] + _mixer_weight_specs(),
        out_specs=[pl.BlockSpec((1, rows, D_MODEL), lambda b, t: (b, t, 0)),
                   per_batch(GLA_HEADS, GLA_DK, GLA_DV),
                   per_batch(WINDOW, SWA_KW), per_batch(WINDOW, SWA_KW)],
        out_shape=[jax.ShapeDtypeStruct((B, T, D_MODEL), f32),
                   jax.ShapeDtypeStruct((B, GLA_HEADS, GLA_DK, GLA_DV), f32),
                   jax.ShapeDtypeStruct((B, WINDOW, SWA_KW), f32),
                   jax.ShapeDtypeStruct((B, WINDOW, SWA_KW), f32)],
        scratch_shapes=[pltpu.VMEM((GLA_HEADS, GLA_DK, GLA_DV), f32),
                        pltpu.VMEM((SWA_KV_HEADS, WINDOW + rows, LANES), _BF),
                        pltpu.VMEM((WINDOW + rows, LANES), _BF),
                        pltpu.VMEM((2, WINDOW + SWA_PAIR, SWA_BLOCK_LANES), f32)],
        compiler_params=_params(("arbitrary", "arbitrary")),
        name="prompt_mixer",
    )(sinks, x, *rope, mk, mv, *mixer_w)

    frows = min(PROMPT_FFN_ROWS, T)
    assert T % frows == 0 and frows >= CONV_W - 1
    y, conv = pl.pallas_call(
        functools.partial(_prompt_ffn_kernel, rows=frows, alpha=alpha),
        grid=(B, T // frows),
        in_specs=[pl.BlockSpec((1, frows, D_MODEL), lambda b, t: (b, t, 0))] + _ffn_weight_specs(),
        out_specs=[pl.BlockSpec((1, frows, D_MODEL), lambda b, t: (b, t, 0)),
                   per_batch(CONV_W - 1, 2 * D_FF)],
        out_shape=[jax.ShapeDtypeStruct((B, T, D_MODEL), f32),
                   jax.ShapeDtypeStruct((B, CONV_W - 1, 2 * D_FF), f32)],
        scratch_shapes=[pltpu.VMEM((2 * SUBLANES + frows, 2 * D_FF), f32)],
        compiler_params=_params(("arbitrary", "arbitrary")),
        name="prompt_ffn",
    )(x1, *ffn_w)
    return y, gla, swak, swav, mk, mv, conv


def _sample_layer(x, gla0, swak0, swav0, memk, memv, conv0, sinks, mixer_w, ffn_w, alpha):
    B, tq, _ = x.shape
    nb = max(1, min(B, SAMPLE_ROWS // tq))
    assert B % nb == 0 and tq % 16 == 0 and tq >= CONV_W - 1
    f32 = jnp.float32
    rope = _rope_tables(PAST_LEN + np.arange(tq))
    rope_spec = pl.BlockSpec((tq, LANES), lambda i: (0, 0))
    blk = lambda *s: pl.BlockSpec((nb,) + s, lambda i: (i,) + (0,) * len(s))
    blk1 = lambda *s: pl.BlockSpec((nb,) + s, lambda i: (i,) + (0,) * len(s), pipeline_mode=pl.Buffered(1))
    x1, gla, swak, swav = pl.pallas_call(
        functools.partial(_sample_mixer_kernel, nb=nb, tq=tq, alpha=alpha),
        grid=(B // nb,),
        in_specs=[_SMEM_SPEC, blk(tq, D_MODEL), rope_spec, rope_spec, rope_spec,
                  blk(_MEM_ROWS, MEM_HD), blk(_MEM_ROWS, MEM_HD),
                  blk1(GLA_HEADS, GLA_DK, GLA_DV), blk1(WINDOW, SWA_KW), blk1(WINDOW, SWA_KW)]
                 + _mixer_weight_specs(),
        out_specs=[blk(tq, D_MODEL), blk(GLA_HEADS, GLA_DK, GLA_DV), blk(WINDOW, SWA_KW), blk(WINDOW, SWA_KW)],
        out_shape=[jax.ShapeDtypeStruct((B, tq, D_MODEL), f32),
                   jax.ShapeDtypeStruct((B, GLA_HEADS, GLA_DK, GLA_DV), f32),
                   jax.ShapeDtypeStruct((B, WINDOW, SWA_KW), f32),
                   jax.ShapeDtypeStruct((B, WINDOW, SWA_KW), f32)],
        compiler_params=_params(("arbitrary",)),
        name="sample_mixer",
    )(sinks, x, *rope, memk, memv, gla0, swak0, swav0, *mixer_w)

    assert B % SUBLANES == 0
    fblk = lambda *s: pl.BlockSpec((SUBLANES,) + s, lambda i: (i,) + (0,) * len(s))
    y, conv = pl.pallas_call(
        functools.partial(_sample_ffn_kernel, nb=SUBLANES, tq=tq, alpha=alpha),
        grid=(B // SUBLANES,),
        in_specs=[fblk(tq, D_MODEL), fblk(CONV_W - 1, 2 * D_FF)] + _ffn_weight_specs(),
        out_specs=[fblk(tq, D_MODEL), fblk(CONV_W - 1, 2 * D_FF)],
        out_shape=[jax.ShapeDtypeStruct((B, tq, D_MODEL), f32),
                   jax.ShapeDtypeStruct((B, CONV_W - 1, 2 * D_FF), f32)],
        scratch_shapes=[pltpu.VMEM(((CONV_W - 1 + tq) * SUBLANES, 2 * D_FF), f32)],
        compiler_params=_params(("arbitrary",)),
        name="sample_ffn",
    )(x1, conv0, *ffn_w)
    return y, gla, swak, swav, conv


def kernel(x_prompt, x_sample, cache_swa_k, cache_swa_v, state_gla, cache_mem_k, cache_mem_v, cache_ffn_conv, mem_prompt, ln1_g, ln1_b, ln2_g, ln2_b, w_in, b_gate, w_gla_a2, b_gla_a, gla_norm_g, swa_sinks, w_mem_kv, w_br_gla, w_br_swa, w_br_mem, w_o, w_up, conv_w, conv_b, w_down):
    depth = w_in.shape[0]
    alpha = float((2 * depth) ** 0.25)
    Bp = x_prompt.shape[0]
    Bs = x_sample.shape[0]
    hp, hs = x_prompt, x_sample
    outs = [[] for _ in range(10)]
    row = lambda a: a.reshape(1, -1)
    for l in range(depth):
        w2 = jnp.pad(w_gla_a2[l], ((0, LANES - GLA_RANK), (0, 0)))
        mixer_w = _split_w_in(w_in[l]) + (
            _bf(w2), row(b_gla_a[l]), row(gla_norm_g[l]),
            _bf(w_br_gla[l]), _bf(w_br_swa[l]), _bf(w_br_mem[l]), row(b_gate[l]), _bf(w_o[l]),
            row(ln1_g[l]), row(ln1_b[l]))
        ffn_w = (_bf(w_up[l]), conv_w[l], row(conv_b[l]), _bf(w_down[l]), row(ln2_g[l]), row(ln2_b[l]))
        sinks = swa_sinks[l]
        hp, g_p, k_p, v_p, mk_p, mv_p, c_p = _prompt_layer(
            hp, mem_prompt, sinks, mixer_w, ffn_w, _bf(w_mem_kv[l]), alpha)
        hs, g_s, k_s, v_s, c_s = _sample_layer(
            hs, state_gla[l],
            cache_swa_k[l].reshape(Bs, WINDOW, SWA_KW), cache_swa_v[l].reshape(Bs, WINDOW, SWA_KW),
            cache_mem_k[l].reshape(Bs, _MEM_ROWS, MEM_HD), cache_mem_v[l].reshape(Bs, _MEM_ROWS, MEM_HD),
            cache_ffn_conv[l], sinks, mixer_w, ffn_w, alpha)
        kv5 = lambda a, b: a.reshape(b, WINDOW, SWA_KV_HEADS, SWA_HD)
        m5 = lambda a: a.reshape(Bp, MEM_TOKENS, MEM_HEADS, MEM_HD)
        for lst, val in zip(outs, (kv5(k_p, Bp), kv5(v_p, Bp), g_p, m5(mk_p), m5(mv_p), c_p,
                                   kv5(k_s, Bs), kv5(v_s, Bs), g_s, c_s)):
            lst.append(val)
    return (hp, hs) + tuple(jnp.stack(o) for o in outs)
```

```python
import functools

import jax
import jax.numpy as jnp
import numpy as np
from jax import lax
from jax.experimental import pallas as pl
from jax.experimental.pallas import tpu as pltpu

D_MODEL = 1024
CHUNK = 64
GLA_HEADS = 4
GLA_DK = 128
GLA_DV = 128
GLA_RANK = 16
GLA_TAU = 16.0
SWA_HEADS = 8
SWA_KV_HEADS = 2
SWA_HD = 64
WINDOW = 128
ROPE_DIM = 16
ROPE_THETA = 500000.0
MEM_TOKENS = 256
MEM_HEADS = 4
MEM_HD = 128
D_FF = 2816
CONV_W = 3
N_BRANCH = 3
PAST_LEN = 2048
LN_EPS = 1e-5

LANES = 128
SUBLANES = 8
V7X_VMEM_LIMIT = 56 * 1024 * 1024
PROMPT_ROWS = 512
PROMPT_FFN_ROWS = 512
FFN_COLS = 256
GLA_BLOCK_ROWS = 256
SAMPLE_ROWS = 256
GLA_W = GLA_HEADS * GLA_DK
SWA_QW = SWA_HEADS * SWA_HD
SWA_KW = SWA_KV_HEADS * SWA_HD
MEM_W = MEM_HEADS * MEM_HD
GATE_W = N_BRANCH * D_MODEL
SWA_PAIR = 2 * CHUNK

OFF_GQ = 0
OFF_GK = OFF_GQ + GLA_W
OFF_GV = OFF_GK + GLA_W
OFF_GR = OFF_GV + GLA_W
IN_A_COLS = OFF_GR + GLA_W
OFF_SQ = 0
OFF_SK = OFF_SQ + SWA_QW
OFF_SV = OFF_SK + SWA_KW
OFF_MQ = OFF_SV + SWA_KW
OFF_GL = OFF_MQ + MEM_W
IN_B_COLS = OFF_GL + GATE_W

_BF = jnp.bfloat16
_F32 = jnp.float32


def _bf(x):
    return x.astype(_BF)


def _dot(a, b):
    return jnp.dot(a, b, preferred_element_type=_F32)


def _dot_nt(a, b):
    return lax.dot_general(a, b, (((1,), (1,)), ((), ())), preferred_element_type=_F32)


def _dot_tn(a, b):
    return lax.dot_general(a, b, (((0,), (0,)), ((), ())), preferred_element_type=_F32)


def _layer_norm(h, g, b):
    mu = jnp.mean(h, axis=-1, keepdims=True)
    d = h - mu
    var = jnp.mean(d * d, axis=-1, keepdims=True)
    return d * lax.rsqrt(var + LN_EPS) * g + b


def _log_sigmoid(x):
    return -(jnp.maximum(-x, 0.0) + jnp.log(1.0 + jnp.exp(-jnp.abs(x))))


def _rope(a, c, s_dn, s_up):
    outs = []
    for j in range(a.shape[1] // LANES):
        slab = a[:, LANES * j:LANES * (j + 1)]
        fwd = pltpu.roll(slab, LANES - ROPE_DIM // 2, axis=1)
        bwd = pltpu.roll(slab, ROPE_DIM // 2, axis=1)
        outs.append(slab * c + fwd * s_dn + bwd * s_up)
    return outs[0] if len(outs) == 1 else jnp.concatenate(outs, axis=1)


def _gla_tile(q, k, v, log2_a, chunk, nchunk, state_in, state_out, chained, between=lambda: None):
    R = chunk * nchunk
    blk = min(R, GLA_BLOCK_ROWS)
    blocks = [slice(r0, r0 + blk) for r0 in range(0, R, blk)]
    row = lax.broadcasted_iota(jnp.int32, (blk, blk), 0)
    col = lax.broadcasted_iota(jnp.int32, (blk, blk), 1)
    tril = (row // chunk == col // chunk) & (col <= row)
    ltri = jnp.where(tril, 1.0, 0.0).astype(_BF)
    hi = _bf(log2_a)
    lo = _bf(log2_a - hi.astype(_F32))
    b = [_dot(ltri, hi[rs]) + _dot(ltri, lo[rs]) for rs in blocks]
    b = jnp.concatenate(b, axis=0) if len(b) > 1 else b[0]
    between()
    decay = [jnp.exp2(b[chunk * (c + 1) - 1:chunk * (c + 1), :]) for c in range(nchunk)]
    q_dec = _bf(q * (GLA_DK ** -0.5) * jnp.exp2(b))
    k_inv = k * jnp.exp2(-b)
    k_tail = _bf(jnp.concatenate([k_inv[chunk * c:chunk * (c + 1)] * decay[c] for c in range(nchunk)], axis=0)
                 if nchunk > 1 else k_inv * decay[0])
    k_inv = _bf(k_inv)
    vb = _bf(v)
    heads = [slice(GLA_DK * h, GLA_DK * (h + 1)) for h in range(GLA_HEADS)]
    o_intra = []
    for hs in heads:
        o = [_dot(_bf(jnp.where(tril, _dot_nt(q_dec[rs, hs], k_inv[rs, hs]), 0.0)), vb[rs, hs]) for rs in blocks]
        o_intra.append(jnp.concatenate(o, axis=0) if len(o) > 1 else o[0])
    o_inter = [[None] * nchunk for _ in heads]
    st = [None] * GLA_HEADS
    for c in range(nchunk):
        rs = slice(chunk * c, chunk * (c + 1))
        for h, hs in enumerate(heads):
            if c == 0 or not chained:
                st[h] = state_in(c, h)
            o_inter[h][c] = _dot_nt(q_dec[rs, hs], _bf(st[h]))
            st[h] = st[h] * decay[c][:, hs] + _dot_tn(vb[rs, hs], k_tail[rs, hs])
            if c == nchunk - 1 or not chained:
                state_out(c, h, st[h])
    outs = [o_intra[h] + (jnp.concatenate(o_inter[h], axis=0) if nchunk > 1 else o_inter[h][0])
            for h in range(GLA_HEADS)]
    return jnp.concatenate(outs, axis=1)


def _gla_gate(o, gr, g):
    outs = []
    for h in range(GLA_HEADS):
        hs = slice(GLA_DV * h, GLA_DV * (h + 1))
        oh = o[:, hs]
        ms = jnp.mean(oh * oh, axis=-1, keepdims=True)
        grh = gr[:, hs]
        outs.append(oh * lax.rsqrt(ms + LN_EPS) * g * (grh * jax.nn.sigmoid(grh)))
    return jnp.concatenate(outs, axis=1)


def _lane_half_masks(shape):
    lane = lax.broadcasted_iota(jnp.int32, shape, 1)
    return lane < SWA_HD, lane >= SWA_HD


def _swa_split_q(q):
    lo_m, hi_m = _lane_half_masks((q.shape[0], LANES))
    out = []
    for j in range(SWA_QW // LANES):
        slab = q[:, LANES * j:LANES * (j + 1)]
        out.append((_bf(jnp.where(lo_m, slab, 0.0)), _bf(jnp.where(hi_m, slab, 0.0))))
    return out


def _swa_dup_k(k):
    lo_m, _ = _lane_half_masks(k.shape)
    kr = pltpu.roll(k, SWA_HD, axis=1)
    return [_bf(jnp.where(lo_m, k, kr)), _bf(jnp.where(lo_m, kr, k))]


def _swa_place_v(v):
    lo_m, hi_m = _lane_half_masks(v.shape)
    vr = pltpu.roll(v, SWA_HD, axis=1)
    return [(_bf(jnp.where(lo_m, v, 0.0)), _bf(jnp.where(hi_m, vr, 0.0))),
            (_bf(jnp.where(lo_m, vr, 0.0)), _bf(jnp.where(hi_m, v, 0.0)))]


def _swa_blocks(qrows, kks, vlos, vhis, sink_col, lowers, upper, tq, between=lambda: None):
    n = len(kks)
    stack = lambda xs: jnp.concatenate(xs, axis=0) if n > 1 else xs[0]
    s = stack([_dot_nt(qrows[i], kks[i]) for i in range(n)])
    between()
    if lowers is not None:
        kcol = lax.broadcasted_iota(jnp.int32, s.shape, 1)
        s = jnp.where(kcol >= stack(lowers), jnp.where(kcol <= stack([upper] * n), s, -jnp.inf), -jnp.inf)
    sink = stack([sink_col] * n)
    m = jnp.maximum(jnp.max(s, axis=-1, keepdims=True), sink)
    p = jnp.exp(s - m)
    den = jnp.sum(p, axis=-1, keepdims=True) + jnp.exp(sink - m)
    p = _bf(p / den)
    outs = []
    for i in range(n):
        r = 4 * tq * i
        outs.append((_dot(p[r:r + tq], vlos[i]) + _dot(p[r + tq:r + 2 * tq], vhis[i]),
                     _dot(p[r + 2 * tq:r + 3 * tq], vlos[i]) + _dot(p[r + 3 * tq:r + 4 * tq], vhis[i])))
    return outs


def _sink_col(sinks_ref, g, tq):
    r = lax.broadcasted_iota(jnp.int32, (4 * tq, 1), 0)
    base = 4 * g
    return jnp.where(r < tq, sinks_ref[base],
                     jnp.where(r < 2 * tq, sinks_ref[base + 1],
                               jnp.where(r < 3 * tq, sinks_ref[base + 2], sinks_ref[base + 3])))


def _mem_head(ref, h):
    return _bf(ref[pl.ds(h, MEM_TOKENS, stride=MEM_HEADS), :])


def _mem_attention(q, mk_refs, mv_refs, between=lambda: None):
    qb = _bf(q)
    nseg = len(mk_refs)
    tq = q.shape[0] // nseg
    outs = []
    for h in range(MEM_HEADS):
        hs = slice(MEM_HD * h, MEM_HD * (h + 1))
        s = [_dot_nt(qb[tq * i:tq * (i + 1), hs], _mem_head(mk_refs[i], h)) for i in range(nseg)]
        s = jnp.concatenate(s, axis=0) if nseg > 1 else s[0]
        if h == 0:
            between()
        m = jnp.max(s, axis=-1, keepdims=True)
        p = jnp.exp(s - m)
        p = _bf(p / jnp.sum(p, axis=-1, keepdims=True))
        o = [_dot(p[tq * i:tq * (i + 1)], _mem_head(mv_refs[i], h)) for i in range(nseg)]
        outs.append(jnp.concatenate(o, axis=0) if nseg > 1 else o[0])
    return jnp.concatenate(outs, axis=1)


SWA_BLOCK_LANES = 4 * SWA_PAIR
LOG2E = float(np.log2(np.e))


def _swa_bias_t(sequence_start):
    nk = WINDOW + SWA_PAIR
    col = lax.broadcasted_iota(jnp.int32, (1, SWA_BLOCK_LANES), 1)
    lower = CHUNK * ((col % SWA_PAIR) // CHUNK)
    upper = lower + (WINDOW + CHUNK - 1)
    if sequence_start:
        lower = jnp.maximum(lower, WINDOW)
    krow = lax.broadcasted_iota(jnp.int32, (nk, SWA_BLOCK_LANES), 0)
    return jnp.where(krow >= lower, jnp.where(krow <= upper, 0.0, -jnp.inf), -jnp.inf)


def _swa_attention_t(qs, kk_scr, v_scr, sinks_ref, bias_first, bias_rest, rows, between=lambda: None):
    nk = WINDOW + SWA_PAIR
    npair = rows // SWA_PAIR
    bw = SWA_BLOCK_LANES
    blocks = [(g, p) for g in range(SWA_KV_HEADS) for p in range(npair)]
    sts = []
    for g, p in blocks:
        rs = slice(SWA_PAIR * p, SWA_PAIR * (p + 1))
        qrows = jnp.concatenate([qs[2 * g][0][rs], qs[2 * g][1][rs],
                                 qs[2 * g + 1][0][rs], qs[2 * g + 1][1][rs]], axis=0)
        sts.append(_dot_nt(kk_scr[g, SWA_PAIR * p:SWA_PAIR * p + nk, :], qrows)
                   + (bias_first if p == 0 else bias_rest))
    st = jnp.concatenate(sts, axis=1)
    between()
    width = bw * len(blocks)
    col = lax.broadcasted_iota(jnp.int32, (1, width), 1)
    head = 4 * (col // (bw * npair)) + (col % bw) // SWA_PAIR
    sink = jnp.zeros((1, width), _F32)
    for h in range(SWA_HEADS):
        sink = jnp.where(head == h, sinks_ref[h] * LOG2E, sink)
    m = jnp.maximum(jnp.max(st, axis=0, keepdims=True), sink)
    pt = jnp.exp2(st - m)
    inv = 1.0 / (jnp.sum(pt, axis=0, keepdims=True) + jnp.exp2(sink - m))
    pb = _bf(pt)
    pieces = [[None] * npair for _ in range(SWA_HEADS)]
    for i, (g, p) in enumerate(blocks):
        cs = slice(bw * i, bw * (i + 1))
        o2 = _dot_tn(v_scr[SWA_PAIR * p:SWA_PAIR * p + nk, :], pb[:, cs]) * inv[:, cs]
        for hh in range(4):
            pieces[4 * g + hh][p] = o2[SWA_HD * g:SWA_HD * (g + 1), SWA_PAIR * hh:SWA_PAIR * (hh + 1)]
    return jnp.concatenate([jnp.concatenate(ps, axis=1) if npair > 1 else ps[0] for ps in pieces], axis=0)


def _mem_attention_t(q, mk_ref, mv_ref, between=lambda: None):
    qb = _bf(q)
    R = q.shape[0]
    st = jnp.concatenate([_dot_nt(_mem_head(mk_ref, h), qb[:, MEM_HD * h:MEM_HD * (h + 1)])
                          for h in range(MEM_HEADS)], axis=1)
    between()
    pt = jnp.exp2(st - jnp.max(st, axis=0, keepdims=True))
    inv = 1.0 / jnp.sum(pt, axis=0, keepdims=True)
    pb = _bf(pt)
    outs = [_dot_tn(_mem_head(mv_ref, h), pb[:, R * h:R * (h + 1)]) * inv[:, R * h:R * (h + 1)]
            for h in range(MEM_HEADS)]
    return jnp.concatenate(outs, axis=0)


def _gate(xb, i, w_inb_ref, bgate_ref):
    gl = _dot(xb, w_inb_ref[:, OFF_GL + D_MODEL * i:OFF_GL + D_MODEL * (i + 1)])
    return jax.nn.sigmoid(gl + bgate_ref[:, D_MODEL * i:D_MODEL * (i + 1)])


def _merge_ln1(x, gated_sum, wo_ref, g_ref, b_ref, alpha):
    mix = _dot(_bf(gated_sum), wo_ref[...])
    return _layer_norm(alpha * x + mix, g_ref[...], b_ref[...])


def _forget_log2(xb, w_ga_ref, w2_ref, ba_ref):
    ga = _dot(xb, w_ga_ref[...])
    return _log_sigmoid(_dot(_bf(ga), w2_ref[...]) + ba_ref[...]) * (LOG2E / GLA_TAU)


def _prompt_mixer_kernel(sinks_ref, x_ref, rc_ref, rdn_ref, rup_ref, mk_ref, mv_ref, w_ina_ref, w_ga_ref,
                         w_inb_ref, w2_ref, ba_ref, gng_ref, wbg_ref, wbs_ref, wbm_ref, bgate_ref, wo_ref,
                         ln_g_ref, ln_b_ref,
                         out_ref, gla_out_ref, swak_out_ref, swav_out_ref,
                         s_scr, kk_scr, vv_scr, bias_scr, *, rows, alpha):
    t = pl.program_id(1)
    nchunk = rows // CHUNK

    @pl.when(t == 0)
    def _():
        s_scr[...] = jnp.zeros_like(s_scr)
        kk_scr[:, 0:WINDOW, :] = jnp.zeros((SWA_KV_HEADS, WINDOW, LANES), _BF)
        vv_scr[0:WINDOW, :] = jnp.zeros((WINDOW, LANES), _BF)
        bias_scr[0] = _swa_bias_t(sequence_start=False)
        bias_scr[1] = _swa_bias_t(sequence_start=True)

    x = x_ref[0]
    xb = _bf(x)

    log_a = _forget_log2(xb, w_ga_ref, w2_ref, ba_ref)

    def proj_a(off):
        return _dot(xb, w_ina_ref[:, off:off + GLA_W])

    def state_in(c, h):
        return s_scr[h]

    def state_out(c, h, st):
        s_scr[h] = st
        gla_out_ref[0, h] = jnp.transpose(st)

    gq, gk, gv = proj_a(OFF_GQ), proj_a(OFF_GK), proj_a(OFF_GV)
    gate_a = _gate(xb, 0, w_inb_ref, bgate_ref)
    late = {}

    def dense_a():
        late["gr"] = proj_a(OFF_GR)
        late["sq"] = _dot(xb, w_inb_ref[:, OFF_SQ:OFF_SQ + SWA_QW])
        late["sk"] = _dot(xb, w_inb_ref[:, OFF_SK:OFF_SK + SWA_KW])
        late["sv"] = _dot(xb, w_inb_ref[:, OFF_SV:OFF_SV + SWA_KW])

    def dense_b():
        late["gate_b"] = _gate(xb, 1, w_inb_ref, bgate_ref)

    def dense_c():
        late["gate_c"] = _gate(xb, 2, w_inb_ref, bgate_ref)

    o_a = _gla_tile(gq, gk, gv, log_a, CHUNK, nchunk, state_in, state_out, True, dense_a)
    o_a = _gla_gate(o_a, late["gr"], gng_ref[...])
    gated = gate_a * _dot(_bf(o_a), wbg_ref[...])

    rc, rdn, rup = rc_ref[...], rdn_ref[...], rup_ref[...]
    q = _rope(late["sq"], rc, rdn, rup) * (SWA_HD ** -0.5 * LOG2E)
    k = _rope(late["sk"], rc, rdn, rup)
    v = late["sv"]
    swak_out_ref[0] = k[rows - WINDOW:rows, :]
    swav_out_ref[0] = v[rows - WINDOW:rows, :]
    kk = _swa_dup_k(k)
    vb = _bf(v)
    for g in range(SWA_KV_HEADS):
        kk_scr[g, WINDOW:WINDOW + rows, :] = kk[g]
    vv_scr[WINDOW:WINDOW + rows, :] = vb
    o_bt = _swa_attention_t(_swa_split_q(q), kk_scr, vv_scr, sinks_ref, bias_scr[jnp.where(t == 0, 1, 0)],
                            bias_scr[0], rows, dense_b)
    gated = gated + late["gate_b"] * _dot_tn(_bf(o_bt), wbs_ref[...])
    for g in range(SWA_KV_HEADS):
        kk_scr[g, 0:WINDOW, :] = kk[g][rows - WINDOW:rows]
    vv_scr[0:WINDOW, :] = vb[rows - WINDOW:rows]

    qm = _dot(xb, w_inb_ref[:, OFF_MQ:OFF_MQ + MEM_W]) * (MEM_HD ** -0.5 * LOG2E)
    o_ct = _mem_attention_t(qm, mk_ref.at[0], mv_ref.at[0], dense_c)
    gated = gated + late["gate_c"] * _dot_tn(_bf(o_ct), wbm_ref[...])

    out_ref[0] = _merge_ln1(x, gated, wo_ref, ln_g_ref, ln_b_ref, alpha)


def _sample_mixer_kernel(sinks_ref, x_ref, rc_ref, rdn_ref, rup_ref, mk_ref, mv_ref, gla_in_ref,
                         swak_in_ref, swav_in_ref, w_ina_ref, w_ga_ref, w_inb_ref, w2_ref,
                         ba_ref, gng_ref, wbg_ref, wbs_ref, wbm_ref, bgate_ref, wo_ref, ln_g_ref, ln_b_ref,
                         out_ref, gla_out_ref, swak_out_ref, swav_out_ref, *, nb, tq, alpha):
    rows = nb * tq
    x = x_ref[...].reshape(rows, D_MODEL)
    xb = _bf(x)

    log_a = _forget_log2(xb, w_ga_ref, w2_ref, ba_ref)

    def proj_a(off):
        return _dot(xb, w_ina_ref[:, off:off + GLA_W])

    def state_in(c, h):
        return jnp.transpose(gla_in_ref[c, h])

    def state_out(c, h, st):
        gla_out_ref[c, h] = jnp.transpose(st)

    gq, gk, gv = proj_a(OFF_GQ), proj_a(OFF_GK), proj_a(OFF_GV)
    gate_a = _gate(xb, 0, w_inb_ref, bgate_ref)
    late = {}

    def dense_a():
        late["gr"] = proj_a(OFF_GR)
        late["sq"] = _dot(xb, w_inb_ref[:, OFF_SQ:OFF_SQ + SWA_QW])
        late["sk"] = _dot(xb, w_inb_ref[:, OFF_SK:OFF_SK + SWA_KW])
        late["sv"] = _dot(xb, w_inb_ref[:, OFF_SV:OFF_SV + SWA_KW])

    def dense_b(g):
        if g == 0:
            late["gate_b"] = _gate(xb, 1, w_inb_ref, bgate_ref)
        else:
            late["qm"] = _dot(xb, w_inb_ref[:, OFF_MQ:OFF_MQ + MEM_W]) * (MEM_HD ** -0.5)

    def dense_c():
        late["gate_c"] = _gate(xb, 2, w_inb_ref, bgate_ref)

    o_a = _gla_tile(gq, gk, gv, log_a, tq, nb, state_in, state_out, False, dense_a)
    o_a = _gla_gate(o_a, late["gr"], gng_ref[...])
    gated = gate_a * _dot(_bf(o_a), wbg_ref[...])

    rc = jnp.concatenate([rc_ref[...]] * nb, axis=0)
    rdn = jnp.concatenate([rdn_ref[...]] * nb, axis=0)
    rup = jnp.concatenate([rup_ref[...]] * nb, axis=0)
    q = _rope(late["sq"], rc, rdn, rup) * (SWA_HD ** -0.5)
    k = _rope(late["sk"], rc, rdn, rup)
    v = late["sv"]
    qs = _swa_split_q(q)
    slabs = [[None] * nb for _ in range(SWA_QW // LANES)]
    kks, vvs = [], []
    for bi in range(nb):
        rs = slice(tq * bi, tq * (bi + 1))
        k_all = jnp.concatenate([swak_in_ref[bi], k[rs]], axis=0)
        v_all = jnp.concatenate([swav_in_ref[bi], v[rs]], axis=0)
        swak_out_ref[bi] = k_all[tq:tq + WINDOW]
        swav_out_ref[bi] = v_all[tq:tq + WINDOW]
        kks.append(_swa_dup_k(k_all))
        vvs.append(_swa_place_v(v_all))
    for g in range(SWA_KV_HEADS):
        qrows = []
        for bi in range(nb):
            rs = slice(tq * bi, tq * (bi + 1))
            qrows.append(jnp.concatenate([qs[2 * g][0][rs], qs[2 * g][1][rs],
                                          qs[2 * g + 1][0][rs], qs[2 * g + 1][1][rs]], axis=0))
        outs = _swa_blocks(qrows, [kk[g] for kk in kks], [vv[g][0] for vv in vvs], [vv[g][1] for vv in vvs],
                           _sink_col(sinks_ref, g, tq), None, None, tq, functools.partial(dense_b, g))
        for bi in range(nb):
            slabs[2 * g][bi], slabs[2 * g + 1][bi] = outs[bi]
    o_b = jnp.concatenate([jnp.concatenate(s, axis=0) if nb > 1 else s[0] for s in slabs], axis=1)
    gated = gated + late["gate_b"] * _dot(_bf(o_b), wbs_ref[...])

    o_c = _mem_attention(late["qm"], [mk_ref.at[bi] for bi in range(nb)], [mv_ref.at[bi] for bi in range(nb)],
                         dense_c)
    gated = gated + late["gate_c"] * _dot(_bf(o_c), wbm_ref[...])

    out = _merge_ln1(x, gated, wo_ref, ln_g_ref, ln_b_ref, alpha)
    out_ref[...] = out.reshape(nb, tq, D_MODEL)


_GELU_K1 = -2.0 * float(np.log2(np.e)) * float(np.sqrt(2.0 / np.pi))
_GELU_K2 = _GELU_K1 * 0.044715


def _conv(u2, u1, u0, cw_ref, cb_ref, cs):
    return cb_ref[:, cs] + u2 * cw_ref[0:1, cs] + u1 * cw_ref[1:2, cs] + u0 * cw_ref[2:3, cs]


def _geglu(g, v):
    return g * v / (1.0 + jnp.exp2(g * (g * g * _GELU_K2 + _GELU_K1)))


def _prompt_ffn_kernel(x_ref, wup_ref, cw_ref, cb_ref, wdn_ref, ln_g_ref, ln_b_ref,
                       out_ref, conv_out_ref, u_scr, *, rows, alpha):
    nj = rows // SUBLANES
    base = 2 * SUBLANES
    last = slice(base + rows - SUBLANES, base + rows)
    last2 = slice(base + rows - 2 * SUBLANES, base + rows - SUBLANES)

    @pl.when(pl.program_id(1) == 0)
    def _():
        u_scr[base + rows - 2 * SUBLANES:base + rows, :] = jnp.zeros((2 * SUBLANES, 2 * D_FF), _F32)

    prev_last, prev_last2 = u_scr[last, :], u_scr[last2, :]
    x = jnp.swapaxes(x_ref[0].reshape(SUBLANES, nj, D_MODEL), 0, 1).reshape(rows, D_MODEL)
    u_scr[base:base + rows, :] = _dot(_bf(x), wup_ref[...])
    sub = lax.broadcasted_iota(jnp.int32, (SUBLANES, FFN_COLS), 0)

    def conv_cols(cs):
        wrap = lambda prev, cur: pltpu.roll(jnp.where(sub == SUBLANES - 1, prev, cur), 1, axis=0)
        u_scr[SUBLANES:base, cs] = wrap(prev_last[:, cs], u_scr[last, cs])
        u_scr[0:SUBLANES, cs] = wrap(prev_last2[:, cs], u_scr[last2, cs])
        return _conv(u_scr[0:rows, cs], u_scr[SUBLANES:SUBLANES + rows, cs], u_scr[base:base + rows, cs],
                     cw_ref, cb_ref, cs)

    h = [_bf(_geglu(conv_cols(slice(c0, c0 + FFN_COLS)), conv_cols(slice(D_FF + c0, D_FF + c0 + FFN_COLS))))
         for c0 in range(0, D_FF, FFN_COLS)]
    conv_out_ref[0] = jnp.concatenate([u_scr[base + rows - SUBLANES - 1:base + rows - SUBLANES, :],
                                       u_scr[base + rows - 1:base + rows, :]], axis=0)
    f = _dot(jnp.concatenate(h, axis=1), wdn_ref[...])
    y = _layer_norm(alpha * x + f, ln_g_ref[...], ln_b_ref[...])
    out_ref[0] = jnp.swapaxes(y.reshape(nj, SUBLANES, D_MODEL), 0, 1).reshape(rows, D_MODEL)


def _sample_ffn_kernel(x_ref, hist_ref, wup_ref, cw_ref, cb_ref, wdn_ref, ln_g_ref, ln_b_ref,
                       out_ref, conv_out_ref, u_scr, *, nb, tq, alpha):
    rows = nb * tq
    base = (CONV_W - 1) * nb
    x = jnp.swapaxes(x_ref[...], 0, 1).reshape(rows, D_MODEL)
    u_scr[0:base, :] = jnp.swapaxes(hist_ref[...], 0, 1).reshape(base, 2 * D_FF)
    u_scr[base:base + rows, :] = _dot(_bf(x), wup_ref[...])
    conv = lambda cs: _conv(u_scr[0:rows, cs], u_scr[nb:nb + rows, cs], u_scr[base:base + rows, cs],
                            cw_ref, cb_ref, cs)
    h = _geglu(conv(slice(0, D_FF)), conv(slice(D_FF, 2 * D_FF)))
    conv_out_ref[...] = jnp.swapaxes(u_scr[rows:rows + base, :].reshape(CONV_W - 1, nb, 2 * D_FF), 0, 1)
    f = _dot(_bf(h), wdn_ref[...])
    y = _layer_norm(alpha * x + f, ln_g_ref[...], ln_b_ref[...])
    out_ref[...] = jnp.swapaxes(y.reshape(tq, nb, D_MODEL), 0, 1)


def _mem_kv_kernel(m_ref, w_ref, k_ref, v_ref):
    kv = _dot(_bf(m_ref[0]), w_ref[...])
    for h in range(MEM_HEADS):
        k_ref[0, pl.ds(h, MEM_TOKENS, stride=MEM_HEADS), :] = kv[:, MEM_HD * h:MEM_HD * (h + 1)]
        v_ref[0, pl.ds(h, MEM_TOKENS, stride=MEM_HEADS), :] = kv[:, MEM_W + MEM_HD * h:MEM_W + MEM_HD * (h + 1)]


def _const_spec(shape):
    nd = len(shape)
    return pl.BlockSpec(shape, lambda *_: (0,) * nd, pipeline_mode=pl.Buffered(1))


def _rope_tables(pos):
    half = ROPE_DIM // 2
    inv = np.float32(ROPE_THETA) ** (-np.arange(half, dtype=np.float32) / np.float32(half))
    ang = pos.astype(np.float32)[:, None] * inv[None, :]
    cos, sin = np.cos(ang), np.sin(ang)
    T = pos.shape[0]
    ones = np.ones((T, SWA_HD - ROPE_DIM), np.float32)
    zeros = np.zeros((T, SWA_HD - ROPE_DIM), np.float32)
    zh = np.zeros((T, half), np.float32)
    c = np.concatenate([cos, cos, ones], axis=1)
    dn = np.concatenate([-sin, zh, zeros], axis=1)
    up = np.concatenate([zh, sin, zeros], axis=1)
    rep = LANES // SWA_HD
    return tuple(jnp.asarray(np.concatenate([a] * rep, axis=1), dtype=_F32) for a in (c, dn, up))


def _pack_cols_kernel(wt_ref, o_ref, *, keep):
    t = jnp.transpose(wt_ref[...])
    if keep < t.shape[1]:
        t = jnp.where(lax.broadcasted_iota(jnp.int32, t.shape, 1) < keep, t, 0.0)
    o_ref[...] = _bf(t)


def _pack_cols(wt, col0, ncols, keep, name):
    blk = ncols // 2 if ncols % (2 * LANES) == 0 else ncols
    k = wt.shape[1]
    return pl.pallas_call(
        functools.partial(_pack_cols_kernel, keep=keep),
        grid=(ncols // blk,),
        in_specs=[pl.BlockSpec((pl.Element(blk), pl.Element(k)),
                               lambda i: (pl.multiple_of(col0 + blk * i, SUBLANES), 0))],
        out_specs=pl.BlockSpec((k, blk), lambda i: (0, i)),
        out_shape=jax.ShapeDtypeStruct((k, ncols), _BF),
        compiler_params=_params(("arbitrary",)),
        name=name,
    )(wt)


def _split_w_in(w_in):
    wt = jnp.transpose(w_in)
    return (_pack_cols(wt, 0, IN_A_COLS, IN_A_COLS, "pack_w_in_a"),
            _pack_cols(wt, IN_A_COLS, LANES, GLA_RANK, "pack_w_in_ga"),
            _pack_cols(wt, IN_A_COLS + GLA_RANK, IN_B_COLS, IN_B_COLS, "pack_w_in_b"))


def _mixer_weight_specs():
    return [
        _const_spec((D_MODEL, IN_A_COLS)),
        _const_spec((D_MODEL, LANES)),
        _const_spec((D_MODEL, IN_B_COLS)),
        _const_spec((LANES, GLA_W)),
        _const_spec((1, GLA_W)),
        _const_spec((1, GLA_DV)),
        _const_spec((GLA_W, D_MODEL)),
        _const_spec((SWA_QW, D_MODEL)),
        _const_spec((MEM_W, D_MODEL)),
        _const_spec((1, GATE_W)),
        _const_spec((D_MODEL, D_MODEL)),
        _const_spec((1, D_MODEL)),
        _const_spec((1, D_MODEL)),
    ]


def _ffn_weight_specs():
    return [
        _const_spec((D_MODEL, 2 * D_FF)),
        _const_spec((CONV_W, 2 * D_FF)),
        _const_spec((1, 2 * D_FF)),
        _const_spec((D_FF, D_MODEL)),
        _const_spec((1, D_MODEL)),
        _const_spec((1, D_MODEL)),
    ]


_SMEM_SPEC = pl.BlockSpec(memory_space=pltpu.SMEM)
_MEM_ROWS = MEM_TOKENS * MEM_HEADS


def _params(sem):
    return pltpu.CompilerParams(dimension_semantics=sem, vmem_limit_bytes=V7X_VMEM_LIMIT)


def _prompt_layer(x, mem, sinks, mixer_w, ffn_w, w_mem_kv, alpha):
    B, T, _ = x.shape
    rows = min(PROMPT_ROWS, T)
    nt = T // rows
    assert T % rows == 0 and rows % SWA_PAIR == 0 and rows >= WINDOW
    f32 = jnp.float32
    mk, mv = pl.pallas_call(
        _mem_kv_kernel,
        grid=(B,),
        in_specs=[pl.BlockSpec((1, MEM_TOKENS, D_MODEL), lambda b: (b, 0, 0)),
                  _const_spec((D_MODEL, 2 * MEM_W))],
        out_specs=[pl.BlockSpec((1, _MEM_ROWS, MEM_HD), lambda b: (b, 0, 0))] * 2,
        out_shape=[jax.ShapeDtypeStruct((B, _MEM_ROWS, MEM_HD), f32)] * 2,
        compiler_params=_params(("arbitrary",)),
        name="mem_kv",
    )(mem, w_mem_kv)

    rope = _rope_tables(np.arange(T))
    rope_spec = pl.BlockSpec((rows, LANES), lambda b, t: (t, 0))
    per_batch = lambda *blk: pl.BlockSpec((1,) + blk, lambda b, t: (b,) + (0,) * len(blk))
    x1, gla, swak, swav = pl.pallas_call(
        functools.partial(_prompt_mixer_kernel, rows=rows, alpha=alpha),
        grid=(B, nt),
        in_specs=[_SMEM_SPEC,
                  pl.BlockSpec((1, rows, D_MODEL), lambda b, t: (b, t, 0)),
                  rope_spec, rope_spec, rope_spec,
                  per_batch(_MEM_ROWS, MEM_HD), per_batch(_MEM_ROWS, MEM_HD)] + _mixer_weight_specs(),
        out_specs=[pl.BlockSpec((1, rows, D_MODEL), lambda b, t: (b, t, 0)),
                   per_batch(GLA_HEADS, GLA_DK, GLA_DV),
                   per_batch(WINDOW, SWA_KW), per_batch(WINDOW, SWA_KW)],
        out_shape=[jax.ShapeDtypeStruct((B, T, D_MODEL), f32),
                   jax.ShapeDtypeStruct((B, GLA_HEADS, GLA_DK, GLA_DV), f32),
                   jax.ShapeDtypeStruct((B, WINDOW, SWA_KW), f32),
                   jax.ShapeDtypeStruct((B, WINDOW, SWA_KW), f32)],
        scratch_shapes=[pltpu.VMEM((GLA_HEADS, GLA_DK, GLA_DV), f32),
                        pltpu.VMEM((SWA_KV_HEADS, WINDOW + rows, LANES), _BF),
                        pltpu.VMEM((WINDOW + rows, LANES), _BF),
                        pltpu.VMEM((2, WINDOW + SWA_PAIR, SWA_BLOCK_LANES), f32)],
        compiler_params=_params(("arbitrary", "arbitrary")),
        name="prompt_mixer",
    )(sinks, x, *rope, mk, mv, *mixer_w)

    frows = min(PROMPT_FFN_ROWS, T)
    assert T % frows == 0 and frows >= CONV_W - 1
    y, conv = pl.pallas_call(
        functools.partial(_prompt_ffn_kernel, rows=frows, alpha=alpha),
        grid=(B, T // frows),
        in_specs=[pl.BlockSpec((1, frows, D_MODEL), lambda b, t: (b, t, 0))] + _ffn_weight_specs(),
        out_specs=[pl.BlockSpec((1, frows, D_MODEL), lambda b, t: (b, t, 0)),
                   per_batch(CONV_W - 1, 2 * D_FF)],
        out_shape=[jax.ShapeDtypeStruct((B, T, D_MODEL), f32),
                   jax.ShapeDtypeStruct((B, CONV_W - 1, 2 * D_FF), f32)],
        scratch_shapes=[pltpu.VMEM((2 * SUBLANES + frows, 2 * D_FF), f32)],
        compiler_params=_params(("arbitrary", "arbitrary")),
        name="prompt_ffn",
    )(x1, *ffn_w)
    return y, gla, swak, swav, mk, mv, conv


def _sample_layer(x, gla0, swak0, swav0, memk, memv, conv0, sinks, mixer_w, ffn_w, alpha):
    B, tq, _ = x.shape
    nb = max(1, min(B, SAMPLE_ROWS // tq))
    assert B % nb == 0 and tq % 16 == 0 and tq >= CONV_W - 1
    f32 = jnp.float32
    rope = _rope_tables(PAST_LEN + np.arange(tq))
    rope_spec = pl.BlockSpec((tq, LANES), lambda i: (0, 0))
    blk = lambda *s: pl.BlockSpec((nb,) + s, lambda i: (i,) + (0,) * len(s))
    blk1 = lambda *s: pl.BlockSpec((nb,) + s, lambda i: (i,) + (0,) * len(s), pipeline_mode=pl.Buffered(1))
    x1, gla, swak, swav = pl.pallas_call(
        functools.partial(_sample_mixer_kernel, nb=nb, tq=tq, alpha=alpha),
        grid=(B // nb,),
        in_specs=[_SMEM_SPEC, blk(tq, D_MODEL), rope_spec, rope_spec, rope_spec,
                  blk(_MEM_ROWS, MEM_HD), blk(_MEM_ROWS, MEM_HD),
                  blk1(GLA_HEADS, GLA_DK, GLA_DV), blk1(WINDOW, SWA_KW), blk1(WINDOW, SWA_KW)]
                 + _mixer_weight_specs(),
        out_specs=[blk(tq, D_MODEL), blk(GLA_HEADS, GLA_DK, GLA_DV), blk(WINDOW, SWA_KW), blk(WINDOW, SWA_KW)],
        out_shape=[jax.ShapeDtypeStruct((B, tq, D_MODEL), f32),
                   jax.ShapeDtypeStruct((B, GLA_HEADS, GLA_DK, GLA_DV), f32),
                   jax.ShapeDtypeStruct((B, WINDOW, SWA_KW), f32),
                   jax.ShapeDtypeStruct((B, WINDOW, SWA_KW), f32)],
        compiler_params=_params(("arbitrary",)),
        name="sample_mixer",
    )(sinks, x, *rope, memk, memv, gla0, swak0, swav0, *mixer_w)

    assert B % SUBLANES == 0
    fblk = lambda *s: pl.BlockSpec((SUBLANES,) + s, lambda i: (i,) + (0,) * len(s))
    y, conv = pl.pallas_call(
        functools.partial(_sample_ffn_kernel, nb=SUBLANES, tq=tq, alpha=alpha),
        grid=(B // SUBLANES,),
        in_specs=[fblk(tq, D_MODEL), fblk(CONV_W - 1, 2 * D_FF)] + _ffn_weight_specs(),
        out_specs=[fblk(tq, D_MODEL), fblk(CONV_W - 1, 2 * D_FF)],
        out_shape=[jax.ShapeDtypeStruct((B, tq, D_MODEL), f32),
                   jax.ShapeDtypeStruct((B, CONV_W - 1, 2 * D_FF), f32)],
        scratch_shapes=[pltpu.VMEM(((CONV_W - 1 + tq) * SUBLANES, 2 * D_FF), f32)],
        compiler_params=_params(("arbitrary",)),
        name="sample_ffn",
    )(x1, conv0, *ffn_w)
    return y, gla, swak, swav, conv


def kernel(x_prompt, x_sample, cache_swa_k, cache_swa_v, state_gla, cache_mem_k, cache_mem_v, cache_ffn_conv, mem_prompt, ln1_g, ln1_b, ln2_g, ln2_b, w_in, b_gate, w_gla_a2, b_gla_a, gla_norm_g, swa_sinks, w_mem_kv, w_br_gla, w_br_swa, w_br_mem, w_o, w_up, conv_w, conv_b, w_down):
    depth = w_in.shape[0]
    alpha = float((2 * depth) ** 0.25)
    Bp = x_prompt.shape[0]
    Bs = x_sample.shape[0]
    hp, hs = x_prompt, x_sample
    outs = [[] for _ in range(10)]
    row = lambda a: a.reshape(1, -1)
    for l in range(depth):
        w2 = jnp.pad(w_gla_a2[l], ((0, LANES - GLA_RANK), (0, 0)))
        mixer_w = _split_w_in(w_in[l]) + (
            _bf(w2), row(b_gla_a[l]), row(gla_norm_g[l]),
            _bf(w_br_gla[l]), _bf(w_br_swa[l]), _bf(w_br_mem[l]), row(b_gate[l]), _bf(w_o[l]),
            row(ln1_g[l]), row(ln1_b[l]))
        ffn_w = (_bf(w_up[l]), conv_w[l], row(conv_b[l]), _bf(w_down[l]), row(ln2_g[l]), row(ln2_b[l]))
        sinks = swa_sinks[l]
        hp, g_p, k_p, v_p, mk_p, mv_p, c_p = _prompt_layer(
            hp, mem_prompt, sinks, mixer_w, ffn_w, _bf(w_mem_kv[l]), alpha)
        hs, g_s, k_s, v_s, c_s = _sample_layer(
            hs, state_gla[l],
            cache_swa_k[l].reshape(Bs, WINDOW, SWA_KW), cache_swa_v[l].reshape(Bs, WINDOW, SWA_KW),
            cache_mem_k[l].reshape(Bs, _MEM_ROWS, MEM_HD), cache_mem_v[l].reshape(Bs, _MEM_ROWS, MEM_HD),
            cache_ffn_conv[l], sinks, mixer_w, ffn_w, alpha)
        kv5 = lambda a, b: a.reshape(b, WINDOW, SWA_KV_HEADS, SWA_HD)
        m5 = lambda a: a.reshape(Bp, MEM_TOKENS, MEM_HEADS, MEM_HD)
        for lst, val in zip(outs, (kv5(k_p, Bp), kv5(v_p, Bp), g_p, m5(mk_p), m5(mv_p), c_p,
                                   kv5(k_s, Bs), kv5(v_s, Bs), g_s, c_s)):
            lst.append(val)
    return (hp, hs) + tuple(jnp.stack(o) for o in outs)
```

```python
import functools

import jax
import jax.numpy as jnp
import numpy as np
from jax import lax
from jax.experimental import pallas as pl
from jax.experimental.pallas import tpu as pltpu

D_MODEL = 1024
CHUNK = 64
GLA_HEADS = 4
GLA_DK = 128
GLA_DV = 128
GLA_RANK = 16
GLA_TAU = 16.0
SWA_HEADS = 8
SWA_KV_HEADS = 2
SWA_HD = 64
WINDOW = 128
ROPE_DIM = 16
ROPE_THETA = 500000.0
MEM_TOKENS = 256
MEM_HEADS = 4
MEM_HD = 128
D_FF = 2816
CONV_W = 3
N_BRANCH = 3
PAST_LEN = 2048
LN_EPS = 1e-5

LANES = 128
SUBLANES = 8
V7X_VMEM_LIMIT = 56 * 1024 * 1024
PROMPT_ROWS = 512
PROMPT_FFN_ROWS = 512
FFN_COLS = 256
GLA_BLOCK_ROWS = 256
SAMPLE_ROWS = 256
GLA_W = GLA_HEADS * GLA_DK
SWA_QW = SWA_HEADS * SWA_HD
SWA_KW = SWA_KV_HEADS * SWA_HD
MEM_W = MEM_HEADS * MEM_HD
GATE_W = N_BRANCH * D_MODEL
SWA_PAIR = 2 * CHUNK

OFF_GQ = 0
OFF_GK = OFF_GQ + GLA_W
OFF_GV = OFF_GK + GLA_W
OFF_GR = OFF_GV + GLA_W
IN_A_COLS = OFF_GR + GLA_W
OFF_SQ = 0
OFF_SK = OFF_SQ + SWA_QW
OFF_SV = OFF_SK + SWA_KW
OFF_MQ = OFF_SV + SWA_KW
OFF_GL = OFF_MQ + MEM_W
IN_B_COLS = OFF_GL + GATE_W

_BF = jnp.bfloat16
_F32 = jnp.float32


def _bf(x):
    return x.astype(_BF)


def _dot(a, b):
    return jnp.dot(a, b, preferred_element_type=_F32)


def _dot_nt(a, b):
    return lax.dot_general(a, b, (((1,), (1,)), ((), ())), preferred_element_type=_F32)


def _dot_tn(a, b):
    return lax.dot_general(a, b, (((0,), (0,)), ((), ())), preferred_element_type=_F32)


def _layer_norm(h, g, b):
    mu = jnp.mean(h, axis=-1, keepdims=True)
    d = h - mu
    var = jnp.mean(d * d, axis=-1, keepdims=True)
    return d * lax.rsqrt(var + LN_EPS) * g + b


def _log_sigmoid(x):
    return -(jnp.maximum(-x, 0.0) + jnp.log(1.0 + jnp.exp(-jnp.abs(x))))


def _rope(a, c, s_dn, s_up):
    outs = []
    for j in range(a.shape[1] // LANES):
        slab = a[:, LANES * j:LANES * (j + 1)]
        fwd = pltpu.roll(slab, LANES - ROPE_DIM // 2, axis=1)
        bwd = pltpu.roll(slab, ROPE_DIM // 2, axis=1)
        outs.append(slab * c + fwd * s_dn + bwd * s_up)
    return outs[0] if len(outs) == 1 else jnp.concatenate(outs, axis=1)


def _gla_tile(q, k, v, log2_a, chunk, nchunk, state_in, state_out, chained, between=lambda: None):
    R = chunk * nchunk
    blk = min(R, GLA_BLOCK_ROWS)
    blocks = [slice(r0, r0 + blk) for r0 in range(0, R, blk)]
    row = lax.broadcasted_iota(jnp.int32, (blk, blk), 0)
    col = lax.broadcasted_iota(jnp.int32, (blk, blk), 1)
    tril = (row // chunk == col // chunk) & (col <= row)
    ltri = jnp.where(tril, 1.0, 0.0).astype(_BF)
    hi = _bf(log2_a)
    lo = _bf(log2_a - hi.astype(_F32))
    b = [_dot(ltri, hi[rs]) + _dot(ltri, lo[rs]) for rs in blocks]
    b = jnp.concatenate(b, axis=0) if len(b) > 1 else b[0]
    between()
    decay = [jnp.exp2(b[chunk * (c + 1) - 1:chunk * (c + 1), :]) for c in range(nchunk)]
    q_dec = _bf(q * (GLA_DK ** -0.5) * jnp.exp2(b))
    k_inv = k * jnp.exp2(-b)
    k_tail = _bf(jnp.concatenate([k_inv[chunk * c:chunk * (c + 1)] * decay[c] for c in range(nchunk)], axis=0)
                 if nchunk > 1 else k_inv * decay[0])
    k_inv = _bf(k_inv)
    vb = _bf(v)
    heads = [slice(GLA_DK * h, GLA_DK * (h + 1)) for h in range(GLA_HEADS)]
    o_intra = []
    for hs in heads:
        o = [_dot(_bf(jnp.where(tril, _dot_nt(q_dec[rs, hs], k_inv[rs, hs]), 0.0)), vb[rs, hs]) for rs in blocks]
        o_intra.append(jnp.concatenate(o, axis=0) if len(o) > 1 else o[0])
    o_inter = [[None] * nchunk for _ in heads]
    st = [None] * GLA_HEADS
    for c in range(nchunk):
        rs = slice(chunk * c, chunk * (c + 1))
        for h, hs in enumerate(heads):
            if c == 0 or not chained:
                st[h] = state_in(c, h)
            o_inter[h][c] = _dot_nt(q_dec[rs, hs], _bf(st[h]))
            st[h] = st[h] * decay[c][:, hs] + _dot_tn(vb[rs, hs], k_tail[rs, hs])
            if c == nchunk - 1 or not chained:
                state_out(c, h, st[h])
    outs = [o_intra[h] + (jnp.concatenate(o_inter[h], axis=0) if nchunk > 1 else o_inter[h][0])
            for h in range(GLA_HEADS)]
    return jnp.concatenate(outs, axis=1)


def _gla_gate(o, gr, g):
    outs = []
    for h in range(GLA_HEADS):
        hs = slice(GLA_DV * h, GLA_DV * (h + 1))
        oh = o[:, hs]
        ms = jnp.mean(oh * oh, axis=-1, keepdims=True)
        grh = gr[:, hs]
        outs.append(oh * lax.rsqrt(ms + LN_EPS) * g * (grh * jax.nn.sigmoid(grh)))
    return jnp.concatenate(outs, axis=1)


def _lane_half_masks(shape):
    lane = lax.broadcasted_iota(jnp.int32, shape, 1)
    return lane < SWA_HD, lane >= SWA_HD


def _swa_split_q(q):
    lo_m, hi_m = _lane_half_masks((q.shape[0], LANES))
    out = []
    for j in range(SWA_QW // LANES):
        slab = q[:, LANES * j:LANES * (j + 1)]
        out.append((_bf(jnp.where(lo_m, slab, 0.0)), _bf(jnp.where(hi_m, slab, 0.0))))
    return out


def _swa_dup_k(k):
    lo_m, _ = _lane_half_masks(k.shape)
    kr = pltpu.roll(k, SWA_HD, axis=1)
    return [_bf(jnp.where(lo_m, k, kr)), _bf(jnp.where(lo_m, kr, k))]


def _swa_place_v(v):
    lo_m, hi_m = _lane_half_masks(v.shape)
    vr = pltpu.roll(v, SWA_HD, axis=1)
    return [(_bf(jnp.where(lo_m, v, 0.0)), _bf(jnp.where(hi_m, vr, 0.0))),
            (_bf(jnp.where(lo_m, vr, 0.0)), _bf(jnp.where(hi_m, v, 0.0)))]


def _swa_blocks(qrows, kks, vlos, vhis, sink_col, lowers, upper, tq, between=lambda: None):
    n = len(kks)
    stack = lambda xs: jnp.concatenate(xs, axis=0) if n > 1 else xs[0]
    s = stack([_dot_nt(qrows[i], kks[i]) for i in range(n)])
    between()
    if lowers is not None:
        kcol = lax.broadcasted_iota(jnp.int32, s.shape, 1)
        s = jnp.where(kcol >= stack(lowers), jnp.where(kcol <= stack([upper] * n), s, -jnp.inf), -jnp.inf)
    sink = stack([sink_col] * n)
    m = jnp.maximum(jnp.max(s, axis=-1, keepdims=True), sink)
    p = jnp.exp(s - m)
    den = jnp.sum(p, axis=-1, keepdims=True) + jnp.exp(sink - m)
    p = _bf(p / den)
    outs = []
    for i in range(n):
        r = 4 * tq * i
        outs.append((_dot(p[r:r + tq], vlos[i]) + _dot(p[r + tq:r + 2 * tq], vhis[i]),
                     _dot(p[r + 2 * tq:r + 3 * tq], vlos[i]) + _dot(p[r + 3 * tq:r + 4 * tq], vhis[i])))
    return outs


def _sink_col(sinks_ref, g, tq):
    r = lax.broadcasted_iota(jnp.int32, (4 * tq, 1), 0)
    base = 4 * g
    return jnp.where(r < tq, sinks_ref[base],
                     jnp.where(r < 2 * tq, sinks_ref[base + 1],
                               jnp.where(r < 3 * tq, sinks_ref[base + 2], sinks_ref[base + 3])))


def _mem_head(ref, h):
    return _bf(ref[pl.ds(h, MEM_TOKENS, stride=MEM_HEADS), :])


def _mem_attention(q, mk_refs, mv_refs, between=lambda: None):
    qb = _bf(q)
    nseg = len(mk_refs)
    tq = q.shape[0] // nseg
    outs = []
    for h in range(MEM_HEADS):
        hs = slice(MEM_HD * h, MEM_HD * (h + 1))
        s = [_dot_nt(qb[tq * i:tq * (i + 1), hs], _mem_head(mk_refs[i], h)) for i in range(nseg)]
        s = jnp.concatenate(s, axis=0) if nseg > 1 else s[0]
        if h == 0:
            between()
        m = jnp.max(s, axis=-1, keepdims=True)
        p = jnp.exp(s - m)
        p = _bf(p / jnp.sum(p, axis=-1, keepdims=True))
        o = [_dot(p[tq * i:tq * (i + 1)], _mem_head(mv_refs[i], h)) for i in range(nseg)]
        outs.append(jnp.concatenate(o, axis=0) if nseg > 1 else o[0])
    return jnp.concatenate(outs, axis=1)


SWA_BLOCK_LANES = 4 * SWA_PAIR
LOG2E = float(np.log2(np.e))


def _swa_bias_t(sequence_start):
    nk = WINDOW + SWA_PAIR
    col = lax.broadcasted_iota(jnp.int32, (1, SWA_BLOCK_LANES), 1)
    lower = CHUNK * ((col % SWA_PAIR) // CHUNK)
    upper = lower + (WINDOW + CHUNK - 1)
    if sequence_start:
        lower = jnp.maximum(lower, WINDOW)
    krow = lax.broadcasted_iota(jnp.int32, (nk, SWA_BLOCK_LANES), 0)
    return jnp.where(krow >= lower, jnp.where(krow <= upper, 0.0, -jnp.inf), -jnp.inf)


def _swa_attention_t(qs, kk_scr, v_scr, sinks_ref, bias_first, bias_rest, rows, between=lambda: None):
    nk = WINDOW + SWA_PAIR
    npair = rows // SWA_PAIR
    bw = SWA_BLOCK_LANES
    blocks = [(g, p) for g in range(SWA_KV_HEADS) for p in range(npair)]
    sts = []
    for g, p in blocks:
        rs = slice(SWA_PAIR * p, SWA_PAIR * (p + 1))
        qrows = jnp.concatenate([qs[2 * g][0][rs], qs[2 * g][1][rs],
                                 qs[2 * g + 1][0][rs], qs[2 * g + 1][1][rs]], axis=0)
        sts.append(_dot_nt(kk_scr[g, SWA_PAIR * p:SWA_PAIR * p + nk, :], qrows)
                   + (bias_first if p == 0 else bias_rest))
    st = jnp.concatenate(sts, axis=1)
    between()
    width = bw * len(blocks)
    col = lax.broadcasted_iota(jnp.int32, (1, width), 1)
    head = 4 * (col // (bw * npair)) + (col % bw) // SWA_PAIR
    sink = jnp.zeros((1, width), _F32)
    for h in range(SWA_HEADS):
        sink = jnp.where(head == h, sinks_ref[h] * LOG2E, sink)
    m = jnp.maximum(jnp.max(st, axis=0, keepdims=True), sink)
    pt = jnp.exp2(st - m)
    inv = 1.0 / (jnp.sum(pt, axis=0, keepdims=True) + jnp.exp2(sink - m))
    pb = _bf(pt)
    pieces = [[None] * npair for _ in range(SWA_HEADS)]
    for i, (g, p) in enumerate(blocks):
        cs = slice(bw * i, bw * (i + 1))
        o2 = _dot_tn(v_scr[SWA_PAIR * p:SWA_PAIR * p + nk, :], pb[:, cs]) * inv[:, cs]
        for hh in range(4):
            pieces[4 * g + hh][p] = o2[SWA_HD * g:SWA_HD * (g + 1), SWA_PAIR * hh:SWA_PAIR * (hh + 1)]
    return jnp.concatenate([jnp.concatenate(ps, axis=1) if npair > 1 else ps[0] for ps in pieces], axis=0)


def _mem_attention_t(q, mk_ref, mv_ref, between=lambda: None):
    qb = _bf(q)
    R = q.shape[0]
    st = jnp.concatenate([_dot_nt(_mem_head(mk_ref, h), qb[:, MEM_HD * h:MEM_HD * (h + 1)])
                          for h in range(MEM_HEADS)], axis=1)
    between()
    pt = jnp.exp2(st - jnp.max(st, axis=0, keepdims=True))
    inv = 1.0 / jnp.sum(pt, axis=0, keepdims=True)
    pb = _bf(pt)
    outs = [_dot_tn(_mem_head(mv_ref, h), pb[:, R * h:R * (h + 1)]) * inv[:, R * h:R * (h + 1)]
            for h in range(MEM_HEADS)]
    return jnp.concatenate(outs, axis=0)


def _gate(xb, i, w_inb_ref, bgate_ref):
    gl = _dot(xb, w_inb_ref[:, OFF_GL + D_MODEL * i:OFF_GL + D_MODEL * (i + 1)])
    return jax.nn.sigmoid(gl + bgate_ref[:, D_MODEL * i:D_MODEL * (i + 1)])


def _merge_ln1(x, gated_sum, wo_ref, g_ref, b_ref, alpha):
    mix = _dot(_bf(gated_sum), wo_ref[...])
    return _layer_norm(alpha * x + mix, g_ref[...], b_ref[...])


def _forget_log2(xb, w_ga_ref, w2_ref, ba_ref):
    ga = _dot(xb, w_ga_ref[...])
    return _log_sigmoid(_dot(_bf(ga), w2_ref[...]) + ba_ref[...]) * (LOG2E / GLA_TAU)


def _prompt_mixer_kernel(sinks_ref, x_ref, rc_ref, rdn_ref, rup_ref, mk_ref, mv_ref, w_ina_ref, w_ga_ref,
                         w_inb_ref, w2_ref, ba_ref, gng_ref, wbg_ref, wbs_ref, wbm_ref, bgate_ref, wo_ref,
                         ln_g_ref, ln_b_ref,
                         out_ref, gla_out_ref, swak_out_ref, swav_out_ref,
                         s_scr, kk_scr, vv_scr, bias_scr, *, rows, alpha):
    t = pl.program_id(1)
    nchunk = rows // CHUNK

    @pl.when(t == 0)
    def _():
        s_scr[...] = jnp.zeros_like(s_scr)
        kk_scr[:, 0:WINDOW, :] = jnp.zeros((SWA_KV_HEADS, WINDOW, LANES), _BF)
        vv_scr[0:WINDOW, :] = jnp.zeros((WINDOW, LANES), _BF)
        bias_scr[0] = _swa_bias_t(sequence_start=False)
        bias_scr[1] = _swa_bias_t(sequence_start=True)

    x = x_ref[0]
    xb = _bf(x)

    log_a = _forget_log2(xb, w_ga_ref, w2_ref, ba_ref)

    def proj_a(off):
        return _dot(xb, w_ina_ref[:, off:off + GLA_W])

    def state_in(c, h):
        return s_scr[h]

    def state_out(c, h, st):
        s_scr[h] = st
        gla_out_ref[0, h] = jnp.transpose(st)

    gq, gk, gv = proj_a(OFF_GQ), proj_a(OFF_GK), proj_a(OFF_GV)
    gate_a = _gate(xb, 0, w_inb_ref, bgate_ref)
    late = {}

    def dense_a():
        late["gr"] = proj_a(OFF_GR)
        late["sq"] = _dot(xb, w_inb_ref[:, OFF_SQ:OFF_SQ + SWA_QW])
        late["sk"] = _dot(xb, w_inb_ref[:, OFF_SK:OFF_SK + SWA_KW])
        late["sv"] = _dot(xb, w_inb_ref[:, OFF_SV:OFF_SV + SWA_KW])

    def dense_b():
        late["gate_b"] = _gate(xb, 1, w_inb_ref, bgate_ref)

    def dense_c():
        late["gate_c"] = _gate(xb, 2, w_inb_ref, bgate_ref)

    o_a = _gla_tile(gq, gk, gv, log_a, CHUNK, nchunk, state_in, state_out, True, dense_a)
    o_a = _gla_gate(o_a, late["gr"], gng_ref[...])
    gated = gate_a * _dot(_bf(o_a), wbg_ref[...])

    rc, rdn, rup = rc_ref[...], rdn_ref[...], rup_ref[...]
    q = _rope(late["sq"], rc, rdn, rup) * (SWA_HD ** -0.5 * LOG2E)
    k = _rope(late["sk"], rc, rdn, rup)
    v = late["sv"]
    swak_out_ref[0] = k[rows - WINDOW:rows, :]
    swav_out_ref[0] = v[rows - WINDOW:rows, :]
    kk = _swa_dup_k(k)
    vb = _bf(v)
    for g in range(SWA_KV_HEADS):
        kk_scr[g, WINDOW:WINDOW + rows, :] = kk[g]
    vv_scr[WINDOW:WINDOW + rows, :] = vb
    o_bt = _swa_attention_t(_swa_split_q(q), kk_scr, vv_scr, sinks_ref, bias_scr[jnp.where(t == 0, 1, 0)],
                            bias_scr[0], rows, dense_b)
    gated = gated + late["gate_b"] * _dot_tn(_bf(o_bt), wbs_ref[...])
    for g in range(SWA_KV_HEADS):
        kk_scr[g, 0:WINDOW, :] = kk[g][rows - WINDOW:rows]
    vv_scr[0:WINDOW, :] = vb[rows - WINDOW:rows]

    qm = _dot(xb, w_inb_ref[:, OFF_MQ:OFF_MQ + MEM_W]) * (MEM_HD ** -0.5 * LOG2E)
    o_ct = _mem_attention_t(qm, mk_ref.at[0], mv_ref.at[0], dense_c)
    gated = gated + late["gate_c"] * _dot_tn(_bf(o_ct), wbm_ref[...])

    out_ref[0] = _merge_ln1(x, gated, wo_ref, ln_g_ref, ln_b_ref, alpha)


def _sample_mixer_kernel(sinks_ref, x_ref, rc_ref, rdn_ref, rup_ref, mk_ref, mv_ref, gla_in_ref,
                         swak_in_ref, swav_in_ref, w_ina_ref, w_ga_ref, w_inb_ref, w2_ref,
                         ba_ref, gng_ref, wbg_ref, wbs_ref, wbm_ref, bgate_ref, wo_ref, ln_g_ref, ln_b_ref,
                         out_ref, gla_out_ref, swak_out_ref, swav_out_ref, *, nb, tq, alpha):
    rows = nb * tq
    x = x_ref[...].reshape(rows, D_MODEL)
    xb = _bf(x)

    log_a = _forget_log2(xb, w_ga_ref, w2_ref, ba_ref)

    def proj_a(off):
        return _dot(xb, w_ina_ref[:, off:off + GLA_W])

    def state_in(c, h):
        return jnp.transpose(gla_in_ref[c, h])

    def state_out(c, h, st):
        gla_out_ref[c, h] = jnp.transpose(st)

    gq, gk, gv = proj_a(OFF_GQ), proj_a(OFF_GK), proj_a(OFF_GV)
    gate_a = _gate(xb, 0, w_inb_ref, bgate_ref)
    late = {}

    def dense_a():
        late["gr"] = proj_a(OFF_GR)
        late["sq"] = _dot(xb, w_inb_ref[:, OFF_SQ:OFF_SQ + SWA_QW])
        late["sk"] = _dot(xb, w_inb_ref[:, OFF_SK:OFF_SK + SWA_KW])
        late["sv"] = _dot(xb, w_inb_ref[:, OFF_SV:OFF_SV + SWA_KW])

    def dense_b(g):
        if g == 0:
            late["gate_b"] = _gate(xb, 1, w_inb_ref, bgate_ref)
        else:
            late["qm"] = _dot(xb, w_inb_ref[:, OFF_MQ:OFF_MQ + MEM_W]) * (MEM_HD ** -0.5)

    def dense_c():
        late["gate_c"] = _gate(xb, 2, w_inb_ref, bgate_ref)

    o_a = _gla_tile(gq, gk, gv, log_a, tq, nb, state_in, state_out, False, dense_a)
    o_a = _gla_gate(o_a, late["gr"], gng_ref[...])
    gated = gate_a * _dot(_bf(o_a), wbg_ref[...])

    rc = jnp.concatenate([rc_ref[...]] * nb, axis=0)
    rdn = jnp.concatenate([rdn_ref[...]] * nb, axis=0)
    rup = jnp.concatenate([rup_ref[...]] * nb, axis=0)
    q = _rope(late["sq"], rc, rdn, rup) * (SWA_HD ** -0.5)
    k = _rope(late["sk"], rc, rdn, rup)
    v = late["sv"]
    qs = _swa_split_q(q)
    slabs = [[None] * nb for _ in range(SWA_QW // LANES)]
    kks, vvs = [], []
    for bi in range(nb):
        rs = slice(tq * bi, tq * (bi + 1))
        k_all = jnp.concatenate([swak_in_ref[bi], k[rs]], axis=0)
        v_all = jnp.concatenate([swav_in_ref[bi], v[rs]], axis=0)
        swak_out_ref[bi] = k_all[tq:tq + WINDOW]
        swav_out_ref[bi] = v_all[tq:tq + WINDOW]
        kks.append(_swa_dup_k(k_all))
        vvs.append(_swa_place_v(v_all))
    for g in range(SWA_KV_HEADS):
        qrows = []
        for bi in range(nb):
            rs = slice(tq * bi, tq * (bi + 1))
            qrows.append(jnp.concatenate([qs[2 * g][0][rs], qs[2 * g][1][rs],
                                          qs[2 * g + 1][0][rs], qs[2 * g + 1][1][rs]], axis=0))
        outs = _swa_blocks(qrows, [kk[g] for kk in kks], [vv[g][0] for vv in vvs], [vv[g][1] for vv in vvs],
                           _sink_col(sinks_ref, g, tq), None, None, tq, functools.partial(dense_b, g))
        for bi in range(nb):
            slabs[2 * g][bi], slabs[2 * g + 1][bi] = outs[bi]
    o_b = jnp.concatenate([jnp.concatenate(s, axis=0) if nb > 1 else s[0] for s in slabs], axis=1)
    gated = gated + late["gate_b"] * _dot(_bf(o_b), wbs_ref[...])

    o_c = _mem_attention(late["qm"], [mk_ref.at[bi] for bi in range(nb)], [mv_ref.at[bi] for bi in range(nb)],
                         dense_c)
    gated = gated + late["gate_c"] * _dot(_bf(o_c), wbm_ref[...])

    out = _merge_ln1(x, gated, wo_ref, ln_g_ref, ln_b_ref, alpha)
    out_ref[...] = out.reshape(nb, tq, D_MODEL)


_GELU_K1 = -2.0 * float(np.log2(np.e)) * float(np.sqrt(2.0 / np.pi))
_GELU_K2 = _GELU_K1 * 0.044715


def _conv(u2, u1, u0, cw_ref, cb_ref, cs):
    return cb_ref[:, cs] + u2 * cw_ref[0:1, cs] + u1 * cw_ref[1:2, cs] + u0 * cw_ref[2:3, cs]


def _geglu(g, v):
    return g * v / (1.0 + jnp.exp2(g * (g * g * _GELU_K2 + _GELU_K1)))


def _prompt_ffn_kernel(x_ref, wup_ref, cw_ref, cb_ref, wdn_ref, ln_g_ref, ln_b_ref,
                       out_ref, conv_out_ref, u_scr, *, rows, alpha):
    nj = rows // SUBLANES
    base = 2 * SUBLANES
    last = slice(base + rows - SUBLANES, base + rows)
    last2 = slice(base + rows - 2 * SUBLANES, base + rows - SUBLANES)

    @pl.when(pl.program_id(1) == 0)
    def _():
        u_scr[base + rows - 2 * SUBLANES:base + rows, :] = jnp.zeros((2 * SUBLANES, 2 * D_FF), _F32)

    prev_last, prev_last2 = u_scr[last, :], u_scr[last2, :]
    x = jnp.swapaxes(x_ref[0].reshape(SUBLANES, nj, D_MODEL), 0, 1).reshape(rows, D_MODEL)
    u_scr[base:base + rows, :] = _dot(_bf(x), wup_ref[...])
    sub = lax.broadcasted_iota(jnp.int32, (SUBLANES, FFN_COLS), 0)

    def conv_cols(cs):
        wrap = lambda prev, cur: pltpu.roll(jnp.where(sub == SUBLANES - 1, prev, cur), 1, axis=0)
        u_scr[SUBLANES:base, cs] = wrap(prev_last[:, cs], u_scr[last, cs])
        u_scr[0:SUBLANES, cs] = wrap(prev_last2[:, cs], u_scr[last2, cs])
        return _conv(u_scr[0:rows, cs], u_scr[SUBLANES:SUBLANES + rows, cs], u_scr[base:base + rows, cs],
                     cw_ref, cb_ref, cs)

    h = [_bf(_geglu(conv_cols(slice(c0, c0 + FFN_COLS)), conv_cols(slice(D_FF + c0, D_FF + c0 + FFN_COLS))))
         for c0 in range(0, D_FF, FFN_COLS)]
    conv_out_ref[0] = jnp.concatenate([u_scr[base + rows - SUBLANES - 1:base + rows - SUBLANES, :],
                                       u_scr[base + rows - 1:base + rows, :]], axis=0)
    f = _dot(jnp.concatenate(h, axis=1), wdn_ref[...])
    y = _layer_norm(alpha * x + f, ln_g_ref[...], ln_b_ref[...])
    out_ref[0] = jnp.swapaxes(y.reshape(nj, SUBLANES, D_MODEL), 0, 1).reshape(rows, D_MODEL)


def _sample_ffn_kernel(x_ref, hist_ref, wup_ref, cw_ref, cb_ref, wdn_ref, ln_g_ref, ln_b_ref,
                       out_ref, conv_out_ref, u_scr, *, nb, tq, alpha):
    rows = nb * tq
    base = (CONV_W - 1) * nb
    x = jnp.swapaxes(x_ref[...], 0, 1).reshape(rows, D_MODEL)
    u_scr[0:base, :] = jnp.swapaxes(hist_ref[...], 0, 1).reshape(base, 2 * D_FF)
    u_scr[base:base + rows, :] = _dot(_bf(x), wup_ref[...])
    conv = lambda cs: _conv(u_scr[0:rows, cs], u_scr[nb:nb + rows, cs], u_scr[base:base + rows, cs],
                            cw_ref, cb_ref, cs)
    h = _geglu(conv(slice(0, D_FF)), conv(slice(D_FF, 2 * D_FF)))
    conv_out_ref[...] = jnp.swapaxes(u_scr[rows:rows + base, :].reshape(CONV_W - 1, nb, 2 * D_FF), 0, 1)
    f = _dot(_bf(h), wdn_ref[...])
    y = _layer_norm(alpha * x + f, ln_g_ref[...], ln_b_ref[...])
    out_ref[...] = jnp.swapaxes(y.reshape(tq, nb, D_MODEL), 0, 1)


def _mem_kv_kernel(m_ref, w_ref, k_ref, v_ref):
    kv = _dot(_bf(m_ref[0]), w_ref[...])
    for h in range(MEM_HEADS):
        k_ref[0, pl.ds(h, MEM_TOKENS, stride=MEM_HEADS), :] = kv[:, MEM_HD * h:MEM_HD * (h + 1)]
        v_ref[0, pl.ds(h, MEM_TOKENS, stride=MEM_HEADS), :] = kv[:, MEM_W + MEM_HD * h:MEM_W + MEM_HD * (h + 1)]


def _const_spec(shape):
    nd = len(shape)
    return pl.BlockSpec(shape, lambda *_: (0,) * nd, pipeline_mode=pl.Buffered(1))


def _rope_tables(pos):
    half = ROPE_DIM // 2
    inv = np.float32(ROPE_THETA) ** (-np.arange(half, dtype=np.float32) / np.float32(half))
    ang = pos.astype(np.float32)[:, None] * inv[None, :]
    cos, sin = np.cos(ang), np.sin(ang)
    T = pos.shape[0]
    ones = np.ones((T, SWA_HD - ROPE_DIM), np.float32)
    zeros = np.zeros((T, SWA_HD - ROPE_DIM), np.float32)
    zh = np.zeros((T, half), np.float32)
    c = np.concatenate([cos, cos, ones], axis=1)
    dn = np.concatenate([-sin, zh, zeros], axis=1)
    up = np.concatenate([zh, sin, zeros], axis=1)
    rep = LANES // SWA_HD
    return tuple(jnp.asarray(np.concatenate([a] * rep, axis=1), dtype=_F32) for a in (c, dn, up))


def _pack_cols_kernel(wt_ref, o_ref, *, keep):
    t = jnp.transpose(wt_ref[...])
    if keep < t.shape[1]:
        t = jnp.where(lax.broadcasted_iota(jnp.int32, t.shape, 1) < keep, t, 0.0)
    o_ref[...] = _bf(t)


def _pack_cols(wt, col0, ncols, keep, name):
    blk = ncols // 2 if ncols % (2 * LANES) == 0 else ncols
    k = wt.shape[1]
    return pl.pallas_call(
        functools.partial(_pack_cols_kernel, keep=keep),
        grid=(ncols // blk,),
        in_specs=[pl.BlockSpec((pl.Element(blk), pl.Element(k)),
                               lambda i: (pl.multiple_of(col0 + blk * i, SUBLANES), 0))],
        out_specs=pl.BlockSpec((k, blk), lambda i: (0, i)),
        out_shape=jax.ShapeDtypeStruct((k, ncols), _BF),
        compiler_params=_params(("arbitrary",)),
        name=name,
    )(wt)


def _split_w_in(w_in):
    wt = jnp.transpose(w_in)
    return (_pack_cols(wt, 0, IN_A_COLS, IN_A_COLS, "pack_w_in_a"),
            _pack_cols(wt, IN_A_COLS, LANES, GLA_RANK, "pack_w_in_ga"),
            _pack_cols(wt, IN_A_COLS + GLA_RANK, IN_B_COLS, IN_B_COLS, "pack_w_in_b"))


def _mixer_weight_specs():
    return [
        _const_spec((D_MODEL, IN_A_COLS)),
        _const_spec((D_MODEL, LANES)),
        _const_spec((D_MODEL, IN_B_COLS)),
        _const_spec((LANES, GLA_W)),
        _const_spec((1, GLA_W)),
        _const_spec((1, GLA_DV)),
        _const_spec((GLA_W, D_MODEL)),
        _const_spec((SWA_QW, D_MODEL)),
        _const_spec((MEM_W, D_MODEL)),
        _const_spec((1, GATE_W)),
        _const_spec((D_MODEL, D_MODEL)),
        _const_spec((1, D_MODEL)),
        _const_spec((1, D_MODEL)),
    ]


def _ffn_weight_specs():
    return [
        _const_spec((D_MODEL, 2 * D_FF)),
        _const_spec((CONV_W, 2 * D_FF)),
        _const_spec((1, 2 * D_FF)),
        _const_spec((D_FF, D_MODEL)),
        _const_spec((1, D_MODEL)),
        _const_spec((1, D_MODEL)),
    ]


_SMEM_SPEC = pl.BlockSpec(memory_space=pltpu.SMEM)
_MEM_ROWS = MEM_TOKENS * MEM_HEADS


def _params(sem):
    return pltpu.CompilerParams(dimension_semantics=sem, vmem_limit_bytes=V7X_VMEM_LIMIT)


def _prompt_layer(x, mem, sinks, mixer_w, ffn_w, w_mem_kv, alpha):
    B, T, _ = x.shape
    rows = min(PROMPT_ROWS, T)
    nt = T // rows
    assert T % rows == 0 and rows % SWA_PAIR == 0 and rows >= WINDOW
    f32 = jnp.float32
    mk, mv = pl.pallas_call(
        _mem_kv_kernel,
        grid=(B,),
        in_specs=[pl.BlockSpec((1, MEM_TOKENS, D_MODEL), lambda b: (b, 0, 0)),
                  _const_spec((D_MODEL, 2 * MEM_W))],
        out_specs=[pl.BlockSpec((1, _MEM_ROWS, MEM_HD), lambda b: (b, 0, 0))] * 2,
        out_shape=[jax.ShapeDtypeStruct((B, _MEM_ROWS, MEM_HD), f32)] * 2,
        compiler_params=_params(("arbitrary",)),
        name="mem_kv",
    )(mem, w_mem_kv)

    rope = _rope_tables(np.arange(T))
    rope_spec = pl.BlockSpec((rows, LANES), lambda b, t: (t, 0))
    per_batch = lambda *blk: pl.BlockSpec((1,) + blk, lambda b, t: (b,) + (0,) * len(blk))
    x1, gla, swak, swav = pl.pallas_call(
        functools.partial(_prompt_mixer_kernel, rows=rows, alpha=alpha),
        grid=(B, nt),
        in_specs=[_SMEM_SPEC,
                  pl.BlockSpec((1, rows, D_MODEL), lambda b, t: (b, t, 0)),
                  rope_spec, rope_spec, rope_spec,
                  per_batch(_MEM_ROWS, MEM_HD), per_batch(_MEM_ROWS, MEM_HD)] + _mixer_weight_specs(),
        out_specs=[pl.BlockSpec((1, rows, D_MODEL), lambda b, t: (b, t, 0)),
                   per_batch(GLA_HEADS, GLA_DK, GLA_DV),
                   per_batch(WINDOW, SWA_KW), per_batch(WINDOW, SWA_KW)],
        out_shape=[jax.ShapeDtypeStruct((B, T, D_MODEL), f32),
                   jax.ShapeDtypeStruct((B, GLA_HEADS, GLA_DK, GLA_DV), f32),
                   jax.ShapeDtypeStruct((B, WINDOW, SWA_KW), f32),
                   jax.ShapeDtypeStruct((B, WINDOW, SWA_KW), f32)],
        scratch_shapes=[pltpu.VMEM((GLA_HEADS, GLA_DK, GLA_DV), f32),
                        pltpu.VMEM((SWA_KV_HEADS, WINDOW + rows, LANES), _BF),
                        pltpu.VMEM((WINDOW + rows, LANES), _BF),
                        pltpu.VMEM((2, WINDOW + SWA_PAIR, SWA_BLOCK_LANES), f32)],
        compiler_params=_params(("arbitrary", "arbitrary")),
        name="prompt_mixer",
    )(sinks, x, *rope, mk, mv, *mixer_w)

    frows = min(PROMPT_FFN_ROWS, T)
    assert T % frows == 0 and frows >= CONV_W - 1
    y, conv = pl.pallas_call(
        functools.partial(_prompt_ffn_kernel, rows=frows, alpha=alpha),
        grid=(B, T // frows),
        in_specs=[pl.BlockSpec((1, frows, D_MODEL), lambda b, t: (b, t, 0))] + _ffn_weight_specs(),
        out_specs=[pl.BlockSpec((1, frows, D_MODEL), lambda b, t: (b, t, 0)),
                   per_batch(CONV_W - 1, 2 * D_FF)],
        out_shape=[jax.ShapeDtypeStruct((B, T, D_MODEL), f32),
                   jax.ShapeDtypeStruct((B, CONV_W - 1, 2 * D_FF), f32)],
        scratch_shapes=[pltpu.VMEM((2 * SUBLANES + frows, 2 * D_FF), f32)],
        compiler_params=_params(("arbitrary", "arbitrary")),
        name="prompt_ffn",
    )(x1, *ffn_w)
    return y, gla, swak, swav, mk, mv, conv


def _sample_layer(x, gla0, swak0, swav0, memk, memv, conv0, sinks, mixer_w, ffn_w, alpha):
    B, tq, _ = x.shape
    nb = max(1, min(B, SAMPLE_ROWS // tq))
    assert B % nb == 0 and tq % 16 == 0 and tq >= CONV_W - 1
    f32 = jnp.float32
    rope = _rope_tables(PAST_LEN + np.arange(tq))
    rope_spec = pl.BlockSpec((tq, LANES), lambda i: (0, 0))
    blk = lambda *s: pl.BlockSpec((nb,) + s, lambda i: (i,) + (0,) * len(s))
    blk1 = lambda *s: pl.BlockSpec((nb,) + s, lambda i: (i,) + (0,) * len(s), pipeline_mode=pl.Buffered(1))
    x1, gla, swak, swav = pl.pallas_call(
        functools.partial(_sample_mixer_kernel, nb=nb, tq=tq, alpha=alpha),
        grid=(B // nb,),
        in_specs=[_SMEM_SPEC, blk(tq, D_MODEL), rope_spec, rope_spec, rope_spec,
                  blk(_MEM_ROWS, MEM_HD), blk(_MEM_ROWS, MEM_HD),
                  blk1(GLA_HEADS, GLA_DK, GLA_DV), blk1(WINDOW, SWA_KW), blk1(WINDOW, SWA_KW)]
                 + _mixer_weight_specs(),
        out_specs=[blk(tq, D_MODEL), blk(GLA_HEADS, GLA_DK, GLA_DV), blk(WINDOW, SWA_KW), blk(WINDOW, SWA_KW)],
        out_shape=[jax.ShapeDtypeStruct((B, tq, D_MODEL), f32),
                   jax.ShapeDtypeStruct((B, GLA_HEADS, GLA_DK, GLA_DV), f32),
                   jax.ShapeDtypeStruct((B, WINDOW, SWA_KW), f32),
                   jax.ShapeDtypeStruct((B, WINDOW, SWA_KW), f32)],
        compiler_params=_params(("arbitrary",)),
        name="sample_mixer",
    )(sinks, x, *rope, memk, memv, gla0, swak0, swav0, *mixer_w)

    assert B % SUBLANES == 0
    fnb = 2 * SUBLANES if B % (2 * SUBLANES) == 0 else SUBLANES
    fblk = lambda *s: pl.BlockSpec((fnb,) + s, lambda i: (i,) + (0,) * len(s))
    y, conv = pl.pallas_call(
        functools.partial(_sample_ffn_kernel, nb=fnb, tq=tq, alpha=alpha),
        grid=(B // fnb,),
        in_specs=[fblk(tq, D_MODEL), fblk(CONV_W - 1, 2 * D_FF)] + _ffn_weight_specs(),
        out_specs=[fblk(tq, D_MODEL), fblk(CONV_W - 1, 2 * D_FF)],
        out_shape=[jax.ShapeDtypeStruct((B, tq, D_MODEL), f32),
                   jax.ShapeDtypeStruct((B, CONV_W - 1, 2 * D_FF), f32)],
        scratch_shapes=[pltpu.VMEM(((CONV_W - 1 + tq) * fnb, 2 * D_FF), f32)],
        compiler_params=_params(("arbitrary",)),
        name="sample_ffn",
    )(x1, conv0, *ffn_w)
    return y, gla, swak, swav, conv


def kernel(x_prompt, x_sample, cache_swa_k, cache_swa_v, state_gla, cache_mem_k, cache_mem_v, cache_ffn_conv, mem_prompt, ln1_g, ln1_b, ln2_g, ln2_b, w_in, b_gate, w_gla_a2, b_gla_a, gla_norm_g, swa_sinks, w_mem_kv, w_br_gla, w_br_swa, w_br_mem, w_o, w_up, conv_w, conv_b, w_down):
    depth = w_in.shape[0]
    alpha = float((2 * depth) ** 0.25)
    Bp = x_prompt.shape[0]
    Bs = x_sample.shape[0]
    hp, hs = x_prompt, x_sample
    outs = [[] for _ in range(10)]
    row = lambda a: a.reshape(1, -1)
    for l in range(depth):
        w2 = jnp.pad(w_gla_a2[l], ((0, LANES - GLA_RANK), (0, 0)))
        mixer_w = _split_w_in(w_in[l]) + (
            _bf(w2), row(b_gla_a[l]), row(gla_norm_g[l]),
            _bf(w_br_gla[l]), _bf(w_br_swa[l]), _bf(w_br_mem[l]), row(b_gate[l]), _bf(w_o[l]),
            row(ln1_g[l]), row(ln1_b[l]))
        ffn_w = (_bf(w_up[l]), conv_w[l], row(conv_b[l]), _bf(w_down[l]), row(ln2_g[l]), row(ln2_b[l]))
        sinks = swa_sinks[l]
        hp, g_p, k_p, v_p, mk_p, mv_p, c_p = _prompt_layer(
            hp, mem_prompt, sinks, mixer_w, ffn_w, _bf(w_mem_kv[l]), alpha)
        hs, g_s, k_s, v_s, c_s = _sample_layer(
            hs, state_gla[l],
            cache_swa_k[l].reshape(Bs, WINDOW, SWA_KW), cache_swa_v[l].reshape(Bs, WINDOW, SWA_KW),
            cache_mem_k[l].reshape(Bs, _MEM_ROWS, MEM_HD), cache_mem_v[l].reshape(Bs, _MEM_ROWS, MEM_HD),
            cache_ffn_conv[l], sinks, mixer_w, ffn_w, alpha)
        kv5 = lambda a, b: a.reshape(b, WINDOW, SWA_KV_HEADS, SWA_HD)
        m5 = lambda a: a.reshape(Bp, MEM_TOKENS, MEM_HEADS, MEM_HD)
        for lst, val in zip(outs, (kv5(k_p, Bp), kv5(v_p, Bp), g_p, m5(mk_p), m5(mv_p), c_p,
                                   kv5(k_s, Bs), kv5(v_s, Bs), g_s, c_s)):
            lst.append(val)
    return (hp, hs) + tuple(jnp.stack(o) for o in outs)
```

```python
import functools

import jax
import jax.numpy as jnp
import numpy as np
from jax import lax
from jax.experimental import pallas as pl
from jax.experimental.pallas import tpu as pltpu

D_MODEL = 1024
CHUNK = 64
GLA_HEADS = 4
GLA_DK = 128
GLA_DV = 128
GLA_RANK = 16
GLA_TAU = 16.0
SWA_HEADS = 8
SWA_KV_HEADS = 2
SWA_HD = 64
WINDOW = 128
ROPE_DIM = 16
ROPE_THETA = 500000.0
MEM_TOKENS = 256
MEM_HEADS = 4
MEM_HD = 128
D_FF = 2816
CONV_W = 3
N_BRANCH = 3
PAST_LEN = 2048
LN_EPS = 1e-5

LANES = 128
SUBLANES = 8
V7X_VMEM_LIMIT = 56 * 1024 * 1024
PROMPT_ROWS = 512
PROMPT_FFN_ROWS = 512
FFN_COLS = 256
GLA_BLOCK_ROWS = 256
SAMPLE_ROWS = 256
GLA_W = GLA_HEADS * GLA_DK
SWA_QW = SWA_HEADS * SWA_HD
SWA_KW = SWA_KV_HEADS * SWA_HD
MEM_W = MEM_HEADS * MEM_HD
GATE_W = N_BRANCH * D_MODEL
SWA_PAIR = 2 * CHUNK

OFF_GQ = 0
OFF_GK = OFF_GQ + GLA_W
OFF_GV = OFF_GK + GLA_W
OFF_GR = OFF_GV + GLA_W
IN_A_COLS = OFF_GR + GLA_W
OFF_SQ = 0
OFF_SK = OFF_SQ + SWA_QW
OFF_SV = OFF_SK + SWA_KW
OFF_MQ = OFF_SV + SWA_KW
OFF_GL = OFF_MQ + MEM_W
IN_B_COLS = OFF_GL + GATE_W

_BF = jnp.bfloat16
_F32 = jnp.float32


def _bf(x):
    return x.astype(_BF)


def _dot(a, b):
    return jnp.dot(a, b, preferred_element_type=_F32)


def _dot_nt(a, b):
    return lax.dot_general(a, b, (((1,), (1,)), ((), ())), preferred_element_type=_F32)


def _dot_tn(a, b):
    return lax.dot_general(a, b, (((0,), (0,)), ((), ())), preferred_element_type=_F32)


def _layer_norm(h, g, b):
    mu = jnp.mean(h, axis=-1, keepdims=True)
    d = h - mu
    var = jnp.mean(d * d, axis=-1, keepdims=True)
    return d * lax.rsqrt(var + LN_EPS) * g + b


def _log_sigmoid(x):
    return -(jnp.maximum(-x, 0.0) + jnp.log(1.0 + jnp.exp(-jnp.abs(x))))


def _rope(a, c, s_dn, s_up):
    outs = []
    for j in range(a.shape[1] // LANES):
        slab = a[:, LANES * j:LANES * (j + 1)]
        fwd = pltpu.roll(slab, LANES - ROPE_DIM // 2, axis=1)
        bwd = pltpu.roll(slab, ROPE_DIM // 2, axis=1)
        outs.append(slab * c + fwd * s_dn + bwd * s_up)
    return outs[0] if len(outs) == 1 else jnp.concatenate(outs, axis=1)


def _gla_tile(q, k, v, log2_a, chunk, nchunk, state_in, state_out, chained, between=lambda: None):
    R = chunk * nchunk
    blk = min(R, GLA_BLOCK_ROWS)
    blocks = [slice(r0, r0 + blk) for r0 in range(0, R, blk)]
    row = lax.broadcasted_iota(jnp.int32, (blk, blk), 0)
    col = lax.broadcasted_iota(jnp.int32, (blk, blk), 1)
    tril = (row // chunk == col // chunk) & (col <= row)
    ltri = jnp.where(tril, 1.0, 0.0).astype(_BF)
    hi = _bf(log2_a)
    lo = _bf(log2_a - hi.astype(_F32))
    b = [_dot(ltri, hi[rs]) + _dot(ltri, lo[rs]) for rs in blocks]
    b = jnp.concatenate(b, axis=0) if len(b) > 1 else b[0]
    between()
    decay = [jnp.exp2(b[chunk * (c + 1) - 1:chunk * (c + 1), :]) for c in range(nchunk)]
    q_dec = _bf(q * (GLA_DK ** -0.5) * jnp.exp2(b))
    k_inv = k * jnp.exp2(-b)
    k_tail = _bf(jnp.concatenate([k_inv[chunk * c:chunk * (c + 1)] * decay[c] for c in range(nchunk)], axis=0)
                 if nchunk > 1 else k_inv * decay[0])
    k_inv = _bf(k_inv)
    vb = _bf(v)
    heads = [slice(GLA_DK * h, GLA_DK * (h + 1)) for h in range(GLA_HEADS)]
    o_intra = []
    for hs in heads:
        o = [_dot(_bf(jnp.where(tril, _dot_nt(q_dec[rs, hs], k_inv[rs, hs]), 0.0)), vb[rs, hs]) for rs in blocks]
        o_intra.append(jnp.concatenate(o, axis=0) if len(o) > 1 else o[0])
    o_inter = [[None] * nchunk for _ in heads]
    st = [None] * GLA_HEADS
    for c in range(nchunk):
        rs = slice(chunk * c, chunk * (c + 1))
        for h, hs in enumerate(heads):
            if c == 0 or not chained:
                st[h] = state_in(c, h)
            o_inter[h][c] = _dot_nt(q_dec[rs, hs], _bf(st[h]))
            st[h] = st[h] * decay[c][:, hs] + _dot_tn(vb[rs, hs], k_tail[rs, hs])
            if c == nchunk - 1 or not chained:
                state_out(c, h, st[h])
    outs = [o_intra[h] + (jnp.concatenate(o_inter[h], axis=0) if nchunk > 1 else o_inter[h][0])
            for h in range(GLA_HEADS)]
    return jnp.concatenate(outs, axis=1)


def _gla_gate(o, gr, g):
    outs = []
    for h in range(GLA_HEADS):
        hs = slice(GLA_DV * h, GLA_DV * (h + 1))
        oh = o[:, hs]
        ms = jnp.mean(oh * oh, axis=-1, keepdims=True)
        grh = gr[:, hs]
        outs.append(oh * lax.rsqrt(ms + LN_EPS) * g * (grh * jax.nn.sigmoid(grh)))
    return jnp.concatenate(outs, axis=1)


def _lane_half_masks(shape):
    lane = lax.broadcasted_iota(jnp.int32, shape, 1)
    return lane < SWA_HD, lane >= SWA_HD


def _swa_split_q(q):
    lo_m, hi_m = _lane_half_masks((q.shape[0], LANES))
    out = []
    for j in range(SWA_QW // LANES):
        slab = q[:, LANES * j:LANES * (j + 1)]
        out.append((_bf(jnp.where(lo_m, slab, 0.0)), _bf(jnp.where(hi_m, slab, 0.0))))
    return out


def _swa_dup_k(k):
    lo_m, _ = _lane_half_masks(k.shape)
    kr = pltpu.roll(k, SWA_HD, axis=1)
    return [_bf(jnp.where(lo_m, k, kr)), _bf(jnp.where(lo_m, kr, k))]


def _swa_place_v(v):
    lo_m, hi_m = _lane_half_masks(v.shape)
    vr = pltpu.roll(v, SWA_HD, axis=1)
    return [(_bf(jnp.where(lo_m, v, 0.0)), _bf(jnp.where(hi_m, vr, 0.0))),
            (_bf(jnp.where(lo_m, vr, 0.0)), _bf(jnp.where(hi_m, v, 0.0)))]


def _swa_blocks(qrows, kks, vlos, vhis, sink_col, lowers, upper, tq, between=lambda: None):
    n = len(kks)
    stack = lambda xs: jnp.concatenate(xs, axis=0) if n > 1 else xs[0]
    s = stack([_dot_nt(qrows[i], kks[i]) for i in range(n)])
    between()
    if lowers is not None:
        kcol = lax.broadcasted_iota(jnp.int32, s.shape, 1)
        s = jnp.where(kcol >= stack(lowers), jnp.where(kcol <= stack([upper] * n), s, -jnp.inf), -jnp.inf)
    sink = stack([sink_col] * n)
    m = jnp.maximum(jnp.max(s, axis=-1, keepdims=True), sink)
    p = jnp.exp(s - m)
    den = jnp.sum(p, axis=-1, keepdims=True) + jnp.exp(sink - m)
    p = _bf(p / den)
    outs = []
    for i in range(n):
        r = 4 * tq * i
        outs.append((_dot(p[r:r + tq], vlos[i]) + _dot(p[r + tq:r + 2 * tq], vhis[i]),
                     _dot(p[r + 2 * tq:r + 3 * tq], vlos[i]) + _dot(p[r + 3 * tq:r + 4 * tq], vhis[i])))
    return outs


def _sink_col(sinks_ref, g, tq):
    r = lax.broadcasted_iota(jnp.int32, (4 * tq, 1), 0)
    base = 4 * g
    return jnp.where(r < tq, sinks_ref[base],
                     jnp.where(r < 2 * tq, sinks_ref[base + 1],
                               jnp.where(r < 3 * tq, sinks_ref[base + 2], sinks_ref[base + 3])))


def _mem_head(ref, h):
    return _bf(ref[pl.ds(h, MEM_TOKENS, stride=MEM_HEADS), :])


def _mem_attention(q, mk_refs, mv_refs, between=lambda: None):
    qb = _bf(q)
    nseg = len(mk_refs)
    tq = q.shape[0] // nseg
    outs = []
    for h in range(MEM_HEADS):
        hs = slice(MEM_HD * h, MEM_HD * (h + 1))
        s = [_dot_nt(qb[tq * i:tq * (i + 1), hs], _mem_head(mk_refs[i], h)) for i in range(nseg)]
        s = jnp.concatenate(s, axis=0) if nseg > 1 else s[0]
        if h == 0:
            between()
        m = jnp.max(s, axis=-1, keepdims=True)
        p = jnp.exp(s - m)
        p = _bf(p / jnp.sum(p, axis=-1, keepdims=True))
        o = [_dot(p[tq * i:tq * (i + 1)], _mem_head(mv_refs[i], h)) for i in range(nseg)]
        outs.append(jnp.concatenate(o, axis=0) if nseg > 1 else o[0])
    return jnp.concatenate(outs, axis=1)


SWA_BLOCK_LANES = 4 * SWA_PAIR
LOG2E = float(np.log2(np.e))


def _swa_bias_t(sequence_start):
    nk = WINDOW + SWA_PAIR
    col = lax.broadcasted_iota(jnp.int32, (1, SWA_BLOCK_LANES), 1)
    lower = CHUNK * ((col % SWA_PAIR) // CHUNK)
    upper = lower + (WINDOW + CHUNK - 1)
    if sequence_start:
        lower = jnp.maximum(lower, WINDOW)
    krow = lax.broadcasted_iota(jnp.int32, (nk, SWA_BLOCK_LANES), 0)
    return jnp.where(krow >= lower, jnp.where(krow <= upper, 0.0, -jnp.inf), -jnp.inf)


def _swa_attention_t(qs, kk_scr, v_scr, sinks_ref, bias_first, bias_rest, rows, between=lambda: None):
    nk = WINDOW + SWA_PAIR
    npair = rows // SWA_PAIR
    bw = SWA_BLOCK_LANES
    blocks = [(g, p) for g in range(SWA_KV_HEADS) for p in range(npair)]
    sts = []
    for g, p in blocks:
        rs = slice(SWA_PAIR * p, SWA_PAIR * (p + 1))
        qrows = jnp.concatenate([qs[2 * g][0][rs], qs[2 * g][1][rs],
                                 qs[2 * g + 1][0][rs], qs[2 * g + 1][1][rs]], axis=0)
        sts.append(_dot_nt(kk_scr[g, SWA_PAIR * p:SWA_PAIR * p + nk, :], qrows)
                   + (bias_first if p == 0 else bias_rest))
    st = jnp.concatenate(sts, axis=1)
    between()
    width = bw * len(blocks)
    col = lax.broadcasted_iota(jnp.int32, (1, width), 1)
    head = 4 * (col // (bw * npair)) + (col % bw) // SWA_PAIR
    sink = jnp.zeros((1, width), _F32)
    for h in range(SWA_HEADS):
        sink = jnp.where(head == h, sinks_ref[h] * LOG2E, sink)
    m = jnp.maximum(jnp.max(st, axis=0, keepdims=True), sink)
    pt = jnp.exp2(st - m)
    inv = 1.0 / (jnp.sum(pt, axis=0, keepdims=True) + jnp.exp2(sink - m))
    pb = _bf(pt)
    pieces = [[None] * npair for _ in range(SWA_HEADS)]
    for i, (g, p) in enumerate(blocks):
        cs = slice(bw * i, bw * (i + 1))
        o2 = _dot_tn(v_scr[SWA_PAIR * p:SWA_PAIR * p + nk, :], pb[:, cs]) * inv[:, cs]
        for hh in range(4):
            pieces[4 * g + hh][p] = o2[SWA_HD * g:SWA_HD * (g + 1), SWA_PAIR * hh:SWA_PAIR * (hh + 1)]
    return jnp.concatenate([jnp.concatenate(ps, axis=1) if npair > 1 else ps[0] for ps in pieces], axis=0)


def _mem_attention_t(q, mk_ref, mv_ref, between=lambda: None):
    qb = _bf(q)
    R = q.shape[0]
    st = jnp.concatenate([_dot_nt(_mem_head(mk_ref, h), qb[:, MEM_HD * h:MEM_HD * (h + 1)])
                          for h in range(MEM_HEADS)], axis=1)
    between()
    pt = jnp.exp2(st - jnp.max(st, axis=0, keepdims=True))
    inv = 1.0 / jnp.sum(pt, axis=0, keepdims=True)
    pb = _bf(pt)
    outs = [_dot_tn(_mem_head(mv_ref, h), pb[:, R * h:R * (h + 1)]) * inv[:, R * h:R * (h + 1)]
            for h in range(MEM_HEADS)]
    return jnp.concatenate(outs, axis=0)


def _gate(xb, i, w_inb_ref, bgate_ref):
    gl = _dot(xb, w_inb_ref[:, OFF_GL + D_MODEL * i:OFF_GL + D_MODEL * (i + 1)])
    return jax.nn.sigmoid(gl + bgate_ref[:, D_MODEL * i:D_MODEL * (i + 1)])


def _merge_ln1(x, gated_sum, wo_ref, g_ref, b_ref, alpha):
    mix = _dot(_bf(gated_sum), wo_ref[...])
    return _layer_norm(alpha * x + mix, g_ref[...], b_ref[...])


def _forget_log2(xb, w_ga_ref, w2_ref, ba_ref):
    ga = _dot(xb, w_ga_ref[...])
    return _log_sigmoid(_dot(_bf(ga), w2_ref[...]) + ba_ref[...]) * (LOG2E / GLA_TAU)


def _prompt_mixer_kernel(sinks_ref, x_ref, rope_ref, mk_ref, mv_ref, w_ina_ref, w_ga_ref,
                         w_inb_ref, w2_ref, ba_ref, gng_ref, wbg_ref, wbs_ref, wbm_ref, bgate_ref, wo_ref,
                         ln_g_ref, ln_b_ref,
                         out_ref, gla_out_ref, swak_out_ref, swav_out_ref,
                         s_scr, kk_scr, vv_scr, bias_scr, *, rows, alpha):
    t = pl.program_id(1)
    nchunk = rows // CHUNK

    @pl.when(t == 0)
    def _():
        s_scr[...] = jnp.zeros_like(s_scr)
        kk_scr[:, 0:WINDOW, :] = jnp.zeros((SWA_KV_HEADS, WINDOW, LANES), _BF)
        vv_scr[0:WINDOW, :] = jnp.zeros((WINDOW, LANES), _BF)
        bias_scr[0] = _swa_bias_t(sequence_start=False)
        bias_scr[1] = _swa_bias_t(sequence_start=True)

    x = x_ref[0]
    xb = _bf(x)

    log_a = _forget_log2(xb, w_ga_ref, w2_ref, ba_ref)

    def proj_a(off):
        return _dot(xb, w_ina_ref[:, off:off + GLA_W])

    def state_in(c, h):
        return s_scr[h]

    def state_out(c, h, st):
        s_scr[h] = st
        gla_out_ref[0, h] = jnp.transpose(st)

    gq, gk, gv = proj_a(OFF_GQ), proj_a(OFF_GK), proj_a(OFF_GV)
    gate_a = _gate(xb, 0, w_inb_ref, bgate_ref)
    late = {}

    def dense_a():
        late["gr"] = proj_a(OFF_GR)
        late["sq"] = _dot(xb, w_inb_ref[:, OFF_SQ:OFF_SQ + SWA_QW])
        late["sk"] = _dot(xb, w_inb_ref[:, OFF_SK:OFF_SK + SWA_KW])
        late["sv"] = _dot(xb, w_inb_ref[:, OFF_SV:OFF_SV + SWA_KW])

    def dense_b():
        late["gate_b"] = _gate(xb, 1, w_inb_ref, bgate_ref)

    def dense_c():
        late["gate_c"] = _gate(xb, 2, w_inb_ref, bgate_ref)

    o_a = _gla_tile(gq, gk, gv, log_a, CHUNK, nchunk, state_in, state_out, True, dense_a)
    o_a = _gla_gate(o_a, late["gr"], gng_ref[...])
    gated = gate_a * _dot(_bf(o_a), wbg_ref[...])

    rc, rdn, rup = (rope_ref[:, LANES * i:LANES * (i + 1)] for i in range(3))
    q = _rope(late["sq"], rc, rdn, rup) * (SWA_HD ** -0.5 * LOG2E)
    k = _rope(late["sk"], rc, rdn, rup)
    v = late["sv"]
    swak_out_ref[0] = k[rows - WINDOW:rows, :]
    swav_out_ref[0] = v[rows - WINDOW:rows, :]
    kk = _swa_dup_k(k)
    vb = _bf(v)
    for g in range(SWA_KV_HEADS):
        kk_scr[g, WINDOW:WINDOW + rows, :] = kk[g]
    vv_scr[WINDOW:WINDOW + rows, :] = vb
    o_bt = _swa_attention_t(_swa_split_q(q), kk_scr, vv_scr, sinks_ref, bias_scr[jnp.where(t == 0, 1, 0)],
                            bias_scr[0], rows, dense_b)
    gated = gated + late["gate_b"] * _dot_tn(_bf(o_bt), wbs_ref[...])
    for g in range(SWA_KV_HEADS):
        kk_scr[g, 0:WINDOW, :] = kk[g][rows - WINDOW:rows]
    vv_scr[0:WINDOW, :] = vb[rows - WINDOW:rows]

    qm = _dot(xb, w_inb_ref[:, OFF_MQ:OFF_MQ + MEM_W]) * (MEM_HD ** -0.5 * LOG2E)
    o_ct = _mem_attention_t(qm, mk_ref.at[0], mv_ref.at[0], dense_c)
    gated = gated + late["gate_c"] * _dot_tn(_bf(o_ct), wbm_ref[...])

    out_ref[0] = _merge_ln1(x, gated, wo_ref, ln_g_ref, ln_b_ref, alpha)


def _sample_mixer_kernel(sinks_ref, x_ref, rc_ref, rdn_ref, rup_ref, mk_ref, mv_ref, gla_in_ref,
                         swak_in_ref, swav_in_ref, w_ina_ref, w_ga_ref, w_inb_ref, w2_ref,
                         ba_ref, gng_ref, wbg_ref, wbs_ref, wbm_ref, bgate_ref, wo_ref, ln_g_ref, ln_b_ref,
                         out_ref, gla_out_ref, swak_out_ref, swav_out_ref, *, nb, tq, alpha):
    rows = nb * tq
    x = x_ref[...].reshape(rows, D_MODEL)
    xb = _bf(x)

    log_a = _forget_log2(xb, w_ga_ref, w2_ref, ba_ref)

    def proj_a(off):
        return _dot(xb, w_ina_ref[:, off:off + GLA_W])

    def state_in(c, h):
        return jnp.transpose(gla_in_ref[c, h])

    def state_out(c, h, st):
        gla_out_ref[c, h] = jnp.transpose(st)

    gq, gk, gv = proj_a(OFF_GQ), proj_a(OFF_GK), proj_a(OFF_GV)
    gate_a = _gate(xb, 0, w_inb_ref, bgate_ref)
    late = {}

    def dense_a():
        late["gr"] = proj_a(OFF_GR)
        late["sq"] = _dot(xb, w_inb_ref[:, OFF_SQ:OFF_SQ + SWA_QW])
        late["sk"] = _dot(xb, w_inb_ref[:, OFF_SK:OFF_SK + SWA_KW])
        late["sv"] = _dot(xb, w_inb_ref[:, OFF_SV:OFF_SV + SWA_KW])

    def dense_b(g):
        if g == 0:
            late["gate_b"] = _gate(xb, 1, w_inb_ref, bgate_ref)
        else:
            late["qm"] = _dot(xb, w_inb_ref[:, OFF_MQ:OFF_MQ + MEM_W]) * (MEM_HD ** -0.5)

    def dense_c():
        late["gate_c"] = _gate(xb, 2, w_inb_ref, bgate_ref)

    o_a = _gla_tile(gq, gk, gv, log_a, tq, nb, state_in, state_out, False, dense_a)
    o_a = _gla_gate(o_a, late["gr"], gng_ref[...])
    gated = gate_a * _dot(_bf(o_a), wbg_ref[...])

    rc = jnp.concatenate([rc_ref[...]] * nb, axis=0)
    rdn = jnp.concatenate([rdn_ref[...]] * nb, axis=0)
    rup = jnp.concatenate([rup_ref[...]] * nb, axis=0)
    q = _rope(late["sq"], rc, rdn, rup) * (SWA_HD ** -0.5)
    k = _rope(late["sk"], rc, rdn, rup)
    v = late["sv"]
    qs = _swa_split_q(q)
    slabs = [[None] * nb for _ in range(SWA_QW // LANES)]
    kks, vvs = [], []
    for bi in range(nb):
        rs = slice(tq * bi, tq * (bi + 1))
        k_all = jnp.concatenate([swak_in_ref[bi], k[rs]], axis=0)
        v_all = jnp.concatenate([swav_in_ref[bi], v[rs]], axis=0)
        swak_out_ref[bi] = k_all[tq:tq + WINDOW]
        swav_out_ref[bi] = v_all[tq:tq + WINDOW]
        kks.append(_swa_dup_k(k_all))
        vvs.append(_swa_place_v(v_all))
    for g in range(SWA_KV_HEADS):
        qrows = []
        for bi in range(nb):
            rs = slice(tq * bi, tq * (bi + 1))
            qrows.append(jnp.concatenate([qs[2 * g][0][rs], qs[2 * g][1][rs],
                                          qs[2 * g + 1][0][rs], qs[2 * g + 1][1][rs]], axis=0))
        outs = _swa_blocks(qrows, [kk[g] for kk in kks], [vv[g][0] for vv in vvs], [vv[g][1] for vv in vvs],
                           _sink_col(sinks_ref, g, tq), None, None, tq, functools.partial(dense_b, g))
        for bi in range(nb):
            slabs[2 * g][bi], slabs[2 * g + 1][bi] = outs[bi]
    o_b = jnp.concatenate([jnp.concatenate(s, axis=0) if nb > 1 else s[0] for s in slabs], axis=1)
    gated = gated + late["gate_b"] * _dot(_bf(o_b), wbs_ref[...])

    o_c = _mem_attention(late["qm"], [mk_ref.at[bi] for bi in range(nb)], [mv_ref.at[bi] for bi in range(nb)],
                         dense_c)
    gated = gated + late["gate_c"] * _dot(_bf(o_c), wbm_ref[...])

    out = _merge_ln1(x, gated, wo_ref, ln_g_ref, ln_b_ref, alpha)
    out_ref[...] = out.reshape(nb, tq, D_MODEL)


_GELU_K1 = -2.0 * float(np.log2(np.e)) * float(np.sqrt(2.0 / np.pi))
_GELU_K2 = _GELU_K1 * 0.044715


def _conv(u2, u1, u0, cw_ref, cb_ref, cs):
    return cb_ref[:, cs] + u2 * cw_ref[0:1, cs] + u1 * cw_ref[1:2, cs] + u0 * cw_ref[2:3, cs]


def _geglu(g, v):
    return g * v / (1.0 + jnp.exp2(g * (g * g * _GELU_K2 + _GELU_K1)))


def _prompt_ffn_kernel(x_ref, wup_ref, cw_ref, cb_ref, wdn_ref, ln_g_ref, ln_b_ref,
                       out_ref, conv_out_ref, u_scr, *, rows, alpha):
    nj = rows // SUBLANES
    base = 2 * SUBLANES
    last = slice(base + rows - SUBLANES, base + rows)
    last2 = slice(base + rows - 2 * SUBLANES, base + rows - SUBLANES)

    @pl.when(pl.program_id(1) == 0)
    def _():
        u_scr[base + rows - 2 * SUBLANES:base + rows, :] = jnp.zeros((2 * SUBLANES, 2 * D_FF), _F32)

    prev_last, prev_last2 = u_scr[last, :], u_scr[last2, :]
    x = jnp.swapaxes(x_ref[0].reshape(SUBLANES, nj, D_MODEL), 0, 1).reshape(rows, D_MODEL)
    u_scr[base:base + rows, :] = _dot(_bf(x), wup_ref[...])
    sub = lax.broadcasted_iota(jnp.int32, (SUBLANES, FFN_COLS), 0)

    def conv_cols(cs):
        wrap = lambda prev, cur: pltpu.roll(jnp.where(sub == SUBLANES - 1, prev, cur), 1, axis=0)
        u_scr[SUBLANES:base, cs] = wrap(prev_last[:, cs], u_scr[last, cs])
        u_scr[0:SUBLANES, cs] = wrap(prev_last2[:, cs], u_scr[last2, cs])
        return _conv(u_scr[0:rows, cs], u_scr[SUBLANES:SUBLANES + rows, cs], u_scr[base:base + rows, cs],
                     cw_ref, cb_ref, cs)

    h = [_bf(_geglu(conv_cols(slice(c0, c0 + FFN_COLS)), conv_cols(slice(D_FF + c0, D_FF + c0 + FFN_COLS))))
         for c0 in range(0, D_FF, FFN_COLS)]
    conv_out_ref[0] = jnp.concatenate([u_scr[base + rows - SUBLANES - 1:base + rows - SUBLANES, :],
                                       u_scr[base + rows - 1:base + rows, :]], axis=0)
    f = _dot(jnp.concatenate(h, axis=1), wdn_ref[...])
    y = _layer_norm(alpha * x + f, ln_g_ref[...], ln_b_ref[...])
    out_ref[0] = jnp.swapaxes(y.reshape(nj, SUBLANES, D_MODEL), 0, 1).reshape(rows, D_MODEL)


def _sample_ffn_kernel(x_ref, hist_ref, wup_ref, cw_ref, cb_ref, wdn_ref, ln_g_ref, ln_b_ref,
                       out_ref, conv_out_ref, u_scr, *, nb, tq, alpha):
    rows = nb * tq
    base = (CONV_W - 1) * nb
    x = jnp.swapaxes(x_ref[...], 0, 1).reshape(rows, D_MODEL)
    u_scr[0:base, :] = jnp.swapaxes(hist_ref[...], 0, 1).reshape(base, 2 * D_FF)
    u_scr[base:base + rows, :] = _dot(_bf(x), wup_ref[...])
    conv = lambda cs: _conv(u_scr[0:rows, cs], u_scr[nb:nb + rows, cs], u_scr[base:base + rows, cs],
                            cw_ref, cb_ref, cs)
    h = _geglu(conv(slice(0, D_FF)), conv(slice(D_FF, 2 * D_FF)))
    conv_out_ref[...] = jnp.swapaxes(u_scr[rows:rows + base, :].reshape(CONV_W - 1, nb, 2 * D_FF), 0, 1)
    f = _dot(_bf(h), wdn_ref[...])
    y = _layer_norm(alpha * x + f, ln_g_ref[...], ln_b_ref[...])
    out_ref[...] = jnp.swapaxes(y.reshape(tq, nb, D_MODEL), 0, 1)


def _mem_kv_kernel(m_ref, w_ref, k_ref, v_ref):
    kv = _dot(_bf(m_ref[0]), w_ref[...])
    for h in range(MEM_HEADS):
        k_ref[0, pl.ds(h, MEM_TOKENS, stride=MEM_HEADS), :] = kv[:, MEM_HD * h:MEM_HD * (h + 1)]
        v_ref[0, pl.ds(h, MEM_TOKENS, stride=MEM_HEADS), :] = kv[:, MEM_W + MEM_HD * h:MEM_W + MEM_HD * (h + 1)]


def _const_spec(shape):
    nd = len(shape)
    return pl.BlockSpec(shape, lambda *_: (0,) * nd, pipeline_mode=pl.Buffered(1))


def _rope_tables(pos):
    half = ROPE_DIM // 2
    inv = np.float32(ROPE_THETA) ** (-np.arange(half, dtype=np.float32) / np.float32(half))
    ang = pos.astype(np.float32)[:, None] * inv[None, :]
    cos, sin = np.cos(ang), np.sin(ang)
    T = pos.shape[0]
    ones = np.ones((T, SWA_HD - ROPE_DIM), np.float32)
    zeros = np.zeros((T, SWA_HD - ROPE_DIM), np.float32)
    zh = np.zeros((T, half), np.float32)
    c = np.concatenate([cos, cos, ones], axis=1)
    dn = np.concatenate([-sin, zh, zeros], axis=1)
    up = np.concatenate([zh, sin, zeros], axis=1)
    rep = LANES // SWA_HD
    return tuple(jnp.asarray(np.concatenate([a] * rep, axis=1), dtype=_F32) for a in (c, dn, up))


def _pack_cols_kernel(wt_ref, o_ref, *, keep):
    t = jnp.transpose(wt_ref[...])
    if keep < t.shape[1]:
        t = jnp.where(lax.broadcasted_iota(jnp.int32, t.shape, 1) < keep, t, 0.0)
    o_ref[...] = _bf(t)


def _pack_cols(wt, col0, ncols, keep, name):
    blk = ncols // 2 if ncols % (2 * LANES) == 0 else ncols
    k = wt.shape[1]
    return pl.pallas_call(
        functools.partial(_pack_cols_kernel, keep=keep),
        grid=(ncols // blk,),
        in_specs=[pl.BlockSpec((pl.Element(blk), pl.Element(k)),
                               lambda i: (pl.multiple_of(col0 + blk * i, SUBLANES), 0))],
        out_specs=pl.BlockSpec((k, blk), lambda i: (0, i)),
        out_shape=jax.ShapeDtypeStruct((k, ncols), _BF),
        compiler_params=_params(("arbitrary",)),
        name=name,
    )(wt)


def _split_w_in(w_in):
    wt = jnp.transpose(w_in)
    return (_pack_cols(wt, 0, IN_A_COLS, IN_A_COLS, "pack_w_in_a"),
            _pack_cols(wt, IN_A_COLS, LANES, GLA_RANK, "pack_w_in_ga"),
            _pack_cols(wt, IN_A_COLS + GLA_RANK, IN_B_COLS, IN_B_COLS, "pack_w_in_b"))


def _mixer_weight_specs():
    return [
        _const_spec((D_MODEL, IN_A_COLS)),
        _const_spec((D_MODEL, LANES)),
        _const_spec((D_MODEL, IN_B_COLS)),
        _const_spec((LANES, GLA_W)),
        _const_spec((1, GLA_W)),
        _const_spec((1, GLA_DV)),
        _const_spec((GLA_W, D_MODEL)),
        _const_spec((SWA_QW, D_MODEL)),
        _const_spec((MEM_W, D_MODEL)),
        _const_spec((1, GATE_W)),
        _const_spec((D_MODEL, D_MODEL)),
        _const_spec((1, D_MODEL)),
        _const_spec((1, D_MODEL)),
    ]


def _ffn_weight_specs():
    return [
        _const_spec((D_MODEL, 2 * D_FF)),
        _const_spec((CONV_W, 2 * D_FF)),
        _const_spec((1, 2 * D_FF)),
        _const_spec((D_FF, D_MODEL)),
        _const_spec((1, D_MODEL)),
        _const_spec((1, D_MODEL)),
    ]


_SMEM_SPEC = pl.BlockSpec(memory_space=pltpu.SMEM)
_MEM_ROWS = MEM_TOKENS * MEM_HEADS


def _params(sem):
    return pltpu.CompilerParams(dimension_semantics=sem, vmem_limit_bytes=V7X_VMEM_LIMIT)


def _prompt_layer(x, mem, sinks, mixer_w, ffn_w, w_mem_kv, alpha):
    B, T, _ = x.shape
    rows = min(PROMPT_ROWS, T)
    nt = T // rows
    assert T % rows == 0 and rows % SWA_PAIR == 0 and rows >= WINDOW
    f32 = jnp.float32
    mk, mv = pl.pallas_call(
        _mem_kv_kernel,
        grid=(B,),
        in_specs=[pl.BlockSpec((1, MEM_TOKENS, D_MODEL), lambda b: (b, 0, 0)),
                  _const_spec((D_MODEL, 2 * MEM_W))],
        out_specs=[pl.BlockSpec((1, _MEM_ROWS, MEM_HD), lambda b: (b, 0, 0))] * 2,
        out_shape=[jax.ShapeDtypeStruct((B, _MEM_ROWS, MEM_HD), f32)] * 2,
        compiler_params=_params(("arbitrary",)),
        name="mem_kv",
    )(mem, w_mem_kv)

    rope = jnp.concatenate(_rope_tables(np.arange(T)), axis=1)
    rope_spec = pl.BlockSpec((rows, 3 * LANES), lambda b, t: (t, 0))
    per_batch = lambda *blk: pl.BlockSpec((1,) + blk, lambda b, t: (b,) + (0,) * len(blk))
    x1, gla, swak, swav = pl.pallas_call(
        functools.partial(_prompt_mixer_kernel, rows=rows, alpha=alpha),
        grid=(B, nt),
        in_specs=[_SMEM_SPEC,
                  pl.BlockSpec((1, rows, D_MODEL), lambda b, t: (b, t, 0)),
                  rope_spec,
                  per_batch(_MEM_ROWS, MEM_HD), per_batch(_MEM_ROWS, MEM_HD)] + _mixer_weight_specs(),
        out_specs=[pl.BlockSpec((1, rows, D_MODEL), lambda b, t: (b, t, 0)),
                   per_batch(GLA_HEADS, GLA_DK, GLA_DV),
                   per_batch(WINDOW, SWA_KW), per_batch(WINDOW, SWA_KW)],
        out_shape=[jax.ShapeDtypeStruct((B, T, D_MODEL), f32),
                   jax.ShapeDtypeStruct((B, GLA_HEADS, GLA_DK, GLA_DV), f32),
                   jax.ShapeDtypeStruct((B, WINDOW, SWA_KW), f32),
                   jax.ShapeDtypeStruct((B, WINDOW, SWA_KW), f32)],
        scratch_shapes=[pltpu.VMEM((GLA_HEADS, GLA_DK, GLA_DV), f32),
                        pltpu.VMEM((SWA_KV_HEADS, WINDOW + rows, LANES), _BF),
                        pltpu.VMEM((WINDOW + rows, LANES), _BF),
                        pltpu.VMEM((2, WINDOW + SWA_PAIR, SWA_BLOCK_LANES), f32)],
        compiler_params=_params(("arbitrary", "arbitrary")),
        name="prompt_mixer",
    )(sinks, x, rope, mk, mv, *mixer_w)

    frows = min(PROMPT_FFN_ROWS, T)
    assert T % frows == 0 and frows >= CONV_W - 1
    y, conv = pl.pallas_call(
        functools.partial(_prompt_ffn_kernel, rows=frows, alpha=alpha),
        grid=(B, T // frows),
        in_specs=[pl.BlockSpec((1, frows, D_MODEL), lambda b, t: (b, t, 0))] + _ffn_weight_specs(),
        out_specs=[pl.BlockSpec((1, frows, D_MODEL), lambda b, t: (b, t, 0)),
                   per_batch(CONV_W - 1, 2 * D_FF)],
        out_shape=[jax.ShapeDtypeStruct((B, T, D_MODEL), f32),
                   jax.ShapeDtypeStruct((B, CONV_W - 1, 2 * D_FF), f32)],
        scratch_shapes=[pltpu.VMEM((2 * SUBLANES + frows, 2 * D_FF), f32)],
        compiler_params=_params(("arbitrary", "arbitrary")),
        name="prompt_ffn",
    )(x1, *ffn_w)
    return y, gla, swak, swav, mk, mv, conv


def _sample_layer(x, gla0, swak0, swav0, memk, memv, conv0, sinks, mixer_w, ffn_w, alpha):
    B, tq, _ = x.shape
    nb = max(1, min(B, SAMPLE_ROWS // tq))
    assert B % nb == 0 and tq % 16 == 0 and tq >= CONV_W - 1
    f32 = jnp.float32
    rope = _rope_tables(PAST_LEN + np.arange(tq))
    rope_spec = pl.BlockSpec((tq, LANES), lambda i: (0, 0))
    blk = lambda *s: pl.BlockSpec((nb,) + s, lambda i: (i,) + (0,) * len(s))
    blk1 = lambda *s: pl.BlockSpec((nb,) + s, lambda i: (i,) + (0,) * len(s), pipeline_mode=pl.Buffered(1))
    x1, gla, swak, swav = pl.pallas_call(
        functools.partial(_sample_mixer_kernel, nb=nb, tq=tq, alpha=alpha),
        grid=(B // nb,),
        in_specs=[_SMEM_SPEC, blk(tq, D_MODEL), rope_spec, rope_spec, rope_spec,
                  blk(_MEM_ROWS, MEM_HD), blk(_MEM_ROWS, MEM_HD),
                  blk1(GLA_HEADS, GLA_DK, GLA_DV), blk1(WINDOW, SWA_KW), blk1(WINDOW, SWA_KW)]
                 + _mixer_weight_specs(),
        out_specs=[blk(tq, D_MODEL), blk(GLA_HEADS, GLA_DK, GLA_DV), blk(WINDOW, SWA_KW), blk(WINDOW, SWA_KW)],
        out_shape=[jax.ShapeDtypeStruct((B, tq, D_MODEL), f32),
                   jax.ShapeDtypeStruct((B, GLA_HEADS, GLA_DK, GLA_DV), f32),
                   jax.ShapeDtypeStruct((B, WINDOW, SWA_KW), f32),
                   jax.ShapeDtypeStruct((B, WINDOW, SWA_KW), f32)],
        compiler_params=_params(("arbitrary",)),
        name="sample_mixer",
    )(sinks, x, *rope, memk, memv, gla0, swak0, swav0, *mixer_w)

    assert B % SUBLANES == 0
    fnb = 2 * SUBLANES if B % (2 * SUBLANES) == 0 else SUBLANES
    fblk = lambda *s: pl.BlockSpec((fnb,) + s, lambda i: (i,) + (0,) * len(s))
    y, conv = pl.pallas_call(
        functools.partial(_sample_ffn_kernel, nb=fnb, tq=tq, alpha=alpha),
        grid=(B // fnb,),
        in_specs=[fblk(tq, D_MODEL), fblk(CONV_W - 1, 2 * D_FF)] + _ffn_weight_specs(),
        out_specs=[fblk(tq, D_MODEL), fblk(CONV_W - 1, 2 * D_FF)],
        out_shape=[jax.ShapeDtypeStruct((B, tq, D_MODEL), f32),
                   jax.ShapeDtypeStruct((B, CONV_W - 1, 2 * D_FF), f32)],
        scratch_shapes=[pltpu.VMEM(((CONV_W - 1 + tq) * fnb, 2 * D_FF), f32)],
        compiler_params=_params(("arbitrary",)),
        name="sample_ffn",
    )(x1, conv0, *ffn_w)
    return y, gla, swak, swav, conv


def kernel(x_prompt, x_sample, cache_swa_k, cache_swa_v, state_gla, cache_mem_k, cache_mem_v, cache_ffn_conv, mem_prompt, ln1_g, ln1_b, ln2_g, ln2_b, w_in, b_gate, w_gla_a2, b_gla_a, gla_norm_g, swa_sinks, w_mem_kv, w_br_gla, w_br_swa, w_br_mem, w_o, w_up, conv_w, conv_b, w_down):
    depth = w_in.shape[0]
    alpha = float((2 * depth) ** 0.25)
    Bp = x_prompt.shape[0]
    Bs = x_sample.shape[0]
    hp, hs = x_prompt, x_sample
    outs = [[] for _ in range(10)]
    row = lambda a: a.reshape(1, -1)
    for l in range(depth):
        w2 = jnp.pad(w_gla_a2[l], ((0, LANES - GLA_RANK), (0, 0)))
        mixer_w = _split_w_in(w_in[l]) + (
            _bf(w2), row(b_gla_a[l]), row(gla_norm_g[l]),
            _bf(w_br_gla[l]), _bf(w_br_swa[l]), _bf(w_br_mem[l]), row(b_gate[l]), _bf(w_o[l]),
            row(ln1_g[l]), row(ln1_b[l]))
        ffn_w = (_bf(w_up[l]), conv_w[l], row(conv_b[l]), _bf(w_down[l]), row(ln2_g[l]), row(ln2_b[l]))
        sinks = swa_sinks[l]
        hp, g_p, k_p, v_p, mk_p, mv_p, c_p = _prompt_layer(
            hp, mem_prompt, sinks, mixer_w, ffn_w, _bf(w_mem_kv[l]), alpha)
        hs, g_s, k_s, v_s, c_s = _sample_layer(
            hs, state_gla[l],
            cache_swa_k[l].reshape(Bs, WINDOW, SWA_KW), cache_swa_v[l].reshape(Bs, WINDOW, SWA_KW),
            cache_mem_k[l].reshape(Bs, _MEM_ROWS, MEM_HD), cache_mem_v[l].reshape(Bs, _MEM_ROWS, MEM_HD),
            cache_ffn_conv[l], sinks, mixer_w, ffn_w, alpha)
        kv5 = lambda a, b: a.reshape(b, WINDOW, SWA_KV_HEADS, SWA_HD)
        m5 = lambda a: a.reshape(Bp, MEM_TOKENS, MEM_HEADS, MEM_HD)
        for lst, val in zip(outs, (kv5(k_p, Bp), kv5(v_p, Bp), g_p, m5(mk_p), m5(mv_p), c_p,
                                   kv5(k_s, Bs), kv5(v_s, Bs), g_s, c_s)):
            lst.append(val)
    return (hp, hs) + tuple(jnp.stack(o) for o in outs)
```
